```python
import jax, jax.numpy as jnp
from jax import lax
import numpy as np

D_MODEL = 1024
BATCH = 2
SEQ = 8192
DEPTH = 1

D_MIX = D_MODEL
D_ATTN = D_MIX // 2
D_LRU = D_MIX - D_ATTN
HEAD_DIM = 64
N_ATTN_HEADS = D_ATTN // HEAD_DIM
N_LRU_BLOCKS = 8
LRU_BLOCK = D_LRU // N_LRU_BLOCKS
CONV_WIDTH = 4
LRU_C = 8.0
Q_BLOCK = 128
N_GROUPS = 4
EXPERTS_PER_GROUP = 4
N_EXPERTS = N_GROUPS * EXPERTS_PER_GROUP
TOP_K_IN_GROUP = 2
D_EXPERT = D_MODEL // 2
LN_EPS = 1e-5
RMS_EPS = 1e-6
NEG_INF = -1e30
DEEPNORM_ALPHA = (2.0 * DEPTH) ** 0.25
DEEPNORM_BETA = (8.0 * DEPTH) ** -0.25
D_IN = 3 * D_ATTN + N_ATTN_HEADS + 2 * D_LRU

kernel_name = "fox_rglru_hmoe_adaln_deepnorm_block"


def layer_norm(x, gain=None, bias=None):
    xf = x.astype(jnp.float32)
    mu = jnp.mean(xf, axis=-1, keepdims=True)
    var = jnp.mean(jnp.square(xf - mu), axis=-1, keepdims=True)
    y = (xf - mu) * lax.rsqrt(var + LN_EPS)
    if gain is not None:
        y = y * gain.astype(jnp.float32) + bias.astype(jnp.float32)
    return y.astype(x.dtype)


def rms_norm(x, gain):
    xf = x.astype(jnp.float32)
    y = xf * lax.rsqrt(jnp.mean(jnp.square(xf), axis=-1, keepdims=True) + RMS_EPS)
    return (y * gain.astype(jnp.float32)).astype(x.dtype)


def forgetting_attention(q, k, v, f_logit):
    b, s, h, dh = q.shape
    cum = jnp.cumsum(jax.nn.log_sigmoid(f_logit.astype(jnp.float32)), axis=1)
    cum = jnp.transpose(cum, (0, 2, 1))
    q = jnp.transpose(q, (0, 2, 1, 3)) * (dh ** -0.5)
    k = jnp.transpose(k, (0, 2, 1, 3))
    v = jnp.transpose(v, (0, 2, 1, 3))
    outs = []
    for i in range(s // Q_BLOCK):
        q0 = i * Q_BLOCK
        kend = q0 + Q_BLOCK
        qb = q[:, :, q0:kend]
        kb = k[:, :, :kend]
        vb = v[:, :, :kend]
        logits = jnp.einsum('bhqd,bhkd->bhqk', qb, kb).astype(jnp.float32)
        logits = logits + (cum[:, :, q0:kend, None] - cum[:, :, None, :kend])
        causal = jnp.arange(kend)[None, :] <= (q0 + jnp.arange(Q_BLOCK))[:, None]
        logits = jnp.where(causal, logits, NEG_INF)
        p = jax.nn.softmax(logits, axis=-1).astype(v.dtype)
        outs.append(jnp.einsum('bhqk,bhkd->bhqd', p, vb))
    o = jnp.concatenate(outs, axis=2)
    return jnp.transpose(o, (0, 2, 1, 3)).reshape(b, s, h * dh)


def rg_lru_branch(xb, gb, conv_w, conv_b, w_rg, b_rg, w_ig, b_ig, lru_lambda):
    b, s, _ = xb.shape
    xc = lax.conv_general_dilated(
        xb, conv_w[:, None, :].astype(xb.dtype), window_strides=(1,),
        padding=((CONV_WIDTH - 1, 0),), dimension_numbers=('NWC', 'WIO', 'NWC'),
        feature_group_count=D_LRU) + conv_b
    xr = xc.reshape(b, s, N_LRU_BLOCKS, LRU_BLOCK)
    r = jax.nn.sigmoid(jnp.einsum('bsni,nij->bsnj', xr, w_rg).reshape(b, s, D_LRU) + b_rg)
    ig = jax.nn.sigmoid(jnp.einsum('bsni,nij->bsnj', xr, w_ig).reshape(b, s, D_LRU) + b_ig)
    log_a = -LRU_C * r.astype(jnp.float32) * jax.nn.softplus(-lru_lambda.astype(jnp.float32))
    a = jnp.exp(log_a)
    u = jnp.sqrt(-jnp.expm1(2.0 * log_a)) * (ig * xc).astype(jnp.float32)

    def combine(e1, e2):
        a1, b1 = e1
        a2, b2 = e2
        return a1 * a2, a2 * b1 + b2

    _, hs = lax.associative_scan(combine, (a, u), axis=1)
    return hs.astype(xb.dtype) * jax.nn.gelu(gb)


def hybrid_mixer(u, w_in, b_f, conv_w, conv_b, w_rg, b_rg, w_ig, b_ig, lru_lambda,
                 g_attn, g_lru, w_out):
    b, s, _ = u.shape
    proj = jnp.einsum('bsd,de->bse', u, w_in)
    q, k, v, f_logit, xb, gb = jnp.split(
        proj, [D_ATTN, 2 * D_ATTN, 3 * D_ATTN, 3 * D_ATTN + N_ATTN_HEADS,
               3 * D_ATTN + N_ATTN_HEADS + D_LRU], axis=-1)
    hs = (b, s, N_ATTN_HEADS, HEAD_DIM)
    attn = forgetting_attention(q.reshape(hs), k.reshape(hs), v.reshape(hs), f_logit + b_f)
    lru = rg_lru_branch(xb, gb, conv_w, conv_b, w_rg, b_rg, w_ig, b_ig, lru_lambda)
    mixed = jnp.concatenate([rms_norm(attn, g_attn), rms_norm(lru, g_lru)], axis=-1)
    return jnp.einsum('bse,ed->bsd', mixed, w_out)


def hierarchical_moe(u, w_grp, b_grp, w_exp, b_exp, w_e_gate, w_e_up, w_e_down):
    b, s, d = u.shape
    t = u.reshape(b * s, d)
    grp_prob = jax.nn.softmax((t @ w_grp + b_grp).astype(jnp.float32), axis=-1)
    grp_w, grp_idx = lax.top_k(grp_prob, 1)
    exp_logits = (t @ w_exp + b_exp).astype(jnp.float32).reshape(-1, N_GROUPS, EXPERTS_PER_GROUP)
    sel_logits = jnp.take_along_axis(exp_logits, grp_idx[:, :, None], axis=1)[:, 0]
    top_w, top_idx = lax.top_k(jax.nn.softmax(sel_logits, axis=-1), TOP_K_IN_GROUP)
    top_w = top_w / jnp.sum(top_w, axis=-1, keepdims=True) * grp_w
    expert_id = grp_idx * EXPERTS_PER_GROUP + top_idx
    comb = jnp.sum(jax.nn.one_hot(expert_id, N_EXPERTS, dtype=jnp.float32)
                   * top_w[..., None], axis=1).astype(t.dtype)
    y = jnp.zeros_like(t)
    for e in range(N_EXPERTS):
        h = jax.nn.silu(t @ w_e_gate[e]) * (t @ w_e_up[e])
        y = y + comb[:, e:e + 1] * (h @ w_e_down[e])
    return y.reshape(b, s, d)


def setup_inputs(seed: int = 0) -> dict:
    key = jax.random.key(seed)
    ks = jax.random.split(key, 28)
    nrm = jax.random.normal
    L = DEPTH
    x = nrm(ks[0], (BATCH, SEQ, D_MODEL), jnp.float32)
    c = nrm(ks[1], (BATCH, D_MODEL), jnp.float32)
    w_ada = nrm(ks[2], (L, D_MODEL, 6 * D_MODEL)) * (0.1 * D_MODEL ** -0.5)
    b_ada = nrm(ks[3], (L, 6 * D_MODEL)) * 0.01
    w_in = nrm(ks[4], (L, D_MODEL, D_IN)) * D_MODEL ** -0.5
    w_in = w_in.at[..., 2 * D_ATTN:3 * D_ATTN].multiply(DEEPNORM_BETA)
    b_f = jax.random.uniform(ks[5], (L, N_ATTN_HEADS), minval=1.0, maxval=6.0)
    conv_w = nrm(ks[6], (L, CONV_WIDTH, D_LRU)) * CONV_WIDTH ** -0.5
    conv_b = nrm(ks[7], (L, D_LRU)) * 0.01
    w_rg = nrm(ks[8], (L, N_LRU_BLOCKS, LRU_BLOCK, LRU_BLOCK)) * LRU_BLOCK ** -0.5
    b_rg = nrm(ks[9], (L, D_LRU)) * 0.01
    w_ig = nrm(ks[10], (L, N_LRU_BLOCKS, LRU_BLOCK, LRU_BLOCK)) * LRU_BLOCK ** -0.5
    b_ig = nrm(ks[11], (L, D_LRU)) * 0.01
    a_c = jax.random.uniform(ks[12], (L, D_LRU), minval=0.9, maxval=0.999)
    a0 = a_c ** (1.0 / LRU_C)
    lru_lambda = jnp.log(a0) - jnp.log1p(-a0)
    g_attn = 1.0 + 0.02 * nrm(ks[13], (L, D_ATTN))
    g_lru = 1.0 + 0.02 * nrm(ks[14], (L, D_LRU))
    w_out = nrm(ks[15], (L, D_MIX, D_MODEL)) * (D_MIX ** -0.5 * DEEPNORM_BETA)
    ln1_g = 1.0 + 0.02 * nrm(ks[16], (L, D_MODEL))
    ln1_b = 0.01 * nrm(ks[17], (L, D_MODEL))
    w_grp = nrm(ks[18], (L, D_MODEL, N_GROUPS)) * D_MODEL ** -0.5
    b_grp = 0.01 * nrm(ks[19], (L, N_GROUPS))
    w_exp = nrm(ks[20], (L, D_MODEL, N_EXPERTS)) * D_MODEL ** -0.5
    b_exp = 0.01 * nrm(ks[21], (L, N_EXPERTS))
    w_e_gate = nrm(ks[22], (L, N_EXPERTS, D_MODEL, D_EXPERT)) * D_MODEL ** -0.5
    w_e_up = nrm(ks[23], (L, N_EXPERTS, D_MODEL, D_EXPERT)) * D_MODEL ** -0.5
    w_e_down = nrm(ks[24], (L, N_EXPERTS, D_EXPERT, D_MODEL)) * (D_EXPERT ** -0.5 * DEEPNORM_BETA)
    ln2_g = 1.0 + 0.02 * nrm(ks[25], (L, D_MODEL))
    ln2_b = 0.01 * nrm(ks[26], (L, D_MODEL))
    return {"x": x, "c": c, "w_ada": w_ada, "b_ada": b_ada, "w_in": w_in, "b_f": b_f,
            "conv_w": conv_w, "conv_b": conv_b, "w_rg": w_rg, "b_rg": b_rg,
            "w_ig": w_ig, "b_ig": b_ig, "lru_lambda": lru_lambda, "g_attn": g_attn,
            "g_lru": g_lru, "w_out": w_out, "ln1_g": ln1_g, "ln1_b": ln1_b,
            "w_grp": w_grp, "b_grp": b_grp, "w_exp": w_exp, "b_exp": b_exp,
            "w_e_gate": w_e_gate, "w_e_up": w_e_up, "w_e_down": w_e_down,
            "ln2_g": ln2_g, "ln2_b": ln2_b}


def reference(x, c, w_ada, b_ada, w_in, b_f, conv_w, conv_b, w_rg, b_rg, w_ig, b_ig,
              lru_lambda, g_attn, g_lru, w_out, ln1_g, ln1_b, w_grp, b_grp, w_exp, b_exp,
              w_e_gate, w_e_up, w_e_down, ln2_g, ln2_b):
    for l in range(DEPTH):
        mod = jnp.einsum('bd,de->be', jax.nn.silu(c), w_ada[l]) + b_ada[l]
        sh_m, sc_m, gt_m, sh_f, sc_f, gt_f = jnp.split(mod[:, None, :], 6, axis=-1)
        u = layer_norm(x) * (1.0 + sc_m) + sh_m
        mix = hybrid_mixer(u, w_in[l], b_f[l], conv_w[l], conv_b[l], w_rg[l], b_rg[l],
                           w_ig[l], b_ig[l], lru_lambda[l], g_attn[l], g_lru[l], w_out[l])
        x = layer_norm(DEEPNORM_ALPHA * x + (1.0 + gt_m) * mix, ln1_g[l], ln1_b[l])
        u = layer_norm(x) * (1.0 + sc_f) + sh_f
        ffn = hierarchical_moe(u, w_grp[l], b_grp[l], w_exp[l], b_exp[l],
                               w_e_gate[l], w_e_up[l], w_e_down[l])
        x = layer_norm(DEEPNORM_ALPHA * x + (1.0 + gt_f) * ffn, ln2_g[l], ln2_b[l])
    return x
```

```python
import functools

import jax
import jax.numpy as jnp
from jax import lax
from jax.experimental import pallas as pl
from jax.experimental.pallas import tpu as pltpu

F32 = jnp.float32
BF16 = jnp.bfloat16

HEAD_DIM = 64
N_HEADS = 8
N_LRU_BLOCKS = 8
CONV_WIDTH = 4
LRU_C = 8.0
N_GROUPS = 4
EXPERTS_PER_GROUP = 4
N_PAIRS = 6
N_BUCKETS = N_GROUPS * N_PAIRS
LN_EPS = 1e-5
RMS_EPS = 1e-6
NEG_INF = -1e30
DEPTH = 1
DEEPNORM_ALPHA = (2.0 * DEPTH) ** 0.25

LANES = 128
SUBLANES = 8
HEAD_PAD = LANES
BUCKET_ROWS = 32
TM = 512
TQ = 512
TM_E = 256
VMEM_LIMIT = 56 * 1024 * 1024

_PAIRS = [(0, 1), (0, 2), (0, 3), (1, 2), (1, 3), (2, 3)]
_BUCKET_EA = [g * EXPERTS_PER_GROUP + a for g in range(N_GROUPS) for (a, b) in _PAIRS]
_BUCKET_EB = [g * EXPERTS_PER_GROUP + b for g in range(N_GROUPS) for (a, b) in _PAIRS]


def _dot(a, b):
    return jnp.dot(a, b, preferred_element_type=F32)


def _split2(a):
    hi = a.astype(BF16)
    lo = (a - hi.astype(F32)).astype(BF16)
    return hi, lo


def _split3(a):
    hi = a.astype(BF16)
    r = a - hi.astype(F32)
    mid = r.astype(BF16)
    lo = (r - mid.astype(F32)).astype(BF16)
    return hi, mid, lo


def _layer_norm(x):
    mu = jnp.mean(x, axis=-1, keepdims=True)
    xc = x - mu
    var = jnp.mean(xc * xc, axis=-1, keepdims=True)
    return xc * lax.rsqrt(var + LN_EPS)


def _sigmoid(x):
    return 1.0 / (1.0 + jnp.exp(-x))


def _log_sigmoid(z):
    return jnp.minimum(z, 0.0) - jnp.log1p(jnp.exp(-jnp.abs(z)))


def _shift_rows(x, k, fill):
    n = x.shape[0]
    if k % SUBLANES == 0:
        return jnp.concatenate([jnp.full((k, x.shape[1]), fill, x.dtype), x[:n - k]], axis=0)
    row = lax.broadcasted_iota(jnp.int32, x.shape, 0)
    return jnp.where(row >= k, pltpu.roll(x, k, 0), fill)


def _mod_kernel(c_ref, w_ref, b_ref, o_ref):
    c = c_ref[...]
    s = c * _sigmoid(c)
    sh, sl = _split2(s)
    wh, wl = _split2(w_ref[...])
    o_ref[...] = _dot(sh, wh) + (_dot(sh, wl) + _dot(sl, wh)) + b_ref[...]


def _mod_call(c_pad, w_ada, b_ada):
    rows, d = c_pad.shape
    n = w_ada.shape[1]
    return pl.pallas_call(
        _mod_kernel,
        grid=(n // d,),
        in_specs=[pl.BlockSpec((rows, d), lambda j: (0, 0)),
                  pl.BlockSpec((d, d), lambda j: (0, j)),
                  pl.BlockSpec((1, d), lambda j: (0, j))],
        out_specs=pl.BlockSpec((rows, d), lambda j: (0, j)),
        out_shape=jax.ShapeDtypeStruct((rows, n), F32),
        compiler_params=pltpu.CompilerParams(vmem_limit_bytes=VMEM_LIMIT),
        name="mod",
    )(c_pad, w_ada, b_ada)


def _inproj_kernel(x_ref, mod_ref, wq_ref, wk_ref, wv_ref, wf_ref, wx_ref, wg_ref, bf_ref,
                   q_ref, k_ref, v_ref, xb_ref, gb_ref, carry_ref):
    j = pl.program_id(1)

    @pl.when(j == 0)
    def _():
        carry_ref[...] = jnp.zeros_like(carry_ref)

    tm = x_ref.shape[1]
    u = _layer_norm(x_ref[0]) * (1.0 + mod_ref[0, 1:2, :]) + mod_ref[0, 0:1, :]
    ub = u.astype(BF16)

    logf = _log_sigmoid(_dot(ub, wf_ref[...]) + bf_ref[...])
    row = lax.broadcasted_iota(jnp.int32, (tm, tm), 0)
    col = lax.broadcasted_iota(jnp.int32, (tm, tm), 1)
    tril = (col <= row).astype(BF16)
    l_hi, l_mid, l_lo = _split3(logf)
    cum = (_dot(tril, l_hi) + (_dot(tril, l_mid) + _dot(tril, l_lo))) + carry_ref[...]
    carry_ref[...] = cum[tm - 1:tm, :]
    c_hi, c_mid, c_lo = [p.astype(F32) for p in _split3(cum)]

    lane = lax.broadcasted_iota(jnp.int32, (tm, HEAD_PAD), 1)
    d = HEAD_DIM
    for hp in range(N_HEADS // 2):
        sl = slice(hp * 2 * HEAD_PAD, (hp + 1) * 2 * HEAD_PAD)
        q2 = _dot(ub, wq_ref[:, sl])
        k2 = _dot(ub, wk_ref[:, sl])
        v2 = _dot(ub, wv_ref[:, sl])
        for hh in range(2):
            h = hp * 2 + hh
            hs = slice(hh * HEAD_PAD, (hh + 1) * HEAD_PAD)
            hi, mid, lo = c_hi[:, h:h + 1], c_mid[:, h:h + 1], c_lo[:, h:h + 1]
            q_ext = jnp.where(lane == d, hi, jnp.where(lane == d + 1, mid, jnp.where(
                lane == d + 2, lo, jnp.where(lane < d + 6, 1.0, 0.0))))
            k_ext = jnp.where(lane < d + 3, 1.0, jnp.where(lane == d + 3, -hi, jnp.where(
                lane == d + 4, -mid, jnp.where(lane == d + 5, -lo, 0.0))))
            q_ref[0, h] = jnp.where(lane < d, q2[:, hs], q_ext).astype(BF16)
            k_ref[0, h] = jnp.where(lane < d, k2[:, hs], k_ext).astype(BF16)
            v_ref[0, h] = jnp.where(lane == d, 1.0, v2[:, hs]).astype(BF16)

    xb_ref[0] = _dot(ub, wx_ref[...])
    gb_ref[0] = _dot(ub, wg_ref[...])


def _inproj_call(x, mod3, wq, wk, wv, wf, wx, wg, bf_pad):
    b, s, d = x.shape
    hw = N_HEADS * HEAD_PAD
    d_lru = wx.shape[1]
    full = lambda shape: pl.BlockSpec(shape, lambda bi, j: (0,) * len(shape))
    head_spec = pl.BlockSpec((1, N_HEADS, TM, HEAD_PAD), lambda bi, j: (bi, 0, j, 0))
    row_spec = pl.BlockSpec((1, TM, d_lru), lambda bi, j: (bi, j, 0))
    head_shape = jax.ShapeDtypeStruct((b, N_HEADS, s, HEAD_PAD), BF16)
    return pl.pallas_call(
        _inproj_kernel,
        grid=(b, s // TM),
        in_specs=[pl.BlockSpec((1, TM, d), lambda bi, j: (bi, j, 0)),
                  pl.BlockSpec((1, 6, d), lambda bi, j: (bi, 0, 0)),
                  full((d, hw)), full((d, hw)), full((d, hw)), full((d, LANES)),
                  full((d, d_lru)), full((d, d_lru)), full((1, LANES))],
        out_specs=[head_spec, head_spec, head_spec, row_spec, row_spec],
        out_shape=[head_shape, head_shape, head_shape,
                   jax.ShapeDtypeStruct((b, s, d_lru), F32),
                   jax.ShapeDtypeStruct((b, s, d_lru), F32)],
        scratch_shapes=[pltpu.VMEM((1, LANES), F32)],
        compiler_params=pltpu.CompilerParams(
            dimension_semantics=("arbitrary", "arbitrary"), vmem_limit_bytes=VMEM_LIMIT),
        name="inproj",
    )(x, mod3, wq, wk, wv, wf, wx, wg, bf_pad)


def _attn_kernel(q_ref, k_ref, v_ref, o_ref):
    i = pl.program_id(2)
    tq = q_ref.shape[2]
    outs = []
    for hh in range(2):
        q = q_ref[0, hh]

        def scores(j):
            k = k_ref[0, hh, pl.ds(pl.multiple_of(j * tq, tq), tq), :]
            return lax.dot_general(q, k, (((1,), (1,)), ((), ())), preferred_element_type=F32)

        def update(j, s, m, acc):
            m_new = jnp.maximum(m, jnp.max(s, axis=1, keepdims=True))
            p = jnp.exp(s - m_new)
            v = v_ref[0, hh, pl.ds(pl.multiple_of(j * tq, tq), tq), :]
            acc = jnp.exp(m - m_new) * acc + _dot(p.astype(BF16), v)
            return m_new, acc

        def body(j, carry):
            return update(j, scores(j), *carry)

        m0 = jnp.full((tq, 1), NEG_INF, F32)
        acc0 = jnp.zeros((tq, HEAD_PAD), F32)
        m, acc = lax.fori_loop(0, i, body, (m0, acc0))
        row = lax.broadcasted_iota(jnp.int32, (tq, tq), 0)
        col = lax.broadcasted_iota(jnp.int32, (tq, tq), 1)
        s = jnp.where(col <= row, scores(i), NEG_INF)
        m, acc = update(i, s, m, acc)
        outs.append(acc[:, :HEAD_DIM] / acc[:, HEAD_DIM:HEAD_DIM + 1])
    o_ref[0] = jnp.concatenate(outs, axis=1)


def _attn_call(q_aug, k_aug, v_aug):
    b, h, s, hp = q_aug.shape
    kv_spec = pl.BlockSpec((1, 2, s, hp), lambda bi, p, i: (bi, p, 0, 0))
    return pl.pallas_call(
        _attn_kernel,
        grid=(b, h // 2, s // TQ),
        in_specs=[pl.BlockSpec((1, 2, TQ, hp), lambda bi, p, i: (bi, p, i, 0)), kv_spec, kv_spec],
        out_specs=pl.BlockSpec((1, TQ, 2 * HEAD_DIM), lambda bi, p, i: (bi, i, p)),
        out_shape=jax.ShapeDtypeStruct((b, s, h * HEAD_DIM), F32),
        compiler_params=pltpu.CompilerParams(
            dimension_semantics=("arbitrary", "arbitrary", "arbitrary"),
            vmem_limit_bytes=VMEM_LIMIT),
        name="attention",
    )(q_aug, k_aug, v_aug)


def _gelu_tanh(x):
    return 0.5 * x * (1.0 + jnp.tanh(0.7978845608028654 * (x + 0.044715 * (x * x * x))))


def _lru_kernel(xb_ref, gb_ref, cw_ref, cb_ref, wgate_ref, brg_ref, big_ref, lam_ref,
                o_ref, tail_ref, h_ref):
    j = pl.program_id(1)

    @pl.when(j == 0)
    def _():
        tail_ref[...] = jnp.zeros_like(tail_ref)
        h_ref[...] = jnp.zeros_like(h_ref)

    x = xb_ref[0]
    tm, dl = x.shape
    tail = tail_ref[...]
    row8 = lax.broadcasted_iota(jnp.int32, (SUBLANES, dl), 0)
    xc = x * cw_ref[CONV_WIDTH - 1:CONV_WIDTH, :] + cb_ref[...]
    for k in range(1, CONV_WIDTH):
        xr = pltpu.roll(x, k, 0)
        top = jnp.where(row8 < k, pltpu.roll(tail, k, 0), xr[:SUBLANES])
        xs = jnp.concatenate([top, xr[SUBLANES:]], axis=0)
        xc = xc + xs * cw_ref[CONV_WIDTH - 1 - k:CONV_WIDTH - k, :]
    tail_ref[...] = x[tm - SUBLANES:, :]

    xcb = xc.astype(BF16)
    n_pairs = dl // LANES
    r_parts, i_parts = [], []
    for p in range(n_pairs):
        g = _dot(xcb[:, p * LANES:(p + 1) * LANES], wgate_ref[p])
        r_parts.append(g[:, :LANES])
        i_parts.append(g[:, LANES:])
    r = _sigmoid(jnp.concatenate(r_parts, axis=1) + brg_ref[...])
    ig = _sigmoid(jnp.concatenate(i_parts, axis=1) + big_ref[...])

    lam = lam_ref[...]
    softplus_neg_lam = jnp.maximum(-lam, 0.0) + jnp.log1p(jnp.exp(-jnp.abs(lam)))
    log_a = (-LRU_C) * r * softplus_neg_lam
    a = jnp.exp(log_a)
    u = jnp.sqrt(-jnp.tanh(log_a) * (a * a + 1.0)) * (ig * xc)

    k = 1
    while k < tm:
        u = a * _shift_rows(u, k, 0.0) + u
        a = a * _shift_rows(a, k, 1.0)
        k *= 2
    h = a * h_ref[...] + u
    h_ref[...] = h[tm - 1:tm, :]
    o_ref[0] = h * _gelu_tanh(gb_ref[0])


def _lru_call(xb, gb, conv_w, conv_b, wgate, b_rg, b_ig, lam):
    b, s, dl = xb.shape
    row_spec = pl.BlockSpec((1, TM, dl), lambda bi, j: (bi, j, 0))
    full = lambda shape: pl.BlockSpec(shape, lambda bi, j: (0,) * len(shape))
    return pl.pallas_call(
        _lru_kernel,
        grid=(b, s // TM),
        in_specs=[row_spec, row_spec, full(conv_w.shape), full((1, dl)), full(wgate.shape),
                  full((1, dl)), full((1, dl)), full((1, dl))],
        out_specs=row_spec,
        out_shape=jax.ShapeDtypeStruct((b, s, dl), F32),
        scratch_shapes=[pltpu.VMEM((SUBLANES, dl), F32), pltpu.VMEM((1, dl), F32)],
        compiler_params=pltpu.CompilerParams(
            dimension_semantics=("arbitrary", "arbitrary"), vmem_limit_bytes=VMEM_LIMIT),
        name="lru",
    )(xb, gb, conv_w, conv_b, wgate, b_rg, b_ig, lam)


def _rms(x, gain):
    return x * lax.rsqrt(jnp.mean(x * x, axis=-1, keepdims=True) + RMS_EPS) * gain


def _mixout_kernel(attn_ref, lru_ref, x_ref, mod_ref, ga_ref, gl_ref, woa_ref, wol_ref,
                   g1_ref, b1_ref, wr_ref, br_ref, x1_ref, u2_ref, rinfo_ref):
    tm, d = x_ref.shape
    na = _rms(attn_ref[...], ga_ref[...]).astype(BF16)
    nl = _rms(lru_ref[...], gl_ref[...]).astype(BF16)
    mix = _dot(na, woa_ref[...]) + _dot(nl, wol_ref[...])
    z = DEEPNORM_ALPHA * x_ref[...] + (1.0 + mod_ref[0, 2:3, :]) * mix
    x1 = _layer_norm(z) * g1_ref[...] + b1_ref[...]
    x1_ref[...] = x1
    u2 = _layer_norm(x1) * (1.0 + mod_ref[0, 4:5, :]) + mod_ref[0, 3:4, :]

    nt = (((1,), (1,)), ((), ()))
    uh, ul = _split2(u2)
    wh, wl = _split2(wr_ref[...])
    dg = lambda a, b_: lax.dot_general(a, b_, nt, preferred_element_type=F32)
    lg = dg(wh, uh) + (dg(wh, ul) + dg(wl, uh)) + br_ref[:, 0:1]

    def first_index(vals, target):
        idx = jnp.full_like(target, float(len(vals) - 1))
        for n in range(len(vals) - 2, -1, -1):
            idx = jnp.where(vals[n] == target, float(n), idx)
        return idx

    g = [lg[n:n + 1, :] for n in range(N_GROUPS)]
    gmax = functools.reduce(jnp.maximum, g)
    gsum = functools.reduce(lambda a, b_: a + b_, [jnp.exp(v - gmax) for v in g])
    grp_w = 1.0 / gsum
    gidx = first_index(g, gmax)

    sel = []
    for e in range(EXPERTS_PER_GROUP):
        v = lg[N_GROUPS + (N_GROUPS - 1) * EXPERTS_PER_GROUP + e:
               N_GROUPS + (N_GROUPS - 1) * EXPERTS_PER_GROUP + e + 1, :]
        for gi in range(N_GROUPS - 2, -1, -1):
            r0 = N_GROUPS + gi * EXPERTS_PER_GROUP + e
            v = jnp.where(gidx == float(gi), lg[r0:r0 + 1, :], v)
        sel.append(v)
    smax = functools.reduce(jnp.maximum, sel)
    i1 = first_index(sel, smax)
    rest = [jnp.where(i1 == float(e), -3e38, sel[e]) for e in range(EXPERTS_PER_GROUP)]
    rmax = functools.reduce(jnp.maximum, rest)
    i2 = first_index(rest, rmax)
    e2 = jnp.exp(rmax - smax)
    w1 = grp_w / (1.0 + e2)
    w2 = grp_w * e2 / (1.0 + e2)
    ia = jnp.minimum(i1, i2)
    ib = jnp.maximum(i1, i2)
    wa = jnp.where(i1 < i2, w1, w2)
    wb = jnp.where(i1 < i2, w2, w1)
    pair = jnp.where(ia == 0.0, ib - 1.0, jnp.where(ia == 1.0, ib + 1.0, 5.0))
    bucket = gidx * float(N_PAIRS) + pair

    zrow = jnp.zeros_like(wa)
    rinfo_ref[...] = jnp.concatenate([bucket, wa, wb] + [zrow] * (SUBLANES - 3), axis=0)
    wt = jnp.concatenate([wa, wb, jnp.zeros((LANES - 2, tm), F32)], axis=0)
    u2_ref[:, :d] = u2
    u2_ref[:, d:] = wt.T


def _mixout_call(attn, lru, x2d, mod3, ga, gl, woa, wol, g1, b1, wr, br, seq):
    t, d = x2d.shape
    dh = attn.shape[1]
    per_b = seq // TM
    full = lambda shape: pl.BlockSpec(shape, lambda i: (0,) * len(shape))
    return pl.pallas_call(
        _mixout_kernel,
        grid=(t // TM,),
        in_specs=[pl.BlockSpec((TM, dh), lambda i: (i, 0)),
                  pl.BlockSpec((TM, dh), lambda i: (i, 0)),
                  pl.BlockSpec((TM, d), lambda i: (i, 0)),
                  pl.BlockSpec((1, 6, d), lambda i: (i // per_b, 0, 0)),
                  full((1, dh)), full((1, dh)), full((dh, d)), full((dh, d)),
                  full((1, d)), full((1, d)), full((BUCKET_ROWS, d)), full((BUCKET_ROWS, LANES))],
        out_specs=[pl.BlockSpec((TM, d), lambda i: (i, 0)),
                   pl.BlockSpec((TM, d + LANES), lambda i: (i, 0)),
                   pl.BlockSpec((SUBLANES, TM), lambda i: (0, i))],
        out_shape=[jax.ShapeDtypeStruct((t, d), F32),
                   jax.ShapeDtypeStruct((t, d + LANES), F32),
                   jax.ShapeDtypeStruct((SUBLANES, t), F32)],
        compiler_params=pltpu.CompilerParams(
            dimension_semantics=("arbitrary",), vmem_limit_bytes=VMEM_LIMIT),
        name="mixout",
    )(attn, lru, x2d, mod3, ga, gl, woa, wol, g1, b1, wr, br)


def _rank_kernel(rinfo_ref, dest_ref, counts_ref, carry_ref, offs_ref):
    phase = pl.program_id(0)
    t = pl.program_id(1)
    tm = rinfo_ref.shape[1]

    @pl.when(t == 0)
    def _():
        carry_ref[...] = jnp.zeros_like(carry_ref)

    bucket = rinfo_ref[0:1, :]
    cid = lax.broadcasted_iota(jnp.int32, (BUCKET_ROWS, tm), 0).astype(F32)
    onehot = jnp.where(cid == bucket, 1.0, 0.0)
    srow = lax.broadcasted_iota(jnp.int32, (tm, tm), 0)
    scol = lax.broadcasted_iota(jnp.int32, (tm, tm), 1)
    upper = (srow <= scol).astype(BF16)
    prefix = _dot(onehot.astype(BF16), upper)
    carry = carry_ref[...]

    @pl.when(phase == 1)
    def _():
        rank = jnp.sum(onehot * (prefix - 1.0 + carry[:, 0:1] + offs_ref[:, 0:1]),
                       axis=0, keepdims=True)
        dest_ref[...] = rank.astype(jnp.int32)

    new_carry = carry + prefix[:, tm - 1:tm]
    carry_ref[...] = new_carry

    @pl.when(jnp.logical_and(phase == 0, t == pl.num_programs(1) - 1))
    def _():
        padded = jnp.floor((new_carry + float(TM_E - 1)) * (1.0 / TM_E)) * float(TM_E)
        inc = padded
        k = 1
        while k < BUCKET_ROWS:
            inc = inc + _shift_rows(inc, k, 0.0)
            k *= 2
        offs_ref[...] = inc - padded
        counts_ref[...] = new_carry


def _rank_call(rinfo):
    t = rinfo.shape[1]
    return pl.pallas_call(
        _rank_kernel,
        grid=(2, t // TM),
        in_specs=[pl.BlockSpec((SUBLANES, TM), lambda p, i: (0, i))],
        out_specs=[pl.BlockSpec((1, TM), lambda p, i: (0, i * p)),
                   pl.BlockSpec((BUCKET_ROWS, LANES), lambda p, i: (0, 0))],
        out_shape=[jax.ShapeDtypeStruct((1, t), jnp.int32),
                   jax.ShapeDtypeStruct((BUCKET_ROWS, LANES), F32)],
        scratch_shapes=[pltpu.VMEM((BUCKET_ROWS, LANES), F32),
                        pltpu.VMEM((BUCKET_ROWS, LANES), F32)],
        compiler_params=pltpu.CompilerParams(
            dimension_semantics=("arbitrary", "arbitrary"), vmem_limit_bytes=VMEM_LIMIT),
        name="rank",
    )(rinfo)


def _dispatch_kernel(dest_ref, u2_ref, xs_in_ref, xs_ref, sem):
    del xs_in_ref
    tm = u2_ref.shape[0]
    t0 = pl.program_id(0) * tm

    def issue(r, c):
        pltpu.make_async_copy(u2_ref.at[pl.ds(r, 1)],
                              xs_ref.at[pl.ds(dest_ref[t0 + r], 1)], sem).start()
        return c

    lax.fori_loop(0, tm, issue, 0)
    pltpu.make_async_copy(u2_ref, xs_ref.at[pl.ds(0, tm)], sem).wait()


def _dispatch_call(dest, u2ext, xs_init):
    t, w = u2ext.shape
    grid_spec = pltpu.PrefetchScalarGridSpec(
        num_scalar_prefetch=1,
        grid=(t // TM,),
        in_specs=[pl.BlockSpec((TM, w), lambda i, dest_ref: (i, 0)),
                  pl.BlockSpec(memory_space=pl.ANY)],
        out_specs=pl.BlockSpec(memory_space=pl.ANY),
        scratch_shapes=[pltpu.SemaphoreType.DMA(())],
    )
    return pl.pallas_call(
        _dispatch_kernel,
        grid_spec=grid_spec,
        out_shape=jax.ShapeDtypeStruct(xs_init.shape, F32),
        input_output_aliases={2: 0},
        compiler_params=pltpu.CompilerParams(
            dimension_semantics=("arbitrary",), vmem_limit_bytes=VMEM_LIMIT),
        name="dispatch",
    )(dest, u2ext, xs_init)


def _experts_kernel(ea_ref, eb_ref, nv_ref, xs_ref, wga_ref, wua_ref, wda_ref,
                    wgb_ref, wub_ref, wdb_ref, ys_ref):
    del ea_ref, eb_ref
    i = pl.program_id(0)
    d = ys_ref.shape[1]

    @pl.when(i < nv_ref[0])
    def _():
        x = xs_ref[:, :d].astype(BF16)

        def expert(wg_ref, wu_ref, wd_ref):
            g = _dot(x, wg_ref[0])
            h = (g * _sigmoid(g)) * _dot(x, wu_ref[0])
            return _dot(h.astype(BF16), wd_ref[0])

        ya = xs_ref[:, d:d + 1] * expert(wga_ref, wua_ref, wda_ref)
        ys_ref[...] = ya + xs_ref[:, d + 1:d + 2] * expert(wgb_ref, wub_ref, wdb_ref)

    @pl.when(i >= nv_ref[0])
    def _():
        ys_ref[...] = jnp.zeros_like(ys_ref)


def _experts_call(tile_ea, tile_eb, n_valid, xs, wg, wu, wd):
    tp, w = xs.shape
    _, d, de = wg.shape
    row = lambda i, ea, eb, nv: (jnp.minimum(i, nv[0] - 1), 0)
    wa = lambda i, ea, eb, nv: (ea[i], 0, 0)
    wb = lambda i, ea, eb, nv: (eb[i], 0, 0)
    grid_spec = pltpu.PrefetchScalarGridSpec(
        num_scalar_prefetch=3,
        grid=(tp // TM_E,),
        in_specs=[pl.BlockSpec((TM_E, w), row),
                  pl.BlockSpec((1, d, de), wa), pl.BlockSpec((1, d, de), wa),
                  pl.BlockSpec((1, de, d), wa),
                  pl.BlockSpec((1, d, de), wb), pl.BlockSpec((1, d, de), wb),
                  pl.BlockSpec((1, de, d), wb)],
        out_specs=pl.BlockSpec((TM_E, d), lambda i, ea, eb, nv: (i, 0)),
    )
    return pl.pallas_call(
        _experts_kernel,
        grid_spec=grid_spec,
        out_shape=jax.ShapeDtypeStruct((tp, d), F32),
        compiler_params=pltpu.CompilerParams(
            dimension_semantics=("arbitrary",), vmem_limit_bytes=VMEM_LIMIT),
        name="experts",
    )(tile_ea, tile_eb, n_valid, xs, wg, wu, wd, wg, wu, wd)


def _final_kernel(dest_ref, x1_ref, mod_ref, g2_ref, b2_ref, ys_ref, o_ref, ybuf, sem):
    tm = x1_ref.shape[0]
    t0 = pl.program_id(0) * tm

    def issue(r, c):
        pltpu.make_async_copy(ys_ref.at[pl.ds(dest_ref[t0 + r], 1)],
                              ybuf.at[pl.ds(r, 1)], sem).start()
        return c

    lax.fori_loop(0, tm, issue, 0)
    pltpu.make_async_copy(ys_ref.at[pl.ds(0, tm)], ybuf, sem).wait()
    z = DEEPNORM_ALPHA * x1_ref[...] + (1.0 + mod_ref[0, 5:6, :]) * ybuf[...]
    o_ref[...] = _layer_norm(z) * g2_ref[...] + b2_ref[...]


def _final_call(dest, x1, mod3, g2, b2, ys, seq):
    t, d = x1.shape
    per_b = seq // TM
    grid_spec = pltpu.PrefetchScalarGridSpec(
        num_scalar_prefetch=1,
        grid=(t // TM,),
        in_specs=[pl.BlockSpec((TM, d), lambda i, dr: (i, 0)),
                  pl.BlockSpec((1, 6, d), lambda i, dr: (i // per_b, 0, 0)),
                  pl.BlockSpec((1, d), lambda i, dr: (0, 0)),
                  pl.BlockSpec((1, d), lambda i, dr: (0, 0)),
                  pl.BlockSpec(memory_space=pl.ANY)],
        out_specs=pl.BlockSpec((TM, d), lambda i, dr: (i, 0)),
        scratch_shapes=[pltpu.VMEM((TM, d), F32), pltpu.SemaphoreType.DMA(())],
    )
    return pl.pallas_call(
        _final_kernel,
        grid_spec=grid_spec,
        out_shape=jax.ShapeDtypeStruct((t, d), F32),
        compiler_params=pltpu.CompilerParams(
            dimension_semantics=("arbitrary",), vmem_limit_bytes=VMEM_LIMIT),
        name="final",
    )(dest, x1, mod3, g2, b2, ys)


def _pad_heads(w, scale=1.0):
    d = w.shape[0]
    w = (w * scale).reshape(d, N_HEADS, HEAD_DIM)
    w = jnp.pad(w, ((0, 0), (0, 0), (0, HEAD_PAD - HEAD_DIM)))
    return w.reshape(d, N_HEADS * HEAD_PAD).astype(BF16)


def _gate_pairs(w_rg, w_ig):
    def pairs(w):
        n, bs, _ = w.shape
        w = w.reshape(n // 2, 2, bs, bs)
        z = jnp.zeros((n // 2, bs, bs), w.dtype)
        top = jnp.concatenate([w[:, 0], z], axis=2)
        bot = jnp.concatenate([z, w[:, 1]], axis=2)
        return jnp.concatenate([top, bot], axis=1)
    return jnp.concatenate([pairs(w_rg), pairs(w_ig)], axis=2).astype(BF16)


def kernel(x, c, w_ada, b_ada, w_in, b_f, conv_w, conv_b, w_rg, b_rg, w_ig, b_ig, lru_lambda,
           g_attn, g_lru, w_out, ln1_g, ln1_b, w_grp, b_grp, w_exp, b_exp,
           w_e_gate, w_e_up, w_e_down, ln2_g, ln2_b):
    assert w_ada.shape[0] == DEPTH
    b, s, d = x.shape
    t = b * s
    d_attn = N_HEADS * HEAD_DIM
    d_lru = conv_w.shape[2]
    n_exp = w_e_gate.shape[1]
    assert s % TM == 0 and s % TQ == 0 and t % TM_E == 0

    c_pad = jnp.pad(c, ((0, SUBLANES - b), (0, 0)))
    mod = _mod_call(c_pad, w_ada[0], b_ada[0][None, :])
    mod3 = mod[:b].reshape(b, 6, d)

    wi = w_in[0]
    o = 3 * d_attn
    wq = _pad_heads(wi[:, :d_attn], HEAD_DIM ** -0.5)
    wk = _pad_heads(wi[:, d_attn:2 * d_attn])
    wv = _pad_heads(wi[:, 2 * d_attn:o])
    wf = jnp.pad(wi[:, o:o + N_HEADS], ((0, 0), (0, LANES - N_HEADS))).astype(BF16)
    wx = wi[:, o + N_HEADS:o + N_HEADS + d_lru].astype(BF16)
    wg = wi[:, o + N_HEADS + d_lru:].astype(BF16)
    bf_pad = jnp.pad(b_f[0], (0, LANES - N_HEADS))[None, :]
    q_aug, k_aug, v_aug, xb, gb = _inproj_call(x, mod3, wq, wk, wv, wf, wx, wg, bf_pad)
    attn = _attn_call(q_aug, k_aug, v_aug)
    lru = _lru_call(xb, gb, conv_w[0], conv_b[0][None, :], _gate_pairs(w_rg[0], w_ig[0]),
                    b_rg[0][None, :], b_ig[0][None, :], lru_lambda[0][None, :])

    wo = w_out[0].astype(BF16)
    n_route = N_GROUPS + n_exp
    wr = jnp.pad(jnp.concatenate([w_grp[0], w_exp[0]], axis=1).T,
                 ((0, BUCKET_ROWS - n_route), (0, 0)))
    br = jnp.pad(jnp.concatenate([b_grp[0], b_exp[0]]), (0, BUCKET_ROWS - n_route))
    br = jnp.broadcast_to(br[:, None], (BUCKET_ROWS, LANES))
    x1, u2ext, rinfo = _mixout_call(
        attn.reshape(t, d_attn), lru.reshape(t, d_lru), x.reshape(t, d), mod3,
        g_attn[0][None, :], g_lru[0][None, :], wo[:d_attn], wo[d_attn:],
        ln1_g[0][None, :], ln1_b[0][None, :], wr, br, s)

    dest2d, counts = _rank_call(rinfo)
    dest = dest2d.reshape(t)
    cnt = counts[:N_BUCKETS, 0].astype(jnp.int32)
    ends = jnp.cumsum((cnt + (TM_E - 1)) // TM_E)
    n_tiles = t // TM_E + N_BUCKETS
    tile_bucket = jnp.sum(ends[None, :] <= jnp.arange(n_tiles)[:, None], axis=1)
    tile_bucket = jnp.minimum(tile_bucket, N_BUCKETS - 1)
    n_valid = ends[N_BUCKETS - 1:]
    last_bucket = tile_bucket[jnp.maximum(n_valid[0] - 1, 0)]
    tile_bucket = jnp.where(jnp.arange(n_tiles) < n_valid[0], tile_bucket, last_bucket)
    tile_ea = jnp.asarray(_BUCKET_EA, jnp.int32)[tile_bucket]
    tile_eb = jnp.asarray(_BUCKET_EB, jnp.int32)[tile_bucket]

    xs = _dispatch_call(dest, u2ext, jnp.zeros((n_tiles * TM_E, d + LANES), F32))
    ys = _experts_call(tile_ea, tile_eb, n_valid.astype(jnp.int32), xs,
                       w_e_gate[0].astype(BF16), w_e_up[0].astype(BF16),
                       w_e_down[0].astype(BF16))
    out = _final_call(dest, x1, mod3, ln2_g[0][None, :], ln2_b[0][None, :], ys, s)
    return out.reshape(b, s, d)
```

```python
import functools

import jax
import jax.numpy as jnp
from jax import lax
from jax.experimental import pallas as pl
from jax.experimental.pallas import tpu as pltpu

F32 = jnp.float32
BF16 = jnp.bfloat16

HEAD_DIM = 64
N_HEADS = 8
N_LRU_BLOCKS = 8
CONV_WIDTH = 4
LRU_C = 8.0
N_GROUPS = 4
EXPERTS_PER_GROUP = 4
N_PAIRS = 6
N_BUCKETS = N_GROUPS * N_PAIRS
LN_EPS = 1e-5
RMS_EPS = 1e-6
NEG_INF = -1e30
DEPTH = 1
DEEPNORM_ALPHA = (2.0 * DEPTH) ** 0.25

LANES = 128
SUBLANES = 8
HEAD_PAD = LANES
BUCKET_ROWS = 32
TM = 512
TQ = 512
TM_E = 256
ISSUE_UNROLL = 8
VMEM_LIMIT = 56 * 1024 * 1024

_PAIRS = [(0, 1), (0, 2), (0, 3), (1, 2), (1, 3), (2, 3)]
_BUCKET_EA = [g * EXPERTS_PER_GROUP + a for g in range(N_GROUPS) for (a, b) in _PAIRS]
_BUCKET_EB = [g * EXPERTS_PER_GROUP + b for g in range(N_GROUPS) for (a, b) in _PAIRS]


def _dot(a, b):
    return jnp.dot(a, b, preferred_element_type=F32)


def _split2(a):
    hi = a.astype(BF16)
    lo = (a - hi.astype(F32)).astype(BF16)
    return hi, lo


def _split3(a):
    hi = a.astype(BF16)
    r = a - hi.astype(F32)
    mid = r.astype(BF16)
    lo = (r - mid.astype(F32)).astype(BF16)
    return hi, mid, lo


def _layer_norm(x):
    mu = jnp.mean(x, axis=-1, keepdims=True)
    xc = x - mu
    var = jnp.mean(xc * xc, axis=-1, keepdims=True)
    return xc * lax.rsqrt(var + LN_EPS)


def _sigmoid(x):
    return 1.0 / (1.0 + jnp.exp(-x))


def _log_sigmoid(z):
    return jnp.minimum(z, 0.0) - jnp.log1p(jnp.exp(-jnp.abs(z)))


def _shift_rows(x, k, fill):
    n = x.shape[0]
    if k % SUBLANES == 0:
        return jnp.concatenate([jnp.full((k, x.shape[1]), fill, x.dtype), x[:n - k]], axis=0)
    row = lax.broadcasted_iota(jnp.int32, x.shape, 0)
    return jnp.where(row >= k, pltpu.roll(x, k, 0), fill)


def _mod_kernel(c_ref, w_ref, b_ref, o_ref):
    c = c_ref[...]
    s = c * _sigmoid(c)
    sh, sl = _split2(s)
    wh, wl = _split2(w_ref[...])
    o_ref[...] = _dot(sh, wh) + (_dot(sh, wl) + _dot(sl, wh)) + b_ref[...]


def _mod_call(c_pad, w_ada, b_ada):
    rows, d = c_pad.shape
    n = w_ada.shape[1]
    return pl.pallas_call(
        _mod_kernel,
        grid=(n // d,),
        in_specs=[pl.BlockSpec((rows, d), lambda j: (0, 0)),
                  pl.BlockSpec((d, d), lambda j: (0, j)),
                  pl.BlockSpec((1, d), lambda j: (0, j))],
        out_specs=pl.BlockSpec((rows, d), lambda j: (0, j)),
        out_shape=jax.ShapeDtypeStruct((rows, n), F32),
        compiler_params=pltpu.CompilerParams(vmem_limit_bytes=VMEM_LIMIT),
        name="mod",
    )(c_pad, w_ada, b_ada)


def _inproj_kernel(x_ref, mod_ref, wq_ref, wk_ref, wv_ref, wf_ref, wx_ref, wg_ref, bf_ref,
                   q_ref, k_ref, v_ref, xb_ref, gb_ref, carry_ref):
    j = pl.program_id(1)

    @pl.when(j == 0)
    def _():
        carry_ref[...] = jnp.zeros_like(carry_ref)

    tm = x_ref.shape[1]
    u = _layer_norm(x_ref[0]) * (1.0 + mod_ref[0, 1:2, :]) + mod_ref[0, 0:1, :]
    ub = u.astype(BF16)

    logf = _log_sigmoid(_dot(ub, wf_ref[...]) + bf_ref[...])
    row = lax.broadcasted_iota(jnp.int32, (tm, tm), 0)
    col = lax.broadcasted_iota(jnp.int32, (tm, tm), 1)
    tril = (col <= row).astype(BF16)
    l_hi, l_mid, l_lo = _split3(logf)
    cum = (_dot(tril, l_hi) + (_dot(tril, l_mid) + _dot(tril, l_lo))) + carry_ref[...]
    carry_ref[...] = cum[tm - 1:tm, :]
    c_hi, c_mid, c_lo = [p.astype(F32) for p in _split3(cum)]

    lane = lax.broadcasted_iota(jnp.int32, (tm, HEAD_PAD), 1)
    d = HEAD_DIM
    for hp in range(N_HEADS // 2):
        sl = slice(hp * 2 * HEAD_PAD, (hp + 1) * 2 * HEAD_PAD)
        q2 = _dot(ub, wq_ref[:, sl])
        k2 = _dot(ub, wk_ref[:, sl])
        v2 = _dot(ub, wv_ref[:, sl])
        for hh in range(2):
            h = hp * 2 + hh
            hs = slice(hh * HEAD_PAD, (hh + 1) * HEAD_PAD)
            hi, mid, lo = c_hi[:, h:h + 1], c_mid[:, h:h + 1], c_lo[:, h:h + 1]
            q_ext = jnp.where(lane == d, hi, jnp.where(lane == d + 1, mid, jnp.where(
                lane == d + 2, lo, jnp.where(lane < d + 6, 1.0, 0.0))))
            k_ext = jnp.where(lane < d + 3, 1.0, jnp.where(lane == d + 3, -hi, jnp.where(
                lane == d + 4, -mid, jnp.where(lane == d + 5, -lo, 0.0))))
            q_ref[0, h] = jnp.where(lane < d, q2[:, hs], q_ext).astype(BF16)
            k_ref[0, h] = jnp.where(lane < d, k2[:, hs], k_ext).astype(BF16)
            v_ref[0, h] = jnp.where(lane == d, 1.0, v2[:, hs]).astype(BF16)

    xb_ref[0] = _dot(ub, wx_ref[...])
    gb_ref[0] = _dot(ub, wg_ref[...])


def _inproj_call(x, mod3, wq, wk, wv, wf, wx, wg, bf_pad):
    b, s, d = x.shape
    hw = N_HEADS * HEAD_PAD
    d_lru = wx.shape[1]
    full = lambda shape: pl.BlockSpec(shape, lambda bi, j: (0,) * len(shape))
    head_spec = pl.BlockSpec((1, N_HEADS, TM, HEAD_PAD), lambda bi, j: (bi, 0, j, 0))
    row_spec = pl.BlockSpec((1, TM, d_lru), lambda bi, j: (bi, j, 0))
    head_shape = jax.ShapeDtypeStruct((b, N_HEADS, s, HEAD_PAD), BF16)
    return pl.pallas_call(
        _inproj_kernel,
        grid=(b, s // TM),
        in_specs=[pl.BlockSpec((1, TM, d), lambda bi, j: (bi, j, 0)),
                  pl.BlockSpec((1, 6, d), lambda bi, j: (bi, 0, 0)),
                  full((d, hw)), full((d, hw)), full((d, hw)), full((d, LANES)),
                  full((d, d_lru)), full((d, d_lru)), full((1, LANES))],
        out_specs=[head_spec, head_spec, head_spec, row_spec, row_spec],
        out_shape=[head_shape, head_shape, head_shape,
                   jax.ShapeDtypeStruct((b, s, d_lru), F32),
                   jax.ShapeDtypeStruct((b, s, d_lru), F32)],
        scratch_shapes=[pltpu.VMEM((1, LANES), F32)],
        compiler_params=pltpu.CompilerParams(
            dimension_semantics=("arbitrary", "arbitrary"), vmem_limit_bytes=VMEM_LIMIT),
        name="inproj",
    )(x, mod3, wq, wk, wv, wf, wx, wg, bf_pad)


def _attn_kernel(q_ref, k_ref, v_ref, o_ref, s_scr, m_scr, acc_scr):
    i = pl.program_id(2)
    tq = q_ref.shape[2]
    heads = range(q_ref.shape[1])

    def kv_block(ref, hh, j):
        return ref[0, hh, pl.ds(pl.multiple_of(j * tq, tq), tq), :]

    def scores_to(slot, j):
        for hh in heads:
            s_scr[slot, hh] = lax.dot_general(
                q_ref[0, hh], kv_block(k_ref, hh, j), (((1,), (1,)), ((), ())),
                preferred_element_type=F32)

    def consume(slot, j, masked):
        for hh in heads:
            s = s_scr[slot, hh]
            if masked:
                row = lax.broadcasted_iota(jnp.int32, (tq, tq), 0)
                col = lax.broadcasted_iota(jnp.int32, (tq, tq), 1)
                s = jnp.where(col <= row, s, NEG_INF)
            m = m_scr[hh]
            m_new = jnp.maximum(m, jnp.max(s, axis=1, keepdims=True))
            p = jnp.exp(s - m_new).astype(BF16)
            acc_scr[hh] = jnp.exp(m - m_new) * acc_scr[hh] + _dot(p, kv_block(v_ref, hh, j))
            m_scr[hh] = m_new

    m_scr[...] = jnp.full(m_scr.shape, NEG_INF, F32)
    acc_scr[...] = jnp.zeros(acc_scr.shape, F32)
    scores_to(0, 0)

    def pair(jj, c):
        j = 2 * jj
        scores_to(1, j + 1)
        consume(0, j, False)
        scores_to(0, j + 2)
        consume(1, j + 1, False)
        return c

    lax.fori_loop(0, i // 2, pair, 0)

    @pl.when(i % 2 == 1)
    def _():
        scores_to(1, i)
        consume(0, i - 1, False)
        consume(1, i, True)

    @pl.when(i % 2 == 0)
    def _():
        consume(0, i, True)

    o_ref[0] = jnp.concatenate(
        [acc_scr[hh][:, :HEAD_DIM] / acc_scr[hh][:, HEAD_DIM:HEAD_DIM + 1] for hh in heads], axis=1)


def _attn_call(q_aug, k_aug, v_aug):
    b, h, s, hp = q_aug.shape
    kv_spec = pl.BlockSpec((1, 2, s, hp), lambda bi, p, i: (bi, p, 0, 0))
    return pl.pallas_call(
        _attn_kernel,
        grid=(b, h // 2, s // TQ),
        in_specs=[pl.BlockSpec((1, 2, TQ, hp), lambda bi, p, i: (bi, p, i, 0)), kv_spec, kv_spec],
        out_specs=pl.BlockSpec((1, TQ, 2 * HEAD_DIM), lambda bi, p, i: (bi, i, p)),
        out_shape=jax.ShapeDtypeStruct((b, s, h * HEAD_DIM), F32),
        scratch_shapes=[pltpu.VMEM((2, 2, TQ, TQ), F32), pltpu.VMEM((2, TQ, 1), F32),
                        pltpu.VMEM((2, TQ, hp), F32)],
        compiler_params=pltpu.CompilerParams(
            dimension_semantics=("arbitrary", "arbitrary", "arbitrary"),
            vmem_limit_bytes=VMEM_LIMIT),
        name="attention",
    )(q_aug, k_aug, v_aug)


def _gelu_tanh(x):
    return 0.5 * x * (1.0 + jnp.tanh(0.7978845608028654 * (x + 0.044715 * (x * x * x))))


def _lru_kernel(xb_ref, gb_ref, cw_ref, cb_ref, wgate_ref, brg_ref, big_ref, lam_ref,
                o_ref, tail_ref, h_ref):
    j = pl.program_id(1)

    @pl.when(j == 0)
    def _():
        tail_ref[...] = jnp.zeros_like(tail_ref)
        h_ref[...] = jnp.zeros_like(h_ref)

    x = xb_ref[0]
    tm, dl = x.shape
    tail = tail_ref[...]
    row8 = lax.broadcasted_iota(jnp.int32, (SUBLANES, dl), 0)
    xc = x * cw_ref[CONV_WIDTH - 1:CONV_WIDTH, :] + cb_ref[...]
    for k in range(1, CONV_WIDTH):
        xr = pltpu.roll(x, k, 0)
        top = jnp.where(row8 < k, pltpu.roll(tail, k, 0), xr[:SUBLANES])
        xs = jnp.concatenate([top, xr[SUBLANES:]], axis=0)
        xc = xc + xs * cw_ref[CONV_WIDTH - 1 - k:CONV_WIDTH - k, :]
    tail_ref[...] = x[tm - SUBLANES:, :]

    xcb = xc.astype(BF16)
    n_pairs = dl // LANES
    r_parts, i_parts = [], []
    for p in range(n_pairs):
        g = _dot(xcb[:, p * LANES:(p + 1) * LANES], wgate_ref[p])
        r_parts.append(g[:, :LANES])
        i_parts.append(g[:, LANES:])
    r = _sigmoid(jnp.concatenate(r_parts, axis=1) + brg_ref[...])
    ig = _sigmoid(jnp.concatenate(i_parts, axis=1) + big_ref[...])

    lam = lam_ref[...]
    softplus_neg_lam = jnp.maximum(-lam, 0.0) + jnp.log1p(jnp.exp(-jnp.abs(lam)))
    log_a = (-LRU_C) * r * softplus_neg_lam
    a = jnp.exp(log_a)
    u = jnp.sqrt(-jnp.tanh(log_a) * (a * a + 1.0)) * (ig * xc)

    k = 1
    while k < tm:
        u = a * _shift_rows(u, k, 0.0) + u
        a = a * _shift_rows(a, k, 1.0)
        k *= 2
    h = a * h_ref[...] + u
    h_ref[...] = h[tm - 1:tm, :]
    o_ref[0] = h * _gelu_tanh(gb_ref[0])


def _lru_call(xb, gb, conv_w, conv_b, wgate, b_rg, b_ig, lam):
    b, s, dl = xb.shape
    row_spec = pl.BlockSpec((1, TM, dl), lambda bi, j: (bi, j, 0))
    full = lambda shape: pl.BlockSpec(shape, lambda bi, j: (0,) * len(shape))
    return pl.pallas_call(
        _lru_kernel,
        grid=(b, s // TM),
        in_specs=[row_spec, row_spec, full(conv_w.shape), full((1, dl)), full(wgate.shape),
                  full((1, dl)), full((1, dl)), full((1, dl))],
        out_specs=row_spec,
        out_shape=jax.ShapeDtypeStruct((b, s, dl), F32),
        scratch_shapes=[pltpu.VMEM((SUBLANES, dl), F32), pltpu.VMEM((1, dl), F32)],
        compiler_params=pltpu.CompilerParams(
            dimension_semantics=("arbitrary", "arbitrary"), vmem_limit_bytes=VMEM_LIMIT),
        name="lru",
    )(xb, gb, conv_w, conv_b, wgate, b_rg, b_ig, lam)


def _rms(x, gain):
    return x * lax.rsqrt(jnp.mean(x * x, axis=-1, keepdims=True) + RMS_EPS) * gain


def _mixout_kernel(attn_ref, lru_ref, x_ref, mod_ref, ga_ref, gl_ref, woa_ref, wol_ref,
                   g1_ref, b1_ref, wr_ref, br_ref, x1_ref, u2_ref, rinfo_ref):
    tm, d = x_ref.shape
    na = _rms(attn_ref[...], ga_ref[...]).astype(BF16)
    nl = _rms(lru_ref[...], gl_ref[...]).astype(BF16)
    mix = _dot(na, woa_ref[...]) + _dot(nl, wol_ref[...])
    z = DEEPNORM_ALPHA * x_ref[...] + (1.0 + mod_ref[0, 2:3, :]) * mix
    x1 = _layer_norm(z) * g1_ref[...] + b1_ref[...]
    x1_ref[...] = x1
    u2 = _layer_norm(x1) * (1.0 + mod_ref[0, 4:5, :]) + mod_ref[0, 3:4, :]

    nt = (((1,), (1,)), ((), ()))
    uh, ul = _split2(u2)
    wh, wl = _split2(wr_ref[...])
    dg = lambda a, b_: lax.dot_general(a, b_, nt, preferred_element_type=F32)
    lg = dg(wh, uh) + (dg(wh, ul) + dg(wl, uh)) + br_ref[:, 0:1]

    def first_index(vals, target):
        idx = jnp.full_like(target, float(len(vals) - 1))
        for n in range(len(vals) - 2, -1, -1):
            idx = jnp.where(vals[n] == target, float(n), idx)
        return idx

    g = [lg[n:n + 1, :] for n in range(N_GROUPS)]
    gmax = functools.reduce(jnp.maximum, g)
    gsum = functools.reduce(lambda a, b_: a + b_, [jnp.exp(v - gmax) for v in g])
    grp_w = 1.0 / gsum
    gidx = first_index(g, gmax)

    sel = []
    for e in range(EXPERTS_PER_GROUP):
        v = lg[N_GROUPS + (N_GROUPS - 1) * EXPERTS_PER_GROUP + e:
               N_GROUPS + (N_GROUPS - 1) * EXPERTS_PER_GROUP + e + 1, :]
        for gi in range(N_GROUPS - 2, -1, -1):
            r0 = N_GROUPS + gi * EXPERTS_PER_GROUP + e
            v = jnp.where(gidx == float(gi), lg[r0:r0 + 1, :], v)
        sel.append(v)
    smax = functools.reduce(jnp.maximum, sel)
    i1 = first_index(sel, smax)
    rest = [jnp.where(i1 == float(e), -3e38, sel[e]) for e in range(EXPERTS_PER_GROUP)]
    rmax = functools.reduce(jnp.maximum, rest)
    i2 = first_index(rest, rmax)
    e2 = jnp.exp(rmax - smax)
    w1 = grp_w / (1.0 + e2)
    w2 = grp_w * e2 / (1.0 + e2)
    ia = jnp.minimum(i1, i2)
    ib = jnp.maximum(i1, i2)
    wa = jnp.where(i1 < i2, w1, w2)
    wb = jnp.where(i1 < i2, w2, w1)
    pair = jnp.where(ia == 0.0, ib - 1.0, jnp.where(ia == 1.0, ib + 1.0, 5.0))
    bucket = gidx * float(N_PAIRS) + pair

    zrow = jnp.zeros_like(wa)
    rinfo_ref[...] = jnp.concatenate([bucket, wa, wb] + [zrow] * (SUBLANES - 3), axis=0)
    wt = jnp.concatenate([wa, wb, jnp.zeros((LANES - 2, tm), F32)], axis=0)
    u2_ref[:, :d] = u2
    u2_ref[:, d:] = wt.T


def _mixout_call(attn, lru, x2d, mod3, ga, gl, woa, wol, g1, b1, wr, br, seq):
    t, d = x2d.shape
    dh = attn.shape[1]
    per_b = seq // TM
    full = lambda shape: pl.BlockSpec(shape, lambda i: (0,) * len(shape))
    return pl.pallas_call(
        _mixout_kernel,
        grid=(t // TM,),
        in_specs=[pl.BlockSpec((TM, dh), lambda i: (i, 0)),
                  pl.BlockSpec((TM, dh), lambda i: (i, 0)),
                  pl.BlockSpec((TM, d), lambda i: (i, 0)),
                  pl.BlockSpec((1, 6, d), lambda i: (i // per_b, 0, 0)),
                  full((1, dh)), full((1, dh)), full((dh, d)), full((dh, d)),
                  full((1, d)), full((1, d)), full((BUCKET_ROWS, d)), full((BUCKET_ROWS, LANES))],
        out_specs=[pl.BlockSpec((TM, d), lambda i: (i, 0)),
                   pl.BlockSpec((TM, d + LANES), lambda i: (i, 0)),
                   pl.BlockSpec((SUBLANES, TM), lambda i: (0, i))],
        out_shape=[jax.ShapeDtypeStruct((t, d), F32),
                   jax.ShapeDtypeStruct((t, d + LANES), F32),
                   jax.ShapeDtypeStruct((SUBLANES, t), F32)],
        compiler_params=pltpu.CompilerParams(
            dimension_semantics=("arbitrary",), vmem_limit_bytes=VMEM_LIMIT),
        name="mixout",
    )(attn, lru, x2d, mod3, ga, gl, woa, wol, g1, b1, wr, br)


def _rank_kernel(rinfo_ref, dest_ref, counts_ref, carry_ref, offs_ref):
    phase = pl.program_id(0)
    t = pl.program_id(1)
    tm = rinfo_ref.shape[1]

    @pl.when(t == 0)
    def _():
        carry_ref[...] = jnp.zeros_like(carry_ref)

    bucket = rinfo_ref[0:1, :]
    cid = lax.broadcasted_iota(jnp.int32, (BUCKET_ROWS, tm), 0).astype(F32)
    onehot = jnp.where(cid == bucket, 1.0, 0.0)
    srow = lax.broadcasted_iota(jnp.int32, (tm, tm), 0)
    scol = lax.broadcasted_iota(jnp.int32, (tm, tm), 1)
    upper = (srow <= scol).astype(BF16)
    prefix = _dot(onehot.astype(BF16), upper)
    carry = carry_ref[...]

    @pl.when(phase == 1)
    def _():
        rank = jnp.sum(onehot * (prefix - 1.0 + carry[:, 0:1] + offs_ref[:, 0:1]),
                       axis=0, keepdims=True)
        dest_ref[...] = rank.astype(jnp.int32)

    new_carry = carry + prefix[:, tm - 1:tm]
    carry_ref[...] = new_carry

    @pl.when(jnp.logical_and(phase == 0, t == pl.num_programs(1) - 1))
    def _():
        padded = jnp.floor((new_carry + float(TM_E - 1)) * (1.0 / TM_E)) * float(TM_E)
        inc = padded
        k = 1
        while k < BUCKET_ROWS:
            inc = inc + _shift_rows(inc, k, 0.0)
            k *= 2
        offs_ref[...] = inc - padded
        counts_ref[...] = new_carry


def _rank_call(rinfo):
    t = rinfo.shape[1]
    return pl.pallas_call(
        _rank_kernel,
        grid=(2, t // TM),
        in_specs=[pl.BlockSpec((SUBLANES, TM), lambda p, i: (0, i))],
        out_specs=[pl.BlockSpec((1, TM), lambda p, i: (0, i * p)),
                   pl.BlockSpec((BUCKET_ROWS, LANES), lambda p, i: (0, 0))],
        out_shape=[jax.ShapeDtypeStruct((1, t), jnp.int32),
                   jax.ShapeDtypeStruct((BUCKET_ROWS, LANES), F32)],
        scratch_shapes=[pltpu.VMEM((BUCKET_ROWS, LANES), F32),
                        pltpu.VMEM((BUCKET_ROWS, LANES), F32)],
        compiler_params=pltpu.CompilerParams(
            dimension_semantics=("arbitrary", "arbitrary"), vmem_limit_bytes=VMEM_LIMIT),
        name="rank",
    )(rinfo)


def _dispatch_kernel(dest_ref, u2_ref, xs_in_ref, xs_ref, sem):
    del xs_in_ref
    tm = u2_ref.shape[0]
    t0 = pl.program_id(0) * tm

    def issue(r, c):
        pltpu.make_async_copy(u2_ref.at[pl.ds(r, 1)],
                              xs_ref.at[pl.ds(dest_ref[t0 + r], 1)], sem).start()
        return c

    lax.fori_loop(0, tm, issue, 0, unroll=ISSUE_UNROLL)
    pltpu.make_async_copy(u2_ref, xs_ref.at[pl.ds(0, tm)], sem).wait()


def _dispatch_call(dest, u2ext, xs_init):
    t, w = u2ext.shape
    grid_spec = pltpu.PrefetchScalarGridSpec(
        num_scalar_prefetch=1,
        grid=(t // TM,),
        in_specs=[pl.BlockSpec((TM, w), lambda i, dest_ref: (i, 0)),
                  pl.BlockSpec(memory_space=pl.ANY)],
        out_specs=pl.BlockSpec(memory_space=pl.ANY),
        scratch_shapes=[pltpu.SemaphoreType.DMA(())],
    )
    return pl.pallas_call(
        _dispatch_kernel,
        grid_spec=grid_spec,
        out_shape=jax.ShapeDtypeStruct(xs_init.shape, F32),
        input_output_aliases={2: 0},
        compiler_params=pltpu.CompilerParams(
            dimension_semantics=("arbitrary",), vmem_limit_bytes=VMEM_LIMIT),
        name="dispatch",
    )(dest, u2ext, xs_init)


def _experts_kernel(ea_ref, eb_ref, nv_ref, xs_ref, wga_ref, wua_ref, wda_ref,
                    wgb_ref, wub_ref, wdb_ref, ys_ref):
    del ea_ref, eb_ref
    i = pl.program_id(0)
    d = ys_ref.shape[1]

    @pl.when(i < nv_ref[0])
    def _():
        x = xs_ref[:, :d].astype(BF16)

        def expert(wg_ref, wu_ref, wd_ref):
            g = _dot(x, wg_ref[0])
            h = (g * _sigmoid(g)) * _dot(x, wu_ref[0])
            return _dot(h.astype(BF16), wd_ref[0])

        ya = xs_ref[:, d:d + 1] * expert(wga_ref, wua_ref, wda_ref)
        ys_ref[...] = ya + xs_ref[:, d + 1:d + 2] * expert(wgb_ref, wub_ref, wdb_ref)

    @pl.when(i >= nv_ref[0])
    def _():
        ys_ref[...] = jnp.zeros_like(ys_ref)


def _experts_call(tile_ea, tile_eb, n_valid, xs, wg, wu, wd):
    tp, w = xs.shape
    _, d, de = wg.shape
    row = lambda i, ea, eb, nv: (jnp.minimum(i, nv[0] - 1), 0)
    wa = lambda i, ea, eb, nv: (ea[i], 0, 0)
    wb = lambda i, ea, eb, nv: (eb[i], 0, 0)
    grid_spec = pltpu.PrefetchScalarGridSpec(
        num_scalar_prefetch=3,
        grid=(tp // TM_E,),
        in_specs=[pl.BlockSpec((TM_E, w), row),
                  pl.BlockSpec((1, d, de), wa), pl.BlockSpec((1, d, de), wa),
                  pl.BlockSpec((1, de, d), wa),
                  pl.BlockSpec((1, d, de), wb), pl.BlockSpec((1, d, de), wb),
                  pl.BlockSpec((1, de, d), wb)],
        out_specs=pl.BlockSpec((TM_E, d), lambda i, ea, eb, nv: (i, 0)),
    )
    return pl.pallas_call(
        _experts_kernel,
        grid_spec=grid_spec,
        out_shape=jax.ShapeDtypeStruct((tp, d), F32),
        compiler_params=pltpu.CompilerParams(
            dimension_semantics=("arbitrary",), vmem_limit_bytes=VMEM_LIMIT),
        name="experts",
    )(tile_ea, tile_eb, n_valid, xs, wg, wu, wd, wg, wu, wd)


def _final_kernel(dest_ref, x1_ref, mod_ref, g2_ref, b2_ref, ys_ref, o_ref, ybuf, sem):
    tm = x1_ref.shape[0]
    t0 = pl.program_id(0) * tm

    def issue(r, c):
        pltpu.make_async_copy(ys_ref.at[pl.ds(dest_ref[t0 + r], 1)],
                              ybuf.at[pl.ds(r, 1)], sem).start()
        return c

    lax.fori_loop(0, tm, issue, 0, unroll=ISSUE_UNROLL)
    pltpu.make_async_copy(ys_ref.at[pl.ds(0, tm)], ybuf, sem).wait()
    z = DEEPNORM_ALPHA * x1_ref[...] + (1.0 + mod_ref[0, 5:6, :]) * ybuf[...]
    o_ref[...] = _layer_norm(z) * g2_ref[...] + b2_ref[...]


def _final_call(dest, x1, mod3, g2, b2, ys, seq):
    t, d = x1.shape
    per_b = seq // TM
    grid_spec = pltpu.PrefetchScalarGridSpec(
        num_scalar_prefetch=1,
        grid=(t // TM,),
        in_specs=[pl.BlockSpec((TM, d), lambda i, dr: (i, 0)),
                  pl.BlockSpec((1, 6, d), lambda i, dr: (i // per_b, 0, 0)),
                  pl.BlockSpec((1, d), lambda i, dr: (0, 0)),
                  pl.BlockSpec((1, d), lambda i, dr: (0, 0)),
                  pl.BlockSpec(memory_space=pl.ANY)],
        out_specs=pl.BlockSpec((TM, d), lambda i, dr: (i, 0)),
        scratch_shapes=[pltpu.VMEM((TM, d), F32), pltpu.SemaphoreType.DMA(())],
    )
    return pl.pallas_call(
        _final_kernel,
        grid_spec=grid_spec,
        out_shape=jax.ShapeDtypeStruct((t, d), F32),
        compiler_params=pltpu.CompilerParams(
            dimension_semantics=("arbitrary",), vmem_limit_bytes=VMEM_LIMIT),
        name="final",
    )(dest, x1, mod3, g2, b2, ys)


def _pad_heads(w, scale=1.0):
    d = w.shape[0]
    w = (w * scale).reshape(d, N_HEADS, HEAD_DIM)
    w = jnp.pad(w, ((0, 0), (0, 0), (0, HEAD_PAD - HEAD_DIM)))
    return w.reshape(d, N_HEADS * HEAD_PAD).astype(BF16)


def _gate_pairs(w_rg, w_ig):
    def pairs(w):
        n, bs, _ = w.shape
        w = w.reshape(n // 2, 2, bs, bs)
        z = jnp.zeros((n // 2, bs, bs), w.dtype)
        top = jnp.concatenate([w[:, 0], z], axis=2)
        bot = jnp.concatenate([z, w[:, 1]], axis=2)
        return jnp.concatenate([top, bot], axis=1)
    return jnp.concatenate([pairs(w_rg), pairs(w_ig)], axis=2).astype(BF16)


def kernel(x, c, w_ada, b_ada, w_in, b_f, conv_w, conv_b, w_rg, b_rg, w_ig, b_ig, lru_lambda,
           g_attn, g_lru, w_out, ln1_g, ln1_b, w_grp, b_grp, w_exp, b_exp,
           w_e_gate, w_e_up, w_e_down, ln2_g, ln2_b):
    assert w_ada.shape[0] == DEPTH
    b, s, d = x.shape
    t = b * s
    d_attn = N_HEADS * HEAD_DIM
    d_lru = conv_w.shape[2]
    n_exp = w_e_gate.shape[1]
    assert s % TM == 0 and s % TQ == 0 and t % TM_E == 0

    c_pad = jnp.pad(c, ((0, SUBLANES - b), (0, 0)))
    mod = _mod_call(c_pad, w_ada[0], b_ada[0][None, :])
    mod3 = mod[:b].reshape(b, 6, d)

    wi = w_in[0]
    o = 3 * d_attn
    wq = _pad_heads(wi[:, :d_attn], HEAD_DIM ** -0.5)
    wk = _pad_heads(wi[:, d_attn:2 * d_attn])
    wv = _pad_heads(wi[:, 2 * d_attn:o])
    wf = jnp.pad(wi[:, o:o + N_HEADS], ((0, 0), (0, LANES - N_HEADS))).astype(BF16)
    wx = wi[:, o + N_HEADS:o + N_HEADS + d_lru].astype(BF16)
    wg = wi[:, o + N_HEADS + d_lru:].astype(BF16)
    bf_pad = jnp.pad(b_f[0], (0, LANES - N_HEADS))[None, :]
    q_aug, k_aug, v_aug, xb, gb = _inproj_call(x, mod3, wq, wk, wv, wf, wx, wg, bf_pad)
    attn = _attn_call(q_aug, k_aug, v_aug)
    lru = _lru_call(xb, gb, conv_w[0], conv_b[0][None, :], _gate_pairs(w_rg[0], w_ig[0]),
                    b_rg[0][None, :], b_ig[0][None, :], lru_lambda[0][None, :])

    wo = w_out[0].astype(BF16)
    n_route = N_GROUPS + n_exp
    wr = jnp.pad(jnp.concatenate([w_grp[0], w_exp[0]], axis=1).T,
                 ((0, BUCKET_ROWS - n_route), (0, 0)))
    br = jnp.pad(jnp.concatenate([b_grp[0], b_exp[0]]), (0, BUCKET_ROWS - n_route))
    br = jnp.broadcast_to(br[:, None], (BUCKET_ROWS, LANES))
    x1, u2ext, rinfo = _mixout_call(
        attn.reshape(t, d_attn), lru.reshape(t, d_lru), x.reshape(t, d), mod3,
        g_attn[0][None, :], g_lru[0][None, :], wo[:d_attn], wo[d_attn:],
        ln1_g[0][None, :], ln1_b[0][None, :], wr, br, s)

    dest2d, counts = _rank_call(rinfo)
    dest = dest2d.reshape(t)
    cnt = counts[:N_BUCKETS, 0].astype(jnp.int32)
    ends = jnp.cumsum((cnt + (TM_E - 1)) // TM_E)
    n_tiles = t // TM_E + N_BUCKETS
    tile_bucket = jnp.sum(ends[None, :] <= jnp.arange(n_tiles)[:, None], axis=1)
    tile_bucket = jnp.minimum(tile_bucket, N_BUCKETS - 1)
    n_valid = ends[N_BUCKETS - 1:]
    last_bucket = tile_bucket[jnp.maximum(n_valid[0] - 1, 0)]
    tile_bucket = jnp.where(jnp.arange(n_tiles) < n_valid[0], tile_bucket, last_bucket)
    tile_ea = jnp.asarray(_BUCKET_EA, jnp.int32)[tile_bucket]
    tile_eb = jnp.asarray(_BUCKET_EB, jnp.int32)[tile_bucket]

    xs = _dispatch_call(dest, u2ext, jnp.zeros((n_tiles * TM_E, d + LANES), F32))
    ys = _experts_call(tile_ea, tile_eb, n_valid.astype(jnp.int32), xs,
                       w_e_gate[0].astype(BF16), w_e_up[0].astype(BF16),
                       w_e_down[0].astype(BF16))
    out = _final_call(dest, x1, mod3, ln2_g[0][None, :], ln2_b[0][None, :], ys, s)
    return out.reshape(b, s, d)
```

```python
import functools

import jax
import jax.numpy as jnp
from jax import lax
from jax.experimental import pallas as pl
from jax.experimental.pallas import tpu as pltpu

F32 = jnp.float32
BF16 = jnp.bfloat16

HEAD_DIM = 64
N_HEADS = 8
N_LRU_BLOCKS = 8
CONV_WIDTH = 4
LRU_C = 8.0
N_GROUPS = 4
EXPERTS_PER_GROUP = 4
N_PAIRS = 6
N_BUCKETS = N_GROUPS * N_PAIRS
LN_EPS = 1e-5
RMS_EPS = 1e-6
NEG_INF = -1e30
DEPTH = 1
DEEPNORM_ALPHA = (2.0 * DEPTH) ** 0.25

LANES = 128
SUBLANES = 8
HEAD_PAD = LANES
VT_ROWS = 80
BUCKET_ROWS = 32
TM = 512
TQ = 512
TM_E = 256
ISSUE_UNROLL = 8
VMEM_LIMIT = 56 * 1024 * 1024

_PAIRS = [(0, 1), (0, 2), (0, 3), (1, 2), (1, 3), (2, 3)]
_BUCKET_EA = [g * EXPERTS_PER_GROUP + a for g in range(N_GROUPS) for (a, b) in _PAIRS]
_BUCKET_EB = [g * EXPERTS_PER_GROUP + b for g in range(N_GROUPS) for (a, b) in _PAIRS]


def _dot(a, b):
    return jnp.dot(a, b, preferred_element_type=F32)


def _split2(a):
    hi = a.astype(BF16)
    lo = (a - hi.astype(F32)).astype(BF16)
    return hi, lo


def _split3(a):
    hi = a.astype(BF16)
    r = a - hi.astype(F32)
    mid = r.astype(BF16)
    lo = (r - mid.astype(F32)).astype(BF16)
    return hi, mid, lo


def _layer_norm(x):
    mu = jnp.mean(x, axis=-1, keepdims=True)
    xc = x - mu
    var = jnp.mean(xc * xc, axis=-1, keepdims=True)
    return xc * lax.rsqrt(var + LN_EPS)


def _sigmoid(x):
    return 1.0 / (1.0 + jnp.exp(-x))


def _log_sigmoid(z):
    return jnp.minimum(z, 0.0) - jnp.log1p(jnp.exp(-jnp.abs(z)))


def _shift_rows(x, k, fill):
    n = x.shape[0]
    if k % SUBLANES == 0:
        return jnp.concatenate([jnp.full((k, x.shape[1]), fill, x.dtype), x[:n - k]], axis=0)
    row = lax.broadcasted_iota(jnp.int32, x.shape, 0)
    return jnp.where(row >= k, pltpu.roll(x, k, 0), fill)


def _mod_kernel(c_ref, w_ref, b_ref, o_ref):
    c = c_ref[...]
    s = c * _sigmoid(c)
    sh, sl = _split2(s)
    wh, wl = _split2(w_ref[...])
    o_ref[...] = _dot(sh, wh) + (_dot(sh, wl) + _dot(sl, wh)) + b_ref[...]


def _mod_call(c_pad, w_ada, b_ada):
    rows, d = c_pad.shape
    n = w_ada.shape[1]
    return pl.pallas_call(
        _mod_kernel,
        grid=(n // d,),
        in_specs=[pl.BlockSpec((rows, d), lambda j: (0, 0)),
                  pl.BlockSpec((d, d), lambda j: (0, j)),
                  pl.BlockSpec((1, d), lambda j: (0, j))],
        out_specs=pl.BlockSpec((rows, d), lambda j: (0, j)),
        out_shape=jax.ShapeDtypeStruct((rows, n), F32),
        compiler_params=pltpu.CompilerParams(vmem_limit_bytes=VMEM_LIMIT),
        name="mod",
    )(c_pad, w_ada, b_ada)


def _inproj_kernel(x_ref, mod_ref, wq_ref, wk_ref, wvt_ref, wf_ref, wx_ref, wg_ref, bf_ref,
                   q_ref, k_ref, vt_ref, xb_ref, gb_ref, carry_ref):
    j = pl.program_id(1)

    @pl.when(j == 0)
    def _():
        carry_ref[...] = jnp.zeros_like(carry_ref)

    tm = x_ref.shape[1]
    u = _layer_norm(x_ref[0]) * (1.0 + mod_ref[0, 1:2, :]) + mod_ref[0, 0:1, :]
    ub = u.astype(BF16)

    logf = _log_sigmoid(_dot(ub, wf_ref[...]) + bf_ref[...])
    row = lax.broadcasted_iota(jnp.int32, (tm, tm), 0)
    col = lax.broadcasted_iota(jnp.int32, (tm, tm), 1)
    tril = (col <= row).astype(BF16)
    l_hi, l_mid, l_lo = _split3(logf)
    cum = (_dot(tril, l_hi) + (_dot(tril, l_mid) + _dot(tril, l_lo))) + carry_ref[...]
    carry_ref[...] = cum[tm - 1:tm, :]
    c_hi, c_mid, c_lo = [p.astype(F32) for p in _split3(cum)]

    lane = lax.broadcasted_iota(jnp.int32, (tm, HEAD_PAD), 1)
    d = HEAD_DIM
    q_all = _dot(ub, wq_ref[...])
    k_all = _dot(ub, wk_ref[...])
    for h in range(N_HEADS):
        ts = slice((h // 2) * LANES, (h // 2 + 1) * LANES)
        qh, kh = q_all[:, ts], k_all[:, ts]
        if h % 2:
            qh, kh = pltpu.roll(qh, d, 1), pltpu.roll(kh, d, 1)
        hi, mid, lo = c_hi[:, h:h + 1], c_mid[:, h:h + 1], c_lo[:, h:h + 1]
        q_ext = jnp.where(lane == d, hi, jnp.where(lane == d + 1, mid, jnp.where(
            lane == d + 2, lo, jnp.where(lane < d + 6, 1.0, 0.0))))
        k_ext = jnp.where(lane < d + 3, 1.0, jnp.where(lane == d + 3, -hi, jnp.where(
            lane == d + 4, -mid, jnp.where(lane == d + 5, -lo, 0.0))))
        q_ref[0, h] = jnp.where(lane < d, qh, q_ext).astype(BF16)
        k_ref[0, h] = jnp.where(lane < d, kh, k_ext).astype(BF16)

    vt_all = lax.dot_general(wvt_ref[...], ub, (((1,), (1,)), ((), ())),
                             preferred_element_type=F32)
    pad_rows = lax.broadcasted_iota(jnp.int32, (VT_ROWS - d, tm), 0)
    ones_row = jnp.where(pad_rows == 0, 1.0, 0.0).astype(BF16)
    for h in range(N_HEADS):
        vt_ref[0, h, 0, :d, :] = vt_all[h * d:(h + 1) * d, :].astype(BF16)
        vt_ref[0, h, 0, d:, :] = ones_row

    xb_ref[0] = _dot(ub, wx_ref[...])
    gb_ref[0] = _dot(ub, wg_ref[...])


def _inproj_call(x, mod3, wq, wk, wvt, wf, wx, wg, bf_pad):
    b, s, d = x.shape
    d_attn = wq.shape[1]
    d_lru = wx.shape[1]
    full = lambda shape: pl.BlockSpec(shape, lambda bi, j: (0,) * len(shape))
    head_spec = pl.BlockSpec((1, N_HEADS, TM, HEAD_PAD), lambda bi, j: (bi, 0, j, 0))
    vt_spec = pl.BlockSpec((1, N_HEADS, 1, VT_ROWS, TM), lambda bi, j: (bi, 0, j, 0, 0))
    row_spec = pl.BlockSpec((1, TM, d_lru), lambda bi, j: (bi, j, 0))
    head_shape = jax.ShapeDtypeStruct((b, N_HEADS, s, HEAD_PAD), BF16)
    return pl.pallas_call(
        _inproj_kernel,
        grid=(b, s // TM),
        in_specs=[pl.BlockSpec((1, TM, d), lambda bi, j: (bi, j, 0)),
                  pl.BlockSpec((1, 6, d), lambda bi, j: (bi, 0, 0)),
                  full((d, d_attn)), full((d, d_attn)), full((d_attn, d)), full((d, LANES)),
                  full((d, d_lru)), full((d, d_lru)), full((1, LANES))],
        out_specs=[head_spec, head_spec, vt_spec, row_spec, row_spec],
        out_shape=[head_shape, head_shape,
                   jax.ShapeDtypeStruct((b, N_HEADS, s // TM, VT_ROWS, TM), BF16),
                   jax.ShapeDtypeStruct((b, s, d_lru), F32),
                   jax.ShapeDtypeStruct((b, s, d_lru), F32)],
        scratch_shapes=[pltpu.VMEM((1, LANES), F32)],
        compiler_params=pltpu.CompilerParams(
            dimension_semantics=("arbitrary", "arbitrary"), vmem_limit_bytes=VMEM_LIMIT),
        name="inproj",
    )(x, mod3, wq, wk, wvt, wf, wx, wg, bf_pad)


def _attn_kernel(q_ref, k_ref, vt_ref, o_ref, s_scr, m_scr, acc_scr):
    i = pl.program_id(2)
    tq = q_ref.shape[2]
    heads = range(q_ref.shape[1])

    def scores_to(slot, j):
        for hh in heads:
            k = k_ref[0, hh, pl.ds(pl.multiple_of(j * tq, tq), tq), :]
            s_scr[slot, hh] = lax.dot_general(
                k, q_ref[0, hh], (((1,), (1,)), ((), ())), preferred_element_type=F32)

    def consume(slot, j, masked):
        for hh in heads:
            s = s_scr[slot, hh]
            if masked:
                key = lax.broadcasted_iota(jnp.int32, (tq, tq), 0)
                qry = lax.broadcasted_iota(jnp.int32, (tq, tq), 1)
                s = jnp.where(key <= qry, s, NEG_INF)
            m = m_scr[hh]
            m_new = jnp.maximum(m, jnp.max(s, axis=0, keepdims=True))
            p = jnp.exp(s - m_new).astype(BF16)
            acc_scr[hh] = jnp.exp(m - m_new) * acc_scr[hh] + _dot(vt_ref[0, hh, j], p)
            m_scr[hh] = m_new

    m_scr[...] = jnp.full(m_scr.shape, NEG_INF, F32)
    acc_scr[...] = jnp.zeros(acc_scr.shape, F32)
    scores_to(0, 0)

    def pair(jj, c):
        j = 2 * jj
        scores_to(1, j + 1)
        consume(0, j, False)
        scores_to(0, j + 2)
        consume(1, j + 1, False)
        return c

    lax.fori_loop(0, i // 2, pair, 0)

    @pl.when(i % 2 == 1)
    def _():
        scores_to(1, i)
        consume(0, i - 1, False)
        consume(1, i, True)

    @pl.when(i % 2 == 0)
    def _():
        consume(0, i, True)

    out_t = jnp.concatenate(
        [acc_scr[hh][:HEAD_DIM, :] / acc_scr[hh][HEAD_DIM:HEAD_DIM + 1, :] for hh in heads], axis=0)
    o_ref[0] = out_t.T


def _attn_call(q_aug, k_aug, vt_aug):
    b, h, s, hp = q_aug.shape
    assert TQ == TM
    return pl.pallas_call(
        _attn_kernel,
        grid=(b, h // 2, s // TQ),
        in_specs=[pl.BlockSpec((1, 2, TQ, hp), lambda bi, p, i: (bi, p, i, 0)),
                  pl.BlockSpec((1, 2, s, hp), lambda bi, p, i: (bi, p, 0, 0)),
                  pl.BlockSpec((1, 2, s // TQ, VT_ROWS, TQ), lambda bi, p, i: (bi, p, 0, 0, 0))],
        out_specs=pl.BlockSpec((1, TQ, 2 * HEAD_DIM), lambda bi, p, i: (bi, i, p)),
        out_shape=jax.ShapeDtypeStruct((b, s, h * HEAD_DIM), F32),
        scratch_shapes=[pltpu.VMEM((2, 2, TQ, TQ), F32), pltpu.VMEM((2, 1, TQ), F32),
                        pltpu.VMEM((2, VT_ROWS, TQ), F32)],
        compiler_params=pltpu.CompilerParams(
            dimension_semantics=("arbitrary", "arbitrary", "arbitrary"),
            vmem_limit_bytes=VMEM_LIMIT),
        name="attention",
    )(q_aug, k_aug, vt_aug)


def _gelu_tanh(x):
    return 0.5 * x * (1.0 + jnp.tanh(0.7978845608028654 * (x + 0.044715 * (x * x * x))))


def _lru_kernel(xb_ref, gb_ref, cw_ref, cb_ref, wgate_ref, brg_ref, big_ref, lam_ref,
                o_ref, tail_ref, h_ref):
    j = pl.program_id(1)

    @pl.when(j == 0)
    def _():
        tail_ref[...] = jnp.zeros_like(tail_ref)
        h_ref[...] = jnp.zeros_like(h_ref)

    x = xb_ref[0]
    tm, dl = x.shape
    tail = tail_ref[...]
    row8 = lax.broadcasted_iota(jnp.int32, (SUBLANES, dl), 0)
    xc = x * cw_ref[CONV_WIDTH - 1:CONV_WIDTH, :] + cb_ref[...]
    for k in range(1, CONV_WIDTH):
        xr = pltpu.roll(x, k, 0)
        top = jnp.where(row8 < k, pltpu.roll(tail, k, 0), xr[:SUBLANES])
        xs = jnp.concatenate([top, xr[SUBLANES:]], axis=0)
        xc = xc + xs * cw_ref[CONV_WIDTH - 1 - k:CONV_WIDTH - k, :]
    tail_ref[...] = x[tm - SUBLANES:, :]

    xcb = xc.astype(BF16)
    n_pairs = dl // LANES
    r_parts, i_parts = [], []
    for p in range(n_pairs):
        g = _dot(xcb[:, p * LANES:(p + 1) * LANES], wgate_ref[p])
        r_parts.append(g[:, :LANES])
        i_parts.append(g[:, LANES:])
    r = _sigmoid(jnp.concatenate(r_parts, axis=1) + brg_ref[...])
    ig = _sigmoid(jnp.concatenate(i_parts, axis=1) + big_ref[...])

    lam = lam_ref[...]
    softplus_neg_lam = jnp.maximum(-lam, 0.0) + jnp.log1p(jnp.exp(-jnp.abs(lam)))
    log_a = (-LRU_C) * r * softplus_neg_lam
    a = jnp.exp(log_a)
    u = jnp.sqrt(-jnp.tanh(log_a) * (a * a + 1.0)) * (ig * xc)

    k = 1
    while k < tm:
        u = a * _shift_rows(u, k, 0.0) + u
        a = a * _shift_rows(a, k, 1.0)
        k *= 2
    h = a * h_ref[...] + u
    h_ref[...] = h[tm - 1:tm, :]
    o_ref[0] = h * _gelu_tanh(gb_ref[0])


def _lru_call(xb, gb, conv_w, conv_b, wgate, b_rg, b_ig, lam):
    b, s, dl = xb.shape
    row_spec = pl.BlockSpec((1, TM, dl), lambda bi, j: (bi, j, 0))
    full = lambda shape: pl.BlockSpec(shape, lambda bi, j: (0,) * len(shape))
    return pl.pallas_call(
        _lru_kernel,
        grid=(b, s // TM),
        in_specs=[row_spec, row_spec, full(conv_w.shape), full((1, dl)), full(wgate.shape),
                  full((1, dl)), full((1, dl)), full((1, dl))],
        out_specs=row_spec,
        out_shape=jax.ShapeDtypeStruct((b, s, dl), F32),
        scratch_shapes=[pltpu.VMEM((SUBLANES, dl), F32), pltpu.VMEM((1, dl), F32)],
        compiler_params=pltpu.CompilerParams(
            dimension_semantics=("arbitrary", "arbitrary"), vmem_limit_bytes=VMEM_LIMIT),
        name="lru",
    )(xb, gb, conv_w, conv_b, wgate, b_rg, b_ig, lam)


def _rms(x, gain):
    return x * lax.rsqrt(jnp.mean(x * x, axis=-1, keepdims=True) + RMS_EPS) * gain


def _mixout_kernel(attn_ref, lru_ref, x_ref, mod_ref, ga_ref, gl_ref, woa_ref, wol_ref,
                   g1_ref, b1_ref, wr_ref, br_ref, x1_ref, u2_ref, rinfo_ref):
    tm, d = x_ref.shape
    na = _rms(attn_ref[...], ga_ref[...]).astype(BF16)
    nl = _rms(lru_ref[...], gl_ref[...]).astype(BF16)
    mix = _dot(na, woa_ref[...]) + _dot(nl, wol_ref[...])
    z = DEEPNORM_ALPHA * x_ref[...] + (1.0 + mod_ref[0, 2:3, :]) * mix
    x1 = _layer_norm(z) * g1_ref[...] + b1_ref[...]
    x1_ref[...] = x1
    u2 = _layer_norm(x1) * (1.0 + mod_ref[0, 4:5, :]) + mod_ref[0, 3:4, :]

    nt = (((1,), (1,)), ((), ()))
    uh, ul = _split2(u2)
    wh, wl = _split2(wr_ref[...])
    dg = lambda a, b_: lax.dot_general(a, b_, nt, preferred_element_type=F32)
    lg = dg(wh, uh) + (dg(wh, ul) + dg(wl, uh)) + br_ref[:, 0:1]

    def first_index(vals, target):
        idx = jnp.full_like(target, float(len(vals) - 1))
        for n in range(len(vals) - 2, -1, -1):
            idx = jnp.where(vals[n] == target, float(n), idx)
        return idx

    g = [lg[n:n + 1, :] for n in range(N_GROUPS)]
    gmax = functools.reduce(jnp.maximum, g)
    gsum = functools.reduce(lambda a, b_: a + b_, [jnp.exp(v - gmax) for v in g])
    grp_w = 1.0 / gsum
    gidx = first_index(g, gmax)

    sel = []
    for e in range(EXPERTS_PER_GROUP):
        v = lg[N_GROUPS + (N_GROUPS - 1) * EXPERTS_PER_GROUP + e:
               N_GROUPS + (N_GROUPS - 1) * EXPERTS_PER_GROUP + e + 1, :]
        for gi in range(N_GROUPS - 2, -1, -1):
            r0 = N_GROUPS + gi * EXPERTS_PER_GROUP + e
            v = jnp.where(gidx == float(gi), lg[r0:r0 + 1, :], v)
        sel.append(v)
    smax = functools.reduce(jnp.maximum, sel)
    i1 = first_index(sel, smax)
    rest = [jnp.where(i1 == float(e), -3e38, sel[e]) for e in range(EXPERTS_PER_GROUP)]
    rmax = functools.reduce(jnp.maximum, rest)
    i2 = first_index(rest, rmax)
    e2 = jnp.exp(rmax - smax)
    w1 = grp_w / (1.0 + e2)
    w2 = grp_w * e2 / (1.0 + e2)
    ia = jnp.minimum(i1, i2)
    ib = jnp.maximum(i1, i2)
    wa = jnp.where(i1 < i2, w1, w2)
    wb = jnp.where(i1 < i2, w2, w1)
    pair = jnp.where(ia == 0.0, ib - 1.0, jnp.where(ia == 1.0, ib + 1.0, 5.0))
    bucket = gidx * float(N_PAIRS) + pair

    zrow = jnp.zeros_like(wa)
    rinfo_ref[...] = jnp.concatenate([bucket, wa, wb] + [zrow] * (SUBLANES - 3), axis=0)
    wt = jnp.concatenate([wa, wb, jnp.zeros((LANES - 2, tm), F32)], axis=0)
    u2_ref[:, :d] = u2
    u2_ref[:, d:] = wt.T


def _mixout_call(attn, lru, x2d, mod3, ga, gl, woa, wol, g1, b1, wr, br, seq):
    t, d = x2d.shape
    dh = attn.shape[1]
    per_b = seq // TM
    full = lambda shape: pl.BlockSpec(shape, lambda i: (0,) * len(shape))
    return pl.pallas_call(
        _mixout_kernel,
        grid=(t // TM,),
        in_specs=[pl.BlockSpec((TM, dh), lambda i: (i, 0)),
                  pl.BlockSpec((TM, dh), lambda i: (i, 0)),
                  pl.BlockSpec((TM, d), lambda i: (i, 0)),
                  pl.BlockSpec((1, 6, d), lambda i: (i // per_b, 0, 0)),
                  full((1, dh)), full((1, dh)), full((dh, d)), full((dh, d)),
                  full((1, d)), full((1, d)), full((BUCKET_ROWS, d)), full((BUCKET_ROWS, LANES))],
        out_specs=[pl.BlockSpec((TM, d), lambda i: (i, 0)),
                   pl.BlockSpec((TM, d + LANES), lambda i: (i, 0)),
                   pl.BlockSpec((SUBLANES, TM), lambda i: (0, i))],
        out_shape=[jax.ShapeDtypeStruct((t, d), F32),
                   jax.ShapeDtypeStruct((t, d + LANES), F32),
                   jax.ShapeDtypeStruct((SUBLANES, t), F32)],
        compiler_params=pltpu.CompilerParams(
            dimension_semantics=("arbitrary",), vmem_limit_bytes=VMEM_LIMIT),
        name="mixout",
    )(attn, lru, x2d, mod3, ga, gl, woa, wol, g1, b1, wr, br)


def _rank_kernel(rinfo_ref, dest_ref, counts_ref, carry_ref, offs_ref):
    phase = pl.program_id(0)
    t = pl.program_id(1)
    tm = rinfo_ref.shape[1]

    @pl.when(t == 0)
    def _():
        carry_ref[...] = jnp.zeros_like(carry_ref)

    bucket = rinfo_ref[0:1, :]
    cid = lax.broadcasted_iota(jnp.int32, (BUCKET_ROWS, tm), 0).astype(F32)
    onehot = jnp.where(cid == bucket, 1.0, 0.0)
    srow = lax.broadcasted_iota(jnp.int32, (tm, tm), 0)
    scol = lax.broadcasted_iota(jnp.int32, (tm, tm), 1)
    upper = (srow <= scol).astype(BF16)
    prefix = _dot(onehot.astype(BF16), upper)
    carry = carry_ref[...]

    @pl.when(phase == 1)
    def _():
        rank = jnp.sum(onehot * (prefix - 1.0 + carry[:, 0:1] + offs_ref[:, 0:1]),
                       axis=0, keepdims=True)
        dest_ref[...] = rank.astype(jnp.int32)

    new_carry = carry + prefix[:, tm - 1:tm]
    carry_ref[...] = new_carry

    @pl.when(jnp.logical_and(phase == 0, t == pl.num_programs(1) - 1))
    def _():
        padded = jnp.floor((new_carry + float(TM_E - 1)) * (1.0 / TM_E)) * float(TM_E)
        inc = padded
        k = 1
        while k < BUCKET_ROWS:
            inc = inc + _shift_rows(inc, k, 0.0)
            k *= 2
        offs_ref[...] = inc - padded
        counts_ref[...] = new_carry


def _rank_call(rinfo):
    t = rinfo.shape[1]
    return pl.pallas_call(
        _rank_kernel,
        grid=(2, t // TM),
        in_specs=[pl.BlockSpec((SUBLANES, TM), lambda p, i: (0, i))],
        out_specs=[pl.BlockSpec((1, TM), lambda p, i: (0, i * p)),
                   pl.BlockSpec((BUCKET_ROWS, LANES), lambda p, i: (0, 0))],
        out_shape=[jax.ShapeDtypeStruct((1, t), jnp.int32),
                   jax.ShapeDtypeStruct((BUCKET_ROWS, LANES), F32)],
        scratch_shapes=[pltpu.VMEM((BUCKET_ROWS, LANES), F32),
                        pltpu.VMEM((BUCKET_ROWS, LANES), F32)],
        compiler_params=pltpu.CompilerParams(
            dimension_semantics=("arbitrary", "arbitrary"), vmem_limit_bytes=VMEM_LIMIT),
        name="rank",
    )(rinfo)


def _dispatch_kernel(dest_ref, u2_ref, xs_in_ref, xs_ref, sem):
    del xs_in_ref
    tm = u2_ref.shape[0]
    t0 = pl.program_id(0) * tm

    def issue(r, c):
        pltpu.make_async_copy(u2_ref.at[pl.ds(r, 1)],
                              xs_ref.at[pl.ds(dest_ref[t0 + r], 1)], sem).start()
        return c

    lax.fori_loop(0, tm, issue, 0, unroll=ISSUE_UNROLL)
    pltpu.make_async_copy(u2_ref, xs_ref.at[pl.ds(0, tm)], sem).wait()


def _dispatch_call(dest, u2ext, xs_init):
    t, w = u2ext.shape
    grid_spec = pltpu.PrefetchScalarGridSpec(
        num_scalar_prefetch=1,
        grid=(t // TM,),
        in_specs=[pl.BlockSpec((TM, w), lambda i, dest_ref: (i, 0)),
                  pl.BlockSpec(memory_space=pl.ANY)],
        out_specs=pl.BlockSpec(memory_space=pl.ANY),
        scratch_shapes=[pltpu.SemaphoreType.DMA(())],
    )
    return pl.pallas_call(
        _dispatch_kernel,
        grid_spec=grid_spec,
        out_shape=jax.ShapeDtypeStruct(xs_init.shape, F32),
        input_output_aliases={2: 0},
        compiler_params=pltpu.CompilerParams(
            dimension_semantics=("arbitrary",), vmem_limit_bytes=VMEM_LIMIT),
        name="dispatch",
    )(dest, u2ext, xs_init)


def _experts_kernel(ea_ref, eb_ref, nv_ref, xs_ref, wga_ref, wua_ref, wda_ref,
                    wgb_ref, wub_ref, wdb_ref, ys_ref):
    del ea_ref, eb_ref
    i = pl.program_id(0)
    d = ys_ref.shape[1]

    @pl.when(i < nv_ref[0])
    def _():
        x = xs_ref[:, :d].astype(BF16)

        def expert(wg_ref, wu_ref, wd_ref):
            g = _dot(x, wg_ref[0])
            h = (g * _sigmoid(g)) * _dot(x, wu_ref[0])
            return _dot(h.astype(BF16), wd_ref[0])

        ya = xs_ref[:, d:d + 1] * expert(wga_ref, wua_ref, wda_ref)
        ys_ref[...] = ya + xs_ref[:, d + 1:d + 2] * expert(wgb_ref, wub_ref, wdb_ref)

    @pl.when(i >= nv_ref[0])
    def _():
        ys_ref[...] = jnp.zeros_like(ys_ref)


def _experts_call(tile_ea, tile_eb, n_valid, xs, wg, wu, wd):
    tp, w = xs.shape
    _, d, de = wg.shape
    row = lambda i, ea, eb, nv: (jnp.minimum(i, nv[0] - 1), 0)
    wa = lambda i, ea, eb, nv: (ea[i], 0, 0)
    wb = lambda i, ea, eb, nv: (eb[i], 0, 0)
    grid_spec = pltpu.PrefetchScalarGridSpec(
        num_scalar_prefetch=3,
        grid=(tp // TM_E,),
        in_specs=[pl.BlockSpec((TM_E, w), row),
                  pl.BlockSpec((1, d, de), wa), pl.BlockSpec((1, d, de), wa),
                  pl.BlockSpec((1, de, d), wa),
                  pl.BlockSpec((1, d, de), wb), pl.BlockSpec((1, d, de), wb),
                  pl.BlockSpec((1, de, d), wb)],
        out_specs=pl.BlockSpec((TM_E, d), lambda i, ea, eb, nv: (i, 0)),
    )
    return pl.pallas_call(
        _experts_kernel,
        grid_spec=grid_spec,
        out_shape=jax.ShapeDtypeStruct((tp, d), F32),
        compiler_params=pltpu.CompilerParams(
            dimension_semantics=("arbitrary",), vmem_limit_bytes=VMEM_LIMIT),
        name="experts",
    )(tile_ea, tile_eb, n_valid, xs, wg, wu, wd, wg, wu, wd)


def _final_kernel(dest_ref, x1_ref, mod_ref, g2_ref, b2_ref, ys_ref, o_ref, ybuf, sem):
    tm = x1_ref.shape[0]
    t0 = pl.program_id(0) * tm

    def issue(r, c):
        pltpu.make_async_copy(ys_ref.at[pl.ds(dest_ref[t0 + r], 1)],
                              ybuf.at[pl.ds(r, 1)], sem).start()
        return c

    lax.fori_loop(0, tm, issue, 0, unroll=ISSUE_UNROLL)
    pltpu.make_async_copy(ys_ref.at[pl.ds(0, tm)], ybuf, sem).wait()
    z = DEEPNORM_ALPHA * x1_ref[...] + (1.0 + mod_ref[0, 5:6, :]) * ybuf[...]
    o_ref[...] = _layer_norm(z) * g2_ref[...] + b2_ref[...]


def _final_call(dest, x1, mod3, g2, b2, ys, seq):
    t, d = x1.shape
    per_b = seq // TM
    grid_spec = pltpu.PrefetchScalarGridSpec(
        num_scalar_prefetch=1,
        grid=(t // TM,),
        in_specs=[pl.BlockSpec((TM, d), lambda i, dr: (i, 0)),
                  pl.BlockSpec((1, 6, d), lambda i, dr: (i // per_b, 0, 0)),
                  pl.BlockSpec((1, d), lambda i, dr: (0, 0)),
                  pl.BlockSpec((1, d), lambda i, dr: (0, 0)),
                  pl.BlockSpec(memory_space=pl.ANY)],
        out_specs=pl.BlockSpec((TM, d), lambda i, dr: (i, 0)),
        scratch_shapes=[pltpu.VMEM((TM, d), F32), pltpu.SemaphoreType.DMA(())],
    )
    return pl.pallas_call(
        _final_kernel,
        grid_spec=grid_spec,
        out_shape=jax.ShapeDtypeStruct((t, d), F32),
        compiler_params=pltpu.CompilerParams(
            dimension_semantics=("arbitrary",), vmem_limit_bytes=VMEM_LIMIT),
        name="final",
    )(dest, x1, mod3, g2, b2, ys)


def _gate_pairs(w_rg, w_ig):
    def pairs(w):
        n, bs, _ = w.shape
        w = w.reshape(n // 2, 2, bs, bs)
        z = jnp.zeros((n // 2, bs, bs), w.dtype)
        top = jnp.concatenate([w[:, 0], z], axis=2)
        bot = jnp.concatenate([z, w[:, 1]], axis=2)
        return jnp.concatenate([top, bot], axis=1)
    return jnp.concatenate([pairs(w_rg), pairs(w_ig)], axis=2).astype(BF16)


def kernel(x, c, w_ada, b_ada, w_in, b_f, conv_w, conv_b, w_rg, b_rg, w_ig, b_ig, lru_lambda,
           g_attn, g_lru, w_out, ln1_g, ln1_b, w_grp, b_grp, w_exp, b_exp,
           w_e_gate, w_e_up, w_e_down, ln2_g, ln2_b):
    assert w_ada.shape[0] == DEPTH
    b, s, d = x.shape
    t = b * s
    d_attn = N_HEADS * HEAD_DIM
    d_lru = conv_w.shape[2]
    n_exp = w_e_gate.shape[1]
    assert s % TM == 0 and s % TQ == 0 and t % TM_E == 0

    c_pad = jnp.pad(c, ((0, SUBLANES - b), (0, 0)))
    mod = _mod_call(c_pad, w_ada[0], b_ada[0][None, :])
    mod3 = mod[:b].reshape(b, 6, d)

    wi = w_in[0]
    o = 3 * d_attn
    wq = (wi[:, :d_attn] * HEAD_DIM ** -0.5).astype(BF16)
    wk = wi[:, d_attn:2 * d_attn].astype(BF16)
    wvt = wi[:, 2 * d_attn:o].T.astype(BF16)
    wf = jnp.pad(wi[:, o:o + N_HEADS], ((0, 0), (0, LANES - N_HEADS))).astype(BF16)
    wx = wi[:, o + N_HEADS:o + N_HEADS + d_lru].astype(BF16)
    wg = wi[:, o + N_HEADS + d_lru:].astype(BF16)
    bf_pad = jnp.pad(b_f[0], (0, LANES - N_HEADS))[None, :]
    q_aug, k_aug, vt_aug, xb, gb = _inproj_call(x, mod3, wq, wk, wvt, wf, wx, wg, bf_pad)
    attn = _attn_call(q_aug, k_aug, vt_aug)
    lru = _lru_call(xb, gb, conv_w[0], conv_b[0][None, :], _gate_pairs(w_rg[0], w_ig[0]),
                    b_rg[0][None, :], b_ig[0][None, :], lru_lambda[0][None, :])

    wo = w_out[0].astype(BF16)
    n_route = N_GROUPS + n_exp
    wr = jnp.pad(jnp.concatenate([w_grp[0], w_exp[0]], axis=1).T,
                 ((0, BUCKET_ROWS - n_route), (0, 0)))
    br = jnp.pad(jnp.concatenate([b_grp[0], b_exp[0]]), (0, BUCKET_ROWS - n_route))
    br = jnp.broadcast_to(br[:, None], (BUCKET_ROWS, LANES))
    x1, u2ext, rinfo = _mixout_call(
        attn.reshape(t, d_attn), lru.reshape(t, d_lru), x.reshape(t, d), mod3,
        g_attn[0][None, :], g_lru[0][None, :], wo[:d_attn], wo[d_attn:],
        ln1_g[0][None, :], ln1_b[0][None, :], wr, br, s)

    dest2d, counts = _rank_call(rinfo)
    dest = dest2d.reshape(t)
    cnt = counts[:N_BUCKETS, 0].astype(jnp.int32)
    ends = jnp.cumsum((cnt + (TM_E - 1)) // TM_E)
    n_tiles = t // TM_E + N_BUCKETS
    tile_bucket = jnp.sum(ends[None, :] <= jnp.arange(n_tiles)[:, None], axis=1)
    tile_bucket = jnp.minimum(tile_bucket, N_BUCKETS - 1)
    n_valid = ends[N_BUCKETS - 1:]
    last_bucket = tile_bucket[jnp.maximum(n_valid[0] - 1, 0)]
    tile_bucket = jnp.where(jnp.arange(n_tiles) < n_valid[0], tile_bucket, last_bucket)
    tile_ea = jnp.asarray(_BUCKET_EA, jnp.int32)[tile_bucket]
    tile_eb = jnp.asarray(_BUCKET_EB, jnp.int32)[tile_bucket]

    xs = _dispatch_call(dest, u2ext, jnp.zeros((n_tiles * TM_E, d + LANES), F32))
    ys = _experts_call(tile_ea, tile_eb, n_valid.astype(jnp.int32), xs,
                       w_e_gate[0].astype(BF16), w_e_up[0].astype(BF16),
                       w_e_down[0].astype(BF16))
    out = _final_call(dest, x1, mod3, ln2_g[0][None, :], ln2_b[0][None, :], ys, s)
    return out.reshape(b, s, d)
```

```python
import functools

import jax
import jax.numpy as jnp
from jax import lax
from jax.experimental import pallas as pl
from jax.experimental.pallas import tpu as pltpu

F32 = jnp.float32
BF16 = jnp.bfloat16

HEAD_DIM = 64
N_HEADS = 8
N_LRU_BLOCKS = 8
CONV_WIDTH = 4
LRU_C = 8.0
N_GROUPS = 4
EXPERTS_PER_GROUP = 4
N_PAIRS = 6
N_BUCKETS = N_GROUPS * N_PAIRS
LN_EPS = 1e-5
RMS_EPS = 1e-6
NEG_INF = -1e30
DEPTH = 1
DEEPNORM_ALPHA = (2.0 * DEPTH) ** 0.25
LOG2E = 1.4426950408889634

LANES = 128
SUBLANES = 8
HEAD_PAD = LANES
VT_ROWS = 80
BUCKET_ROWS = 32
TM = 512
TQ = 512
TM_E = 256
ISSUE_UNROLL = 8
VMEM_LIMIT = 56 * 1024 * 1024

_PAIRS = [(0, 1), (0, 2), (0, 3), (1, 2), (1, 3), (2, 3)]
_BUCKET_EA = [g * EXPERTS_PER_GROUP + a for g in range(N_GROUPS) for (a, b) in _PAIRS]
_BUCKET_EB = [g * EXPERTS_PER_GROUP + b for g in range(N_GROUPS) for (a, b) in _PAIRS]


def _dot(a, b):
    return jnp.dot(a, b, preferred_element_type=F32)


def _split2(a):
    hi = a.astype(BF16)
    lo = (a - hi.astype(F32)).astype(BF16)
    return hi, lo


def _split3(a):
    hi = a.astype(BF16)
    r = a - hi.astype(F32)
    mid = r.astype(BF16)
    lo = (r - mid.astype(F32)).astype(BF16)
    return hi, mid, lo


def _layer_norm(x):
    mu = jnp.mean(x, axis=-1, keepdims=True)
    xc = x - mu
    var = jnp.mean(xc * xc, axis=-1, keepdims=True)
    return xc * lax.rsqrt(var + LN_EPS)


def _sigmoid(x):
    return 0.5 * jnp.tanh(0.5 * x) + 0.5


def _rot_in_group(x, k):
    n, w = x.shape
    return pltpu.roll(x.reshape(n // SUBLANES, SUBLANES, w), k, 1).reshape(n, w)


def _log_sigmoid(z):
    return jnp.minimum(z, 0.0) - jnp.log1p(jnp.exp(-jnp.abs(z)))


def _shift_rows(x, k, fill):
    n = x.shape[0]
    if k % SUBLANES == 0:
        return jnp.concatenate([jnp.full((k, x.shape[1]), fill, x.dtype), x[:n - k]], axis=0)
    row = lax.broadcasted_iota(jnp.int32, x.shape, 0)
    return jnp.where(row >= k, pltpu.roll(x, k, 0), fill)


def _mod_kernel(c_ref, w_ref, b_ref, o_ref):
    c = c_ref[...]
    s = c * _sigmoid(c)
    sh, sl = _split2(s)
    wh, wl = _split2(w_ref[...])
    o_ref[...] = _dot(sh, wh) + (_dot(sh, wl) + _dot(sl, wh)) + b_ref[...]


def _mod_call(c_pad, w_ada, b_ada):
    rows, d = c_pad.shape
    n = w_ada.shape[1]
    return pl.pallas_call(
        _mod_kernel,
        grid=(n // d,),
        in_specs=[pl.BlockSpec((rows, d), lambda j: (0, 0)),
                  pl.BlockSpec((d, d), lambda j: (0, j)),
                  pl.BlockSpec((1, d), lambda j: (0, j))],
        out_specs=pl.BlockSpec((rows, d), lambda j: (0, j)),
        out_shape=jax.ShapeDtypeStruct((rows, n), F32),
        compiler_params=pltpu.CompilerParams(vmem_limit_bytes=VMEM_LIMIT),
        name="mod",
    )(c_pad, w_ada, b_ada)


def _inproj_kernel(x_ref, mod_ref, wq_ref, wk_ref, wvt_ref, wf_ref, wx_ref, wg_ref, bf_ref,
                   q_ref, k_ref, vt_ref, xb_ref, gb_ref, carry_ref):
    j = pl.program_id(1)

    @pl.when(j == 0)
    def _():
        carry_ref[...] = jnp.zeros_like(carry_ref)

    tm = x_ref.shape[1]
    u = _layer_norm(x_ref[0]) * (1.0 + mod_ref[0, 1:2, :]) + mod_ref[0, 0:1, :]
    ub = u.astype(BF16)

    logf = _log_sigmoid(_dot(ub, wf_ref[...]) + bf_ref[...])
    row = lax.broadcasted_iota(jnp.int32, (tm, tm), 0)
    col = lax.broadcasted_iota(jnp.int32, (tm, tm), 1)
    tril = (col <= row).astype(BF16)
    l_hi, l_mid, l_lo = _split3(logf)
    cum = (_dot(tril, l_hi) + (_dot(tril, l_mid) + _dot(tril, l_lo))) + carry_ref[...]
    carry_ref[...] = cum[tm - 1:tm, :]
    c_hi, c_mid, c_lo = [p.astype(F32) for p in _split3(cum * LOG2E)]

    lane = lax.broadcasted_iota(jnp.int32, (tm, HEAD_PAD), 1)
    d = HEAD_DIM
    q_all = _dot(ub, wq_ref[...])
    k_all = _dot(ub, wk_ref[...])
    for h in range(N_HEADS):
        ts = slice((h // 2) * LANES, (h // 2 + 1) * LANES)
        qh, kh = q_all[:, ts], k_all[:, ts]
        if h % 2:
            qh, kh = pltpu.roll(qh, d, 1), pltpu.roll(kh, d, 1)
        hi, mid, lo = c_hi[:, h:h + 1], c_mid[:, h:h + 1], c_lo[:, h:h + 1]
        q_ext = jnp.where(lane == d, hi, jnp.where(lane == d + 1, mid, jnp.where(
            lane == d + 2, lo, jnp.where(lane < d + 6, 1.0, 0.0))))
        k_ext = jnp.where(lane < d + 3, 1.0, jnp.where(lane == d + 3, -hi, jnp.where(
            lane == d + 4, -mid, jnp.where(lane == d + 5, -lo, 0.0))))
        q_ref[0, h] = jnp.where(lane < d, qh, q_ext).astype(BF16)
        k_ref[0, h] = jnp.where(lane < d, kh, k_ext).astype(BF16)

    vt_all = lax.dot_general(wvt_ref[...], ub, (((1,), (1,)), ((), ())),
                             preferred_element_type=F32)
    pad_rows = lax.broadcasted_iota(jnp.int32, (VT_ROWS - d, tm), 0)
    ones_row = jnp.where(pad_rows == 0, 1.0, 0.0).astype(BF16)
    for h in range(N_HEADS):
        vt_ref[0, h, 0, :d, :] = vt_all[h * d:(h + 1) * d, :].astype(BF16)
        vt_ref[0, h, 0, d:, :] = ones_row

    xb_ref[0] = _dot(ub, wx_ref[...])
    gb_ref[0] = _dot(ub, wg_ref[...])


def _inproj_call(x, mod3, wq, wk, wvt, wf, wx, wg, bf_pad):
    b, s, d = x.shape
    d_attn = wq.shape[1]
    d_lru = wx.shape[1]
    full = lambda shape: pl.BlockSpec(shape, lambda bi, j: (0,) * len(shape))
    head_spec = pl.BlockSpec((1, N_HEADS, TM, HEAD_PAD), lambda bi, j: (bi, 0, j, 0))
    vt_spec = pl.BlockSpec((1, N_HEADS, 1, VT_ROWS, TM), lambda bi, j: (bi, 0, j, 0, 0))
    row_spec = pl.BlockSpec((1, TM, d_lru), lambda bi, j: (bi, j, 0))
    head_shape = jax.ShapeDtypeStruct((b, N_HEADS, s, HEAD_PAD), BF16)
    return pl.pallas_call(
        _inproj_kernel,
        grid=(b, s // TM),
        in_specs=[pl.BlockSpec((1, TM, d), lambda bi, j: (bi, j, 0)),
                  pl.BlockSpec((1, 6, d), lambda bi, j: (bi, 0, 0)),
                  full((d, d_attn)), full((d, d_attn)), full((d_attn, d)), full((d, LANES)),
                  full((d, d_lru)), full((d, d_lru)), full((1, LANES))],
        out_specs=[head_spec, head_spec, vt_spec, row_spec, row_spec],
        out_shape=[head_shape, head_shape,
                   jax.ShapeDtypeStruct((b, N_HEADS, s // TM, VT_ROWS, TM), BF16),
                   jax.ShapeDtypeStruct((b, s, d_lru), F32),
                   jax.ShapeDtypeStruct((b, s, d_lru), F32)],
        scratch_shapes=[pltpu.VMEM((1, LANES), F32)],
        compiler_params=pltpu.CompilerParams(
            dimension_semantics=("arbitrary", "arbitrary"), vmem_limit_bytes=VMEM_LIMIT),
        name="inproj",
    )(x, mod3, wq, wk, wvt, wf, wx, wg, bf_pad)


def _attn_kernel(q_ref, k_ref, vt_ref, o_ref, s_scr, m_scr, acc_scr):
    i = pl.program_id(2)
    tq = q_ref.shape[2]
    heads = range(q_ref.shape[1])

    def scores_to(slot, j):
        for hh in heads:
            k = k_ref[0, hh, pl.ds(pl.multiple_of(j * tq, tq), tq), :]
            s_scr[slot, hh] = lax.dot_general(
                k, q_ref[0, hh], (((1,), (1,)), ((), ())), preferred_element_type=F32)

    def consume(slot, j, masked):
        for hh in heads:
            s = s_scr[slot, hh]
            if masked:
                key = lax.broadcasted_iota(jnp.int32, (tq, tq), 0)
                qry = lax.broadcasted_iota(jnp.int32, (tq, tq), 1)
                s = jnp.where(key <= qry, s, NEG_INF)
            m = m_scr[hh]
            m_new = jnp.maximum(m, jnp.max(s, axis=0, keepdims=True))
            p = jnp.exp2(s - m_new).astype(BF16)
            acc_scr[hh] = jnp.exp2(m - m_new) * acc_scr[hh] + _dot(vt_ref[0, hh, j], p)
            m_scr[hh] = m_new

    m_scr[...] = jnp.full(m_scr.shape, NEG_INF, F32)
    acc_scr[...] = jnp.zeros(acc_scr.shape, F32)
    scores_to(0, 0)

    def pair(jj, c):
        j = 2 * jj
        scores_to(1, j + 1)
        consume(0, j, False)
        scores_to(0, j + 2)
        consume(1, j + 1, False)
        return c

    lax.fori_loop(0, i // 2, pair, 0)

    @pl.when(i % 2 == 1)
    def _():
        scores_to(1, i)
        consume(0, i - 1, False)
        consume(1, i, True)

    @pl.when(i % 2 == 0)
    def _():
        consume(0, i, True)

    for hh in heads:
        o_ref[0, hh * HEAD_DIM:(hh + 1) * HEAD_DIM, :] = (
            acc_scr[hh, :HEAD_DIM, :] / acc_scr[hh, HEAD_DIM:HEAD_DIM + 1, :])


def _attn_call(q_aug, k_aug, vt_aug):
    b, h, s, hp = q_aug.shape
    assert TQ == TM
    return pl.pallas_call(
        _attn_kernel,
        grid=(b, h // 2, s // TQ),
        in_specs=[pl.BlockSpec((1, 2, TQ, hp), lambda bi, p, i: (bi, p, i, 0)),
                  pl.BlockSpec((1, 2, s, hp), lambda bi, p, i: (bi, p, 0, 0)),
                  pl.BlockSpec((1, 2, s // TQ, VT_ROWS, TQ), lambda bi, p, i: (bi, p, 0, 0, 0))],
        out_specs=pl.BlockSpec((1, 2 * HEAD_DIM, TQ), lambda bi, p, i: (bi, p, i)),
        out_shape=jax.ShapeDtypeStruct((b, h * HEAD_DIM, s), F32),
        scratch_shapes=[pltpu.VMEM((2, 2, TQ, TQ), F32), pltpu.VMEM((2, 1, TQ), F32),
                        pltpu.VMEM((2, VT_ROWS, TQ), F32)],
        compiler_params=pltpu.CompilerParams(
            dimension_semantics=("arbitrary", "arbitrary", "arbitrary"),
            vmem_limit_bytes=VMEM_LIMIT),
        name="attention",
    )(q_aug, k_aug, vt_aug)


def _gelu_tanh(x):
    return 0.5 * x * (1.0 + jnp.tanh(0.7978845608028654 * (x + 0.044715 * (x * x * x))))


def _lru_kernel(xb_ref, gb_ref, cw_ref, cb_ref, wgate_ref, brg_ref, big_ref, lam_ref,
                o_ref, tail_ref, h_ref):
    j = pl.program_id(1)

    @pl.when(j == 0)
    def _():
        tail_ref[...] = jnp.zeros_like(tail_ref)
        h_ref[...] = jnp.zeros_like(h_ref)

    x = xb_ref[0]
    tm, dl = x.shape
    row_in_group = lax.broadcasted_iota(jnp.int32, (tm, dl), 0) % SUBLANES
    x_prev_group = jnp.concatenate([tail_ref[...], x[:tm - SUBLANES]], axis=0)
    xc = x * cw_ref[CONV_WIDTH - 1:CONV_WIDTH, :] + cb_ref[...]
    for k in range(1, CONV_WIDTH):
        xs = jnp.where(row_in_group < k, _rot_in_group(x_prev_group, k), _rot_in_group(x, k))
        xc = xc + xs * cw_ref[CONV_WIDTH - 1 - k:CONV_WIDTH - k, :]
    tail_ref[...] = x[tm - SUBLANES:, :]

    xcb = xc.astype(BF16)
    n_pairs = dl // LANES
    r_parts, i_parts = [], []
    for p in range(n_pairs):
        g = _dot(xcb[:, p * LANES:(p + 1) * LANES], wgate_ref[p])
        r_parts.append(g[:, :LANES])
        i_parts.append(g[:, LANES:])
    r = _sigmoid(jnp.concatenate(r_parts, axis=1) + brg_ref[...])
    ig = _sigmoid(jnp.concatenate(i_parts, axis=1) + big_ref[...])

    lam = lam_ref[...]
    softplus_neg_lam = jnp.maximum(-lam, 0.0) + jnp.log1p(jnp.exp(-jnp.abs(lam)))
    log_a = (-LRU_C) * r * softplus_neg_lam
    a = jnp.exp(log_a)
    v = 1.0 - a * a
    u = jnp.where(v > 0.0, v * lax.rsqrt(v), 0.0) * (ig * xc)

    k = 1
    while k < SUBLANES:
        keep = row_in_group >= k
        u = a * jnp.where(keep, _rot_in_group(u, k), 0.0) + u
        a = a * jnp.where(keep, _rot_in_group(a, k), 1.0)
        k *= 2
    h_prev = h_ref[...]
    groups = []
    for g in range(tm // SUBLANES):
        rows = slice(g * SUBLANES, (g + 1) * SUBLANES)
        hg = a[rows] * h_prev + u[rows]
        groups.append(hg)
        h_prev = hg[SUBLANES - 1:SUBLANES]
    h_ref[...] = h_prev
    o_ref[0] = jnp.concatenate(groups, axis=0) * _gelu_tanh(gb_ref[0])


def _lru_call(xb, gb, conv_w, conv_b, wgate, b_rg, b_ig, lam):
    b, s, dl = xb.shape
    row_spec = pl.BlockSpec((1, TM, dl), lambda bi, j: (bi, j, 0))
    full = lambda shape: pl.BlockSpec(shape, lambda bi, j: (0,) * len(shape))
    return pl.pallas_call(
        _lru_kernel,
        grid=(b, s // TM),
        in_specs=[row_spec, row_spec, full(conv_w.shape), full((1, dl)), full(wgate.shape),
                  full((1, dl)), full((1, dl)), full((1, dl))],
        out_specs=row_spec,
        out_shape=jax.ShapeDtypeStruct((b, s, dl), F32),
        scratch_shapes=[pltpu.VMEM((SUBLANES, dl), F32), pltpu.VMEM((1, dl), F32)],
        compiler_params=pltpu.CompilerParams(
            dimension_semantics=("arbitrary", "arbitrary"), vmem_limit_bytes=VMEM_LIMIT),
        name="lru",
    )(xb, gb, conv_w, conv_b, wgate, b_rg, b_ig, lam)


def _rms(x, gain):
    return x * lax.rsqrt(jnp.mean(x * x, axis=-1, keepdims=True) + RMS_EPS) * gain


def _mixout_kernel(attn_ref, lru_ref, x_ref, mod_ref, ga_ref, gl_ref, woa_ref, wol_ref,
                   g1_ref, b1_ref, wr_ref, br_ref, x1_ref, u2_ref, rinfo_ref, cnt_ref):
    tm, d = x_ref.shape
    na = _rms(attn_ref[0].T, ga_ref[...]).astype(BF16)
    nl = _rms(lru_ref[...], gl_ref[...]).astype(BF16)
    mix = _dot(na, woa_ref[...]) + _dot(nl, wol_ref[...])
    z = DEEPNORM_ALPHA * x_ref[...] + (1.0 + mod_ref[0, 2:3, :]) * mix
    x1 = _layer_norm(z) * g1_ref[...] + b1_ref[...]
    x1_ref[...] = x1
    u2 = _layer_norm(x1) * (1.0 + mod_ref[0, 4:5, :]) + mod_ref[0, 3:4, :]

    nt = (((1,), (1,)), ((), ()))
    uh, ul = _split2(u2)
    wh, wl = _split2(wr_ref[...])
    dg = lambda a, b_: lax.dot_general(a, b_, nt, preferred_element_type=F32)
    lg = dg(wh, uh) + (dg(wh, ul) + dg(wl, uh)) + br_ref[:, 0:1]

    def first_index(vals, target):
        idx = jnp.full_like(target, float(len(vals) - 1))
        for n in range(len(vals) - 2, -1, -1):
            idx = jnp.where(vals[n] == target, float(n), idx)
        return idx

    g = [lg[n:n + 1, :] for n in range(N_GROUPS)]
    gmax = functools.reduce(jnp.maximum, g)
    gsum = functools.reduce(lambda a, b_: a + b_, [jnp.exp(v - gmax) for v in g])
    grp_w = 1.0 / gsum
    gidx = first_index(g, gmax)

    sel = []
    for e in range(EXPERTS_PER_GROUP):
        v = lg[N_GROUPS + (N_GROUPS - 1) * EXPERTS_PER_GROUP + e:
               N_GROUPS + (N_GROUPS - 1) * EXPERTS_PER_GROUP + e + 1, :]
        for gi in range(N_GROUPS - 2, -1, -1):
            r0 = N_GROUPS + gi * EXPERTS_PER_GROUP + e
            v = jnp.where(gidx == float(gi), lg[r0:r0 + 1, :], v)
        sel.append(v)
    smax = functools.reduce(jnp.maximum, sel)
    i1 = first_index(sel, smax)
    rest = [jnp.where(i1 == float(e), -3e38, sel[e]) for e in range(EXPERTS_PER_GROUP)]
    rmax = functools.reduce(jnp.maximum, rest)
    i2 = first_index(rest, rmax)
    e2 = jnp.exp(rmax - smax)
    w1 = grp_w / (1.0 + e2)
    w2 = grp_w * e2 / (1.0 + e2)
    ia = jnp.minimum(i1, i2)
    ib = jnp.maximum(i1, i2)
    wa = jnp.where(i1 < i2, w1, w2)
    wb = jnp.where(i1 < i2, w2, w1)
    pair = jnp.where(ia == 0.0, ib - 1.0, jnp.where(ia == 1.0, ib + 1.0, 5.0))
    bucket = gidx * float(N_PAIRS) + pair

    @pl.when(pl.program_id(0) == 0)
    def _():
        cnt_ref[...] = jnp.zeros_like(cnt_ref)

    cid = lax.broadcasted_iota(jnp.int32, (BUCKET_ROWS, tm), 0).astype(F32)
    cnt_ref[...] += jnp.sum(jnp.where(cid == bucket, 1.0, 0.0), axis=1, keepdims=True)

    zrow = jnp.zeros_like(wa)
    rinfo_ref[...] = jnp.concatenate([bucket, wa, wb] + [zrow] * (SUBLANES - 3), axis=0)
    wt = jnp.concatenate([wa, wb, jnp.zeros((LANES - 2, tm), F32)], axis=0)
    u2_ref[:, :d] = u2
    u2_ref[:, d:] = wt.T


def _mixout_call(attn_t, lru, x2d, mod3, ga, gl, woa, wol, g1, b1, wr, br, seq):
    t, d = x2d.shape
    dh = attn_t.shape[1]
    per_b = seq // TM
    full = lambda shape: pl.BlockSpec(shape, lambda i: (0,) * len(shape))
    return pl.pallas_call(
        _mixout_kernel,
        grid=(t // TM,),
        in_specs=[pl.BlockSpec((1, dh, TM), lambda i: (i // per_b, 0, i % per_b)),
                  pl.BlockSpec((TM, dh), lambda i: (i, 0)),
                  pl.BlockSpec((TM, d), lambda i: (i, 0)),
                  pl.BlockSpec((1, 6, d), lambda i: (i // per_b, 0, 0)),
                  full((1, dh)), full((1, dh)), full((dh, d)), full((dh, d)),
                  full((1, d)), full((1, d)), full((BUCKET_ROWS, d)), full((BUCKET_ROWS, LANES))],
        out_specs=[pl.BlockSpec((TM, d), lambda i: (i, 0)),
                   pl.BlockSpec((TM, d + LANES), lambda i: (i, 0)),
                   pl.BlockSpec((SUBLANES, TM), lambda i: (0, i)),
                   pl.BlockSpec((BUCKET_ROWS, LANES), lambda i: (0, 0))],
        out_shape=[jax.ShapeDtypeStruct((t, d), F32),
                   jax.ShapeDtypeStruct((t, d + LANES), F32),
                   jax.ShapeDtypeStruct((SUBLANES, t), F32),
                   jax.ShapeDtypeStruct((BUCKET_ROWS, LANES), F32)],
        compiler_params=pltpu.CompilerParams(
            dimension_semantics=("arbitrary",), vmem_limit_bytes=VMEM_LIMIT),
        name="mixout",
    )(attn_t, lru, x2d, mod3, ga, gl, woa, wol, g1, b1, wr, br)


def _rank_kernel(rinfo_ref, counts_ref, dest_ref, carry_ref, offs_ref):
    tm = rinfo_ref.shape[1]

    @pl.when(pl.program_id(0) == 0)
    def _():
        carry_ref[...] = jnp.zeros_like(carry_ref)
        padded = jnp.floor((counts_ref[...] + float(TM_E - 1)) * (1.0 / TM_E)) * float(TM_E)
        inc = padded
        k = 1
        while k < BUCKET_ROWS:
            inc = inc + _shift_rows(inc, k, 0.0)
            k *= 2
        offs_ref[...] = inc - padded

    bucket = rinfo_ref[0:1, :]
    cid = lax.broadcasted_iota(jnp.int32, (BUCKET_ROWS, tm), 0).astype(F32)
    onehot = jnp.where(cid == bucket, 1.0, 0.0)
    srow = lax.broadcasted_iota(jnp.int32, (tm, tm), 0)
    scol = lax.broadcasted_iota(jnp.int32, (tm, tm), 1)
    upper = (srow <= scol).astype(BF16)
    prefix = _dot(onehot.astype(BF16), upper)
    carry = carry_ref[...]
    rank = jnp.sum(onehot * (prefix - 1.0 + carry[:, 0:1] + offs_ref[:, 0:1]),
                   axis=0, keepdims=True)
    dest_ref[...] = rank.astype(jnp.int32)
    carry_ref[...] = carry + prefix[:, tm - 1:tm]


def _rank_call(rinfo, counts):
    t = rinfo.shape[1]
    return pl.pallas_call(
        _rank_kernel,
        grid=(t // TM,),
        in_specs=[pl.BlockSpec((SUBLANES, TM), lambda i: (0, i)),
                  pl.BlockSpec((BUCKET_ROWS, LANES), lambda i: (0, 0))],
        out_specs=pl.BlockSpec((1, TM), lambda i: (0, i)),
        out_shape=jax.ShapeDtypeStruct((1, t), jnp.int32),
        scratch_shapes=[pltpu.VMEM((BUCKET_ROWS, LANES), F32),
                        pltpu.VMEM((BUCKET_ROWS, LANES), F32)],
        compiler_params=pltpu.CompilerParams(
            dimension_semantics=("arbitrary",), vmem_limit_bytes=VMEM_LIMIT),
        name="rank",
    )(rinfo, counts)


def _dispatch_kernel(dest_ref, u2_ref, xs_in_ref, xs_ref, sem):
    del xs_in_ref
    tm = u2_ref.shape[0]
    t0 = pl.program_id(0) * tm

    def issue(r, c):
        pltpu.make_async_copy(u2_ref.at[pl.ds(r, 1)],
                              xs_ref.at[pl.ds(dest_ref[t0 + r], 1)], sem).start()
        return c

    lax.fori_loop(0, tm, issue, 0, unroll=ISSUE_UNROLL)
    pltpu.make_async_copy(u2_ref, xs_ref.at[pl.ds(0, tm)], sem).wait()


def _dispatch_call(dest, u2ext, xs_init):
    t, w = u2ext.shape
    grid_spec = pltpu.PrefetchScalarGridSpec(
        num_scalar_prefetch=1,
        grid=(t // TM,),
        in_specs=[pl.BlockSpec((TM, w), lambda i, dest_ref: (i, 0)),
                  pl.BlockSpec(memory_space=pl.ANY)],
        out_specs=pl.BlockSpec(memory_space=pl.ANY),
        scratch_shapes=[pltpu.SemaphoreType.DMA(())],
    )
    return pl.pallas_call(
        _dispatch_kernel,
        grid_spec=grid_spec,
        out_shape=jax.ShapeDtypeStruct(xs_init.shape, F32),
        input_output_aliases={2: 0},
        compiler_params=pltpu.CompilerParams(
            dimension_semantics=("arbitrary",), vmem_limit_bytes=VMEM_LIMIT),
        name="dispatch",
    )(dest, u2ext, xs_init)


def _experts_kernel(ea_ref, eb_ref, nv_ref, xs_ref, wga_ref, wua_ref, wda_ref,
                    wgb_ref, wub_ref, wdb_ref, ys_ref):
    del ea_ref, eb_ref
    i = pl.program_id(0)
    d = ys_ref.shape[1]

    @pl.when(i < nv_ref[0])
    def _():
        x = xs_ref[:, :d].astype(BF16)

        def expert(wg_ref, wu_ref, wd_ref):
            g = _dot(x, wg_ref[0])
            h = (g * _sigmoid(g)) * _dot(x, wu_ref[0])
            return _dot(h.astype(BF16), wd_ref[0])

        ya = xs_ref[:, d:d + 1] * expert(wga_ref, wua_ref, wda_ref)
        ys_ref[...] = ya + xs_ref[:, d + 1:d + 2] * expert(wgb_ref, wub_ref, wdb_ref)

    @pl.when(i >= nv_ref[0])
    def _():
        ys_ref[...] = jnp.zeros_like(ys_ref)


def _experts_call(tile_ea, tile_eb, n_valid, xs, wg, wu, wd):
    tp, w = xs.shape
    _, d, de = wg.shape
    row = lambda i, ea, eb, nv: (jnp.minimum(i, nv[0] - 1), 0)
    wa = lambda i, ea, eb, nv: (ea[i], 0, 0)
    wb = lambda i, ea, eb, nv: (eb[i], 0, 0)
    grid_spec = pltpu.PrefetchScalarGridSpec(
        num_scalar_prefetch=3,
        grid=(tp // TM_E,),
        in_specs=[pl.BlockSpec((TM_E, w), row),
                  pl.BlockSpec((1, d, de), wa), pl.BlockSpec((1, d, de), wa),
                  pl.BlockSpec((1, de, d), wa),
                  pl.BlockSpec((1, d, de), wb), pl.BlockSpec((1, d, de), wb),
                  pl.BlockSpec((1, de, d), wb)],
        out_specs=pl.BlockSpec((TM_E, d), lambda i, ea, eb, nv: (i, 0)),
    )
    return pl.pallas_call(
        _experts_kernel,
        grid_spec=grid_spec,
        out_shape=jax.ShapeDtypeStruct((tp, d), F32),
        compiler_params=pltpu.CompilerParams(
            dimension_semantics=("arbitrary",), vmem_limit_bytes=VMEM_LIMIT),
        name="experts",
    )(tile_ea, tile_eb, n_valid, xs, wg, wu, wd, wg, wu, wd)


def _final_kernel(dest_ref, x1_ref, mod_ref, g2_ref, b2_ref, ys_ref, o_ref, ybuf, sem):
    tm = x1_ref.shape[0]
    i = pl.program_id(0)
    slot = i % 2

    def gather(tile, to_slot):
        def issue(r, c):
            pltpu.make_async_copy(ys_ref.at[pl.ds(dest_ref[tile * tm + r], 1)],
                                  ybuf.at[to_slot, pl.ds(r, 1)], sem.at[to_slot]).start()
            return c
        lax.fori_loop(0, tm, issue, 0, unroll=ISSUE_UNROLL)

    @pl.when(i == 0)
    def _():
        gather(0, 0)

    @pl.when(i + 1 < pl.num_programs(0))
    def _():
        gather(i + 1, 1 - slot)

    pltpu.make_async_copy(ys_ref.at[pl.ds(0, tm)], ybuf.at[slot], sem.at[slot]).wait()
    z = DEEPNORM_ALPHA * x1_ref[...] + (1.0 + mod_ref[0, 5:6, :]) * ybuf[slot]
    o_ref[...] = _layer_norm(z) * g2_ref[...] + b2_ref[...]


def _final_call(dest, x1, mod3, g2, b2, ys, seq):
    t, d = x1.shape
    per_b = seq // TM
    grid_spec = pltpu.PrefetchScalarGridSpec(
        num_scalar_prefetch=1,
        grid=(t // TM,),
        in_specs=[pl.BlockSpec((TM, d), lambda i, dr: (i, 0)),
                  pl.BlockSpec((1, 6, d), lambda i, dr: (i // per_b, 0, 0)),
                  pl.BlockSpec((1, d), lambda i, dr: (0, 0)),
                  pl.BlockSpec((1, d), lambda i, dr: (0, 0)),
                  pl.BlockSpec(memory_space=pl.ANY)],
        out_specs=pl.BlockSpec((TM, d), lambda i, dr: (i, 0)),
        scratch_shapes=[pltpu.VMEM((2, TM, d), F32), pltpu.SemaphoreType.DMA((2,))],
    )
    return pl.pallas_call(
        _final_kernel,
        grid_spec=grid_spec,
        out_shape=jax.ShapeDtypeStruct((t, d), F32),
        compiler_params=pltpu.CompilerParams(
            dimension_semantics=("arbitrary",), vmem_limit_bytes=VMEM_LIMIT),
        name="final",
    )(dest, x1, mod3, g2, b2, ys)


def _gate_pairs(w_rg, w_ig):
    def pairs(w):
        n, bs, _ = w.shape
        w = w.reshape(n // 2, 2, bs, bs)
        z = jnp.zeros((n // 2, bs, bs), w.dtype)
        top = jnp.concatenate([w[:, 0], z], axis=2)
        bot = jnp.concatenate([z, w[:, 1]], axis=2)
        return jnp.concatenate([top, bot], axis=1)
    return jnp.concatenate([pairs(w_rg), pairs(w_ig)], axis=2).astype(BF16)


def kernel(x, c, w_ada, b_ada, w_in, b_f, conv_w, conv_b, w_rg, b_rg, w_ig, b_ig, lru_lambda,
           g_attn, g_lru, w_out, ln1_g, ln1_b, w_grp, b_grp, w_exp, b_exp,
           w_e_gate, w_e_up, w_e_down, ln2_g, ln2_b):
    assert w_ada.shape[0] == DEPTH
    b, s, d = x.shape
    t = b * s
    d_attn = N_HEADS * HEAD_DIM
    d_lru = conv_w.shape[2]
    n_exp = w_e_gate.shape[1]
    assert s % TM == 0 and s % TQ == 0 and t % TM_E == 0

    c_pad = jnp.pad(c, ((0, SUBLANES - b), (0, 0)))
    mod = _mod_call(c_pad, w_ada[0], b_ada[0][None, :])
    mod3 = mod[:b].reshape(b, 6, d)

    wi = w_in[0]
    o = 3 * d_attn
    wq = (wi[:, :d_attn] * (HEAD_DIM ** -0.5 * LOG2E)).astype(BF16)
    wk = wi[:, d_attn:2 * d_attn].astype(BF16)
    wvt = wi[:, 2 * d_attn:o].T.astype(BF16)
    wf = jnp.pad(wi[:, o:o + N_HEADS], ((0, 0), (0, LANES - N_HEADS))).astype(BF16)
    wx = wi[:, o + N_HEADS:o + N_HEADS + d_lru].astype(BF16)
    wg = wi[:, o + N_HEADS + d_lru:].astype(BF16)
    bf_pad = jnp.pad(b_f[0], (0, LANES - N_HEADS))[None, :]
    q_aug, k_aug, vt_aug, xb, gb = _inproj_call(x, mod3, wq, wk, wvt, wf, wx, wg, bf_pad)
    attn_t = _attn_call(q_aug, k_aug, vt_aug)
    lru = _lru_call(xb, gb, conv_w[0], conv_b[0][None, :], _gate_pairs(w_rg[0], w_ig[0]),
                    b_rg[0][None, :], b_ig[0][None, :], lru_lambda[0][None, :])

    wo = w_out[0].astype(BF16)
    n_route = N_GROUPS + n_exp
    wr = jnp.pad(jnp.concatenate([w_grp[0], w_exp[0]], axis=1).T,
                 ((0, BUCKET_ROWS - n_route), (0, 0)))
    br = jnp.pad(jnp.concatenate([b_grp[0], b_exp[0]]), (0, BUCKET_ROWS - n_route))
    br = jnp.broadcast_to(br[:, None], (BUCKET_ROWS, LANES))
    x1, u2ext, rinfo, counts = _mixout_call(
        attn_t, lru.reshape(t, d_lru), x.reshape(t, d), mod3,
        g_attn[0][None, :], g_lru[0][None, :], wo[:d_attn], wo[d_attn:],
        ln1_g[0][None, :], ln1_b[0][None, :], wr, br, s)

    dest = _rank_call(rinfo, counts).reshape(t)
    cnt = counts[:N_BUCKETS, 0].astype(jnp.int32)
    ends = jnp.cumsum((cnt + (TM_E - 1)) // TM_E)
    n_tiles = t // TM_E + N_BUCKETS
    tile_bucket = jnp.sum(ends[None, :] <= jnp.arange(n_tiles)[:, None], axis=1)
    tile_bucket = jnp.minimum(tile_bucket, N_BUCKETS - 1)
    n_valid = ends[N_BUCKETS - 1:]
    last_bucket = tile_bucket[jnp.maximum(n_valid[0] - 1, 0)]
    tile_bucket = jnp.where(jnp.arange(n_tiles) < n_valid[0], tile_bucket, last_bucket)
    tile_ea = jnp.asarray(_BUCKET_EA, jnp.int32)[tile_bucket]
    tile_eb = jnp.asarray(_BUCKET_EB, jnp.int32)[tile_bucket]

    xs = _dispatch_call(dest, u2ext, jnp.zeros((n_tiles * TM_E, d + LANES), F32))
    ys = _experts_call(tile_ea, tile_eb, n_valid.astype(jnp.int32), xs,
                       w_e_gate[0].astype(BF16), w_e_up[0].astype(BF16),
                       w_e_down[0].astype(BF16))
    out = _final_call(dest, x1, mod3, ln2_g[0][None, :], ln2_b[0][None, :], ys, s)
    return out.reshape(b, s, d)
```

```python
import functools

import jax
import jax.numpy as jnp
from jax import lax
from jax.experimental import pallas as pl
from jax.experimental.pallas import tpu as pltpu

F32 = jnp.float32
BF16 = jnp.bfloat16

HEAD_DIM = 64
N_HEADS = 8
N_LRU_BLOCKS = 8
CONV_WIDTH = 4
LRU_C = 8.0
N_GROUPS = 4
EXPERTS_PER_GROUP = 4
N_PAIRS = 6
N_BUCKETS = N_GROUPS * N_PAIRS
LN_EPS = 1e-5
RMS_EPS = 1e-6
NEG_INF = -1e30
DEPTH = 1
DEEPNORM_ALPHA = (2.0 * DEPTH) ** 0.25
LOG2E = 1.4426950408889634

LANES = 128
SUBLANES = 8
HEAD_PAD = LANES
VT_ROWS = 80
BUCKET_ROWS = 32
TM = 512
TQ = 512
TM_E = 256
ISSUE_UNROLL = 64
VMEM_LIMIT = 56 * 1024 * 1024

_PAIRS = [(0, 1), (0, 2), (0, 3), (1, 2), (1, 3), (2, 3)]
_BUCKET_EA = [g * EXPERTS_PER_GROUP + a for g in range(N_GROUPS) for (a, b) in _PAIRS]
_BUCKET_EB = [g * EXPERTS_PER_GROUP + b for g in range(N_GROUPS) for (a, b) in _PAIRS]


def _dot(a, b):
    return jnp.dot(a, b, preferred_element_type=F32)


def _split2(a):
    hi = a.astype(BF16)
    lo = (a - hi.astype(F32)).astype(BF16)
    return hi, lo


def _split3(a):
    hi = a.astype(BF16)
    r = a - hi.astype(F32)
    mid = r.astype(BF16)
    lo = (r - mid.astype(F32)).astype(BF16)
    return hi, mid, lo


def _layer_norm(x):
    mu = jnp.mean(x, axis=-1, keepdims=True)
    xc = x - mu
    var = jnp.mean(xc * xc, axis=-1, keepdims=True)
    return xc * lax.rsqrt(var + LN_EPS)


def _sigmoid(x):
    return 0.5 * jnp.tanh(0.5 * x) + 0.5


def _rot_in_group(x, k):
    n, w = x.shape
    return pltpu.roll(x.reshape(n // SUBLANES, SUBLANES, w), k, 1).reshape(n, w)


def _log_sigmoid(z):
    return jnp.minimum(z, 0.0) - jnp.log1p(jnp.exp(-jnp.abs(z)))


def _shift_rows(x, k, fill):
    n = x.shape[0]
    if k % SUBLANES == 0:
        return jnp.concatenate([jnp.full((k, x.shape[1]), fill, x.dtype), x[:n - k]], axis=0)
    row = lax.broadcasted_iota(jnp.int32, x.shape, 0)
    return jnp.where(row >= k, pltpu.roll(x, k, 0), fill)


def _mod_kernel(c_ref, w_ref, b_ref, o_ref):
    c = c_ref[...]
    s = c * _sigmoid(c)
    sh, sl = _split2(s)
    wh, wl = _split2(w_ref[0])
    o_ref[...] = _dot(sh, wh) + (_dot(sh, wl) + _dot(sl, wh)) + b_ref[...]


def _mod_call(c_pad, w_ada, b_ada):
    rows, d = c_pad.shape
    n = w_ada.shape[2]
    return pl.pallas_call(
        _mod_kernel,
        grid=(n // d,),
        in_specs=[pl.BlockSpec((rows, d), lambda j: (0, 0)),
                  pl.BlockSpec((1, d, d), lambda j: (0, 0, j)),
                  pl.BlockSpec((1, d), lambda j: (0, j))],
        out_specs=pl.BlockSpec((rows, d), lambda j: (0, j)),
        out_shape=jax.ShapeDtypeStruct((rows, n), F32),
        compiler_params=pltpu.CompilerParams(vmem_limit_bytes=VMEM_LIMIT),
        name="mod",
    )(c_pad, w_ada, b_ada)


def _inproj_kernel(x_ref, mod_ref, wq_ref, wk_ref, wvt_ref, wf_ref, wx_ref, wg_ref, bf_ref,
                   q_ref, k_ref, vt_ref, xb_ref, gb_ref, carry_ref):
    j = pl.program_id(1)

    @pl.when(j == 0)
    def _():
        carry_ref[...] = jnp.zeros_like(carry_ref)

    tm = x_ref.shape[1]
    u = _layer_norm(x_ref[0]) * (1.0 + mod_ref[0, 1:2, :]) + mod_ref[0, 0:1, :]
    ub = u.astype(BF16)

    logf = _log_sigmoid(_dot(ub, wf_ref[...]) + bf_ref[...])
    row = lax.broadcasted_iota(jnp.int32, (tm, tm), 0)
    col = lax.broadcasted_iota(jnp.int32, (tm, tm), 1)
    tril = (col <= row).astype(BF16)
    l_hi, l_mid, l_lo = _split3(logf)
    cum = (_dot(tril, l_hi) + (_dot(tril, l_mid) + _dot(tril, l_lo))) + carry_ref[...]
    carry_ref[...] = cum[tm - 1:tm, :]
    c_hi, c_mid, c_lo = [p.astype(F32) for p in _split3(cum * LOG2E)]

    lane = lax.broadcasted_iota(jnp.int32, (tm, HEAD_PAD), 1)
    d = HEAD_DIM
    q_all = _dot(ub, wq_ref[...])
    k_all = _dot(ub, wk_ref[...])
    for h in range(N_HEADS):
        ts = slice((h // 2) * LANES, (h // 2 + 1) * LANES)
        qh, kh = q_all[:, ts], k_all[:, ts]
        if h % 2:
            qh, kh = pltpu.roll(qh, d, 1), pltpu.roll(kh, d, 1)
        hi, mid, lo = c_hi[:, h:h + 1], c_mid[:, h:h + 1], c_lo[:, h:h + 1]
        q_ext = jnp.where(lane == d, hi, jnp.where(lane == d + 1, mid, jnp.where(
            lane == d + 2, lo, jnp.where(lane < d + 6, 1.0, 0.0))))
        k_ext = jnp.where(lane < d + 3, 1.0, jnp.where(lane == d + 3, -hi, jnp.where(
            lane == d + 4, -mid, jnp.where(lane == d + 5, -lo, 0.0))))
        q_ref[0, h] = jnp.where(lane < d, qh, q_ext).astype(BF16)
        k_ref[0, h] = jnp.where(lane < d, kh, k_ext).astype(BF16)

    vt_all = lax.dot_general(wvt_ref[...], ub, (((1,), (1,)), ((), ())),
                             preferred_element_type=F32)
    pad_rows = lax.broadcasted_iota(jnp.int32, (VT_ROWS - d, tm), 0)
    ones_row = jnp.where(pad_rows == 0, 1.0, 0.0).astype(BF16)
    for h in range(N_HEADS):
        vt_ref[0, h, 0, :d, :] = vt_all[h * d:(h + 1) * d, :].astype(BF16)
        vt_ref[0, h, 0, d:, :] = ones_row

    xb_ref[0] = _dot(ub, wx_ref[...])
    gb_ref[0] = _dot(ub, wg_ref[...])


def _inproj_call(x, mod3, wq, wk, wvt, wf, wx, wg, bf_pad):
    b, s, d = x.shape
    d_attn = wq.shape[1]
    d_lru = wx.shape[1]
    full = lambda shape: pl.BlockSpec(shape, lambda bi, j: (0,) * len(shape))
    head_spec = pl.BlockSpec((1, N_HEADS, TM, HEAD_PAD), lambda bi, j: (bi, 0, j, 0))
    vt_spec = pl.BlockSpec((1, N_HEADS, 1, VT_ROWS, TM), lambda bi, j: (bi, 0, j, 0, 0))
    row_spec = pl.BlockSpec((1, TM, d_lru), lambda bi, j: (bi, j, 0))
    head_shape = jax.ShapeDtypeStruct((b, N_HEADS, s, HEAD_PAD), BF16)
    return pl.pallas_call(
        _inproj_kernel,
        grid=(b, s // TM),
        in_specs=[pl.BlockSpec((1, TM, d), lambda bi, j: (bi, j, 0)),
                  pl.BlockSpec((1, 6, d), lambda bi, j: (bi, 0, 0)),
                  full((d, d_attn)), full((d, d_attn)), full((d_attn, d)), full((d, LANES)),
                  full((d, d_lru)), full((d, d_lru)), full((1, LANES))],
        out_specs=[head_spec, head_spec, vt_spec, row_spec, row_spec],
        out_shape=[head_shape, head_shape,
                   jax.ShapeDtypeStruct((b, N_HEADS, s // TM, VT_ROWS, TM), BF16),
                   jax.ShapeDtypeStruct((b, s, d_lru), F32),
                   jax.ShapeDtypeStruct((b, s, d_lru), F32)],
        scratch_shapes=[pltpu.VMEM((1, LANES), F32)],
        compiler_params=pltpu.CompilerParams(
            dimension_semantics=("arbitrary", "arbitrary"), vmem_limit_bytes=VMEM_LIMIT),
        name="inproj",
    )(x, mod3, wq, wk, wvt, wf, wx, wg, bf_pad)


def _attn_kernel(q_ref, k_ref, vt_ref, o_ref, s_scr, m_scr, acc_scr):
    tq = TQ
    nq = q_ref.shape[2] // tq
    heads = range(q_ref.shape[1])

    def scores_to(slot, i, j):
        for hh in heads:
            k = k_ref[0, hh, pl.ds(pl.multiple_of(j * tq, tq), tq), :]
            q = q_ref[0, hh, pl.ds(pl.multiple_of(i * tq, tq), tq), :]
            s_scr[slot, hh] = lax.dot_general(
                k, q, (((1,), (1,)), ((), ())), preferred_element_type=F32)

    def consume(slot, j, masked):
        for hh in heads:
            s = s_scr[slot, hh]
            if masked:
                key = lax.broadcasted_iota(jnp.int32, (tq, tq), 0)
                qry = lax.broadcasted_iota(jnp.int32, (tq, tq), 1)
                s = jnp.where(key <= qry, s, NEG_INF)
            m = m_scr[hh]
            m_new = jnp.maximum(m, jnp.max(s, axis=0, keepdims=True))
            p = jnp.exp2(s - m_new).astype(BF16)
            acc_scr[hh] = jnp.exp2(m - m_new) * acc_scr[hh] + _dot(vt_ref[0, hh, j], p)
            m_scr[hh] = m_new

    scores_to(2, 0, 0)

    def query_block(i, carry):
        m_scr[...] = jnp.full(m_scr.shape, NEG_INF, F32)
        acc_scr[...] = jnp.zeros(acc_scr.shape, F32)
        nxt = jnp.minimum(i + 1, nq - 1)
        n_mid = i - 1

        @pl.when(i == 0)
        def _():
            consume(2, 0, True)
            scores_to(2, nxt, 0)

        @pl.when(i >= 1)
        def _():
            scores_to(1, i, 1)
            consume(2, 0, False)

        def pair(kk, c):
            j = 1 + 2 * kk
            scores_to(0, i, j + 1)
            consume(1, j, False)
            scores_to(1, i, j + 2)
            consume(0, j + 1, False)
            return c

        lax.fori_loop(0, jnp.maximum(n_mid, 0) // 2, pair, 0)

        @pl.when(jnp.logical_and(i >= 1, n_mid % 2 == 1))
        def _():
            scores_to(0, i, i)
            consume(1, i - 1, False)
            scores_to(2, nxt, 0)
            consume(0, i, True)

        @pl.when(jnp.logical_and(i >= 1, n_mid % 2 == 0))
        def _():
            scores_to(2, nxt, 0)
            consume(1, i, True)

        for hh in heads:
            o_ref[0, 0, i, hh * HEAD_DIM:(hh + 1) * HEAD_DIM, :] = (
                acc_scr[hh, :HEAD_DIM, :] / acc_scr[hh, HEAD_DIM:HEAD_DIM + 1, :])
        return carry

    lax.fori_loop(0, nq, query_block, 0)


def _attn_call(q_aug, k_aug, vt_aug):
    b, h, s, hp = q_aug.shape
    assert TQ == TM
    nq = s // TQ
    return pl.pallas_call(
        _attn_kernel,
        grid=(b, h // 2),
        in_specs=[pl.BlockSpec((1, 2, s, hp), lambda bi, p: (bi, p, 0, 0)),
                  pl.BlockSpec((1, 2, s, hp), lambda bi, p: (bi, p, 0, 0)),
                  pl.BlockSpec((1, 2, nq, VT_ROWS, TQ), lambda bi, p: (bi, p, 0, 0, 0))],
        out_specs=pl.BlockSpec((1, 1, nq, 2 * HEAD_DIM, TQ), lambda bi, p: (bi, p, 0, 0, 0)),
        out_shape=jax.ShapeDtypeStruct((b, h // 2, nq, 2 * HEAD_DIM, TQ), F32),
        scratch_shapes=[pltpu.VMEM((3, 2, TQ, TQ), F32), pltpu.VMEM((2, 1, TQ), F32),
                        pltpu.VMEM((2, VT_ROWS, TQ), F32)],
        compiler_params=pltpu.CompilerParams(
            dimension_semantics=("arbitrary", "arbitrary"), vmem_limit_bytes=VMEM_LIMIT),
        name="attention",
    )(q_aug, k_aug, vt_aug)


def _gelu_tanh(x):
    return 0.5 * x * (1.0 + jnp.tanh(0.7978845608028654 * (x + 0.044715 * (x * x * x))))


def _lru_kernel(xb_ref, gb_ref, cw_ref, cb_ref, wgate_ref, brg_ref, big_ref, lam_ref,
                o_ref, tail_ref, h_ref):
    j = pl.program_id(1)

    @pl.when(j == 0)
    def _():
        tail_ref[...] = jnp.zeros_like(tail_ref)
        h_ref[...] = jnp.zeros_like(h_ref)

    x = xb_ref[0]
    tm, dl = x.shape
    row_in_group = lax.broadcasted_iota(jnp.int32, (tm, dl), 0) % SUBLANES
    x_prev_group = jnp.concatenate([tail_ref[...], x[:tm - SUBLANES]], axis=0)
    xc = x * cw_ref[CONV_WIDTH - 1:CONV_WIDTH, :] + cb_ref[...]
    for k in range(1, CONV_WIDTH):
        xs = jnp.where(row_in_group < k, _rot_in_group(x_prev_group, k), _rot_in_group(x, k))
        xc = xc + xs * cw_ref[CONV_WIDTH - 1 - k:CONV_WIDTH - k, :]
    tail_ref[...] = x[tm - SUBLANES:, :]

    xcb = xc.astype(BF16)
    n_pairs = dl // LANES
    r_parts, i_parts = [], []
    for p in range(n_pairs):
        g = _dot(xcb[:, p * LANES:(p + 1) * LANES], wgate_ref[p])
        r_parts.append(g[:, :LANES])
        i_parts.append(g[:, LANES:])
    r = _sigmoid(jnp.concatenate(r_parts, axis=1) + brg_ref[...])
    ig = _sigmoid(jnp.concatenate(i_parts, axis=1) + big_ref[...])

    lam = lam_ref[...]
    softplus_neg_lam = jnp.maximum(-lam, 0.0) + jnp.log1p(jnp.exp(-jnp.abs(lam)))
    log_a = (-LRU_C) * r * softplus_neg_lam
    a = jnp.exp(log_a)
    v = 1.0 - a * a
    u = jnp.where(v > 0.0, v * lax.rsqrt(v), 0.0) * (ig * xc)

    k = 1
    while k < SUBLANES:
        keep = row_in_group >= k
        u = a * jnp.where(keep, _rot_in_group(u, k), 0.0) + u
        a = a * jnp.where(keep, _rot_in_group(a, k), 1.0)
        k *= 2
    h_prev = h_ref[...]
    groups = []
    for g in range(tm // SUBLANES):
        rows = slice(g * SUBLANES, (g + 1) * SUBLANES)
        hg = a[rows] * h_prev + u[rows]
        groups.append(hg)
        h_prev = hg[SUBLANES - 1:SUBLANES]
    h_ref[...] = h_prev
    o_ref[0] = jnp.concatenate(groups, axis=0) * _gelu_tanh(gb_ref[0])


def _lru_call(xb, gb, conv_w, conv_b, wgate, b_rg, b_ig, lam):
    b, s, dl = xb.shape
    row_spec = pl.BlockSpec((1, TM, dl), lambda bi, j: (bi, j, 0))
    full = lambda shape: pl.BlockSpec(shape, lambda bi, j: (0,) * len(shape))
    return pl.pallas_call(
        _lru_kernel,
        grid=(b, s // TM),
        in_specs=[row_spec, row_spec, full(conv_w.shape), full((1, dl)), full(wgate.shape),
                  full((1, dl)), full((1, dl)), full((1, dl))],
        out_specs=row_spec,
        out_shape=jax.ShapeDtypeStruct((b, s, dl), F32),
        scratch_shapes=[pltpu.VMEM((SUBLANES, dl), F32), pltpu.VMEM((1, dl), F32)],
        compiler_params=pltpu.CompilerParams(
            dimension_semantics=("arbitrary", "arbitrary"), vmem_limit_bytes=VMEM_LIMIT),
        name="lru",
    )(xb, gb, conv_w, conv_b, wgate, b_rg, b_ig, lam)


def _rms(x, gain):
    return x * lax.rsqrt(jnp.mean(x * x, axis=-1, keepdims=True) + RMS_EPS) * gain


def _mixout_kernel(attn_ref, lru_ref, x_ref, mod_ref, ga_ref, gl_ref, woa_ref, wol_ref,
                   g1_ref, b1_ref, wr_ref, br_ref, x1_ref, u2_ref, rinfo_ref, cnt_ref):
    tm, d = x_ref.shape
    attn = attn_ref[0, :, 0].reshape(ga_ref.shape[1], tm).T
    na = _rms(attn, ga_ref[...]).astype(BF16)
    nl = _rms(lru_ref[...], gl_ref[...]).astype(BF16)
    mix = _dot(na, woa_ref[...]) + _dot(nl, wol_ref[...])
    z = DEEPNORM_ALPHA * x_ref[...] + (1.0 + mod_ref[0, 2:3, :]) * mix
    x1 = _layer_norm(z) * g1_ref[...] + b1_ref[...]
    x1_ref[...] = x1
    u2 = _layer_norm(x1) * (1.0 + mod_ref[0, 4:5, :]) + mod_ref[0, 3:4, :]

    nt = (((1,), (1,)), ((), ()))
    uh, ul = _split2(u2)
    wh, wl = _split2(wr_ref[...])
    dg = lambda a, b_: lax.dot_general(a, b_, nt, preferred_element_type=F32)
    lg = dg(wh, uh) + (dg(wh, ul) + dg(wl, uh)) + br_ref[:, 0:1]

    def first_index(vals, target):
        idx = jnp.full_like(target, float(len(vals) - 1))
        for n in range(len(vals) - 2, -1, -1):
            idx = jnp.where(vals[n] == target, float(n), idx)
        return idx

    g = [lg[n:n + 1, :] for n in range(N_GROUPS)]
    gmax = functools.reduce(jnp.maximum, g)
    gsum = functools.reduce(lambda a, b_: a + b_, [jnp.exp(v - gmax) for v in g])
    grp_w = 1.0 / gsum
    gidx = first_index(g, gmax)

    sel = []
    for e in range(EXPERTS_PER_GROUP):
        v = lg[N_GROUPS + (N_GROUPS - 1) * EXPERTS_PER_GROUP + e:
               N_GROUPS + (N_GROUPS - 1) * EXPERTS_PER_GROUP + e + 1, :]
        for gi in range(N_GROUPS - 2, -1, -1):
            r0 = N_GROUPS + gi * EXPERTS_PER_GROUP + e
            v = jnp.where(gidx == float(gi), lg[r0:r0 + 1, :], v)
        sel.append(v)
    smax = functools.reduce(jnp.maximum, sel)
    i1 = first_index(sel, smax)
    rest = [jnp.where(i1 == float(e), -3e38, sel[e]) for e in range(EXPERTS_PER_GROUP)]
    rmax = functools.reduce(jnp.maximum, rest)
    i2 = first_index(rest, rmax)
    e2 = jnp.exp(rmax - smax)
    w1 = grp_w / (1.0 + e2)
    w2 = grp_w * e2 / (1.0 + e2)
    ia = jnp.minimum(i1, i2)
    ib = jnp.maximum(i1, i2)
    wa = jnp.where(i1 < i2, w1, w2)
    wb = jnp.where(i1 < i2, w2, w1)
    pair = jnp.where(ia == 0.0, ib - 1.0, jnp.where(ia == 1.0, ib + 1.0, 5.0))
    bucket = gidx * float(N_PAIRS) + pair

    @pl.when(pl.program_id(0) == 0)
    def _():
        cnt_ref[...] = jnp.zeros_like(cnt_ref)

    cid = lax.broadcasted_iota(jnp.int32, (BUCKET_ROWS, tm), 0).astype(F32)
    cnt_ref[...] += jnp.sum(jnp.where(cid == bucket, 1.0, 0.0), axis=1, keepdims=True)

    zrow = jnp.zeros_like(wa)
    rinfo_ref[...] = jnp.concatenate([bucket, wa, wb] + [zrow] * (SUBLANES - 3), axis=0)
    wt = jnp.concatenate([wa, wb, jnp.zeros((LANES - 2, tm), F32)], axis=0)
    u2_ref[:, :d] = u2
    u2_ref[:, d:] = wt.T


def _mixout_call(attn_t, lru, x2d, mod3, ga, gl, woa, wol, g1, b1, wr, br, seq):
    t, d = x2d.shape
    _, n_pairs, _, pair_w, _ = attn_t.shape
    dh = n_pairs * pair_w
    per_b = seq // TM
    full = lambda shape: pl.BlockSpec(shape, lambda i: (0,) * len(shape))
    return pl.pallas_call(
        _mixout_kernel,
        grid=(t // TM,),
        in_specs=[pl.BlockSpec((1, n_pairs, 1, pair_w, TM),
                               lambda i: (i // per_b, 0, i % per_b, 0, 0)),
                  pl.BlockSpec((TM, dh), lambda i: (i, 0)),
                  pl.BlockSpec((TM, d), lambda i: (i, 0)),
                  pl.BlockSpec((1, 6, d), lambda i: (i // per_b, 0, 0)),
                  full((1, dh)), full((1, dh)), full((dh, d)), full((dh, d)),
                  full((1, d)), full((1, d)), full((BUCKET_ROWS, d)), full((BUCKET_ROWS, LANES))],
        out_specs=[pl.BlockSpec((TM, d), lambda i: (i, 0)),
                   pl.BlockSpec((TM, d + LANES), lambda i: (i, 0)),
                   pl.BlockSpec((SUBLANES, TM), lambda i: (0, i)),
                   pl.BlockSpec((BUCKET_ROWS, LANES), lambda i: (0, 0))],
        out_shape=[jax.ShapeDtypeStruct((t, d), F32),
                   jax.ShapeDtypeStruct((t, d + LANES), F32),
                   jax.ShapeDtypeStruct((SUBLANES, t), F32),
                   jax.ShapeDtypeStruct((BUCKET_ROWS, LANES), F32)],
        compiler_params=pltpu.CompilerParams(
            dimension_semantics=("arbitrary",), vmem_limit_bytes=VMEM_LIMIT),
        name="mixout",
    )(attn_t, lru, x2d, mod3, ga, gl, woa, wol, g1, b1, wr, br)


def _rank_kernel(rinfo_ref, counts_ref, dest_ref, carry_ref, offs_ref):
    tm = rinfo_ref.shape[1]

    @pl.when(pl.program_id(0) == 0)
    def _():
        carry_ref[...] = jnp.zeros_like(carry_ref)
        padded = jnp.floor((counts_ref[...] + float(TM_E - 1)) * (1.0 / TM_E)) * float(TM_E)
        inc = padded
        k = 1
        while k < BUCKET_ROWS:
            inc = inc + _shift_rows(inc, k, 0.0)
            k *= 2
        offs_ref[...] = inc - padded

    bucket = rinfo_ref[0:1, :]
    cid = lax.broadcasted_iota(jnp.int32, (BUCKET_ROWS, tm), 0).astype(F32)
    onehot = jnp.where(cid == bucket, 1.0, 0.0)
    srow = lax.broadcasted_iota(jnp.int32, (tm, tm), 0)
    scol = lax.broadcasted_iota(jnp.int32, (tm, tm), 1)
    upper = (srow <= scol).astype(BF16)
    prefix = _dot(onehot.astype(BF16), upper)
    carry = carry_ref[...]
    rank = jnp.sum(onehot * (prefix - 1.0 + carry[:, 0:1] + offs_ref[:, 0:1]),
                   axis=0, keepdims=True)
    dest_ref[...] = rank.astype(jnp.int32)
    carry_ref[...] = carry + prefix[:, tm - 1:tm]


def _rank_call(rinfo, counts):
    t = rinfo.shape[1]
    return pl.pallas_call(
        _rank_kernel,
        grid=(t // TM,),
        in_specs=[pl.BlockSpec((SUBLANES, TM), lambda i: (0, i)),
                  pl.BlockSpec((BUCKET_ROWS, LANES), lambda i: (0, 0))],
        out_specs=pl.BlockSpec((1, TM), lambda i: (0, i)),
        out_shape=jax.ShapeDtypeStruct((1, t), jnp.int32),
        scratch_shapes=[pltpu.VMEM((BUCKET_ROWS, LANES), F32),
                        pltpu.VMEM((BUCKET_ROWS, LANES), F32)],
        compiler_params=pltpu.CompilerParams(
            dimension_semantics=("arbitrary",), vmem_limit_bytes=VMEM_LIMIT),
        name="rank",
    )(rinfo, counts)


def _dispatch_kernel(dest_ref, ends_ref, u2_ref, xs_ref, zbuf, sem, zsem):
    tm = u2_ref.shape[0]
    t0 = pl.program_id(0) * tm

    @pl.when(pl.program_id(0) == 0)
    def _():
        zbuf[...] = jnp.zeros_like(zbuf)

        def tail_copy(bkt):
            end = ends_ref[bkt]
            start = ends_ref[bkt - 1] if bkt else 0
            tail = pl.multiple_of(jnp.maximum(end - TM_E, 0), TM_E)
            return end > start, pltpu.make_async_copy(zbuf, xs_ref.at[pl.ds(tail, TM_E)], zsem)

        for bkt in range(N_BUCKETS):
            nonempty, copy = tail_copy(bkt)
            pl.when(nonempty)(copy.start)
        for bkt in range(N_BUCKETS):
            nonempty, copy = tail_copy(bkt)
            pl.when(nonempty)(copy.wait)

        def unused_tile_copy(k):
            return pltpu.make_async_copy(
                zbuf, xs_ref.at[pl.ds(pl.multiple_of(k * TM_E, TM_E), TM_E)], zsem)

        first_unused = ends_ref[N_BUCKETS - 1] // TM_E
        n_tiles = xs_ref.shape[0] // TM_E
        lax.fori_loop(first_unused, n_tiles, lambda k, c: (unused_tile_copy(k).start(), c)[1], 0)
        lax.fori_loop(first_unused, n_tiles, lambda k, c: (unused_tile_copy(k).wait(), c)[1], 0)

    def issue(r, c):
        pltpu.make_async_copy(u2_ref.at[pl.ds(r, 1)],
                              xs_ref.at[pl.ds(dest_ref[t0 + r], 1)], sem).start()
        return c

    lax.fori_loop(0, tm, issue, 0, unroll=ISSUE_UNROLL)
    pltpu.make_async_copy(u2_ref, xs_ref.at[pl.ds(0, tm)], sem).wait()


def _dispatch_call(dest, bucket_ends, u2ext, n_rows):
    t, w = u2ext.shape
    grid_spec = pltpu.PrefetchScalarGridSpec(
        num_scalar_prefetch=2,
        grid=(t // TM,),
        in_specs=[pl.BlockSpec((TM, w), lambda i, dest_ref, ends_ref: (i, 0))],
        out_specs=pl.BlockSpec(memory_space=pl.ANY),
        scratch_shapes=[pltpu.VMEM((TM_E, w), F32), pltpu.SemaphoreType.DMA(()),
                        pltpu.SemaphoreType.DMA(())],
    )
    return pl.pallas_call(
        _dispatch_kernel,
        grid_spec=grid_spec,
        out_shape=jax.ShapeDtypeStruct((n_rows, w), F32),
        compiler_params=pltpu.CompilerParams(
            dimension_semantics=("arbitrary",), vmem_limit_bytes=VMEM_LIMIT),
        name="dispatch",
    )(dest, bucket_ends, u2ext)


def _experts_kernel(ea_ref, eb_ref, nv_ref, xs_ref, wga_ref, wua_ref, wda_ref,
                    wgb_ref, wub_ref, wdb_ref, ys_ref):
    del ea_ref, eb_ref
    i = pl.program_id(0)
    d = ys_ref.shape[1]

    @pl.when(i < nv_ref[0])
    def _():
        x = xs_ref[:, :d].astype(BF16)

        def expert(wg_ref, wu_ref, wd_ref):
            g = _dot(x, wg_ref[0])
            h = (g * _sigmoid(g)) * _dot(x, wu_ref[0])
            return _dot(h.astype(BF16), wd_ref[0])

        ya = xs_ref[:, d:d + 1] * expert(wga_ref, wua_ref, wda_ref)
        ys_ref[...] = ya + xs_ref[:, d + 1:d + 2] * expert(wgb_ref, wub_ref, wdb_ref)

    @pl.when(i >= nv_ref[0])
    def _():
        ys_ref[...] = jnp.zeros_like(ys_ref)


def _experts_call(tile_ea, tile_eb, n_valid, xs, wg, wu, wd):
    tp, w = xs.shape
    _, d, de = wg.shape
    row = lambda i, ea, eb, nv: (jnp.minimum(i, nv[0] - 1), 0)
    wa = lambda i, ea, eb, nv: (ea[i], 0, 0)
    wb = lambda i, ea, eb, nv: (eb[i], 0, 0)
    grid_spec = pltpu.PrefetchScalarGridSpec(
        num_scalar_prefetch=3,
        grid=(tp // TM_E,),
        in_specs=[pl.BlockSpec((TM_E, w), row),
                  pl.BlockSpec((1, d, de), wa), pl.BlockSpec((1, d, de), wa),
                  pl.BlockSpec((1, de, d), wa),
                  pl.BlockSpec((1, d, de), wb), pl.BlockSpec((1, d, de), wb),
                  pl.BlockSpec((1, de, d), wb)],
        out_specs=pl.BlockSpec((TM_E, d), lambda i, ea, eb, nv: (i, 0)),
    )
    return pl.pallas_call(
        _experts_kernel,
        grid_spec=grid_spec,
        out_shape=jax.ShapeDtypeStruct((tp, d), F32),
        compiler_params=pltpu.CompilerParams(
            dimension_semantics=("arbitrary",), vmem_limit_bytes=VMEM_LIMIT),
        name="experts",
    )(tile_ea, tile_eb, n_valid, xs, wg, wu, wd, wg, wu, wd)


def _final_kernel(dest_ref, x1_ref, mod_ref, g2_ref, b2_ref, ys_ref, o_ref, ybuf, sem):
    tm = x1_ref.shape[0]
    i = pl.program_id(0)
    slot = i % 2

    def gather(tile, to_slot):
        def issue(r, c):
            pltpu.make_async_copy(ys_ref.at[pl.ds(dest_ref[tile * tm + r], 1)],
                                  ybuf.at[to_slot, pl.ds(r, 1)], sem.at[to_slot]).start()
            return c
        lax.fori_loop(0, tm, issue, 0, unroll=ISSUE_UNROLL)

    @pl.when(i == 0)
    def _():
        gather(0, 0)

    @pl.when(i + 1 < pl.num_programs(0))
    def _():
        gather(i + 1, 1 - slot)

    pltpu.make_async_copy(ys_ref.at[pl.ds(0, tm)], ybuf.at[slot], sem.at[slot]).wait()
    z = DEEPNORM_ALPHA * x1_ref[...] + (1.0 + mod_ref[0, 5:6, :]) * ybuf[slot]
    o_ref[...] = _layer_norm(z) * g2_ref[...] + b2_ref[...]


def _final_call(dest, x1, mod3, g2, b2, ys, seq):
    t, d = x1.shape
    per_b = seq // TM
    grid_spec = pltpu.PrefetchScalarGridSpec(
        num_scalar_prefetch=1,
        grid=(t // TM,),
        in_specs=[pl.BlockSpec((TM, d), lambda i, dr: (i, 0)),
                  pl.BlockSpec((1, 6, d), lambda i, dr: (i // per_b, 0, 0)),
                  pl.BlockSpec((1, d), lambda i, dr: (0, 0)),
                  pl.BlockSpec((1, d), lambda i, dr: (0, 0)),
                  pl.BlockSpec(memory_space=pl.ANY)],
        out_specs=pl.BlockSpec((TM, d), lambda i, dr: (i, 0)),
        scratch_shapes=[pltpu.VMEM((2, TM, d), F32), pltpu.SemaphoreType.DMA((2,))],
    )
    return pl.pallas_call(
        _final_kernel,
        grid_spec=grid_spec,
        out_shape=jax.ShapeDtypeStruct((t, d), F32),
        compiler_params=pltpu.CompilerParams(
            dimension_semantics=("arbitrary",), vmem_limit_bytes=VMEM_LIMIT),
        name="final",
    )(dest, x1, mod3, g2, b2, ys)


def _gate_pairs(w_rg, w_ig):
    def pairs(w):
        n, bs, _ = w.shape
        w = w.reshape(n // 2, 2, bs, bs)
        z = jnp.zeros((n // 2, bs, bs), w.dtype)
        top = jnp.concatenate([w[:, 0], z], axis=2)
        bot = jnp.concatenate([z, w[:, 1]], axis=2)
        return jnp.concatenate([top, bot], axis=1)
    return jnp.concatenate([pairs(w_rg), pairs(w_ig)], axis=2).astype(BF16)


def kernel(x, c, w_ada, b_ada, w_in, b_f, conv_w, conv_b, w_rg, b_rg, w_ig, b_ig, lru_lambda,
           g_attn, g_lru, w_out, ln1_g, ln1_b, w_grp, b_grp, w_exp, b_exp,
           w_e_gate, w_e_up, w_e_down, ln2_g, ln2_b):
    assert w_ada.shape[0] == DEPTH
    b, s, d = x.shape
    t = b * s
    d_attn = N_HEADS * HEAD_DIM
    d_lru = conv_w.shape[2]
    n_exp = w_e_gate.shape[1]
    assert s % TM == 0 and s % TQ == 0 and t % TM_E == 0

    c_pad = jnp.pad(c, ((0, SUBLANES - b), (0, 0)))
    mod = _mod_call(c_pad, w_ada, b_ada[0][None, :])
    mod3 = mod[:b].reshape(b, 6, d)

    wi = w_in[0]
    o = 3 * d_attn
    wq = (wi[:, :d_attn] * (HEAD_DIM ** -0.5 * LOG2E)).astype(BF16)
    wk = wi[:, d_attn:2 * d_attn].astype(BF16)
    wvt = wi[:, 2 * d_attn:o].T.astype(BF16)
    wf = jnp.pad(wi[:, o:o + N_HEADS], ((0, 0), (0, LANES - N_HEADS))).astype(BF16)
    wx = wi[:, o + N_HEADS:o + N_HEADS + d_lru].astype(BF16)
    wg = wi[:, o + N_HEADS + d_lru:].astype(BF16)
    bf_pad = jnp.pad(b_f[0], (0, LANES - N_HEADS))[None, :]
    q_aug, k_aug, vt_aug, xb, gb = _inproj_call(x, mod3, wq, wk, wvt, wf, wx, wg, bf_pad)
    attn_t = _attn_call(q_aug, k_aug, vt_aug)
    lru = _lru_call(xb, gb, conv_w[0], conv_b[0][None, :], _gate_pairs(w_rg[0], w_ig[0]),
                    b_rg[0][None, :], b_ig[0][None, :], lru_lambda[0][None, :])

    wo = w_out[0].astype(BF16)
    n_route = N_GROUPS + n_exp
    wr = jnp.pad(jnp.concatenate([w_grp[0], w_exp[0]], axis=1).T,
                 ((0, BUCKET_ROWS - n_route), (0, 0)))
    br = jnp.pad(jnp.concatenate([b_grp[0], b_exp[0]]), (0, BUCKET_ROWS - n_route))
    br = jnp.broadcast_to(br[:, None], (BUCKET_ROWS, LANES))
    x1, u2ext, rinfo, counts = _mixout_call(
        attn_t, lru.reshape(t, d_lru), x.reshape(t, d), mod3,
        g_attn[0][None, :], g_lru[0][None, :], wo[:d_attn], wo[d_attn:],
        ln1_g[0][None, :], ln1_b[0][None, :], wr, br, s)

    dest = _rank_call(rinfo, counts).reshape(t)
    cnt = counts[:N_BUCKETS, 0].astype(jnp.int32)
    ends = jnp.cumsum((cnt + (TM_E - 1)) // TM_E)
    n_tiles = t // TM_E + N_BUCKETS
    tile_bucket = jnp.sum(ends[None, :] <= jnp.arange(n_tiles)[:, None], axis=1)
    tile_bucket = jnp.minimum(tile_bucket, N_BUCKETS - 1)
    n_valid = ends[N_BUCKETS - 1:]
    last_bucket = tile_bucket[jnp.maximum(n_valid[0] - 1, 0)]
    tile_bucket = jnp.where(jnp.arange(n_tiles) < n_valid[0], tile_bucket, last_bucket)
    tile_ea = jnp.asarray(_BUCKET_EA, jnp.int32)[tile_bucket]
    tile_eb = jnp.asarray(_BUCKET_EB, jnp.int32)[tile_bucket]

    xs = _dispatch_call(dest, (ends * TM_E).astype(jnp.int32), u2ext, n_tiles * TM_E)
    ys = _experts_call(tile_ea, tile_eb, n_valid.astype(jnp.int32), xs,
                       w_e_gate[0].astype(BF16), w_e_up[0].astype(BF16),
                       w_e_down[0].astype(BF16))
    out = _final_call(dest, x1, mod3, ln2_g[0][None, :], ln2_b[0][None, :], ys, s)
    return out.reshape(b, s, d)
```

```python
import functools

import jax
import jax.numpy as jnp
from jax import lax
from jax.experimental import pallas as pl
from jax.experimental.pallas import tpu as pltpu

F32 = jnp.float32
BF16 = jnp.bfloat16

HEAD_DIM = 64
N_HEADS = 8
N_LRU_BLOCKS = 8
CONV_WIDTH = 4
LRU_C = 8.0
N_GROUPS = 4
EXPERTS_PER_GROUP = 4
N_PAIRS = 6
N_BUCKETS = N_GROUPS * N_PAIRS
LN_EPS = 1e-5
RMS_EPS = 1e-6
NEG_INF = -1e30
DEPTH = 1
DEEPNORM_ALPHA = (2.0 * DEPTH) ** 0.25
LOG2E = 1.4426950408889634

LANES = 128
SUBLANES = 8
HEAD_PAD = LANES
VT_ROWS = 80
BUCKET_ROWS = 32
TM = 512
TQ = 512
TM_E = 256
ISSUE_UNROLL = 64
GATHER_AHEAD = 2
VMEM_LIMIT = 56 * 1024 * 1024

_PAIRS = [(0, 1), (0, 2), (0, 3), (1, 2), (1, 3), (2, 3)]
_BUCKET_EA = [g * EXPERTS_PER_GROUP + a for g in range(N_GROUPS) for (a, b) in _PAIRS]
_BUCKET_EB = [g * EXPERTS_PER_GROUP + b for g in range(N_GROUPS) for (a, b) in _PAIRS]


def _dot(a, b):
    return jnp.dot(a, b, preferred_element_type=F32)


def _split2(a):
    hi = a.astype(BF16)
    lo = (a - hi.astype(F32)).astype(BF16)
    return hi, lo


def _split3(a):
    hi = a.astype(BF16)
    r = a - hi.astype(F32)
    mid = r.astype(BF16)
    lo = (r - mid.astype(F32)).astype(BF16)
    return hi, mid, lo


def _layer_norm(x):
    mu = jnp.mean(x, axis=-1, keepdims=True)
    xc = x - mu
    var = jnp.mean(xc * xc, axis=-1, keepdims=True)
    return xc * lax.rsqrt(var + LN_EPS)


def _sigmoid(x):
    return 0.5 * jnp.tanh(0.5 * x) + 0.5


def _rot_in_group(x, k):
    n, w = x.shape
    return pltpu.roll(x.reshape(n // SUBLANES, SUBLANES, w), k, 1).reshape(n, w)


def _log_sigmoid(z):
    return jnp.minimum(z, 0.0) - jnp.log1p(jnp.exp(-jnp.abs(z)))


def _shift_rows(x, k, fill):
    n = x.shape[0]
    if k % SUBLANES == 0:
        return jnp.concatenate([jnp.full((k, x.shape[1]), fill, x.dtype), x[:n - k]], axis=0)
    row = lax.broadcasted_iota(jnp.int32, x.shape, 0)
    return jnp.where(row >= k, pltpu.roll(x, k, 0), fill)


def _mod_kernel(c_ref, w_ref, b_ref, o_ref):
    c = c_ref[...]
    s = c * _sigmoid(c)
    sh, sl = _split2(s)
    wh, wl = _split2(w_ref[0])
    o_ref[...] = _dot(sh, wh) + (_dot(sh, wl) + _dot(sl, wh)) + b_ref[...]


def _mod_call(c_pad, w_ada, b_ada):
    rows, d = c_pad.shape
    n = w_ada.shape[2]
    return pl.pallas_call(
        _mod_kernel,
        grid=(n // d,),
        in_specs=[pl.BlockSpec((rows, d), lambda j: (0, 0)),
                  pl.BlockSpec((1, d, d), lambda j: (0, 0, j)),
                  pl.BlockSpec((1, d), lambda j: (0, j))],
        out_specs=pl.BlockSpec((rows, d), lambda j: (0, j)),
        out_shape=jax.ShapeDtypeStruct((rows, n), F32),
        compiler_params=pltpu.CompilerParams(vmem_limit_bytes=VMEM_LIMIT),
        name="mod",
    )(c_pad, w_ada, b_ada)


def _inproj_kernel(x_ref, mod_ref, wq_ref, wk_ref, wvt_ref, wf_ref, wx_ref, wg_ref, bf_ref,
                   q_ref, k_ref, vt_ref, xb_ref, gb_ref, carry_ref):
    j = pl.program_id(1)

    @pl.when(j == 0)
    def _():
        carry_ref[...] = jnp.zeros_like(carry_ref)

    tm = x_ref.shape[1]
    u = _layer_norm(x_ref[0]) * (1.0 + mod_ref[0, 1:2, :]) + mod_ref[0, 0:1, :]
    ub = u.astype(BF16)

    logf = _log_sigmoid(_dot(ub, wf_ref[...]) + bf_ref[...])
    row = lax.broadcasted_iota(jnp.int32, (tm, tm), 0)
    col = lax.broadcasted_iota(jnp.int32, (tm, tm), 1)
    tril = (col <= row).astype(BF16)
    l_hi, l_mid, l_lo = _split3(logf)
    cum = (_dot(tril, l_hi) + (_dot(tril, l_mid) + _dot(tril, l_lo))) + carry_ref[...]
    carry_ref[...] = cum[tm - 1:tm, :]
    c_hi, c_mid, c_lo = [p.astype(F32) for p in _split3(cum * LOG2E)]

    lane = lax.broadcasted_iota(jnp.int32, (tm, HEAD_PAD), 1)
    d = HEAD_DIM
    q_all = _dot(ub, wq_ref[...])
    k_all = _dot(ub, wk_ref[...])
    for h in range(N_HEADS):
        ts = slice((h // 2) * LANES, (h // 2 + 1) * LANES)
        qh, kh = q_all[:, ts], k_all[:, ts]
        if h % 2:
            qh, kh = pltpu.roll(qh, d, 1), pltpu.roll(kh, d, 1)
        hi, mid, lo = c_hi[:, h:h + 1], c_mid[:, h:h + 1], c_lo[:, h:h + 1]
        q_ext = jnp.where(lane == d, hi, jnp.where(lane == d + 1, mid, jnp.where(
            lane == d + 2, lo, jnp.where(lane < d + 6, 1.0, 0.0))))
        k_ext = jnp.where(lane < d + 3, 1.0, jnp.where(lane == d + 3, -hi, jnp.where(
            lane == d + 4, -mid, jnp.where(lane == d + 5, -lo, 0.0))))
        q_ref[0, h] = jnp.where(lane < d, qh, q_ext).astype(BF16)
        k_ref[0, h] = jnp.where(lane < d, kh, k_ext).astype(BF16)

    vt_all = lax.dot_general(wvt_ref[...], ub, (((1,), (1,)), ((), ())),
                             preferred_element_type=F32)
    pad_rows = lax.broadcasted_iota(jnp.int32, (VT_ROWS - d, tm), 0)
    ones_row = jnp.where(pad_rows == 0, 1.0, 0.0).astype(BF16)
    for h in range(N_HEADS):
        vt_ref[0, h, 0, :d, :] = vt_all[h * d:(h + 1) * d, :].astype(BF16)
        vt_ref[0, h, 0, d:, :] = ones_row

    xb_ref[0] = _dot(ub, wx_ref[...])
    gb_ref[0] = _dot(ub, wg_ref[...])


def _inproj_call(x, mod3, wq, wk, wvt, wf, wx, wg, bf_pad):
    b, s, d = x.shape
    d_attn = wq.shape[1]
    d_lru = wx.shape[1]
    full = lambda shape: pl.BlockSpec(shape, lambda bi, j: (0,) * len(shape))
    head_spec = pl.BlockSpec((1, N_HEADS, TM, HEAD_PAD), lambda bi, j: (bi, 0, j, 0))
    vt_spec = pl.BlockSpec((1, N_HEADS, 1, VT_ROWS, TM), lambda bi, j: (bi, 0, j, 0, 0))
    row_spec = pl.BlockSpec((1, TM, d_lru), lambda bi, j: (bi, j, 0))
    head_shape = jax.ShapeDtypeStruct((b, N_HEADS, s, HEAD_PAD), BF16)
    return pl.pallas_call(
        _inproj_kernel,
        grid=(b, s // TM),
        in_specs=[pl.BlockSpec((1, TM, d), lambda bi, j: (bi, j, 0)),
                  pl.BlockSpec((1, 6, d), lambda bi, j: (bi, 0, 0)),
                  full((d, d_attn)), full((d, d_attn)), full((d_attn, d)), full((d, LANES)),
                  full((d, d_lru)), full((d, d_lru)), full((1, LANES))],
        out_specs=[head_spec, head_spec, vt_spec, row_spec, row_spec],
        out_shape=[head_shape, head_shape,
                   jax.ShapeDtypeStruct((b, N_HEADS, s // TM, VT_ROWS, TM), BF16),
                   jax.ShapeDtypeStruct((b, s, d_lru), F32),
                   jax.ShapeDtypeStruct((b, s, d_lru), F32)],
        scratch_shapes=[pltpu.VMEM((1, LANES), F32)],
        compiler_params=pltpu.CompilerParams(
            dimension_semantics=("arbitrary", "arbitrary"), vmem_limit_bytes=VMEM_LIMIT),
        name="inproj",
    )(x, mod3, wq, wk, wvt, wf, wx, wg, bf_pad)


def _attn_kernel(q_ref, k_ref, vt_ref, o_ref, s_scr, m_scr, acc_scr):
    tq = TQ
    nq = q_ref.shape[2] // tq
    heads = range(q_ref.shape[1])

    def scores_to(slot, i, j):
        for hh in heads:
            k = k_ref[0, hh, pl.ds(pl.multiple_of(j * tq, tq), tq), :]
            q = q_ref[0, hh, pl.ds(pl.multiple_of(i * tq, tq), tq), :]
            s_scr[slot, hh] = lax.dot_general(
                k, q, (((1,), (1,)), ((), ())), preferred_element_type=F32)

    def consume(slot, j, masked):
        for hh in heads:
            s = s_scr[slot, hh]
            if masked:
                key = lax.broadcasted_iota(jnp.int32, (tq, tq), 0)
                qry = lax.broadcasted_iota(jnp.int32, (tq, tq), 1)
                s = jnp.where(key <= qry, s, NEG_INF)
            m = m_scr[hh]
            m_new = jnp.maximum(m, jnp.max(s, axis=0, keepdims=True))
            p = jnp.exp2(s - m_new).astype(BF16)
            acc_scr[hh] = jnp.exp2(m - m_new) * acc_scr[hh] + _dot(vt_ref[0, hh, j], p)
            m_scr[hh] = m_new

    scores_to(2, 0, 0)

    def query_block(i, carry):
        m_scr[...] = jnp.full(m_scr.shape, NEG_INF, F32)
        acc_scr[...] = jnp.zeros(acc_scr.shape, F32)
        nxt = jnp.minimum(i + 1, nq - 1)
        n_mid = i - 1

        @pl.when(i == 0)
        def _():
            consume(2, 0, True)
            scores_to(2, nxt, 0)

        @pl.when(i >= 1)
        def _():
            scores_to(1, i, 1)
            consume(2, 0, False)

        def pair(kk, c):
            j = 1 + 2 * kk
            scores_to(0, i, j + 1)
            consume(1, j, False)
            scores_to(1, i, j + 2)
            consume(0, j + 1, False)
            return c

        lax.fori_loop(0, jnp.maximum(n_mid, 0) // 2, pair, 0)

        @pl.when(jnp.logical_and(i >= 1, n_mid % 2 == 1))
        def _():
            scores_to(0, i, i)
            consume(1, i - 1, False)
            scores_to(2, nxt, 0)
            consume(0, i, True)

        @pl.when(jnp.logical_and(i >= 1, n_mid % 2 == 0))
        def _():
            scores_to(2, nxt, 0)
            consume(1, i, True)

        for hh in heads:
            o_ref[0, 0, i, hh * HEAD_DIM:(hh + 1) * HEAD_DIM, :] = (
                acc_scr[hh, :HEAD_DIM, :] / acc_scr[hh, HEAD_DIM:HEAD_DIM + 1, :])
        return carry

    lax.fori_loop(0, nq, query_block, 0)


def _attn_call(q_aug, k_aug, vt_aug):
    b, h, s, hp = q_aug.shape
    assert TQ == TM
    nq = s // TQ
    return pl.pallas_call(
        _attn_kernel,
        grid=(b, h // 2),
        in_specs=[pl.BlockSpec((1, 2, s, hp), lambda bi, p: (bi, p, 0, 0)),
                  pl.BlockSpec((1, 2, s, hp), lambda bi, p: (bi, p, 0, 0)),
                  pl.BlockSpec((1, 2, nq, VT_ROWS, TQ), lambda bi, p: (bi, p, 0, 0, 0))],
        out_specs=pl.BlockSpec((1, 1, nq, 2 * HEAD_DIM, TQ), lambda bi, p: (bi, p, 0, 0, 0)),
        out_shape=jax.ShapeDtypeStruct((b, h // 2, nq, 2 * HEAD_DIM, TQ), F32),
        scratch_shapes=[pltpu.VMEM((3, 2, TQ, TQ), F32), pltpu.VMEM((2, 1, TQ), F32),
                        pltpu.VMEM((2, VT_ROWS, TQ), F32)],
        compiler_params=pltpu.CompilerParams(
            dimension_semantics=("arbitrary", "arbitrary"), vmem_limit_bytes=VMEM_LIMIT),
        name="attention",
    )(q_aug, k_aug, vt_aug)


def _gelu_tanh(x):
    return 0.5 * x * (1.0 + jnp.tanh(0.7978845608028654 * (x + 0.044715 * (x * x * x))))


def _lru_kernel(xb_ref, gb_ref, cw_ref, cb_ref, wgate_ref, brg_ref, big_ref, lam_ref,
                o_ref, tail_ref, h_ref):
    j = pl.program_id(1)

    @pl.when(j == 0)
    def _():
        tail_ref[...] = jnp.zeros_like(tail_ref)
        h_ref[...] = jnp.zeros_like(h_ref)

    x = xb_ref[0]
    tm, dl = x.shape
    row_in_group = lax.broadcasted_iota(jnp.int32, (tm, dl), 0) % SUBLANES
    x_prev_group = jnp.concatenate([tail_ref[...], x[:tm - SUBLANES]], axis=0)
    xc = x * cw_ref[CONV_WIDTH - 1:CONV_WIDTH, :] + cb_ref[...]
    for k in range(1, CONV_WIDTH):
        xs = jnp.where(row_in_group < k, _rot_in_group(x_prev_group, k), _rot_in_group(x, k))
        xc = xc + xs * cw_ref[CONV_WIDTH - 1 - k:CONV_WIDTH - k, :]
    tail_ref[...] = x[tm - SUBLANES:, :]

    xcb = xc.astype(BF16)
    n_pairs = dl // LANES
    r_parts, i_parts = [], []
    for p in range(n_pairs):
        g = _dot(xcb[:, p * LANES:(p + 1) * LANES], wgate_ref[p])
        r_parts.append(g[:, :LANES])
        i_parts.append(g[:, LANES:])
    r = _sigmoid(jnp.concatenate(r_parts, axis=1) + brg_ref[...])
    ig = _sigmoid(jnp.concatenate(i_parts, axis=1) + big_ref[...])

    lam = lam_ref[...]
    softplus_neg_lam = jnp.maximum(-lam, 0.0) + jnp.log1p(jnp.exp(-jnp.abs(lam)))
    log_a = (-LRU_C) * r * softplus_neg_lam
    a = jnp.exp(log_a)
    v = 1.0 - a * a
    u = jnp.where(v > 0.0, v * lax.rsqrt(v), 0.0) * (ig * xc)

    k = 1
    while k < SUBLANES:
        keep = row_in_group >= k
        u = a * jnp.where(keep, _rot_in_group(u, k), 0.0) + u
        a = a * jnp.where(keep, _rot_in_group(a, k), 1.0)
        k *= 2
    h_prev = h_ref[...]
    groups = []
    for g in range(tm // SUBLANES):
        rows = slice(g * SUBLANES, (g + 1) * SUBLANES)
        hg = a[rows] * h_prev + u[rows]
        groups.append(hg)
        h_prev = hg[SUBLANES - 1:SUBLANES]
    h_ref[...] = h_prev
    o_ref[0] = jnp.concatenate(groups, axis=0) * _gelu_tanh(gb_ref[0])


def _lru_call(xb, gb, conv_w, conv_b, wgate, b_rg, b_ig, lam):
    b, s, dl = xb.shape
    row_spec = pl.BlockSpec((1, TM, dl), lambda bi, j: (bi, j, 0))
    full = lambda shape: pl.BlockSpec(shape, lambda bi, j: (0,) * len(shape))
    return pl.pallas_call(
        _lru_kernel,
        grid=(b, s // TM),
        in_specs=[row_spec, row_spec, full(conv_w.shape), full((1, dl)), full(wgate.shape),
                  full((1, dl)), full((1, dl)), full((1, dl))],
        out_specs=row_spec,
        out_shape=jax.ShapeDtypeStruct((b, s, dl), F32),
        scratch_shapes=[pltpu.VMEM((SUBLANES, dl), F32), pltpu.VMEM((1, dl), F32)],
        compiler_params=pltpu.CompilerParams(
            dimension_semantics=("arbitrary", "arbitrary"), vmem_limit_bytes=VMEM_LIMIT),
        name="lru",
    )(xb, gb, conv_w, conv_b, wgate, b_rg, b_ig, lam)


def _rms(x, gain):
    return x * lax.rsqrt(jnp.mean(x * x, axis=-1, keepdims=True) + RMS_EPS) * gain


def _mixout_kernel(attn_ref, lru_ref, x_ref, mod_ref, ga_ref, gl_ref, woa_ref, wol_ref,
                   g1_ref, b1_ref, wr_ref, br_ref, x1_ref, u2_ref, rinfo_ref, cnt_ref):
    tm, d = x_ref.shape
    attn = attn_ref[0, :, 0].reshape(ga_ref.shape[1], tm).T
    na = _rms(attn, ga_ref[...]).astype(BF16)
    nl = _rms(lru_ref[...], gl_ref[...]).astype(BF16)
    mix = _dot(na, woa_ref[...]) + _dot(nl, wol_ref[...])
    z = DEEPNORM_ALPHA * x_ref[...] + (1.0 + mod_ref[0, 2:3, :]) * mix
    x1 = _layer_norm(z) * g1_ref[...] + b1_ref[...]
    x1_ref[...] = x1
    u2 = _layer_norm(x1) * (1.0 + mod_ref[0, 4:5, :]) + mod_ref[0, 3:4, :]

    nt = (((1,), (1,)), ((), ()))
    uh, ul = _split2(u2)
    wh, wl = _split2(wr_ref[...])
    dg = lambda a, b_: lax.dot_general(a, b_, nt, preferred_element_type=F32)
    lg = dg(wh, uh) + (dg(wh, ul) + dg(wl, uh)) + br_ref[:, 0:1]

    def first_index(vals, target):
        idx = jnp.full_like(target, float(len(vals) - 1))
        for n in range(len(vals) - 2, -1, -1):
            idx = jnp.where(vals[n] == target, float(n), idx)
        return idx

    g = [lg[n:n + 1, :] for n in range(N_GROUPS)]
    gmax = functools.reduce(jnp.maximum, g)
    gsum = functools.reduce(lambda a, b_: a + b_, [jnp.exp(v - gmax) for v in g])
    grp_w = 1.0 / gsum
    gidx = first_index(g, gmax)

    sel = []
    for e in range(EXPERTS_PER_GROUP):
        v = lg[N_GROUPS + (N_GROUPS - 1) * EXPERTS_PER_GROUP + e:
               N_GROUPS + (N_GROUPS - 1) * EXPERTS_PER_GROUP + e + 1, :]
        for gi in range(N_GROUPS - 2, -1, -1):
            r0 = N_GROUPS + gi * EXPERTS_PER_GROUP + e
            v = jnp.where(gidx == float(gi), lg[r0:r0 + 1, :], v)
        sel.append(v)
    smax = functools.reduce(jnp.maximum, sel)
    i1 = first_index(sel, smax)
    rest = [jnp.where(i1 == float(e), -3e38, sel[e]) for e in range(EXPERTS_PER_GROUP)]
    rmax = functools.reduce(jnp.maximum, rest)
    i2 = first_index(rest, rmax)
    e2 = jnp.exp(rmax - smax)
    w1 = grp_w / (1.0 + e2)
    w2 = grp_w * e2 / (1.0 + e2)
    ia = jnp.minimum(i1, i2)
    ib = jnp.maximum(i1, i2)
    wa = jnp.where(i1 < i2, w1, w2)
    wb = jnp.where(i1 < i2, w2, w1)
    pair = jnp.where(ia == 0.0, ib - 1.0, jnp.where(ia == 1.0, ib + 1.0, 5.0))
    bucket = gidx * float(N_PAIRS) + pair

    @pl.when(pl.program_id(0) == 0)
    def _():
        cnt_ref[...] = jnp.zeros_like(cnt_ref)

    cid = lax.broadcasted_iota(jnp.int32, (BUCKET_ROWS, tm), 0).astype(F32)
    cnt_ref[...] += jnp.sum(jnp.where(cid == bucket, 1.0, 0.0), axis=1, keepdims=True)

    zrow = jnp.zeros_like(wa)
    rinfo_ref[...] = jnp.concatenate([bucket, wa, wb] + [zrow] * (SUBLANES - 3), axis=0)
    wt = jnp.concatenate([wa, wb, jnp.zeros((LANES - 2, tm), F32)], axis=0)
    u2_ref[:, :d] = u2
    u2_ref[:, d:] = wt.T


def _mixout_call(attn_t, lru, x2d, mod3, ga, gl, wo, g1, b1, wr, br, seq):
    t, d = x2d.shape
    _, n_pairs, _, pair_w, _ = attn_t.shape
    dh = n_pairs * pair_w
    assert wo.shape[0] == 2 * dh and lru.shape[1] == dh
    per_b = seq // TM
    full = lambda shape: pl.BlockSpec(shape, lambda i: (0,) * len(shape))
    return pl.pallas_call(
        _mixout_kernel,
        grid=(t // TM,),
        in_specs=[pl.BlockSpec((1, n_pairs, 1, pair_w, TM),
                               lambda i: (i // per_b, 0, i % per_b, 0, 0)),
                  pl.BlockSpec((TM, dh), lambda i: (i, 0)),
                  pl.BlockSpec((TM, d), lambda i: (i, 0)),
                  pl.BlockSpec((1, 6, d), lambda i: (i // per_b, 0, 0)),
                  full((1, dh)), full((1, dh)),
                  pl.BlockSpec((dh, d), lambda i: (0, 0)), pl.BlockSpec((dh, d), lambda i: (1, 0)),
                  full((1, d)), full((1, d)), full((BUCKET_ROWS, d)), full((BUCKET_ROWS, LANES))],
        out_specs=[pl.BlockSpec((TM, d), lambda i: (i, 0)),
                   pl.BlockSpec((TM, d + LANES), lambda i: (i, 0)),
                   pl.BlockSpec((SUBLANES, TM), lambda i: (0, i)),
                   pl.BlockSpec((BUCKET_ROWS, LANES), lambda i: (0, 0))],
        out_shape=[jax.ShapeDtypeStruct((t, d), F32),
                   jax.ShapeDtypeStruct((t, d + LANES), F32),
                   jax.ShapeDtypeStruct((SUBLANES, t), F32),
                   jax.ShapeDtypeStruct((BUCKET_ROWS, LANES), F32)],
        compiler_params=pltpu.CompilerParams(
            dimension_semantics=("arbitrary",), vmem_limit_bytes=VMEM_LIMIT),
        name="mixout",
    )(attn_t, lru, x2d, mod3, ga, gl, wo, wo, g1, b1, wr, br)


def _rank_kernel(rinfo_ref, counts_ref, dest_ref, carry_ref, offs_ref):
    tm = rinfo_ref.shape[1]

    @pl.when(pl.program_id(0) == 0)
    def _():
        carry_ref[...] = jnp.zeros_like(carry_ref)
        padded = jnp.floor((counts_ref[...] + float(TM_E - 1)) * (1.0 / TM_E)) * float(TM_E)
        inc = padded
        k = 1
        while k < BUCKET_ROWS:
            inc = inc + _shift_rows(inc, k, 0.0)
            k *= 2
        offs_ref[...] = inc - padded

    bucket = rinfo_ref[0:1, :]
    cid = lax.broadcasted_iota(jnp.int32, (BUCKET_ROWS, tm), 0).astype(F32)
    onehot = jnp.where(cid == bucket, 1.0, 0.0)
    srow = lax.broadcasted_iota(jnp.int32, (tm, tm), 0)
    scol = lax.broadcasted_iota(jnp.int32, (tm, tm), 1)
    upper = (srow <= scol).astype(BF16)
    prefix = _dot(onehot.astype(BF16), upper)
    carry = carry_ref[...]
    rank = jnp.sum(onehot * (prefix - 1.0 + carry[:, 0:1] + offs_ref[:, 0:1]),
                   axis=0, keepdims=True)
    dest_ref[...] = rank.astype(jnp.int32)
    carry_ref[...] = carry + prefix[:, tm - 1:tm]


def _rank_call(rinfo, counts):
    t = rinfo.shape[1]
    return pl.pallas_call(
        _rank_kernel,
        grid=(t // TM,),
        in_specs=[pl.BlockSpec((SUBLANES, TM), lambda i: (0, i)),
                  pl.BlockSpec((BUCKET_ROWS, LANES), lambda i: (0, 0))],
        out_specs=pl.BlockSpec((1, TM), lambda i: (0, i)),
        out_shape=jax.ShapeDtypeStruct((1, t), jnp.int32),
        scratch_shapes=[pltpu.VMEM((BUCKET_ROWS, LANES), F32),
                        pltpu.VMEM((BUCKET_ROWS, LANES), F32)],
        compiler_params=pltpu.CompilerParams(
            dimension_semantics=("arbitrary",), vmem_limit_bytes=VMEM_LIMIT),
        name="rank",
    )(rinfo, counts)


def _dispatch_kernel(dest_ref, ends_ref, u2_ref, xs_ref, zbuf, sem, zsem):
    tm = u2_ref.shape[0]
    t0 = pl.program_id(0) * tm

    @pl.when(pl.program_id(0) == 0)
    def _():
        zbuf[...] = jnp.zeros_like(zbuf)

        def tail_copy(bkt):
            end = ends_ref[bkt]
            start = ends_ref[bkt - 1] if bkt else 0
            tail = pl.multiple_of(jnp.maximum(end - TM_E, 0), TM_E)
            return end > start, pltpu.make_async_copy(zbuf, xs_ref.at[pl.ds(tail, TM_E)], zsem)

        for bkt in range(N_BUCKETS):
            nonempty, copy = tail_copy(bkt)
            pl.when(nonempty)(copy.start)
        for bkt in range(N_BUCKETS):
            nonempty, copy = tail_copy(bkt)
            pl.when(nonempty)(copy.wait)

        def unused_tile_copy(k):
            return pltpu.make_async_copy(
                zbuf, xs_ref.at[pl.ds(pl.multiple_of(k * TM_E, TM_E), TM_E)], zsem)

        first_unused = ends_ref[N_BUCKETS - 1] // TM_E
        n_tiles = xs_ref.shape[0] // TM_E
        lax.fori_loop(first_unused, n_tiles, lambda k, c: (unused_tile_copy(k).start(), c)[1], 0)
        lax.fori_loop(first_unused, n_tiles, lambda k, c: (unused_tile_copy(k).wait(), c)[1], 0)

    def issue(r, c):
        pltpu.make_async_copy(u2_ref.at[pl.ds(r, 1)],
                              xs_ref.at[pl.ds(dest_ref[t0 + r], 1)], sem).start()
        return c

    lax.fori_loop(0, tm, issue, 0, unroll=ISSUE_UNROLL)
    pltpu.make_async_copy(u2_ref, xs_ref.at[pl.ds(0, tm)], sem).wait()


def _dispatch_call(dest, bucket_ends, u2ext, n_rows):
    t, w = u2ext.shape
    grid_spec = pltpu.PrefetchScalarGridSpec(
        num_scalar_prefetch=2,
        grid=(t // TM,),
        in_specs=[pl.BlockSpec((TM, w), lambda i, dest_ref, ends_ref: (i, 0))],
        out_specs=pl.BlockSpec(memory_space=pl.ANY),
        scratch_shapes=[pltpu.VMEM((TM_E, w), F32), pltpu.SemaphoreType.DMA(()),
                        pltpu.SemaphoreType.DMA(())],
    )
    return pl.pallas_call(
        _dispatch_kernel,
        grid_spec=grid_spec,
        out_shape=jax.ShapeDtypeStruct((n_rows, w), F32),
        compiler_params=pltpu.CompilerParams(
            dimension_semantics=("arbitrary",), vmem_limit_bytes=VMEM_LIMIT),
        name="dispatch",
    )(dest, bucket_ends, u2ext)


def _experts_kernel(ea_ref, eb_ref, nv_ref, rows_ref, xs_ref, wga_ref, wua_ref, wda_ref,
                    wgb_ref, wub_ref, wdb_ref, ys_ref):
    del ea_ref, eb_ref, nv_ref
    tm, d = ys_ref.shape
    rows = rows_ref[pl.program_id(0)]

    def compute(n):
        x = xs_ref[:n, :d].astype(BF16)

        def expert(wg_ref, wu_ref, wd_ref):
            g = _dot(x, wg_ref[0])
            h = (g * _sigmoid(g)) * _dot(x, wu_ref[0])
            return _dot(h.astype(BF16), wd_ref[0])

        ya = xs_ref[:n, d:d + 1] * expert(wga_ref, wua_ref, wda_ref)
        ys_ref[:n, :] = ya + xs_ref[:n, d + 1:d + 2] * expert(wgb_ref, wub_ref, wdb_ref)
        if n < tm:
            ys_ref[n:, :] = jnp.zeros((tm - n, d), F32)

    pl.when(rows > tm // 2)(lambda: compute(tm))
    pl.when(jnp.logical_and(rows > 0, rows <= tm // 2))(lambda: compute(tm // 2))

    @pl.when(rows == 0)
    def _():
        ys_ref[...] = jnp.zeros_like(ys_ref)


def _experts_call(tile_ea, tile_eb, n_valid, tile_rows, xs, wg, wu, wd):
    tp, w = xs.shape
    _, d, de = wg.shape
    row = lambda i, ea, eb, nv, tr: (jnp.minimum(i, nv[0] - 1), 0)
    wa = lambda i, ea, eb, nv, tr: (ea[i], 0, 0)
    wb = lambda i, ea, eb, nv, tr: (eb[i], 0, 0)
    grid_spec = pltpu.PrefetchScalarGridSpec(
        num_scalar_prefetch=4,
        grid=(tp // TM_E,),
        in_specs=[pl.BlockSpec((TM_E, w), row),
                  pl.BlockSpec((1, d, de), wa), pl.BlockSpec((1, d, de), wa),
                  pl.BlockSpec((1, de, d), wa),
                  pl.BlockSpec((1, d, de), wb), pl.BlockSpec((1, d, de), wb),
                  pl.BlockSpec((1, de, d), wb)],
        out_specs=pl.BlockSpec((TM_E, d), lambda i, ea, eb, nv, tr: (i, 0)),
    )
    return pl.pallas_call(
        _experts_kernel,
        grid_spec=grid_spec,
        out_shape=jax.ShapeDtypeStruct((tp, d), F32),
        compiler_params=pltpu.CompilerParams(
            dimension_semantics=("arbitrary",), vmem_limit_bytes=VMEM_LIMIT),
        name="experts",
    )(tile_ea, tile_eb, n_valid, tile_rows, xs, wg, wu, wd, wg, wu, wd)


def _final_kernel(dest_ref, x1_ref, mod_ref, g2_ref, b2_ref, ys_ref, o_ref, ybuf, sem):
    tm = x1_ref.shape[0]
    i = pl.program_id(0)
    slot = i % GATHER_AHEAD

    def issue_rows(tile, to_slot, r0, n):
        for r in range(n):
            pltpu.make_async_copy(ys_ref.at[pl.ds(dest_ref[tile * tm + r0 + r], 1)],
                                  ybuf.at[to_slot, pl.ds(r0 + r, 1)], sem.at[to_slot]).start()

    def normalise_rows(r0, n):
        rows = pl.ds(r0, n)
        z = DEEPNORM_ALPHA * x1_ref[rows, :] + (1.0 + mod_ref[0, 5:6, :]) * ybuf[slot, rows, :]
        o_ref[rows, :] = _layer_norm(z) * g2_ref[...] + b2_ref[...]

    n_chunks = tm // ISSUE_UNROLL

    @pl.when(i == 0)
    def _():
        for tile in range(GATHER_AHEAD):
            lax.fori_loop(0, n_chunks, lambda c, x, tile=tile: (
                issue_rows(tile, tile, c * ISSUE_UNROLL, ISSUE_UNROLL), x)[1], 0)

    pltpu.make_async_copy(ys_ref.at[pl.ds(0, tm)], ybuf.at[slot], sem.at[slot]).wait()

    def chunk(c, issue_ahead):
        r0 = pl.multiple_of(c * ISSUE_UNROLL, ISSUE_UNROLL)
        normalise_rows(r0, ISSUE_UNROLL)
        if issue_ahead:
            issue_rows(i + GATHER_AHEAD, slot, r0, ISSUE_UNROLL)

    more = i + GATHER_AHEAD < pl.num_programs(0)

    @pl.when(more)
    def _():
        lax.fori_loop(0, n_chunks, lambda c, x: (chunk(c, True), x)[1], 0)

    @pl.when(jnp.logical_not(more))
    def _():
        lax.fori_loop(0, n_chunks, lambda c, x: (chunk(c, False), x)[1], 0)


def _final_call(dest, x1, mod3, g2, b2, ys, seq):
    t, d = x1.shape
    per_b = seq // TM
    grid_spec = pltpu.PrefetchScalarGridSpec(
        num_scalar_prefetch=1,
        grid=(t // TM,),
        in_specs=[pl.BlockSpec((TM, d), lambda i, dr: (i, 0)),
                  pl.BlockSpec((1, 6, d), lambda i, dr: (i // per_b, 0, 0)),
                  pl.BlockSpec((1, d), lambda i, dr: (0, 0)),
                  pl.BlockSpec((1, d), lambda i, dr: (0, 0)),
                  pl.BlockSpec(memory_space=pl.ANY)],
        out_specs=pl.BlockSpec((TM, d), lambda i, dr: (i, 0)),
        scratch_shapes=[pltpu.VMEM((GATHER_AHEAD, TM, d), F32),
                        pltpu.SemaphoreType.DMA((GATHER_AHEAD,))],
    )
    return pl.pallas_call(
        _final_kernel,
        grid_spec=grid_spec,
        out_shape=jax.ShapeDtypeStruct((t, d), F32),
        compiler_params=pltpu.CompilerParams(
            dimension_semantics=("arbitrary",), vmem_limit_bytes=VMEM_LIMIT),
        name="final",
    )(dest, x1, mod3, g2, b2, ys)


def _gate_pairs(w_rg, w_ig):
    def pairs(w):
        n, bs, _ = w.shape
        w = w.reshape(n // 2, 2, bs, bs)
        z = jnp.zeros((n // 2, bs, bs), w.dtype)
        top = jnp.concatenate([w[:, 0], z], axis=2)
        bot = jnp.concatenate([z, w[:, 1]], axis=2)
        return jnp.concatenate([top, bot], axis=1)
    return jnp.concatenate([pairs(w_rg), pairs(w_ig)], axis=2).astype(BF16)


def kernel(x, c, w_ada, b_ada, w_in, b_f, conv_w, conv_b, w_rg, b_rg, w_ig, b_ig, lru_lambda,
           g_attn, g_lru, w_out, ln1_g, ln1_b, w_grp, b_grp, w_exp, b_exp,
           w_e_gate, w_e_up, w_e_down, ln2_g, ln2_b):
    assert w_ada.shape[0] == DEPTH
    b, s, d = x.shape
    t = b * s
    d_attn = N_HEADS * HEAD_DIM
    d_lru = conv_w.shape[2]
    n_exp = w_e_gate.shape[1]
    assert s % TM == 0 and s % TQ == 0 and t % TM_E == 0

    c_pad = jnp.pad(c, ((0, SUBLANES - b), (0, 0)))
    mod = _mod_call(c_pad, w_ada, b_ada[0][None, :])
    mod3 = mod[:b].reshape(b, 6, d)

    wi = w_in[0]
    o = 3 * d_attn
    wq = (wi[:, :d_attn] * (HEAD_DIM ** -0.5 * LOG2E)).astype(BF16)
    wk = wi[:, d_attn:2 * d_attn].astype(BF16)
    wvt = wi[:, 2 * d_attn:o].T.astype(BF16)
    wf = jnp.pad(wi[:, o:o + N_HEADS], ((0, 0), (0, LANES - N_HEADS))).astype(BF16)
    wx = wi[:, o + N_HEADS:o + N_HEADS + d_lru].astype(BF16)
    wg = wi[:, o + N_HEADS + d_lru:].astype(BF16)
    bf_pad = jnp.pad(b_f[0], (0, LANES - N_HEADS))[None, :]
    q_aug, k_aug, vt_aug, xb, gb = _inproj_call(x, mod3, wq, wk, wvt, wf, wx, wg, bf_pad)
    attn_t = _attn_call(q_aug, k_aug, vt_aug)
    lru = _lru_call(xb, gb, conv_w[0], conv_b[0][None, :], _gate_pairs(w_rg[0], w_ig[0]),
                    b_rg[0][None, :], b_ig[0][None, :], lru_lambda[0][None, :])

    wo = w_out[0].astype(BF16)
    n_route = N_GROUPS + n_exp
    wr = jnp.pad(jnp.concatenate([w_grp[0], w_exp[0]], axis=1).T,
                 ((0, BUCKET_ROWS - n_route), (0, 0)))
    br = jnp.pad(jnp.concatenate([b_grp[0], b_exp[0]]), (0, BUCKET_ROWS - n_route))
    br = jnp.broadcast_to(br[:, None], (BUCKET_ROWS, LANES))
    x1, u2ext, rinfo, counts = _mixout_call(
        attn_t, lru.reshape(t, d_lru), x.reshape(t, d), mod3,
        g_attn[0][None, :], g_lru[0][None, :], wo,
        ln1_g[0][None, :], ln1_b[0][None, :], wr, br, s)

    dest = _rank_call(rinfo, counts).reshape(t)
    cnt = counts[:N_BUCKETS, 0].astype(jnp.int32)
    ends = jnp.cumsum((cnt + (TM_E - 1)) // TM_E)
    n_tiles = t // TM_E + N_BUCKETS
    tile_bucket = jnp.sum(ends[None, :] <= jnp.arange(n_tiles)[:, None], axis=1)
    tile_bucket = jnp.minimum(tile_bucket, N_BUCKETS - 1)
    n_valid = ends[N_BUCKETS - 1:]
    tokens_end = (ends - (cnt + (TM_E - 1)) // TM_E) * TM_E + cnt
    tile_rows = jnp.clip(tokens_end[tile_bucket] - jnp.arange(n_tiles) * TM_E, 0, TM_E)
    tile_rows = jnp.where(jnp.arange(n_tiles) < n_valid[0], tile_rows, 0).astype(jnp.int32)
    last_bucket = tile_bucket[jnp.maximum(n_valid[0] - 1, 0)]
    tile_bucket = jnp.where(jnp.arange(n_tiles) < n_valid[0], tile_bucket, last_bucket)
    tile_ea = jnp.asarray(_BUCKET_EA, jnp.int32)[tile_bucket]
    tile_eb = jnp.asarray(_BUCKET_EB, jnp.int32)[tile_bucket]

    xs = _dispatch_call(dest, (ends * TM_E).astype(jnp.int32), u2ext, n_tiles * TM_E)
    ys = _experts_call(tile_ea, tile_eb, n_valid.astype(jnp.int32), tile_rows, xs,
                       w_e_gate[0].astype(BF16), w_e_up[0].astype(BF16),
                       w_e_down[0].astype(BF16))
    out = _final_call(dest, x1, mod3, ln2_g[0][None, :], ln2_b[0][None, :], ys, s)
    return out.reshape(b, s, d)
```

```python
import functools

import jax
import jax.numpy as jnp
from jax import lax
from jax.experimental import pallas as pl
from jax.experimental.pallas import tpu as pltpu

F32 = jnp.float32
BF16 = jnp.bfloat16

HEAD_DIM = 64
N_HEADS = 8
N_LRU_BLOCKS = 8
CONV_WIDTH = 4
LRU_C = 8.0
N_GROUPS = 4
EXPERTS_PER_GROUP = 4
N_PAIRS = 6
N_BUCKETS = N_GROUPS * N_PAIRS
LN_EPS = 1e-5
RMS_EPS = 1e-6
NEG_INF = -1e30
DEPTH = 1
DEEPNORM_ALPHA = (2.0 * DEPTH) ** 0.25
LOG2E = 1.4426950408889634

LANES = 128
SUBLANES = 8
HEAD_PAD = LANES
VT_ROWS = 80
BUCKET_ROWS = 32
TM = 512
TQ = 512
TM_E = 256
ISSUE_UNROLL = 64
GATHER_AHEAD = 2
VMEM_LIMIT = 56 * 1024 * 1024

_PAIRS = [(0, 1), (0, 2), (0, 3), (1, 2), (1, 3), (2, 3)]
_BUCKET_EA = [g * EXPERTS_PER_GROUP + a for g in range(N_GROUPS) for (a, b) in _PAIRS]
_BUCKET_EB = [g * EXPERTS_PER_GROUP + b for g in range(N_GROUPS) for (a, b) in _PAIRS]


def _dot(a, b):
    return jnp.dot(a, b, preferred_element_type=F32)


def _split2(a):
    hi = a.astype(BF16)
    lo = (a - hi.astype(F32)).astype(BF16)
    return hi, lo


def _split3(a):
    hi = a.astype(BF16)
    r = a - hi.astype(F32)
    mid = r.astype(BF16)
    lo = (r - mid.astype(F32)).astype(BF16)
    return hi, mid, lo


def _layer_norm(x):
    mu = jnp.mean(x, axis=-1, keepdims=True)
    xc = x - mu
    var = jnp.mean(xc * xc, axis=-1, keepdims=True)
    return xc * lax.rsqrt(var + LN_EPS)


def _sigmoid(x):
    return 0.5 * jnp.tanh(0.5 * x) + 0.5


def _rot_in_group(x, k):
    n, w = x.shape
    return pltpu.roll(x.reshape(n // SUBLANES, SUBLANES, w), k, 1).reshape(n, w)


def _log_sigmoid(z):
    return jnp.minimum(z, 0.0) - jnp.log1p(jnp.exp(-jnp.abs(z)))


def _cumsum_rows(x, carry):
    n, w = x.shape
    row_in_group = lax.broadcasted_iota(jnp.int32, (n, w), 0) % SUBLANES
    k = 1
    while k < SUBLANES:
        x = x + jnp.where(row_in_group >= k, _rot_in_group(x, k), 0.0)
        k *= 2
    groups = []
    for g in range(n // SUBLANES):
        blk = x[g * SUBLANES:(g + 1) * SUBLANES] + carry
        groups.append(blk)
        carry = blk[SUBLANES - 1:SUBLANES]
    return jnp.concatenate(groups, axis=0)


def _shift_rows(x, k, fill):
    n = x.shape[0]
    if k % SUBLANES == 0:
        return jnp.concatenate([jnp.full((k, x.shape[1]), fill, x.dtype), x[:n - k]], axis=0)
    row = lax.broadcasted_iota(jnp.int32, x.shape, 0)
    return jnp.where(row >= k, pltpu.roll(x, k, 0), fill)


def _mod_kernel(c_ref, w_ref, b_ref, o_ref):
    c = c_ref[...]
    s = c * _sigmoid(c)
    sh, sl = _split2(s)
    wh, wl = _split2(w_ref[0])
    o_ref[...] = _dot(sh, wh) + (_dot(sh, wl) + _dot(sl, wh)) + b_ref[...]


def _mod_call(c_pad, w_ada, b_ada):
    rows, d = c_pad.shape
    n = w_ada.shape[2]
    return pl.pallas_call(
        _mod_kernel,
        grid=(n // d,),
        in_specs=[pl.BlockSpec((rows, d), lambda j: (0, 0)),
                  pl.BlockSpec((1, d, d), lambda j: (0, 0, j)),
                  pl.BlockSpec((1, d), lambda j: (0, j))],
        out_specs=pl.BlockSpec((rows, d), lambda j: (0, j)),
        out_shape=jax.ShapeDtypeStruct((rows, n), F32),
        compiler_params=pltpu.CompilerParams(vmem_limit_bytes=VMEM_LIMIT),
        name="mod",
    )(c_pad, w_ada, b_ada)


def _inproj_kernel(x_ref, mod_ref, wq_ref, wk_ref, wv_ref, wf_ref, wx_ref, wg_ref, bf_ref,
                   q_ref, k_ref, vt_ref, xb_ref, gb_ref, carry_ref, v_scr):
    j = pl.program_id(1)

    @pl.when(j == 0)
    def _():
        carry_ref[...] = jnp.zeros_like(carry_ref)

    tm = x_ref.shape[1]
    u = _layer_norm(x_ref[0]) * (1.0 + mod_ref[0, 1:2, :]) + mod_ref[0, 0:1, :]
    ub = u.astype(BF16)

    logf = _log_sigmoid(_dot(ub, wf_ref[...]) + bf_ref[...])
    cum = _cumsum_rows(logf, carry_ref[...])
    carry_ref[...] = cum[tm - 1:tm, :]
    c_hi, c_mid, c_lo = [p.astype(F32) for p in _split3(cum * LOG2E)]

    lane = lax.broadcasted_iota(jnp.int32, (tm, HEAD_PAD), 1)
    d = HEAD_DIM
    q_all = _dot(ub, wq_ref[...])
    k_all = _dot(ub, wk_ref[...])
    for h in range(N_HEADS):
        ts = slice((h // 2) * LANES, (h // 2 + 1) * LANES)
        qh, kh = q_all[:, ts], k_all[:, ts]
        if h % 2:
            qh, kh = pltpu.roll(qh, d, 1), pltpu.roll(kh, d, 1)
        hi, mid, lo = c_hi[:, h:h + 1], c_mid[:, h:h + 1], c_lo[:, h:h + 1]
        q_ext = jnp.where(lane == d, hi, jnp.where(lane == d + 1, mid, jnp.where(
            lane == d + 2, lo, jnp.where(lane < d + 6, 1.0, 0.0))))
        k_ext = jnp.where(lane < d + 3, 1.0, jnp.where(lane == d + 3, -hi, jnp.where(
            lane == d + 4, -mid, jnp.where(lane == d + 5, -lo, 0.0))))
        q_ref[0, h] = jnp.where(lane < d, qh, q_ext).astype(BF16)
        k_ref[0, h] = jnp.where(lane < d, kh, k_ext).astype(BF16)

    v_scr[...] = _dot(ub, wv_ref[...])
    vt_all = v_scr[...].T
    pad_rows = lax.broadcasted_iota(jnp.int32, (VT_ROWS - d, tm), 0)
    ones_row = jnp.where(pad_rows == 0, 1.0, 0.0).astype(BF16)
    for h in range(N_HEADS):
        vt_ref[0, h, 0, :d, :] = vt_all[h * d:(h + 1) * d, :].astype(BF16)
        vt_ref[0, h, 0, d:, :] = ones_row

    xb_ref[0] = _dot(ub, wx_ref[...])
    gb_ref[0] = _dot(ub, wg_ref[...])


def _inproj_call(x, mod3, wq, wk, wv, wf, wx, wg, bf_pad):
    b, s, d = x.shape
    d_attn = wq.shape[1]
    d_lru = wx.shape[1]
    full = lambda shape: pl.BlockSpec(shape, lambda bi, j: (0,) * len(shape))
    head_spec = pl.BlockSpec((1, N_HEADS, TM, HEAD_PAD), lambda bi, j: (bi, 0, j, 0))
    vt_spec = pl.BlockSpec((1, N_HEADS, 1, VT_ROWS, TM), lambda bi, j: (bi, 0, j, 0, 0))
    row_spec = pl.BlockSpec((1, TM, d_lru), lambda bi, j: (bi, j, 0))
    head_shape = jax.ShapeDtypeStruct((b, N_HEADS, s, HEAD_PAD), BF16)
    return pl.pallas_call(
        _inproj_kernel,
        grid=(b, s // TM),
        in_specs=[pl.BlockSpec((1, TM, d), lambda bi, j: (bi, j, 0)),
                  pl.BlockSpec((1, 6, d), lambda bi, j: (bi, 0, 0)),
                  full((d, d_attn)), full((d, d_attn)), full((d, d_attn)), full((d, LANES)),
                  full((d, d_lru)), full((d, d_lru)), full((1, LANES))],
        out_specs=[head_spec, head_spec, vt_spec, row_spec, row_spec],
        out_shape=[head_shape, head_shape,
                   jax.ShapeDtypeStruct((b, N_HEADS, s // TM, VT_ROWS, TM), BF16),
                   jax.ShapeDtypeStruct((b, s, d_lru), F32),
                   jax.ShapeDtypeStruct((b, s, d_lru), F32)],
        scratch_shapes=[pltpu.VMEM((1, LANES), F32), pltpu.VMEM((TM, d_attn), F32)],
        compiler_params=pltpu.CompilerParams(
            dimension_semantics=("arbitrary", "arbitrary"), vmem_limit_bytes=VMEM_LIMIT),
        name="inproj",
    )(x, mod3, wq, wk, wv, wf, wx, wg, bf_pad)


def _attn_kernel(q_ref, k_ref, vt_ref, o_ref, s_scr, m_scr, acc_scr):
    tq = TQ
    nq = q_ref.shape[2] // tq
    heads = range(q_ref.shape[1])

    def scores_to(slot, i, j):
        for hh in heads:
            k = k_ref[0, hh, pl.ds(pl.multiple_of(j * tq, tq), tq), :]
            q = q_ref[0, hh, pl.ds(pl.multiple_of(i * tq, tq), tq), :]
            s_scr[slot, hh] = lax.dot_general(
                k, q, (((1,), (1,)), ((), ())), preferred_element_type=F32)

    def consume(slot, j, masked):
        for hh in heads:
            s = s_scr[slot, hh]
            if masked:
                key = lax.broadcasted_iota(jnp.int32, (tq, tq), 0)
                qry = lax.broadcasted_iota(jnp.int32, (tq, tq), 1)
                s = jnp.where(key <= qry, s, NEG_INF)
            m = m_scr[hh]
            m_new = jnp.maximum(m, jnp.max(s, axis=0, keepdims=True))
            p = jnp.exp2(s - m_new).astype(BF16)
            acc_scr[hh] = jnp.exp2(m - m_new) * acc_scr[hh] + _dot(vt_ref[0, hh, j], p)
            m_scr[hh] = m_new

    scores_to(2, 0, 0)

    def query_block(i, carry):
        m_scr[...] = jnp.full(m_scr.shape, NEG_INF, F32)
        acc_scr[...] = jnp.zeros(acc_scr.shape, F32)
        nxt = jnp.minimum(i + 1, nq - 1)
        n_mid = i - 1

        @pl.when(i == 0)
        def _():
            consume(2, 0, True)
            scores_to(2, nxt, 0)

        @pl.when(i >= 1)
        def _():
            scores_to(1, i, 1)
            consume(2, 0, False)

        def pair(kk, c):
            j = 1 + 2 * kk
            scores_to(0, i, j + 1)
            consume(1, j, False)
            scores_to(1, i, j + 2)
            consume(0, j + 1, False)
            return c

        lax.fori_loop(0, jnp.maximum(n_mid, 0) // 2, pair, 0)

        @pl.when(jnp.logical_and(i >= 1, n_mid % 2 == 1))
        def _():
            scores_to(0, i, i)
            consume(1, i - 1, False)
            scores_to(2, nxt, 0)
            consume(0, i, True)

        @pl.when(jnp.logical_and(i >= 1, n_mid % 2 == 0))
        def _():
            scores_to(2, nxt, 0)
            consume(1, i, True)

        for hh in heads:
            o_ref[0, 0, i, hh * HEAD_DIM:(hh + 1) * HEAD_DIM, :] = (
                acc_scr[hh, :HEAD_DIM, :] / acc_scr[hh, HEAD_DIM:HEAD_DIM + 1, :])
        return carry

    lax.fori_loop(0, nq, query_block, 0)


def _attn_call(q_aug, k_aug, vt_aug):
    b, h, s, hp = q_aug.shape
    assert TQ == TM
    nq = s // TQ
    return pl.pallas_call(
        _attn_kernel,
        grid=(b, h // 2),
        in_specs=[pl.BlockSpec((1, 2, s, hp), lambda bi, p: (bi, p, 0, 0)),
                  pl.BlockSpec((1, 2, s, hp), lambda bi, p: (bi, p, 0, 0)),
                  pl.BlockSpec((1, 2, nq, VT_ROWS, TQ), lambda bi, p: (bi, p, 0, 0, 0))],
        out_specs=pl.BlockSpec((1, 1, nq, 2 * HEAD_DIM, TQ), lambda bi, p: (bi, p, 0, 0, 0)),
        out_shape=jax.ShapeDtypeStruct((b, h // 2, nq, 2 * HEAD_DIM, TQ), F32),
        scratch_shapes=[pltpu.VMEM((3, 2, TQ, TQ), F32), pltpu.VMEM((2, 1, TQ), F32),
                        pltpu.VMEM((2, VT_ROWS, TQ), F32)],
        compiler_params=pltpu.CompilerParams(
            dimension_semantics=("arbitrary", "arbitrary"), vmem_limit_bytes=VMEM_LIMIT),
        name="attention",
    )(q_aug, k_aug, vt_aug)


def _gelu_tanh(x):
    return 0.5 * x * (1.0 + jnp.tanh(0.7978845608028654 * (x + 0.044715 * (x * x * x))))


def _lru_kernel(xb_ref, gb_ref, cw_ref, cb_ref, wgate_ref, brg_ref, big_ref, lam_ref,
                o_ref, tail_ref, h_ref):
    j = pl.program_id(1)

    @pl.when(j == 0)
    def _():
        tail_ref[...] = jnp.zeros_like(tail_ref)
        h_ref[...] = jnp.zeros_like(h_ref)

    x = xb_ref[0]
    tm, dl = x.shape
    row_in_group = lax.broadcasted_iota(jnp.int32, (tm, dl), 0) % SUBLANES
    x_prev_group = jnp.concatenate([tail_ref[...], x[:tm - SUBLANES]], axis=0)
    xc = x * cw_ref[CONV_WIDTH - 1:CONV_WIDTH, :] + cb_ref[...]
    for k in range(1, CONV_WIDTH):
        xs = jnp.where(row_in_group < k, _rot_in_group(x_prev_group, k), _rot_in_group(x, k))
        xc = xc + xs * cw_ref[CONV_WIDTH - 1 - k:CONV_WIDTH - k, :]
    tail_ref[...] = x[tm - SUBLANES:, :]

    xcb = xc.astype(BF16)
    n_pairs = dl // LANES
    r_parts, i_parts = [], []
    for p in range(n_pairs):
        g = _dot(xcb[:, p * LANES:(p + 1) * LANES], wgate_ref[p])
        r_parts.append(g[:, :LANES])
        i_parts.append(g[:, LANES:])
    r = _sigmoid(jnp.concatenate(r_parts, axis=1) + brg_ref[...])
    ig = _sigmoid(jnp.concatenate(i_parts, axis=1) + big_ref[...])

    lam = lam_ref[...]
    softplus_neg_lam = jnp.maximum(-lam, 0.0) + jnp.log1p(jnp.exp(-jnp.abs(lam)))
    log_a = (-LRU_C) * r * softplus_neg_lam
    a = jnp.exp(log_a)
    v = 1.0 - a * a
    u = jnp.where(v > 0.0, v * lax.rsqrt(v), 0.0) * (ig * xc)

    k = 1
    while k < SUBLANES:
        keep = row_in_group >= k
        u = a * jnp.where(keep, _rot_in_group(u, k), 0.0) + u
        a = a * jnp.where(keep, _rot_in_group(a, k), 1.0)
        k *= 2
    h_prev = h_ref[...]
    groups = []
    for g in range(tm // SUBLANES):
        rows = slice(g * SUBLANES, (g + 1) * SUBLANES)
        hg = a[rows] * h_prev + u[rows]
        groups.append(hg)
        h_prev = hg[SUBLANES - 1:SUBLANES]
    h_ref[...] = h_prev
    o_ref[0] = jnp.concatenate(groups, axis=0) * _gelu_tanh(gb_ref[0])


def _lru_call(xb, gb, conv_w, conv_b, wgate, b_rg, b_ig, lam):
    b, s, dl = xb.shape
    row_spec = pl.BlockSpec((1, TM, dl), lambda bi, j: (bi, j, 0))
    full = lambda shape: pl.BlockSpec(shape, lambda bi, j: (0,) * len(shape))
    return pl.pallas_call(
        _lru_kernel,
        grid=(b, s // TM),
        in_specs=[row_spec, row_spec, full(conv_w.shape), full((1, dl)), full(wgate.shape),
                  full((1, dl)), full((1, dl)), full((1, dl))],
        out_specs=row_spec,
        out_shape=jax.ShapeDtypeStruct((b, s, dl), F32),
        scratch_shapes=[pltpu.VMEM((SUBLANES, dl), F32), pltpu.VMEM((1, dl), F32)],
        compiler_params=pltpu.CompilerParams(
            dimension_semantics=("arbitrary", "arbitrary"), vmem_limit_bytes=VMEM_LIMIT),
        name="lru",
    )(xb, gb, conv_w, conv_b, wgate, b_rg, b_ig, lam)


def _rms(x, gain):
    return x * lax.rsqrt(jnp.mean(x * x, axis=-1, keepdims=True) + RMS_EPS) * gain


def _mixout_kernel(attn_ref, lru_ref, x_ref, mod_ref, ga_ref, gl_ref, woa_ref, wol_ref,
                   g1_ref, b1_ref, wr_ref, br_ref, x1_ref, u2_ref, rinfo_ref, cnt_ref):
    tm, d = x_ref.shape
    attn = attn_ref[0, :, 0].reshape(ga_ref.shape[1], tm).T
    na = _rms(attn, ga_ref[...]).astype(BF16)
    nl = _rms(lru_ref[...], gl_ref[...]).astype(BF16)
    mix = _dot(na, woa_ref[...]) + _dot(nl, wol_ref[...])
    z = DEEPNORM_ALPHA * x_ref[...] + (1.0 + mod_ref[0, 2:3, :]) * mix
    x1 = _layer_norm(z) * g1_ref[...] + b1_ref[...]
    x1_ref[...] = x1
    u2 = _layer_norm(x1) * (1.0 + mod_ref[0, 4:5, :]) + mod_ref[0, 3:4, :]

    nt = (((1,), (1,)), ((), ()))
    uh, ul = _split2(u2)
    wh, wl = _split2(wr_ref[...])
    dg = lambda a, b_: lax.dot_general(a, b_, nt, preferred_element_type=F32)
    lg = dg(wh, uh) + (dg(wh, ul) + dg(wl, uh)) + br_ref[:, 0:1]

    def first_index(vals, target):
        idx = jnp.full_like(target, float(len(vals) - 1))
        for n in range(len(vals) - 2, -1, -1):
            idx = jnp.where(vals[n] == target, float(n), idx)
        return idx

    g = [lg[n:n + 1, :] for n in range(N_GROUPS)]
    gmax = functools.reduce(jnp.maximum, g)
    gsum = functools.reduce(lambda a, b_: a + b_, [jnp.exp(v - gmax) for v in g])
    grp_w = 1.0 / gsum
    gidx = first_index(g, gmax)

    sel = []
    for e in range(EXPERTS_PER_GROUP):
        v = lg[N_GROUPS + (N_GROUPS - 1) * EXPERTS_PER_GROUP + e:
               N_GROUPS + (N_GROUPS - 1) * EXPERTS_PER_GROUP + e + 1, :]
        for gi in range(N_GROUPS - 2, -1, -1):
            r0 = N_GROUPS + gi * EXPERTS_PER_GROUP + e
            v = jnp.where(gidx == float(gi), lg[r0:r0 + 1, :], v)
        sel.append(v)
    smax = functools.reduce(jnp.maximum, sel)
    i1 = first_index(sel, smax)
    rest = [jnp.where(i1 == float(e), -3e38, sel[e]) for e in range(EXPERTS_PER_GROUP)]
    rmax = functools.reduce(jnp.maximum, rest)
    i2 = first_index(rest, rmax)
    e2 = jnp.exp(rmax - smax)
    w1 = grp_w / (1.0 + e2)
    w2 = grp_w * e2 / (1.0 + e2)
    ia = jnp.minimum(i1, i2)
    ib = jnp.maximum(i1, i2)
    wa = jnp.where(i1 < i2, w1, w2)
    wb = jnp.where(i1 < i2, w2, w1)
    pair = jnp.where(ia == 0.0, ib - 1.0, jnp.where(ia == 1.0, ib + 1.0, 5.0))
    bucket = gidx * float(N_PAIRS) + pair

    @pl.when(pl.program_id(0) == 0)
    def _():
        cnt_ref[...] = jnp.zeros_like(cnt_ref)

    cid = lax.broadcasted_iota(jnp.int32, (BUCKET_ROWS, tm), 0).astype(F32)
    cnt_ref[...] += jnp.sum(jnp.where(cid == bucket, 1.0, 0.0), axis=1, keepdims=True)

    zrow = jnp.zeros_like(wa)
    rinfo_ref[...] = jnp.concatenate([bucket, wa, wb] + [zrow] * (SUBLANES - 3), axis=0)
    wt = jnp.concatenate([wa, wb, jnp.zeros((LANES - 2, tm), F32)], axis=0)
    u2_ref[:, :d] = u2
    u2_ref[:, d:] = wt.T


def _mixout_call(attn_t, lru, x2d, mod3, ga, gl, wo, g1, b1, wr, br, seq):
    t, d = x2d.shape
    _, n_pairs, _, pair_w, _ = attn_t.shape
    dh = n_pairs * pair_w
    assert wo.shape[0] == 2 * dh and lru.shape[1] == dh
    per_b = seq // TM
    full = lambda shape: pl.BlockSpec(shape, lambda i: (0,) * len(shape))
    return pl.pallas_call(
        _mixout_kernel,
        grid=(t // TM,),
        in_specs=[pl.BlockSpec((1, n_pairs, 1, pair_w, TM),
                               lambda i: (i // per_b, 0, i % per_b, 0, 0)),
                  pl.BlockSpec((TM, dh), lambda i: (i, 0)),
                  pl.BlockSpec((TM, d), lambda i: (i, 0)),
                  pl.BlockSpec((1, 6, d), lambda i: (i // per_b, 0, 0)),
                  full((1, dh)), full((1, dh)),
                  pl.BlockSpec((dh, d), lambda i: (0, 0)), pl.BlockSpec((dh, d), lambda i: (1, 0)),
                  full((1, d)), full((1, d)), full((BUCKET_ROWS, d)), full((BUCKET_ROWS, LANES))],
        out_specs=[pl.BlockSpec((TM, d), lambda i: (i, 0)),
                   pl.BlockSpec((TM, d + LANES), lambda i: (i, 0)),
                   pl.BlockSpec((SUBLANES, TM), lambda i: (0, i)),
                   pl.BlockSpec((BUCKET_ROWS, LANES), lambda i: (0, 0))],
        out_shape=[jax.ShapeDtypeStruct((t, d), F32),
                   jax.ShapeDtypeStruct((t, d + LANES), F32),
                   jax.ShapeDtypeStruct((SUBLANES, t), F32),
                   jax.ShapeDtypeStruct((BUCKET_ROWS, LANES), F32)],
        compiler_params=pltpu.CompilerParams(
            dimension_semantics=("arbitrary",), vmem_limit_bytes=VMEM_LIMIT),
        name="mixout",
    )(attn_t, lru, x2d, mod3, ga, gl, wo, wo, g1, b1, wr, br)


def _rank_kernel(rinfo_ref, counts_ref, dest_ref, carry_ref, offs_ref):
    tm = rinfo_ref.shape[1]

    @pl.when(pl.program_id(0) == 0)
    def _():
        carry_ref[...] = jnp.zeros_like(carry_ref)
        padded = jnp.floor((counts_ref[...] + float(TM_E - 1)) * (1.0 / TM_E)) * float(TM_E)
        inc = padded
        k = 1
        while k < BUCKET_ROWS:
            inc = inc + _shift_rows(inc, k, 0.0)
            k *= 2
        offs_ref[...] = inc - padded

    bucket = rinfo_ref[0:1, :]
    cid = lax.broadcasted_iota(jnp.int32, (BUCKET_ROWS, tm), 0).astype(F32)
    onehot = jnp.where(cid == bucket, 1.0, 0.0)
    srow = lax.broadcasted_iota(jnp.int32, (tm, tm), 0)
    scol = lax.broadcasted_iota(jnp.int32, (tm, tm), 1)
    upper = (srow <= scol).astype(BF16)
    prefix = _dot(onehot.astype(BF16), upper)
    carry = carry_ref[...]
    rank = jnp.sum(onehot * (prefix - 1.0 + carry[:, 0:1] + offs_ref[:, 0:1]),
                   axis=0, keepdims=True)
    dest_ref[...] = rank.astype(jnp.int32)
    carry_ref[...] = carry + prefix[:, tm - 1:tm]


def _rank_call(rinfo, counts):
    t = rinfo.shape[1]
    return pl.pallas_call(
        _rank_kernel,
        grid=(t // TM,),
        in_specs=[pl.BlockSpec((SUBLANES, TM), lambda i: (0, i)),
                  pl.BlockSpec((BUCKET_ROWS, LANES), lambda i: (0, 0))],
        out_specs=pl.BlockSpec((1, TM), lambda i: (0, i)),
        out_shape=jax.ShapeDtypeStruct((1, t), jnp.int32),
        scratch_shapes=[pltpu.VMEM((BUCKET_ROWS, LANES), F32),
                        pltpu.VMEM((BUCKET_ROWS, LANES), F32)],
        compiler_params=pltpu.CompilerParams(
            dimension_semantics=("arbitrary",), vmem_limit_bytes=VMEM_LIMIT),
        name="rank",
    )(rinfo, counts)


def _dispatch_kernel(dest_ref, ends_ref, u2_ref, xs_ref, zbuf, sem, zsem):
    tm = u2_ref.shape[0]
    t0 = pl.program_id(0) * tm

    @pl.when(pl.program_id(0) == 0)
    def _():
        zbuf[...] = jnp.zeros_like(zbuf)

        def tail_copy(bkt):
            end = ends_ref[bkt]
            start = ends_ref[bkt - 1] if bkt else 0
            tail = pl.multiple_of(jnp.maximum(end - TM_E, 0), TM_E)
            return end > start, pltpu.make_async_copy(zbuf, xs_ref.at[pl.ds(tail, TM_E)], zsem)

        for bkt in range(N_BUCKETS):
            nonempty, copy = tail_copy(bkt)
            pl.when(nonempty)(copy.start)
        for bkt in range(N_BUCKETS):
            nonempty, copy = tail_copy(bkt)
            pl.when(nonempty)(copy.wait)

        def unused_tile_copy(k):
            return pltpu.make_async_copy(
                zbuf, xs_ref.at[pl.ds(pl.multiple_of(k * TM_E, TM_E), TM_E)], zsem)

        first_unused = ends_ref[N_BUCKETS - 1] // TM_E
        n_tiles = xs_ref.shape[0] // TM_E
        lax.fori_loop(first_unused, n_tiles, lambda k, c: (unused_tile_copy(k).start(), c)[1], 0)
        lax.fori_loop(first_unused, n_tiles, lambda k, c: (unused_tile_copy(k).wait(), c)[1], 0)

    def issue(r, c):
        pltpu.make_async_copy(u2_ref.at[pl.ds(r, 1)],
                              xs_ref.at[pl.ds(dest_ref[t0 + r], 1)], sem).start()
        return c

    lax.fori_loop(0, tm, issue, 0, unroll=ISSUE_UNROLL)
    pltpu.make_async_copy(u2_ref, xs_ref.at[pl.ds(0, tm)], sem).wait()


def _dispatch_call(dest, bucket_ends, u2ext, n_rows):
    t, w = u2ext.shape
    grid_spec = pltpu.PrefetchScalarGridSpec(
        num_scalar_prefetch=2,
        grid=(t // TM,),
        in_specs=[pl.BlockSpec((TM, w), lambda i, dest_ref, ends_ref: (i, 0))],
        out_specs=pl.BlockSpec(memory_space=pl.ANY),
        scratch_shapes=[pltpu.VMEM((TM_E, w), F32), pltpu.SemaphoreType.DMA(()),
                        pltpu.SemaphoreType.DMA(())],
    )
    return pl.pallas_call(
        _dispatch_kernel,
        grid_spec=grid_spec,
        out_shape=jax.ShapeDtypeStruct((n_rows, w), F32),
        compiler_params=pltpu.CompilerParams(
            dimension_semantics=("arbitrary",), vmem_limit_bytes=VMEM_LIMIT),
        name="dispatch",
    )(dest, bucket_ends, u2ext)


def _experts_kernel(ea_ref, eb_ref, nv_ref, xs_ref, wga_ref, wua_ref, wda_ref,
                    wgb_ref, wub_ref, wdb_ref, ys_ref):
    del ea_ref, eb_ref
    i = pl.program_id(0)
    d = ys_ref.shape[1]

    @pl.when(i < nv_ref[0])
    def _():
        x = xs_ref[:, :d].astype(BF16)

        def expert(wg_ref, wu_ref, wd_ref):
            g = _dot(x, wg_ref[0])
            h = (g * _sigmoid(g)) * _dot(x, wu_ref[0])
            return _dot(h.astype(BF16), wd_ref[0])

        ya = xs_ref[:, d:d + 1] * expert(wga_ref, wua_ref, wda_ref)
        ys_ref[...] = ya + xs_ref[:, d + 1:d + 2] * expert(wgb_ref, wub_ref, wdb_ref)

    @pl.when(i >= nv_ref[0])
    def _():
        ys_ref[...] = jnp.zeros_like(ys_ref)


def _experts_call(tile_ea, tile_eb, n_valid, xs, wg, wu, wd):
    tp, w = xs.shape
    _, d, de = wg.shape
    row = lambda i, ea, eb, nv: (jnp.minimum(i, nv[0] - 1), 0)
    wa = lambda i, ea, eb, nv: (ea[i], 0, 0)
    wb = lambda i, ea, eb, nv: (eb[i], 0, 0)
    grid_spec = pltpu.PrefetchScalarGridSpec(
        num_scalar_prefetch=3,
        grid=(tp // TM_E,),
        in_specs=[pl.BlockSpec((TM_E, w), row),
                  pl.BlockSpec((1, d, de), wa), pl.BlockSpec((1, d, de), wa),
                  pl.BlockSpec((1, de, d), wa),
                  pl.BlockSpec((1, d, de), wb), pl.BlockSpec((1, d, de), wb),
                  pl.BlockSpec((1, de, d), wb)],
        out_specs=pl.BlockSpec((TM_E, d), lambda i, ea, eb, nv: (i, 0)),
    )
    return pl.pallas_call(
        _experts_kernel,
        grid_spec=grid_spec,
        out_shape=jax.ShapeDtypeStruct((tp, d), F32),
        compiler_params=pltpu.CompilerParams(
            dimension_semantics=("arbitrary",), vmem_limit_bytes=VMEM_LIMIT),
        name="experts",
    )(tile_ea, tile_eb, n_valid, xs, wg, wu, wd, wg, wu, wd)


def _final_kernel(dest_ref, x1_ref, mod_ref, g2_ref, b2_ref, ys_ref, o_ref, ybuf, sem):
    tm = x1_ref.shape[0]
    i = pl.program_id(0)
    slot = i % GATHER_AHEAD

    def issue_rows(tile, to_slot, r0, n):
        for r in range(n):
            pltpu.make_async_copy(ys_ref.at[pl.ds(dest_ref[tile * tm + r0 + r], 1)],
                                  ybuf.at[to_slot, pl.ds(r0 + r, 1)], sem.at[to_slot]).start()

    def normalise_rows(r0, n):
        rows = pl.ds(r0, n)
        z = DEEPNORM_ALPHA * x1_ref[rows, :] + (1.0 + mod_ref[0, 5:6, :]) * ybuf[slot, rows, :]
        o_ref[rows, :] = _layer_norm(z) * g2_ref[...] + b2_ref[...]

    n_chunks = tm // ISSUE_UNROLL

    @pl.when(i == 0)
    def _():
        for tile in range(GATHER_AHEAD):
            lax.fori_loop(0, n_chunks, lambda c, x, tile=tile: (
                issue_rows(tile, tile, c * ISSUE_UNROLL, ISSUE_UNROLL), x)[1], 0)

    pltpu.make_async_copy(ys_ref.at[pl.ds(0, tm)], ybuf.at[slot], sem.at[slot]).wait()

    def chunk(c, issue_ahead):
        r0 = pl.multiple_of(c * ISSUE_UNROLL, ISSUE_UNROLL)
        normalise_rows(r0, ISSUE_UNROLL)
        if issue_ahead:
            issue_rows(i + GATHER_AHEAD, slot, r0, ISSUE_UNROLL)

    more = i + GATHER_AHEAD < pl.num_programs(0)

    @pl.when(more)
    def _():
        lax.fori_loop(0, n_chunks, lambda c, x: (chunk(c, True), x)[1], 0)

    @pl.when(jnp.logical_not(more))
    def _():
        lax.fori_loop(0, n_chunks, lambda c, x: (chunk(c, False), x)[1], 0)


def _final_call(dest, x1, mod3, g2, b2, ys, seq):
    t, d = x1.shape
    per_b = seq // TM
    grid_spec = pltpu.PrefetchScalarGridSpec(
        num_scalar_prefetch=1,
        grid=(t // TM,),
        in_specs=[pl.BlockSpec((TM, d), lambda i, dr: (i, 0)),
                  pl.BlockSpec((1, 6, d), lambda i, dr: (i // per_b, 0, 0)),
                  pl.BlockSpec((1, d), lambda i, dr: (0, 0)),
                  pl.BlockSpec((1, d), lambda i, dr: (0, 0)),
                  pl.BlockSpec(memory_space=pl.ANY)],
        out_specs=pl.BlockSpec((TM, d), lambda i, dr: (i, 0)),
        scratch_shapes=[pltpu.VMEM((GATHER_AHEAD, TM, d), F32),
                        pltpu.SemaphoreType.DMA((GATHER_AHEAD,))],
    )
    return pl.pallas_call(
        _final_kernel,
        grid_spec=grid_spec,
        out_shape=jax.ShapeDtypeStruct((t, d), F32),
        compiler_params=pltpu.CompilerParams(
            dimension_semantics=("arbitrary",), vmem_limit_bytes=VMEM_LIMIT),
        name="final",
    )(dest, x1, mod3, g2, b2, ys)


def _gate_pairs(w_rg, w_ig):
    def pairs(w):
        n, bs, _ = w.shape
        w = w.reshape(n // 2, 2, bs, bs)
        z = jnp.zeros((n // 2, bs, bs), w.dtype)
        top = jnp.concatenate([w[:, 0], z], axis=2)
        bot = jnp.concatenate([z, w[:, 1]], axis=2)
        return jnp.concatenate([top, bot], axis=1)
    return jnp.concatenate([pairs(w_rg), pairs(w_ig)], axis=2).astype(BF16)


def kernel(x, c, w_ada, b_ada, w_in, b_f, conv_w, conv_b, w_rg, b_rg, w_ig, b_ig, lru_lambda,
           g_attn, g_lru, w_out, ln1_g, ln1_b, w_grp, b_grp, w_exp, b_exp,
           w_e_gate, w_e_up, w_e_down, ln2_g, ln2_b):
    assert w_ada.shape[0] == DEPTH
    b, s, d = x.shape
    t = b * s
    d_attn = N_HEADS * HEAD_DIM
    d_lru = conv_w.shape[2]
    n_exp = w_e_gate.shape[1]
    assert s % TM == 0 and s % TQ == 0 and t % TM_E == 0

    c_pad = jnp.pad(c, ((0, SUBLANES - b), (0, 0)))
    mod = _mod_call(c_pad, w_ada, b_ada[0][None, :])
    mod3 = mod[:b].reshape(b, 6, d)

    wi = w_in[0]
    o = 3 * d_attn
    wq = (wi[:, :d_attn] * (HEAD_DIM ** -0.5 * LOG2E)).astype(BF16)
    wk = wi[:, d_attn:2 * d_attn].astype(BF16)
    wv = wi[:, 2 * d_attn:o].astype(BF16)
    wf = jnp.pad(wi[:, o:o + N_HEADS], ((0, 0), (0, LANES - N_HEADS))).astype(BF16)
    wx = wi[:, o + N_HEADS:o + N_HEADS + d_lru].astype(BF16)
    wg = wi[:, o + N_HEADS + d_lru:].astype(BF16)
    bf_pad = jnp.pad(b_f[0], (0, LANES - N_HEADS))[None, :]
    q_aug, k_aug, vt_aug, xb, gb = _inproj_call(x, mod3, wq, wk, wv, wf, wx, wg, bf_pad)
    attn_t = _attn_call(q_aug, k_aug, vt_aug)
    lru = _lru_call(xb, gb, conv_w[0], conv_b[0][None, :], _gate_pairs(w_rg[0], w_ig[0]),
                    b_rg[0][None, :], b_ig[0][None, :], lru_lambda[0][None, :])

    wo = w_out[0].astype(BF16)
    n_route = N_GROUPS + n_exp
    wr = jnp.pad(jnp.concatenate([w_grp[0], w_exp[0]], axis=1).T,
                 ((0, BUCKET_ROWS - n_route), (0, 0)))
    br = jnp.pad(jnp.concatenate([b_grp[0], b_exp[0]]), (0, BUCKET_ROWS - n_route))
    br = jnp.broadcast_to(br[:, None], (BUCKET_ROWS, LANES))
    x1, u2ext, rinfo, counts = _mixout_call(
        attn_t, lru.reshape(t, d_lru), x.reshape(t, d), mod3,
        g_attn[0][None, :], g_lru[0][None, :], wo,
        ln1_g[0][None, :], ln1_b[0][None, :], wr, br, s)

    dest = _rank_call(rinfo, counts).reshape(t)
    cnt = counts[:N_BUCKETS, 0].astype(jnp.int32)
    ends = jnp.cumsum((cnt + (TM_E - 1)) // TM_E)
    n_tiles = t // TM_E + N_BUCKETS
    tile_bucket = jnp.sum(ends[None, :] <= jnp.arange(n_tiles)[:, None], axis=1)
    tile_bucket = jnp.minimum(tile_bucket, N_BUCKETS - 1)
    n_valid = ends[N_BUCKETS - 1:]
    last_bucket = tile_bucket[jnp.maximum(n_valid[0] - 1, 0)]
    tile_bucket = jnp.where(jnp.arange(n_tiles) < n_valid[0], tile_bucket, last_bucket)
    tile_ea = jnp.asarray(_BUCKET_EA, jnp.int32)[tile_bucket]
    tile_eb = jnp.asarray(_BUCKET_EB, jnp.int32)[tile_bucket]

    xs = _dispatch_call(dest, (ends * TM_E).astype(jnp.int32), u2ext, n_tiles * TM_E)
    ys = _experts_call(tile_ea, tile_eb, n_valid.astype(jnp.int32), xs,
                       w_e_gate[0].astype(BF16), w_e_up[0].astype(BF16),
                       w_e_down[0].astype(BF16))
    out = _final_call(dest, x1, mod3, ln2_g[0][None, :], ln2_b[0][None, :], ys, s)
    return out.reshape(b, s, d)
```

```python
import functools

import jax
import jax.numpy as jnp
from jax import lax
from jax.experimental import pallas as pl
from jax.experimental.pallas import tpu as pltpu

F32 = jnp.float32
BF16 = jnp.bfloat16

HEAD_DIM = 64
N_HEADS = 8
N_LRU_BLOCKS = 8
CONV_WIDTH = 4
LRU_C = 8.0
N_GROUPS = 4
EXPERTS_PER_GROUP = 4
N_PAIRS = 6
N_BUCKETS = N_GROUPS * N_PAIRS
LN_EPS = 1e-5
RMS_EPS = 1e-6
NEG_INF = -1e30
DEPTH = 1
DEEPNORM_ALPHA = (2.0 * DEPTH) ** 0.25
LOG2E = 1.4426950408889634

LANES = 128
SUBLANES = 8
HEAD_PAD = LANES
VT_ROWS = 80
BUCKET_ROWS = 32
TM = 512
TQ = 512
TM_E = 256
ISSUE_UNROLL = 64
GATHER_AHEAD = 2
VMEM_LIMIT = 56 * 1024 * 1024

_PAIRS = [(0, 1), (0, 2), (0, 3), (1, 2), (1, 3), (2, 3)]
_BUCKET_EA = [g * EXPERTS_PER_GROUP + a for g in range(N_GROUPS) for (a, b) in _PAIRS]
_BUCKET_EB = [g * EXPERTS_PER_GROUP + b for g in range(N_GROUPS) for (a, b) in _PAIRS]


def _dot(a, b):
    return jnp.dot(a, b, preferred_element_type=F32)


def _split2(a):
    hi = a.astype(BF16)
    lo = (a - hi.astype(F32)).astype(BF16)
    return hi, lo


def _split3(a):
    hi = a.astype(BF16)
    r = a - hi.astype(F32)
    mid = r.astype(BF16)
    lo = (r - mid.astype(F32)).astype(BF16)
    return hi, mid, lo


def _layer_norm(x):
    mu = jnp.mean(x, axis=-1, keepdims=True)
    xc = x - mu
    var = jnp.mean(xc * xc, axis=-1, keepdims=True)
    return xc * lax.rsqrt(var + LN_EPS)


def _sigmoid(x):
    return 0.5 * jnp.tanh(0.5 * x) + 0.5


def _rot_in_group(x, k):
    n, w = x.shape
    return pltpu.roll(x.reshape(n // SUBLANES, SUBLANES, w), k, 1).reshape(n, w)


def _log_sigmoid(z):
    return jnp.minimum(z, 0.0) - jnp.log1p(jnp.exp(-jnp.abs(z)))


def _cumsum_rows(x, carry):
    n, w = x.shape
    row_in_group = lax.broadcasted_iota(jnp.int32, (n, w), 0) % SUBLANES
    k = 1
    while k < SUBLANES:
        x = x + jnp.where(row_in_group >= k, _rot_in_group(x, k), 0.0)
        k *= 2
    groups = []
    for g in range(n // SUBLANES):
        blk = x[g * SUBLANES:(g + 1) * SUBLANES] + carry
        groups.append(blk)
        carry = blk[SUBLANES - 1:SUBLANES]
    return jnp.concatenate(groups, axis=0)


def _shift_rows(x, k, fill):
    n = x.shape[0]
    if k % SUBLANES == 0:
        return jnp.concatenate([jnp.full((k, x.shape[1]), fill, x.dtype), x[:n - k]], axis=0)
    row = lax.broadcasted_iota(jnp.int32, x.shape, 0)
    return jnp.where(row >= k, pltpu.roll(x, k, 0), fill)


def _mod_kernel(c_ref, w_ref, b_ref, o_ref):
    c = c_ref[...]
    s = c * _sigmoid(c)
    sh, sl = _split2(s)
    wh, wl = _split2(w_ref[0])
    o_ref[...] = _dot(sh, wh) + (_dot(sh, wl) + _dot(sl, wh)) + b_ref[...]


def _mod_call(c_pad, w_ada, b_ada):
    rows, d = c_pad.shape
    n = w_ada.shape[2]
    return pl.pallas_call(
        _mod_kernel,
        grid=(n // d,),
        in_specs=[pl.BlockSpec((rows, d), lambda j: (0, 0)),
                  pl.BlockSpec((1, d, d), lambda j: (0, 0, j)),
                  pl.BlockSpec((1, d), lambda j: (0, j))],
        out_specs=pl.BlockSpec((rows, d), lambda j: (0, j)),
        out_shape=jax.ShapeDtypeStruct((rows, n), F32),
        compiler_params=pltpu.CompilerParams(vmem_limit_bytes=VMEM_LIMIT),
        name="mod",
    )(c_pad, w_ada, b_ada)


def _inproj_kernel(x_ref, mod_ref, wq_ref, wk_ref, wv_ref, wf_ref, wx_ref, wg_ref, bf_ref,
                   q_ref, k_ref, vt_ref, xb_ref, gb_ref, carry_ref, v_scr):
    j = pl.program_id(1)

    @pl.when(j == 0)
    def _():
        carry_ref[...] = jnp.zeros_like(carry_ref)

    tm = x_ref.shape[1]
    u = _layer_norm(x_ref[0]) * (1.0 + mod_ref[0, 1:2, :]) + mod_ref[0, 0:1, :]
    ub = u.astype(BF16)

    logf = _log_sigmoid(_dot(ub, wf_ref[...]) + bf_ref[...])
    cum = _cumsum_rows(logf, carry_ref[...])
    carry_ref[...] = cum[tm - 1:tm, :]
    c_hi, c_mid, c_lo = [p.astype(F32) for p in _split3(cum * LOG2E)]

    lane = lax.broadcasted_iota(jnp.int32, (tm, HEAD_PAD), 1)
    d = HEAD_DIM
    q_all = _dot(ub, wq_ref[...])
    k_all = _dot(ub, wk_ref[...])
    for h in range(N_HEADS):
        ts = slice((h // 2) * LANES, (h // 2 + 1) * LANES)
        qh, kh = q_all[:, ts], k_all[:, ts]
        if h % 2:
            qh, kh = pltpu.roll(qh, d, 1), pltpu.roll(kh, d, 1)
        hi, mid, lo = c_hi[:, h:h + 1], c_mid[:, h:h + 1], c_lo[:, h:h + 1]
        q_ext = jnp.where(lane == d, hi, jnp.where(lane == d + 1, mid, jnp.where(
            lane == d + 2, lo, jnp.where(lane < d + 6, 1.0, 0.0))))
        k_ext = jnp.where(lane < d + 3, 1.0, jnp.where(lane == d + 3, -hi, jnp.where(
            lane == d + 4, -mid, jnp.where(lane == d + 5, -lo, 0.0))))
        q_ref[0, h] = jnp.where(lane < d, qh, q_ext).astype(BF16)
        k_ref[0, h] = jnp.where(lane < d, kh, k_ext).astype(BF16)

    v_scr[...] = _dot(ub, wv_ref[...])
    vt_all = v_scr[...].T
    pad_rows = lax.broadcasted_iota(jnp.int32, (VT_ROWS - d, tm), 0)
    ones_row = jnp.where(pad_rows == 0, 1.0, 0.0).astype(BF16)
    for h in range(N_HEADS):
        vt_ref[0, h, 0, :d, :] = vt_all[h * d:(h + 1) * d, :].astype(BF16)
        vt_ref[0, h, 0, d:, :] = ones_row

    xb_ref[0] = _dot(ub, wx_ref[...])
    gb_ref[0] = _dot(ub, wg_ref[...])


def _inproj_call(x, mod3, wq, wk, wv, wf, wx, wg, bf_pad):
    b, s, d = x.shape
    d_attn = wq.shape[1]
    d_lru = wx.shape[1]
    full = lambda shape: pl.BlockSpec(shape, lambda bi, j: (0,) * len(shape))
    head_spec = pl.BlockSpec((1, N_HEADS, TM, HEAD_PAD), lambda bi, j: (bi, 0, j, 0))
    vt_spec = pl.BlockSpec((1, N_HEADS, 1, VT_ROWS, TM), lambda bi, j: (bi, 0, j, 0, 0))
    row_spec = pl.BlockSpec((1, TM, d_lru), lambda bi, j: (bi, j, 0))
    head_shape = jax.ShapeDtypeStruct((b, N_HEADS, s, HEAD_PAD), BF16)
    return pl.pallas_call(
        _inproj_kernel,
        grid=(b, s // TM),
        in_specs=[pl.BlockSpec((1, TM, d), lambda bi, j: (bi, j, 0)),
                  pl.BlockSpec((1, 6, d), lambda bi, j: (bi, 0, 0)),
                  full((d, d_attn)), full((d, d_attn)), full((d, d_attn)), full((d, LANES)),
                  full((d, d_lru)), full((d, d_lru)), full((1, LANES))],
        out_specs=[head_spec, head_spec, vt_spec, row_spec, row_spec],
        out_shape=[head_shape, head_shape,
                   jax.ShapeDtypeStruct((b, N_HEADS, s // TM, VT_ROWS, TM), BF16),
                   jax.ShapeDtypeStruct((b, s, d_lru), F32),
                   jax.ShapeDtypeStruct((b, s, d_lru), F32)],
        scratch_shapes=[pltpu.VMEM((1, LANES), F32), pltpu.VMEM((TM, d_attn), F32)],
        compiler_params=pltpu.CompilerParams(
            dimension_semantics=("arbitrary", "arbitrary"), vmem_limit_bytes=VMEM_LIMIT),
        name="inproj",
    )(x, mod3, wq, wk, wv, wf, wx, wg, bf_pad)


def _attn_kernel(q_ref, k_ref, vt_ref, o_ref, s_scr, m_scr, acc_scr):
    tq = TQ
    nq = q_ref.shape[2] // tq
    heads = range(q_ref.shape[1])

    def scores_to(slot, i, j):
        for hh in heads:
            k = k_ref[0, hh, pl.ds(pl.multiple_of(j * tq, tq), tq), :]
            q = q_ref[0, hh, pl.ds(pl.multiple_of(i * tq, tq), tq), :]
            s_scr[slot, hh] = lax.dot_general(
                k, q, (((1,), (1,)), ((), ())), preferred_element_type=F32)

    def consume(slot, j, masked):
        for hh in heads:
            s = s_scr[slot, hh]
            if masked:
                key = lax.broadcasted_iota(jnp.int32, (tq, tq), 0)
                qry = lax.broadcasted_iota(jnp.int32, (tq, tq), 1)
                s = jnp.where(key <= qry, s, NEG_INF)
            m = m_scr[hh]
            m_new = jnp.maximum(m, jnp.max(s, axis=0, keepdims=True))
            p = jnp.exp2(s - m_new).astype(BF16)
            acc_scr[hh] = jnp.exp2(m - m_new) * acc_scr[hh] + _dot(vt_ref[0, hh, j], p)
            m_scr[hh] = m_new

    scores_to(2, 0, 0)

    def query_block(i, carry):
        m_scr[...] = jnp.full(m_scr.shape, NEG_INF, F32)
        acc_scr[...] = jnp.zeros(acc_scr.shape, F32)
        nxt = jnp.minimum(i + 1, nq - 1)
        n_mid = i - 1

        @pl.when(i == 0)
        def _():
            consume(2, 0, True)
            scores_to(2, nxt, 0)

        @pl.when(i >= 1)
        def _():
            scores_to(1, i, 1)
            consume(2, 0, False)

        def pair(kk, c):
            j = 1 + 2 * kk
            scores_to(0, i, j + 1)
            consume(1, j, False)
            scores_to(1, i, j + 2)
            consume(0, j + 1, False)
            return c

        lax.fori_loop(0, jnp.maximum(n_mid, 0) // 2, pair, 0)

        @pl.when(jnp.logical_and(i >= 1, n_mid % 2 == 1))
        def _():
            scores_to(0, i, i)
            consume(1, i - 1, False)
            scores_to(2, nxt, 0)
            consume(0, i, True)

        @pl.when(jnp.logical_and(i >= 1, n_mid % 2 == 0))
        def _():
            scores_to(2, nxt, 0)
            consume(1, i, True)

        for hh in heads:
            o_ref[0, 0, i, hh * HEAD_DIM:(hh + 1) * HEAD_DIM, :] = (
                acc_scr[hh, :HEAD_DIM, :] / acc_scr[hh, HEAD_DIM:HEAD_DIM + 1, :])
        return carry

    lax.fori_loop(0, nq, query_block, 0)


def _attn_call(q_aug, k_aug, vt_aug):
    b, h, s, hp = q_aug.shape
    assert TQ == TM
    nq = s // TQ
    return pl.pallas_call(
        _attn_kernel,
        grid=(b, h // 2),
        in_specs=[pl.BlockSpec((1, 2, s, hp), lambda bi, p: (bi, p, 0, 0)),
                  pl.BlockSpec((1, 2, s, hp), lambda bi, p: (bi, p, 0, 0)),
                  pl.BlockSpec((1, 2, nq, VT_ROWS, TQ), lambda bi, p: (bi, p, 0, 0, 0))],
        out_specs=pl.BlockSpec((1, 1, nq, 2 * HEAD_DIM, TQ), lambda bi, p: (bi, p, 0, 0, 0)),
        out_shape=jax.ShapeDtypeStruct((b, h // 2, nq, 2 * HEAD_DIM, TQ), F32),
        scratch_shapes=[pltpu.VMEM((3, 2, TQ, TQ), F32), pltpu.VMEM((2, 1, TQ), F32),
                        pltpu.VMEM((2, VT_ROWS, TQ), F32)],
        compiler_params=pltpu.CompilerParams(
            dimension_semantics=("arbitrary", "arbitrary"), vmem_limit_bytes=VMEM_LIMIT),
        name="attention",
    )(q_aug, k_aug, vt_aug)


def _gelu_tanh(x):
    return 0.5 * x * (1.0 + jnp.tanh(0.7978845608028654 * (x + 0.044715 * (x * x * x))))


def _lru_kernel(xb_ref, gb_ref, cw_ref, cb_ref, wgate_ref, brg_ref, big_ref, lam_ref,
                o_ref, tail_ref, h_ref):
    j = pl.program_id(1)

    @pl.when(j == 0)
    def _():
        tail_ref[...] = jnp.zeros_like(tail_ref)
        h_ref[...] = jnp.zeros_like(h_ref)

    x = xb_ref[0]
    tm, dl = x.shape
    row_in_group = lax.broadcasted_iota(jnp.int32, (tm, dl), 0) % SUBLANES
    x_prev_group = jnp.concatenate([tail_ref[...], x[:tm - SUBLANES]], axis=0)
    xc = x * cw_ref[CONV_WIDTH - 1:CONV_WIDTH, :] + cb_ref[...]
    for k in range(1, CONV_WIDTH):
        xs = jnp.where(row_in_group < k, _rot_in_group(x_prev_group, k), _rot_in_group(x, k))
        xc = xc + xs * cw_ref[CONV_WIDTH - 1 - k:CONV_WIDTH - k, :]
    tail_ref[...] = x[tm - SUBLANES:, :]

    xcb = xc.astype(BF16)
    n_pairs = dl // LANES
    r_parts, i_parts = [], []
    for p in range(n_pairs):
        g = _dot(xcb[:, p * LANES:(p + 1) * LANES], wgate_ref[p])
        r_parts.append(g[:, :LANES])
        i_parts.append(g[:, LANES:])
    r = _sigmoid(jnp.concatenate(r_parts, axis=1) + brg_ref[...])
    ig = _sigmoid(jnp.concatenate(i_parts, axis=1) + big_ref[...])

    lam = lam_ref[...]
    softplus_neg_lam = jnp.maximum(-lam, 0.0) + jnp.log1p(jnp.exp(-jnp.abs(lam)))
    log_a = (-LRU_C) * r * softplus_neg_lam
    a = jnp.exp(log_a)
    v = 1.0 - a * a
    u = jnp.where(v > 0.0, v * lax.rsqrt(v), 0.0) * (ig * xc)

    k = 1
    while k < SUBLANES:
        keep = row_in_group >= k
        u = a * jnp.where(keep, _rot_in_group(u, k), 0.0) + u
        a = a * jnp.where(keep, _rot_in_group(a, k), 1.0)
        k *= 2
    h_prev = h_ref[...]
    groups = []
    for g in range(tm // SUBLANES):
        rows = slice(g * SUBLANES, (g + 1) * SUBLANES)
        hg = a[rows] * h_prev + u[rows]
        groups.append(hg)
        h_prev = hg[SUBLANES - 1:SUBLANES]
    h_ref[...] = h_prev
    o_ref[0] = jnp.concatenate(groups, axis=0) * _gelu_tanh(gb_ref[0])


def _lru_call(xb, gb, conv_w, conv_b, wgate, b_rg, b_ig, lam):
    b, s, dl = xb.shape
    row_spec = pl.BlockSpec((1, TM, dl), lambda bi, j: (bi, j, 0))
    full = lambda shape: pl.BlockSpec(shape, lambda bi, j: (0,) * len(shape))
    return pl.pallas_call(
        _lru_kernel,
        grid=(b, s // TM),
        in_specs=[row_spec, row_spec, full(conv_w.shape), full((1, dl)), full(wgate.shape),
                  full((1, dl)), full((1, dl)), full((1, dl))],
        out_specs=row_spec,
        out_shape=jax.ShapeDtypeStruct((b, s, dl), F32),
        scratch_shapes=[pltpu.VMEM((SUBLANES, dl), F32), pltpu.VMEM((1, dl), F32)],
        compiler_params=pltpu.CompilerParams(
            dimension_semantics=("arbitrary", "arbitrary"), vmem_limit_bytes=VMEM_LIMIT),
        name="lru",
    )(xb, gb, conv_w, conv_b, wgate, b_rg, b_ig, lam)


def _rms(x, gain):
    return x * lax.rsqrt(jnp.mean(x * x, axis=-1, keepdims=True) + RMS_EPS) * gain


def _mixout_kernel(attn_ref, lru_ref, x_ref, mod_ref, ga_ref, gl_ref, woa_ref, wol_ref,
                   g1_ref, b1_ref, wr_ref, br_ref, x1_ref, u2_ref, rinfo_ref, cnt_ref, wo_bf):
    tm, d = x_ref.shape

    @pl.when(pl.program_id(0) == 0)
    def _():
        wo_bf[0] = woa_ref[0].astype(BF16)
        wo_bf[1] = wol_ref[0].astype(BF16)

    attn = attn_ref[0, :, 0].reshape(ga_ref.shape[1], tm).T
    na = _rms(attn, ga_ref[...]).astype(BF16)
    nl = _rms(lru_ref[...], gl_ref[...]).astype(BF16)
    mix = _dot(na, wo_bf[0]) + _dot(nl, wo_bf[1])
    z = DEEPNORM_ALPHA * x_ref[...] + (1.0 + mod_ref[0, 2:3, :]) * mix
    x1 = _layer_norm(z) * g1_ref[...] + b1_ref[...]
    x1_ref[...] = x1
    u2 = _layer_norm(x1) * (1.0 + mod_ref[0, 4:5, :]) + mod_ref[0, 3:4, :]

    nt = (((1,), (1,)), ((), ()))
    uh, ul = _split2(u2)
    wh, wl = _split2(wr_ref[...])
    dg = lambda a, b_: lax.dot_general(a, b_, nt, preferred_element_type=F32)
    lg = dg(wh, uh) + (dg(wh, ul) + dg(wl, uh)) + br_ref[:, 0:1]

    def first_index(vals, target):
        idx = jnp.full_like(target, float(len(vals) - 1))
        for n in range(len(vals) - 2, -1, -1):
            idx = jnp.where(vals[n] == target, float(n), idx)
        return idx

    g = [lg[n:n + 1, :] for n in range(N_GROUPS)]
    gmax = functools.reduce(jnp.maximum, g)
    gsum = functools.reduce(lambda a, b_: a + b_, [jnp.exp(v - gmax) for v in g])
    grp_w = 1.0 / gsum
    gidx = first_index(g, gmax)

    sel = []
    for e in range(EXPERTS_PER_GROUP):
        v = lg[N_GROUPS + (N_GROUPS - 1) * EXPERTS_PER_GROUP + e:
               N_GROUPS + (N_GROUPS - 1) * EXPERTS_PER_GROUP + e + 1, :]
        for gi in range(N_GROUPS - 2, -1, -1):
            r0 = N_GROUPS + gi * EXPERTS_PER_GROUP + e
            v = jnp.where(gidx == float(gi), lg[r0:r0 + 1, :], v)
        sel.append(v)
    smax = functools.reduce(jnp.maximum, sel)
    i1 = first_index(sel, smax)
    rest = [jnp.where(i1 == float(e), -3e38, sel[e]) for e in range(EXPERTS_PER_GROUP)]
    rmax = functools.reduce(jnp.maximum, rest)
    i2 = first_index(rest, rmax)
    e2 = jnp.exp(rmax - smax)
    w1 = grp_w / (1.0 + e2)
    w2 = grp_w * e2 / (1.0 + e2)
    ia = jnp.minimum(i1, i2)
    ib = jnp.maximum(i1, i2)
    wa = jnp.where(i1 < i2, w1, w2)
    wb = jnp.where(i1 < i2, w2, w1)
    pair = jnp.where(ia == 0.0, ib - 1.0, jnp.where(ia == 1.0, ib + 1.0, 5.0))
    bucket = gidx * float(N_PAIRS) + pair

    @pl.when(pl.program_id(0) == 0)
    def _():
        cnt_ref[...] = jnp.zeros_like(cnt_ref)

    cid = lax.broadcasted_iota(jnp.int32, (BUCKET_ROWS, tm), 0).astype(F32)
    cnt_ref[...] += jnp.sum(jnp.where(cid == bucket, 1.0, 0.0), axis=1, keepdims=True)

    zrow = jnp.zeros_like(wa)
    rinfo_ref[...] = jnp.concatenate([bucket, wa, wb] + [zrow] * (SUBLANES - 3), axis=0)
    wt = jnp.concatenate([wa, wb, jnp.zeros((LANES - 2, tm), F32)], axis=0)
    u2_ref[:, :d] = u2
    u2_ref[:, d:] = wt.T


def _mixout_call(attn_t, lru, x2d, mod3, ga, gl, wo, g1, b1, wr, br, seq):
    t, d = x2d.shape
    _, n_pairs, _, pair_w, _ = attn_t.shape
    dh = n_pairs * pair_w
    assert wo.shape[1] == 2 * dh and lru.shape[1] == dh
    per_b = seq // TM
    full = lambda shape: pl.BlockSpec(shape, lambda i: (0,) * len(shape))
    return pl.pallas_call(
        _mixout_kernel,
        grid=(t // TM,),
        in_specs=[pl.BlockSpec((1, n_pairs, 1, pair_w, TM),
                               lambda i: (i // per_b, 0, i % per_b, 0, 0)),
                  pl.BlockSpec((TM, dh), lambda i: (i, 0)),
                  pl.BlockSpec((TM, d), lambda i: (i, 0)),
                  pl.BlockSpec((1, 6, d), lambda i: (i // per_b, 0, 0)),
                  full((1, dh)), full((1, dh)),
                  pl.BlockSpec((1, dh, d), lambda i: (0, 0, 0)),
                  pl.BlockSpec((1, dh, d), lambda i: (0, 1, 0)),
                  full((1, d)), full((1, d)), full((BUCKET_ROWS, d)), full((BUCKET_ROWS, LANES))],
        out_specs=[pl.BlockSpec((TM, d), lambda i: (i, 0)),
                   pl.BlockSpec((TM, d + LANES), lambda i: (i, 0)),
                   pl.BlockSpec((SUBLANES, TM), lambda i: (0, i)),
                   pl.BlockSpec((BUCKET_ROWS, LANES), lambda i: (0, 0))],
        out_shape=[jax.ShapeDtypeStruct((t, d), F32),
                   jax.ShapeDtypeStruct((t, d + LANES), F32),
                   jax.ShapeDtypeStruct((SUBLANES, t), F32),
                   jax.ShapeDtypeStruct((BUCKET_ROWS, LANES), F32)],
        scratch_shapes=[pltpu.VMEM((2, dh, d), BF16)],
        compiler_params=pltpu.CompilerParams(
            dimension_semantics=("arbitrary",), vmem_limit_bytes=VMEM_LIMIT),
        name="mixout",
    )(attn_t, lru, x2d, mod3, ga, gl, wo, wo, g1, b1, wr, br)


def _rank_kernel(rinfo_ref, counts_ref, dest_ref, carry_ref, offs_ref):
    tm = rinfo_ref.shape[1]

    @pl.when(pl.program_id(0) == 0)
    def _():
        carry_ref[...] = jnp.zeros_like(carry_ref)
        padded = jnp.floor((counts_ref[...] + float(TM_E - 1)) * (1.0 / TM_E)) * float(TM_E)
        inc = padded
        k = 1
        while k < BUCKET_ROWS:
            inc = inc + _shift_rows(inc, k, 0.0)
            k *= 2
        offs_ref[...] = inc - padded

    bucket = rinfo_ref[0:1, :]
    cid = lax.broadcasted_iota(jnp.int32, (BUCKET_ROWS, tm), 0).astype(F32)
    onehot = jnp.where(cid == bucket, 1.0, 0.0)
    srow = lax.broadcasted_iota(jnp.int32, (tm, tm), 0)
    scol = lax.broadcasted_iota(jnp.int32, (tm, tm), 1)
    upper = (srow <= scol).astype(BF16)
    prefix = _dot(onehot.astype(BF16), upper)
    carry = carry_ref[...]
    rank = jnp.sum(onehot * (prefix - 1.0 + carry[:, 0:1] + offs_ref[:, 0:1]),
                   axis=0, keepdims=True)
    dest_ref[...] = rank.astype(jnp.int32)
    carry_ref[...] = carry + prefix[:, tm - 1:tm]


def _rank_call(rinfo, counts):
    t = rinfo.shape[1]
    return pl.pallas_call(
        _rank_kernel,
        grid=(t // TM,),
        in_specs=[pl.BlockSpec((SUBLANES, TM), lambda i: (0, i)),
                  pl.BlockSpec((BUCKET_ROWS, LANES), lambda i: (0, 0))],
        out_specs=pl.BlockSpec((1, TM), lambda i: (0, i)),
        out_shape=jax.ShapeDtypeStruct((1, t), jnp.int32),
        scratch_shapes=[pltpu.VMEM((BUCKET_ROWS, LANES), F32),
                        pltpu.VMEM((BUCKET_ROWS, LANES), F32)],
        compiler_params=pltpu.CompilerParams(
            dimension_semantics=("arbitrary",), vmem_limit_bytes=VMEM_LIMIT),
        name="rank",
    )(rinfo, counts)


def _dispatch_kernel(dest_ref, ends_ref, u2_ref, wg_ref, wu_ref, wd_ref,
                     xs_ref, wg_bf_ref, wu_bf_ref, wd_bf_ref, zbuf, sem, zsem):
    tm = u2_ref.shape[0]
    t0 = pl.program_id(0) * tm
    wg_bf_ref[0] = wg_ref[0, 0].astype(BF16)
    wu_bf_ref[0] = wu_ref[0, 0].astype(BF16)
    wd_bf_ref[0] = wd_ref[0, 0].astype(BF16)

    @pl.when(pl.program_id(0) == 0)
    def _():
        zbuf[...] = jnp.zeros_like(zbuf)

        def tail_copy(bkt):
            end = ends_ref[bkt]
            start = ends_ref[bkt - 1] if bkt else 0
            tail = pl.multiple_of(jnp.maximum(end - TM_E, 0), TM_E)
            return end > start, pltpu.make_async_copy(zbuf, xs_ref.at[pl.ds(tail, TM_E)], zsem)

        for bkt in range(N_BUCKETS):
            nonempty, copy = tail_copy(bkt)
            pl.when(nonempty)(copy.start)
        for bkt in range(N_BUCKETS):
            nonempty, copy = tail_copy(bkt)
            pl.when(nonempty)(copy.wait)

        def unused_tile_copy(k):
            return pltpu.make_async_copy(
                zbuf, xs_ref.at[pl.ds(pl.multiple_of(k * TM_E, TM_E), TM_E)], zsem)

        first_unused = ends_ref[N_BUCKETS - 1] // TM_E
        n_tiles = xs_ref.shape[0] // TM_E
        lax.fori_loop(first_unused, n_tiles, lambda k, c: (unused_tile_copy(k).start(), c)[1], 0)
        lax.fori_loop(first_unused, n_tiles, lambda k, c: (unused_tile_copy(k).wait(), c)[1], 0)

    def issue(r, c):
        pltpu.make_async_copy(u2_ref.at[pl.ds(r, 1)],
                              xs_ref.at[pl.ds(dest_ref[t0 + r], 1)], sem).start()
        return c

    lax.fori_loop(0, tm, issue, 0, unroll=ISSUE_UNROLL)
    pltpu.make_async_copy(u2_ref, xs_ref.at[pl.ds(0, tm)], sem).wait()


def _dispatch_call(dest, bucket_ends, u2ext, n_rows, w_gate, w_up, w_down):
    t, w = u2ext.shape
    _, n_exp, d, de = w_gate.shape
    tm = t // n_exp
    assert t % n_exp == 0 and tm % ISSUE_UNROLL == 0
    w_in = lambda shape: pl.BlockSpec((1, 1) + shape, lambda i, dr, er: (0, i, 0, 0))
    w_out = lambda shape: pl.BlockSpec((1,) + shape, lambda i, dr, er: (i, 0, 0))
    grid_spec = pltpu.PrefetchScalarGridSpec(
        num_scalar_prefetch=2,
        grid=(n_exp,),
        in_specs=[pl.BlockSpec((tm, w), lambda i, dr, er: (i, 0)),
                  w_in((d, de)), w_in((d, de)), w_in((de, d))],
        out_specs=[pl.BlockSpec(memory_space=pl.ANY),
                   w_out((d, de)), w_out((d, de)), w_out((de, d))],
        scratch_shapes=[pltpu.VMEM((TM_E, w), F32), pltpu.SemaphoreType.DMA(()),
                        pltpu.SemaphoreType.DMA(())],
    )
    return pl.pallas_call(
        _dispatch_kernel,
        grid_spec=grid_spec,
        out_shape=[jax.ShapeDtypeStruct((n_rows, w), F32),
                   jax.ShapeDtypeStruct((n_exp, d, de), BF16),
                   jax.ShapeDtypeStruct((n_exp, d, de), BF16),
                   jax.ShapeDtypeStruct((n_exp, de, d), BF16)],
        compiler_params=pltpu.CompilerParams(
            dimension_semantics=("arbitrary",), vmem_limit_bytes=VMEM_LIMIT),
        name="dispatch",
    )(dest, bucket_ends, u2ext, w_gate, w_up, w_down)


def _experts_kernel(ea_ref, eb_ref, nv_ref, xs_ref, wga_ref, wua_ref, wda_ref,
                    wgb_ref, wub_ref, wdb_ref, ys_ref):
    del ea_ref, eb_ref
    i = pl.program_id(0)
    d = ys_ref.shape[1]

    @pl.when(i < nv_ref[0])
    def _():
        x = xs_ref[:, :d].astype(BF16)

        def expert(wg_ref, wu_ref, wd_ref):
            g = _dot(x, wg_ref[0])
            h = (g * _sigmoid(g)) * _dot(x, wu_ref[0])
            return _dot(h.astype(BF16), wd_ref[0])

        ya = xs_ref[:, d:d + 1] * expert(wga_ref, wua_ref, wda_ref)
        ys_ref[...] = ya + xs_ref[:, d + 1:d + 2] * expert(wgb_ref, wub_ref, wdb_ref)

    @pl.when(i >= nv_ref[0])
    def _():
        ys_ref[...] = jnp.zeros_like(ys_ref)


def _experts_call(tile_ea, tile_eb, n_valid, xs, wg, wu, wd):
    tp, w = xs.shape
    _, d, de = wg.shape
    row = lambda i, ea, eb, nv: (jnp.minimum(i, nv[0] - 1), 0)
    wa = lambda i, ea, eb, nv: (ea[i], 0, 0)
    wb = lambda i, ea, eb, nv: (eb[i], 0, 0)
    grid_spec = pltpu.PrefetchScalarGridSpec(
        num_scalar_prefetch=3,
        grid=(tp // TM_E,),
        in_specs=[pl.BlockSpec((TM_E, w), row),
                  pl.BlockSpec((1, d, de), wa), pl.BlockSpec((1, d, de), wa),
                  pl.BlockSpec((1, de, d), wa),
                  pl.BlockSpec((1, d, de), wb), pl.BlockSpec((1, d, de), wb),
                  pl.BlockSpec((1, de, d), wb)],
        out_specs=pl.BlockSpec((TM_E, d), lambda i, ea, eb, nv: (i, 0)),
    )
    return pl.pallas_call(
        _experts_kernel,
        grid_spec=grid_spec,
        out_shape=jax.ShapeDtypeStruct((tp, d), F32),
        compiler_params=pltpu.CompilerParams(
            dimension_semantics=("arbitrary",), vmem_limit_bytes=VMEM_LIMIT),
        name="experts",
    )(tile_ea, tile_eb, n_valid, xs, wg, wu, wd, wg, wu, wd)


def _final_kernel(dest_ref, x1_ref, mod_ref, g2_ref, b2_ref, ys_ref, o_ref, ybuf, sem):
    tm = x1_ref.shape[0]
    i = pl.program_id(0)
    slot = i % GATHER_AHEAD

    def issue_rows(tile, to_slot, r0, n):
        for r in range(n):
            pltpu.make_async_copy(ys_ref.at[pl.ds(dest_ref[tile * tm + r0 + r], 1)],
                                  ybuf.at[to_slot, pl.ds(r0 + r, 1)], sem.at[to_slot]).start()

    def normalise_rows(r0, n):
        rows = pl.ds(r0, n)
        z = DEEPNORM_ALPHA * x1_ref[rows, :] + (1.0 + mod_ref[0, 5:6, :]) * ybuf[slot, rows, :]
        o_ref[rows, :] = _layer_norm(z) * g2_ref[...] + b2_ref[...]

    n_chunks = tm // ISSUE_UNROLL

    @pl.when(i == 0)
    def _():
        for tile in range(GATHER_AHEAD):
            lax.fori_loop(0, n_chunks, lambda c, x, tile=tile: (
                issue_rows(tile, tile, c * ISSUE_UNROLL, ISSUE_UNROLL), x)[1], 0)

    pltpu.make_async_copy(ys_ref.at[pl.ds(0, tm)], ybuf.at[slot], sem.at[slot]).wait()

    def chunk(c, issue_ahead):
        r0 = pl.multiple_of(c * ISSUE_UNROLL, ISSUE_UNROLL)
        normalise_rows(r0, ISSUE_UNROLL)
        if issue_ahead:
            issue_rows(i + GATHER_AHEAD, slot, r0, ISSUE_UNROLL)

    more = i + GATHER_AHEAD < pl.num_programs(0)

    @pl.when(more)
    def _():
        lax.fori_loop(0, n_chunks, lambda c, x: (chunk(c, True), x)[1], 0)

    @pl.when(jnp.logical_not(more))
    def _():
        lax.fori_loop(0, n_chunks, lambda c, x: (chunk(c, False), x)[1], 0)


def _final_call(dest, x1, mod3, g2, b2, ys, seq):
    t, d = x1.shape
    per_b = seq // TM
    grid_spec = pltpu.PrefetchScalarGridSpec(
        num_scalar_prefetch=1,
        grid=(t // TM,),
        in_specs=[pl.BlockSpec((TM, d), lambda i, dr: (i, 0)),
                  pl.BlockSpec((1, 6, d), lambda i, dr: (i // per_b, 0, 0)),
                  pl.BlockSpec((1, d), lambda i, dr: (0, 0)),
                  pl.BlockSpec((1, d), lambda i, dr: (0, 0)),
                  pl.BlockSpec(memory_space=pl.ANY)],
        out_specs=pl.BlockSpec((TM, d), lambda i, dr: (i, 0)),
        scratch_shapes=[pltpu.VMEM((GATHER_AHEAD, TM, d), F32),
                        pltpu.SemaphoreType.DMA((GATHER_AHEAD,))],
    )
    return pl.pallas_call(
        _final_kernel,
        grid_spec=grid_spec,
        out_shape=jax.ShapeDtypeStruct((t, d), F32),
        compiler_params=pltpu.CompilerParams(
            dimension_semantics=("arbitrary",), vmem_limit_bytes=VMEM_LIMIT),
        name="final",
    )(dest, x1, mod3, g2, b2, ys)


def _gate_pairs(w_rg, w_ig):
    def pairs(w):
        n, bs, _ = w.shape
        w = w.reshape(n // 2, 2, bs, bs)
        z = jnp.zeros((n // 2, bs, bs), w.dtype)
        top = jnp.concatenate([w[:, 0], z], axis=2)
        bot = jnp.concatenate([z, w[:, 1]], axis=2)
        return jnp.concatenate([top, bot], axis=1)
    return jnp.concatenate([pairs(w_rg), pairs(w_ig)], axis=2).astype(BF16)


def kernel(x, c, w_ada, b_ada, w_in, b_f, conv_w, conv_b, w_rg, b_rg, w_ig, b_ig, lru_lambda,
           g_attn, g_lru, w_out, ln1_g, ln1_b, w_grp, b_grp, w_exp, b_exp,
           w_e_gate, w_e_up, w_e_down, ln2_g, ln2_b):
    assert w_ada.shape[0] == DEPTH
    b, s, d = x.shape
    t = b * s
    d_attn = N_HEADS * HEAD_DIM
    d_lru = conv_w.shape[2]
    n_exp = w_e_gate.shape[1]
    assert s % TM == 0 and s % TQ == 0 and t % TM_E == 0

    c_pad = jnp.pad(c, ((0, SUBLANES - b), (0, 0)))
    mod = _mod_call(c_pad, w_ada, b_ada[0][None, :])
    mod3 = mod[:b].reshape(b, 6, d)

    wi = w_in[0]
    o = 3 * d_attn
    wq = (wi[:, :d_attn] * (HEAD_DIM ** -0.5 * LOG2E)).astype(BF16)
    wk = wi[:, d_attn:2 * d_attn].astype(BF16)
    wv = wi[:, 2 * d_attn:o].astype(BF16)
    wf = jnp.pad(wi[:, o:o + N_HEADS], ((0, 0), (0, LANES - N_HEADS))).astype(BF16)
    wx = wi[:, o + N_HEADS:o + N_HEADS + d_lru].astype(BF16)
    wg = wi[:, o + N_HEADS + d_lru:].astype(BF16)
    bf_pad = jnp.pad(b_f[0], (0, LANES - N_HEADS))[None, :]
    q_aug, k_aug, vt_aug, xb, gb = _inproj_call(x, mod3, wq, wk, wv, wf, wx, wg, bf_pad)
    attn_t = _attn_call(q_aug, k_aug, vt_aug)
    lru = _lru_call(xb, gb, conv_w[0], conv_b[0][None, :], _gate_pairs(w_rg[0], w_ig[0]),
                    b_rg[0][None, :], b_ig[0][None, :], lru_lambda[0][None, :])

    n_route = N_GROUPS + n_exp
    wr = jnp.pad(jnp.concatenate([w_grp[0], w_exp[0]], axis=1).T,
                 ((0, BUCKET_ROWS - n_route), (0, 0)))
    br = jnp.pad(jnp.concatenate([b_grp[0], b_exp[0]]), (0, BUCKET_ROWS - n_route))
    br = jnp.broadcast_to(br[:, None], (BUCKET_ROWS, LANES))
    x1, u2ext, rinfo, counts = _mixout_call(
        attn_t, lru.reshape(t, d_lru), x.reshape(t, d), mod3,
        g_attn[0][None, :], g_lru[0][None, :], w_out,
        ln1_g[0][None, :], ln1_b[0][None, :], wr, br, s)

    dest = _rank_call(rinfo, counts).reshape(t)
    cnt = counts[:N_BUCKETS, 0].astype(jnp.int32)
    ends = jnp.cumsum((cnt + (TM_E - 1)) // TM_E)
    n_tiles = t // TM_E + N_BUCKETS
    tile_bucket = jnp.sum(ends[None, :] <= jnp.arange(n_tiles)[:, None], axis=1)
    tile_bucket = jnp.minimum(tile_bucket, N_BUCKETS - 1)
    n_valid = ends[N_BUCKETS - 1:]
    last_bucket = tile_bucket[jnp.maximum(n_valid[0] - 1, 0)]
    tile_bucket = jnp.where(jnp.arange(n_tiles) < n_valid[0], tile_bucket, last_bucket)
    tile_ea = jnp.asarray(_BUCKET_EA, jnp.int32)[tile_bucket]
    tile_eb = jnp.asarray(_BUCKET_EB, jnp.int32)[tile_bucket]

    xs, wg_bf, wu_bf, wd_bf = _dispatch_call(
        dest, (ends * TM_E).astype(jnp.int32), u2ext, n_tiles * TM_E, w_e_gate, w_e_up, w_e_down)
    ys = _experts_call(tile_ea, tile_eb, n_valid.astype(jnp.int32), xs, wg_bf, wu_bf, wd_bf)
    out = _final_call(dest, x1, mod3, ln2_g[0][None, :], ln2_b[0][None, :], ys, s)
    return out.reshape(b, s, d)
```

```python
import functools

import jax
import jax.numpy as jnp
from jax import lax
from jax.experimental import pallas as pl
from jax.experimental.pallas import tpu as pltpu

F32 = jnp.float32
BF16 = jnp.bfloat16

HEAD_DIM = 64
N_HEADS = 8
N_LRU_BLOCKS = 8
CONV_WIDTH = 4
LRU_C = 8.0
N_GROUPS = 4
EXPERTS_PER_GROUP = 4
N_PAIRS = 6
N_BUCKETS = N_GROUPS * N_PAIRS
LN_EPS = 1e-5
RMS_EPS = 1e-6
NEG_INF = -1e30
DEPTH = 1
DEEPNORM_ALPHA = (2.0 * DEPTH) ** 0.25
LOG2E = 1.4426950408889634

LANES = 128
SUBLANES = 8
HEAD_PAD = LANES
VT_ROWS = 80
BUCKET_ROWS = 32
TM = 512
TQ = 512
TM_E = 256
ISSUE_UNROLL = 64
GATHER_AHEAD = 2
VMEM_LIMIT = 56 * 1024 * 1024

_PAIRS = [(0, 1), (0, 2), (0, 3), (1, 2), (1, 3), (2, 3)]
_BUCKET_EA = [g * EXPERTS_PER_GROUP + a for g in range(N_GROUPS) for (a, b) in _PAIRS]
_BUCKET_EB = [g * EXPERTS_PER_GROUP + b for g in range(N_GROUPS) for (a, b) in _PAIRS]


def _dot(a, b):
    return jnp.dot(a, b, preferred_element_type=F32)


def _split2(a):
    hi = a.astype(BF16)
    lo = (a - hi.astype(F32)).astype(BF16)
    return hi, lo


def _split3(a):
    hi = a.astype(BF16)
    r = a - hi.astype(F32)
    mid = r.astype(BF16)
    lo = (r - mid.astype(F32)).astype(BF16)
    return hi, mid, lo


def _layer_norm(x):
    mu = jnp.mean(x, axis=-1, keepdims=True)
    xc = x - mu
    var = jnp.mean(xc * xc, axis=-1, keepdims=True)
    return xc * lax.rsqrt(var + LN_EPS)


def _sigmoid(x):
    return 0.5 * jnp.tanh(0.5 * x) + 0.5


def _rot_in_group(x, k):
    n, w = x.shape
    return pltpu.roll(x.reshape(n // SUBLANES, SUBLANES, w), k, 1).reshape(n, w)


def _log_sigmoid(z):
    return jnp.minimum(z, 0.0) - jnp.log1p(jnp.exp(-jnp.abs(z)))


def _cumsum_rows(x, carry):
    n, w = x.shape
    row_in_group = lax.broadcasted_iota(jnp.int32, (n, w), 0) % SUBLANES
    k = 1
    while k < SUBLANES:
        x = x + jnp.where(row_in_group >= k, _rot_in_group(x, k), 0.0)
        k *= 2
    groups = []
    for g in range(n // SUBLANES):
        blk = x[g * SUBLANES:(g + 1) * SUBLANES] + carry
        groups.append(blk)
        carry = blk[SUBLANES - 1:SUBLANES]
    return jnp.concatenate(groups, axis=0)


def _shift_rows(x, k, fill):
    n = x.shape[0]
    if k % SUBLANES == 0:
        return jnp.concatenate([jnp.full((k, x.shape[1]), fill, x.dtype), x[:n - k]], axis=0)
    row = lax.broadcasted_iota(jnp.int32, x.shape, 0)
    return jnp.where(row >= k, pltpu.roll(x, k, 0), fill)


def _mod_kernel(c_ref, w_ref, b_ref, o_ref):
    c = c_ref[...]
    s = c * _sigmoid(c)
    sh, sl = _split2(s)
    wh, wl = _split2(w_ref[0])
    o_ref[...] = _dot(sh, wh) + (_dot(sh, wl) + _dot(sl, wh)) + b_ref[...]


def _mod_call(c_pad, w_ada, b_ada):
    rows, d = c_pad.shape
    n = w_ada.shape[2]
    return pl.pallas_call(
        _mod_kernel,
        grid=(n // d,),
        in_specs=[pl.BlockSpec((rows, d), lambda j: (0, 0)),
                  pl.BlockSpec((1, d, d), lambda j: (0, 0, j)),
                  pl.BlockSpec((1, d), lambda j: (0, j))],
        out_specs=pl.BlockSpec((rows, d), lambda j: (0, j)),
        out_shape=jax.ShapeDtypeStruct((rows, n), F32),
        compiler_params=pltpu.CompilerParams(vmem_limit_bytes=VMEM_LIMIT),
        name="mod",
    )(c_pad, w_ada, b_ada)


def _inproj_kernel(x_ref, mod_ref, wq_ref, wk_ref, wv_ref, wf_ref, wx_ref, wg_ref, bf_ref,
                   q_ref, k_ref, vt_ref, xb_ref, gb_ref, carry_ref, v_scr):
    j = pl.program_id(1)

    @pl.when(j == 0)
    def _():
        carry_ref[...] = jnp.zeros_like(carry_ref)

    tm = x_ref.shape[1]
    u = _layer_norm(x_ref[0]) * (1.0 + mod_ref[0, 1:2, :]) + mod_ref[0, 0:1, :]
    ub = u.astype(BF16)

    logf = _log_sigmoid(_dot(ub, wf_ref[...]) + bf_ref[...])
    cum = _cumsum_rows(logf, carry_ref[...])
    carry_ref[...] = cum[tm - 1:tm, :]
    c_hi, c_mid, c_lo = [p.astype(F32) for p in _split3(cum * LOG2E)]

    lane = lax.broadcasted_iota(jnp.int32, (tm, HEAD_PAD), 1)
    d = HEAD_DIM
    q_all = _dot(ub, wq_ref[...])
    k_all = _dot(ub, wk_ref[...])
    for h in range(N_HEADS):
        ts = slice((h // 2) * LANES, (h // 2 + 1) * LANES)
        qh, kh = q_all[:, ts], k_all[:, ts]
        if h % 2:
            qh, kh = pltpu.roll(qh, d, 1), pltpu.roll(kh, d, 1)
        hi, mid, lo = c_hi[:, h:h + 1], c_mid[:, h:h + 1], c_lo[:, h:h + 1]
        q_ext = jnp.where(lane == d, hi, jnp.where(lane == d + 1, mid, jnp.where(
            lane == d + 2, lo, jnp.where(lane < d + 6, 1.0, 0.0))))
        k_ext = jnp.where(lane < d + 3, 1.0, jnp.where(lane == d + 3, -hi, jnp.where(
            lane == d + 4, -mid, jnp.where(lane == d + 5, -lo, 0.0))))
        q_ref[0, h] = jnp.where(lane < d, qh, q_ext).astype(BF16)
        k_ref[0, h] = jnp.where(lane < d, kh, k_ext).astype(BF16)

    v_scr[...] = _dot(ub, wv_ref[...])
    vt_all = v_scr[...].T
    pad_rows = lax.broadcasted_iota(jnp.int32, (VT_ROWS - d, tm), 0)
    ones_row = jnp.where(pad_rows == 0, 1.0, 0.0).astype(BF16)
    for h in range(N_HEADS):
        vt_ref[0, h, 0, :d, :] = vt_all[h * d:(h + 1) * d, :].astype(BF16)
        vt_ref[0, h, 0, d:, :] = ones_row

    xb_ref[0] = _dot(ub, wx_ref[...])
    gb_ref[0] = _dot(ub, wg_ref[...])


def _inproj_call(x, mod3, wq, wk, wv, wf, wx, wg, bf_pad):
    b, s, d = x.shape
    d_attn = wq.shape[1]
    d_lru = wx.shape[1]
    full = lambda shape: pl.BlockSpec(shape, lambda bi, j: (0,) * len(shape))
    head_spec = pl.BlockSpec((1, N_HEADS, TM, HEAD_PAD), lambda bi, j: (bi, 0, j, 0))
    vt_spec = pl.BlockSpec((1, N_HEADS, 1, VT_ROWS, TM), lambda bi, j: (bi, 0, j, 0, 0))
    row_spec = pl.BlockSpec((1, TM, d_lru), lambda bi, j: (bi, j, 0))
    head_shape = jax.ShapeDtypeStruct((b, N_HEADS, s, HEAD_PAD), BF16)
    return pl.pallas_call(
        _inproj_kernel,
        grid=(b, s // TM),
        in_specs=[pl.BlockSpec((1, TM, d), lambda bi, j: (bi, j, 0)),
                  pl.BlockSpec((1, 6, d), lambda bi, j: (bi, 0, 0)),
                  full((d, d_attn)), full((d, d_attn)), full((d, d_attn)), full((d, LANES)),
                  full((d, d_lru)), full((d, d_lru)), full((1, LANES))],
        out_specs=[head_spec, head_spec, vt_spec, row_spec, row_spec],
        out_shape=[head_shape, head_shape,
                   jax.ShapeDtypeStruct((b, N_HEADS, s // TM, VT_ROWS, TM), BF16),
                   jax.ShapeDtypeStruct((b, s, d_lru), F32),
                   jax.ShapeDtypeStruct((b, s, d_lru), F32)],
        scratch_shapes=[pltpu.VMEM((1, LANES), F32), pltpu.VMEM((TM, d_attn), F32)],
        compiler_params=pltpu.CompilerParams(
            dimension_semantics=("arbitrary", "arbitrary"), vmem_limit_bytes=VMEM_LIMIT),
        name="inproj",
    )(x, mod3, wq, wk, wv, wf, wx, wg, bf_pad)


def _attn_kernel(q_ref, k_ref, vt_ref, o_ref, s_scr, smax_scr, m_scr, acc_scr):
    tq = TQ
    nq = q_ref.shape[2] // tq
    heads = range(q_ref.shape[1])

    def scores_to(slot, i, j):
        for hh in heads:
            k = k_ref[0, hh, pl.ds(pl.multiple_of(j * tq, tq), tq), :]
            q = q_ref[0, hh, pl.ds(pl.multiple_of(i * tq, tq), tq), :]
            s = lax.dot_general(k, q, (((1,), (1,)), ((), ())), preferred_element_type=F32)
            s_scr[slot, hh] = s
            smax_scr[slot, hh] = jnp.max(s, axis=0, keepdims=True)

    def consume(slot, j, masked):
        for hh in heads:
            s = s_scr[slot, hh]
            if masked:
                key = lax.broadcasted_iota(jnp.int32, (tq, tq), 0)
                qry = lax.broadcasted_iota(jnp.int32, (tq, tq), 1)
                s = jnp.where(key <= qry, s, NEG_INF)
                s_max = jnp.max(s, axis=0, keepdims=True)
            else:
                s_max = smax_scr[slot, hh]
            m = m_scr[hh]
            m_new = jnp.maximum(m, s_max)
            p = jnp.exp2(s - m_new).astype(BF16)
            acc_scr[hh] = jnp.exp2(m - m_new) * acc_scr[hh] + _dot(vt_ref[0, hh, j], p)
            m_scr[hh] = m_new

    scores_to(2, 0, 0)

    def query_block(i, carry):
        m_scr[...] = jnp.full(m_scr.shape, NEG_INF, F32)
        acc_scr[...] = jnp.zeros(acc_scr.shape, F32)
        nxt = jnp.minimum(i + 1, nq - 1)
        n_mid = i - 1

        @pl.when(i == 0)
        def _():
            consume(2, 0, True)
            scores_to(2, nxt, 0)

        @pl.when(i >= 1)
        def _():
            scores_to(1, i, 1)
            consume(2, 0, False)

        def pair(kk, c):
            j = 1 + 2 * kk
            scores_to(0, i, j + 1)
            consume(1, j, False)
            scores_to(1, i, j + 2)
            consume(0, j + 1, False)
            return c

        lax.fori_loop(0, jnp.maximum(n_mid, 0) // 2, pair, 0)

        @pl.when(jnp.logical_and(i >= 1, n_mid % 2 == 1))
        def _():
            scores_to(0, i, i)
            consume(1, i - 1, False)
            scores_to(2, nxt, 0)
            consume(0, i, True)

        @pl.when(jnp.logical_and(i >= 1, n_mid % 2 == 0))
        def _():
            scores_to(2, nxt, 0)
            consume(1, i, True)

        for hh in heads:
            o_ref[0, 0, i, hh * HEAD_DIM:(hh + 1) * HEAD_DIM, :] = (
                acc_scr[hh, :HEAD_DIM, :] / acc_scr[hh, HEAD_DIM:HEAD_DIM + 1, :])
        return carry

    lax.fori_loop(0, nq, query_block, 0)


def _attn_call(q_aug, k_aug, vt_aug):
    b, h, s, hp = q_aug.shape
    assert TQ == TM
    nq = s // TQ
    return pl.pallas_call(
        _attn_kernel,
        grid=(b, h // 2),
        in_specs=[pl.BlockSpec((1, 2, s, hp), lambda bi, p: (bi, p, 0, 0)),
                  pl.BlockSpec((1, 2, s, hp), lambda bi, p: (bi, p, 0, 0)),
                  pl.BlockSpec((1, 2, nq, VT_ROWS, TQ), lambda bi, p: (bi, p, 0, 0, 0))],
        out_specs=pl.BlockSpec((1, 1, nq, 2 * HEAD_DIM, TQ), lambda bi, p: (bi, p, 0, 0, 0)),
        out_shape=jax.ShapeDtypeStruct((b, h // 2, nq, 2 * HEAD_DIM, TQ), F32),
        scratch_shapes=[pltpu.VMEM((3, 2, TQ, TQ), F32), pltpu.VMEM((3, 2, 1, TQ), F32),
                        pltpu.VMEM((2, 1, TQ), F32), pltpu.VMEM((2, VT_ROWS, TQ), F32)],
        compiler_params=pltpu.CompilerParams(
            dimension_semantics=("arbitrary", "arbitrary"), vmem_limit_bytes=VMEM_LIMIT),
        name="attention",
    )(q_aug, k_aug, vt_aug)


def _gelu_tanh(x):
    return 0.5 * x * (1.0 + jnp.tanh(0.7978845608028654 * (x + 0.044715 * (x * x * x))))


def _lru_kernel(xb_ref, gb_ref, cw_ref, cb_ref, wgate_ref, brg_ref, big_ref, lam_ref,
                o_ref, tail_ref, h_ref):
    j = pl.program_id(1)

    @pl.when(j == 0)
    def _():
        tail_ref[...] = jnp.zeros_like(tail_ref)
        h_ref[...] = jnp.zeros_like(h_ref)

    x = xb_ref[0]
    tm, dl = x.shape
    row_in_group = lax.broadcasted_iota(jnp.int32, (tm, dl), 0) % SUBLANES
    x_prev_group = jnp.concatenate([tail_ref[...], x[:tm - SUBLANES]], axis=0)
    xc = x * cw_ref[CONV_WIDTH - 1:CONV_WIDTH, :] + cb_ref[...]
    for k in range(1, CONV_WIDTH):
        xs = jnp.where(row_in_group < k, _rot_in_group(x_prev_group, k), _rot_in_group(x, k))
        xc = xc + xs * cw_ref[CONV_WIDTH - 1 - k:CONV_WIDTH - k, :]
    tail_ref[...] = x[tm - SUBLANES:, :]

    xcb = xc.astype(BF16)
    n_pairs = dl // LANES
    r_parts, i_parts = [], []
    for p in range(n_pairs):
        g = _dot(xcb[:, p * LANES:(p + 1) * LANES], wgate_ref[p])
        r_parts.append(g[:, :LANES])
        i_parts.append(g[:, LANES:])
    r = _sigmoid(jnp.concatenate(r_parts, axis=1) + brg_ref[...])
    ig = _sigmoid(jnp.concatenate(i_parts, axis=1) + big_ref[...])

    lam = lam_ref[...]
    softplus_neg_lam = jnp.maximum(-lam, 0.0) + jnp.log1p(jnp.exp(-jnp.abs(lam)))
    log_a = (-LRU_C) * r * softplus_neg_lam
    a = jnp.exp(log_a)
    v = 1.0 - a * a
    u = jnp.where(v > 0.0, v * lax.rsqrt(v), 0.0) * (ig * xc)

    k = 1
    while k < SUBLANES:
        keep = row_in_group >= k
        u = a * jnp.where(keep, _rot_in_group(u, k), 0.0) + u
        a = a * jnp.where(keep, _rot_in_group(a, k), 1.0)
        k *= 2
    h_prev = h_ref[...]
    groups = []
    for g in range(tm // SUBLANES):
        rows = slice(g * SUBLANES, (g + 1) * SUBLANES)
        hg = a[rows] * h_prev + u[rows]
        groups.append(hg)
        h_prev = hg[SUBLANES - 1:SUBLANES]
    h_ref[...] = h_prev
    o_ref[0] = jnp.concatenate(groups, axis=0) * _gelu_tanh(gb_ref[0])


def _lru_call(xb, gb, conv_w, conv_b, wgate, b_rg, b_ig, lam):
    b, s, dl = xb.shape
    row_spec = pl.BlockSpec((1, TM, dl), lambda bi, j: (bi, j, 0))
    full = lambda shape: pl.BlockSpec(shape, lambda bi, j: (0,) * len(shape))
    return pl.pallas_call(
        _lru_kernel,
        grid=(b, s // TM),
        in_specs=[row_spec, row_spec, full(conv_w.shape), full((1, dl)), full(wgate.shape),
                  full((1, dl)), full((1, dl)), full((1, dl))],
        out_specs=row_spec,
        out_shape=jax.ShapeDtypeStruct((b, s, dl), F32),
        scratch_shapes=[pltpu.VMEM((SUBLANES, dl), F32), pltpu.VMEM((1, dl), F32)],
        compiler_params=pltpu.CompilerParams(
            dimension_semantics=("arbitrary", "arbitrary"), vmem_limit_bytes=VMEM_LIMIT),
        name="lru",
    )(xb, gb, conv_w, conv_b, wgate, b_rg, b_ig, lam)


def _rms(x, gain):
    return x * lax.rsqrt(jnp.mean(x * x, axis=-1, keepdims=True) + RMS_EPS) * gain


def _mixout_kernel(attn_ref, lru_ref, x_ref, mod_ref, ga_ref, gl_ref, woa_ref, wol_ref,
                   g1_ref, b1_ref, wr_ref, br_ref, x1_ref, u2_ref, rinfo_ref, cnt_ref, wo_bf):
    tm, d = x_ref.shape

    @pl.when(pl.program_id(0) == 0)
    def _():
        wo_bf[0] = woa_ref[0].astype(BF16)
        wo_bf[1] = wol_ref[0].astype(BF16)

    attn = attn_ref[0, :, 0].reshape(ga_ref.shape[1], tm).T
    na = _rms(attn, ga_ref[...]).astype(BF16)
    nl = _rms(lru_ref[...], gl_ref[...]).astype(BF16)
    mix = _dot(na, wo_bf[0]) + _dot(nl, wo_bf[1])
    z = DEEPNORM_ALPHA * x_ref[...] + (1.0 + mod_ref[0, 2:3, :]) * mix
    x1 = _layer_norm(z) * g1_ref[...] + b1_ref[...]
    x1_ref[...] = x1
    u2 = _layer_norm(x1) * (1.0 + mod_ref[0, 4:5, :]) + mod_ref[0, 3:4, :]

    nt = (((1,), (1,)), ((), ()))
    uh, ul = _split2(u2)
    wh, wl = _split2(wr_ref[...])
    dg = lambda a, b_: lax.dot_general(a, b_, nt, preferred_element_type=F32)
    lg = dg(wh, uh) + (dg(wh, ul) + dg(wl, uh)) + br_ref[:, 0:1]

    def first_index(vals, target):
        idx = jnp.full_like(target, float(len(vals) - 1))
        for n in range(len(vals) - 2, -1, -1):
            idx = jnp.where(vals[n] == target, float(n), idx)
        return idx

    g = [lg[n:n + 1, :] for n in range(N_GROUPS)]
    gmax = functools.reduce(jnp.maximum, g)
    gsum = functools.reduce(lambda a, b_: a + b_, [jnp.exp(v - gmax) for v in g])
    grp_w = 1.0 / gsum
    gidx = first_index(g, gmax)

    sel = []
    for e in range(EXPERTS_PER_GROUP):
        v = lg[N_GROUPS + (N_GROUPS - 1) * EXPERTS_PER_GROUP + e:
               N_GROUPS + (N_GROUPS - 1) * EXPERTS_PER_GROUP + e + 1, :]
        for gi in range(N_GROUPS - 2, -1, -1):
            r0 = N_GROUPS + gi * EXPERTS_PER_GROUP + e
            v = jnp.where(gidx == float(gi), lg[r0:r0 + 1, :], v)
        sel.append(v)
    smax = functools.reduce(jnp.maximum, sel)
    i1 = first_index(sel, smax)
    rest = [jnp.where(i1 == float(e), -3e38, sel[e]) for e in range(EXPERTS_PER_GROUP)]
    rmax = functools.reduce(jnp.maximum, rest)
    i2 = first_index(rest, rmax)
    e2 = jnp.exp(rmax - smax)
    w1 = grp_w / (1.0 + e2)
    w2 = grp_w * e2 / (1.0 + e2)
    ia = jnp.minimum(i1, i2)
    ib = jnp.maximum(i1, i2)
    wa = jnp.where(i1 < i2, w1, w2)
    wb = jnp.where(i1 < i2, w2, w1)
    pair = jnp.where(ia == 0.0, ib - 1.0, jnp.where(ia == 1.0, ib + 1.0, 5.0))
    bucket = gidx * float(N_PAIRS) + pair

    @pl.when(pl.program_id(0) == 0)
    def _():
        cnt_ref[...] = jnp.zeros_like(cnt_ref)

    cid = lax.broadcasted_iota(jnp.int32, (BUCKET_ROWS, tm), 0).astype(F32)
    cnt_ref[...] += jnp.sum(jnp.where(cid == bucket, 1.0, 0.0), axis=1, keepdims=True)

    zrow = jnp.zeros_like(wa)
    rinfo_ref[...] = jnp.concatenate([bucket, wa, wb] + [zrow] * (SUBLANES - 3), axis=0)
    wt = jnp.concatenate([wa, wb, jnp.zeros((LANES - 2, tm), F32)], axis=0)
    u2_ref[:, :d] = u2
    u2_ref[:, d:] = wt.T


def _mixout_call(attn_t, lru, x2d, mod3, ga, gl, wo, g1, b1, wr, br, seq):
    t, d = x2d.shape
    _, n_pairs, _, pair_w, _ = attn_t.shape
    dh = n_pairs * pair_w
    assert wo.shape[1] == 2 * dh and lru.shape[1] == dh
    per_b = seq // TM
    full = lambda shape: pl.BlockSpec(shape, lambda i: (0,) * len(shape))
    return pl.pallas_call(
        _mixout_kernel,
        grid=(t // TM,),
        in_specs=[pl.BlockSpec((1, n_pairs, 1, pair_w, TM),
                               lambda i: (i // per_b, 0, i % per_b, 0, 0)),
                  pl.BlockSpec((TM, dh), lambda i: (i, 0)),
                  pl.BlockSpec((TM, d), lambda i: (i, 0)),
                  pl.BlockSpec((1, 6, d), lambda i: (i // per_b, 0, 0)),
                  full((1, dh)), full((1, dh)),
                  pl.BlockSpec((1, dh, d), lambda i: (0, 0, 0)),
                  pl.BlockSpec((1, dh, d), lambda i: (0, 1, 0)),
                  full((1, d)), full((1, d)), full((BUCKET_ROWS, d)), full((BUCKET_ROWS, LANES))],
        out_specs=[pl.BlockSpec((TM, d), lambda i: (i, 0)),
                   pl.BlockSpec((TM, d + LANES), lambda i: (i, 0)),
                   pl.BlockSpec((SUBLANES, TM), lambda i: (0, i)),
                   pl.BlockSpec((BUCKET_ROWS, LANES), lambda i: (0, 0))],
        out_shape=[jax.ShapeDtypeStruct((t, d), F32),
                   jax.ShapeDtypeStruct((t, d + LANES), F32),
                   jax.ShapeDtypeStruct((SUBLANES, t), F32),
                   jax.ShapeDtypeStruct((BUCKET_ROWS, LANES), F32)],
        scratch_shapes=[pltpu.VMEM((2, dh, d), BF16)],
        compiler_params=pltpu.CompilerParams(
            dimension_semantics=("arbitrary",), vmem_limit_bytes=VMEM_LIMIT),
        name="mixout",
    )(attn_t, lru, x2d, mod3, ga, gl, wo, wo, g1, b1, wr, br)


def _rank_kernel(rinfo_ref, counts_ref, dest_ref, carry_ref, offs_ref):
    tm = rinfo_ref.shape[1]

    @pl.when(pl.program_id(0) == 0)
    def _():
        carry_ref[...] = jnp.zeros_like(carry_ref)
        padded = jnp.floor((counts_ref[...] + float(TM_E - 1)) * (1.0 / TM_E)) * float(TM_E)
        inc = padded
        k = 1
        while k < BUCKET_ROWS:
            inc = inc + _shift_rows(inc, k, 0.0)
            k *= 2
        offs_ref[...] = inc - padded

    bucket = rinfo_ref[0:1, :]
    cid = lax.broadcasted_iota(jnp.int32, (BUCKET_ROWS, tm), 0).astype(F32)
    onehot = jnp.where(cid == bucket, 1.0, 0.0)
    srow = lax.broadcasted_iota(jnp.int32, (tm, tm), 0)
    scol = lax.broadcasted_iota(jnp.int32, (tm, tm), 1)
    upper = (srow <= scol).astype(BF16)
    prefix = _dot(onehot.astype(BF16), upper)
    carry = carry_ref[...]
    rank = jnp.sum(onehot * (prefix - 1.0 + carry[:, 0:1] + offs_ref[:, 0:1]),
                   axis=0, keepdims=True)
    dest_ref[...] = rank.astype(jnp.int32)
    carry_ref[...] = carry + prefix[:, tm - 1:tm]


def _rank_call(rinfo, counts):
    t = rinfo.shape[1]
    return pl.pallas_call(
        _rank_kernel,
        grid=(t // TM,),
        in_specs=[pl.BlockSpec((SUBLANES, TM), lambda i: (0, i)),
                  pl.BlockSpec((BUCKET_ROWS, LANES), lambda i: (0, 0))],
        out_specs=pl.BlockSpec((1, TM), lambda i: (0, i)),
        out_shape=jax.ShapeDtypeStruct((1, t), jnp.int32),
        scratch_shapes=[pltpu.VMEM((BUCKET_ROWS, LANES), F32),
                        pltpu.VMEM((BUCKET_ROWS, LANES), F32)],
        compiler_params=pltpu.CompilerParams(
            dimension_semantics=("arbitrary",), vmem_limit_bytes=VMEM_LIMIT),
        name="rank",
    )(rinfo, counts)


def _dispatch_kernel(dest_ref, ends_ref, u2_ref, wg_ref, wu_ref, wd_ref,
                     xs_ref, wg_bf_ref, wu_bf_ref, wd_bf_ref, zbuf, sem, zsem):
    tm = u2_ref.shape[0]
    t0 = pl.program_id(0) * tm
    wg_bf_ref[0] = wg_ref[0, 0].astype(BF16)
    wu_bf_ref[0] = wu_ref[0, 0].astype(BF16)
    wd_bf_ref[0] = wd_ref[0, 0].astype(BF16)

    @pl.when(pl.program_id(0) == 0)
    def _():
        zbuf[...] = jnp.zeros_like(zbuf)

        def tail_copy(bkt):
            end = ends_ref[bkt]
            start = ends_ref[bkt - 1] if bkt else 0
            tail = pl.multiple_of(jnp.maximum(end - TM_E, 0), TM_E)
            return end > start, pltpu.make_async_copy(zbuf, xs_ref.at[pl.ds(tail, TM_E)], zsem)

        for bkt in range(N_BUCKETS):
            nonempty, copy = tail_copy(bkt)
            pl.when(nonempty)(copy.start)
        for bkt in range(N_BUCKETS):
            nonempty, copy = tail_copy(bkt)
            pl.when(nonempty)(copy.wait)

        def unused_tile_copy(k):
            return pltpu.make_async_copy(
                zbuf, xs_ref.at[pl.ds(pl.multiple_of(k * TM_E, TM_E), TM_E)], zsem)

        first_unused = ends_ref[N_BUCKETS - 1] // TM_E
        n_tiles = xs_ref.shape[0] // TM_E
        lax.fori_loop(first_unused, n_tiles, lambda k, c: (unused_tile_copy(k).start(), c)[1], 0)
        lax.fori_loop(first_unused, n_tiles, lambda k, c: (unused_tile_copy(k).wait(), c)[1], 0)

    def issue(r, c):
        pltpu.make_async_copy(u2_ref.at[pl.ds(r, 1)],
                              xs_ref.at[pl.ds(dest_ref[t0 + r], 1)], sem).start()
        return c

    lax.fori_loop(0, tm, issue, 0, unroll=ISSUE_UNROLL)
    pltpu.make_async_copy(u2_ref, xs_ref.at[pl.ds(0, tm)], sem).wait()


def _dispatch_call(dest, bucket_ends, u2ext, n_rows, w_gate, w_up, w_down):
    t, w = u2ext.shape
    _, n_exp, d, de = w_gate.shape
    tm = t // n_exp
    assert t % n_exp == 0 and tm % ISSUE_UNROLL == 0
    w_in = lambda shape: pl.BlockSpec((1, 1) + shape, lambda i, dr, er: (0, i, 0, 0))
    w_out = lambda shape: pl.BlockSpec((1,) + shape, lambda i, dr, er: (i, 0, 0))
    grid_spec = pltpu.PrefetchScalarGridSpec(
        num_scalar_prefetch=2,
        grid=(n_exp,),
        in_specs=[pl.BlockSpec((tm, w), lambda i, dr, er: (i, 0)),
                  w_in((d, de)), w_in((d, de)), w_in((de, d))],
        out_specs=[pl.BlockSpec(memory_space=pl.ANY),
                   w_out((d, de)), w_out((d, de)), w_out((de, d))],
        scratch_shapes=[pltpu.VMEM((TM_E, w), F32), pltpu.SemaphoreType.DMA(()),
                        pltpu.SemaphoreType.DMA(())],
    )
    return pl.pallas_call(
        _dispatch_kernel,
        grid_spec=grid_spec,
        out_shape=[jax.ShapeDtypeStruct((n_rows, w), F32),
                   jax.ShapeDtypeStruct((n_exp, d, de), BF16),
                   jax.ShapeDtypeStruct((n_exp, d, de), BF16),
                   jax.ShapeDtypeStruct((n_exp, de, d), BF16)],
        compiler_params=pltpu.CompilerParams(
            dimension_semantics=("arbitrary",), vmem_limit_bytes=VMEM_LIMIT),
        name="dispatch",
    )(dest, bucket_ends, u2ext, w_gate, w_up, w_down)


def _experts_kernel(ea_ref, eb_ref, nv_ref, xs_ref, wga_ref, wua_ref, wda_ref,
                    wgb_ref, wub_ref, wdb_ref, ys_ref):
    del ea_ref, eb_ref
    i = pl.program_id(0)
    d = ys_ref.shape[1]

    @pl.when(i < nv_ref[0])
    def _():
        x = xs_ref[:, :d].astype(BF16)

        def expert(wg_ref, wu_ref, wd_ref):
            g = _dot(x, wg_ref[0])
            h = (g * _sigmoid(g)) * _dot(x, wu_ref[0])
            return _dot(h.astype(BF16), wd_ref[0])

        ya = xs_ref[:, d:d + 1] * expert(wga_ref, wua_ref, wda_ref)
        ys_ref[...] = ya + xs_ref[:, d + 1:d + 2] * expert(wgb_ref, wub_ref, wdb_ref)

    @pl.when(i >= nv_ref[0])
    def _():
        ys_ref[...] = jnp.zeros_like(ys_ref)


def _experts_call(tile_ea, tile_eb, n_valid, xs, wg, wu, wd):
    tp, w = xs.shape
    _, d, de = wg.shape
    row = lambda i, ea, eb, nv: (jnp.minimum(i, nv[0] - 1), 0)
    wa = lambda i, ea, eb, nv: (ea[i], 0, 0)
    wb = lambda i, ea, eb, nv: (eb[i], 0, 0)
    grid_spec = pltpu.PrefetchScalarGridSpec(
        num_scalar_prefetch=3,
        grid=(tp // TM_E,),
        in_specs=[pl.BlockSpec((TM_E, w), row),
                  pl.BlockSpec((1, d, de), wa), pl.BlockSpec((1, d, de), wa),
                  pl.BlockSpec((1, de, d), wa),
                  pl.BlockSpec((1, d, de), wb), pl.BlockSpec((1, d, de), wb),
                  pl.BlockSpec((1, de, d), wb)],
        out_specs=pl.BlockSpec((TM_E, d), lambda i, ea, eb, nv: (i, 0)),
    )
    return pl.pallas_call(
        _experts_kernel,
        grid_spec=grid_spec,
        out_shape=jax.ShapeDtypeStruct((tp, d), F32),
        compiler_params=pltpu.CompilerParams(
            dimension_semantics=("arbitrary",), vmem_limit_bytes=VMEM_LIMIT),
        name="experts",
    )(tile_ea, tile_eb, n_valid, xs, wg, wu, wd, wg, wu, wd)


def _final_kernel(dest_ref, x1_ref, mod_ref, g2_ref, b2_ref, ys_ref, o_ref, ybuf, sem):
    tm = x1_ref.shape[0]
    i = pl.program_id(0)
    slot = i % GATHER_AHEAD

    def issue_rows(tile, to_slot, r0, n):
        for r in range(n):
            pltpu.make_async_copy(ys_ref.at[pl.ds(dest_ref[tile * tm + r0 + r], 1)],
                                  ybuf.at[to_slot, pl.ds(r0 + r, 1)], sem.at[to_slot]).start()

    def normalise_rows(r0, n):
        rows = pl.ds(r0, n)
        z = DEEPNORM_ALPHA * x1_ref[rows, :] + (1.0 + mod_ref[0, 5:6, :]) * ybuf[slot, rows, :]
        o_ref[rows, :] = _layer_norm(z) * g2_ref[...] + b2_ref[...]

    n_chunks = tm // ISSUE_UNROLL

    @pl.when(i == 0)
    def _():
        for tile in range(GATHER_AHEAD):
            lax.fori_loop(0, n_chunks, lambda c, x, tile=tile: (
                issue_rows(tile, tile, c * ISSUE_UNROLL, ISSUE_UNROLL), x)[1], 0)

    pltpu.make_async_copy(ys_ref.at[pl.ds(0, tm)], ybuf.at[slot], sem.at[slot]).wait()

    def chunk(c, issue_ahead):
        r0 = pl.multiple_of(c * ISSUE_UNROLL, ISSUE_UNROLL)
        normalise_rows(r0, ISSUE_UNROLL)
        if issue_ahead:
            issue_rows(i + GATHER_AHEAD, slot, r0, ISSUE_UNROLL)

    more = i + GATHER_AHEAD < pl.num_programs(0)

    @pl.when(more)
    def _():
        lax.fori_loop(0, n_chunks, lambda c, x: (chunk(c, True), x)[1], 0)

    @pl.when(jnp.logical_not(more))
    def _():
        lax.fori_loop(0, n_chunks, lambda c, x: (chunk(c, False), x)[1], 0)


def _final_call(dest, x1, mod3, g2, b2, ys, seq):
    t, d = x1.shape
    per_b = seq // TM
    grid_spec = pltpu.PrefetchScalarGridSpec(
        num_scalar_prefetch=1,
        grid=(t // TM,),
        in_specs=[pl.BlockSpec((TM, d), lambda i, dr: (i, 0)),
                  pl.BlockSpec((1, 6, d), lambda i, dr: (i // per_b, 0, 0)),
                  pl.BlockSpec((1, d), lambda i, dr: (0, 0)),
                  pl.BlockSpec((1, d), lambda i, dr: (0, 0)),
                  pl.BlockSpec(memory_space=pl.ANY)],
        out_specs=pl.BlockSpec((TM, d), lambda i, dr: (i, 0)),
        scratch_shapes=[pltpu.VMEM((GATHER_AHEAD, TM, d), F32),
                        pltpu.SemaphoreType.DMA((GATHER_AHEAD,))],
    )
    return pl.pallas_call(
        _final_kernel,
        grid_spec=grid_spec,
        out_shape=jax.ShapeDtypeStruct((t, d), F32),
        compiler_params=pltpu.CompilerParams(
            dimension_semantics=("arbitrary",), vmem_limit_bytes=VMEM_LIMIT),
        name="final",
    )(dest, x1, mod3, g2, b2, ys)


def _gate_pairs(w_rg, w_ig):
    def pairs(w):
        n, bs, _ = w.shape
        w = w.reshape(n // 2, 2, bs, bs)
        z = jnp.zeros((n // 2, bs, bs), w.dtype)
        top = jnp.concatenate([w[:, 0], z], axis=2)
        bot = jnp.concatenate([z, w[:, 1]], axis=2)
        return jnp.concatenate([top, bot], axis=1)
    return jnp.concatenate([pairs(w_rg), pairs(w_ig)], axis=2).astype(BF16)


def kernel(x, c, w_ada, b_ada, w_in, b_f, conv_w, conv_b, w_rg, b_rg, w_ig, b_ig, lru_lambda,
           g_attn, g_lru, w_out, ln1_g, ln1_b, w_grp, b_grp, w_exp, b_exp,
           w_e_gate, w_e_up, w_e_down, ln2_g, ln2_b):
    assert w_ada.shape[0] == DEPTH
    b, s, d = x.shape
    t = b * s
    d_attn = N_HEADS * HEAD_DIM
    d_lru = conv_w.shape[2]
    n_exp = w_e_gate.shape[1]
    assert s % TM == 0 and s % TQ == 0 and t % TM_E == 0

    c_pad = jnp.pad(c, ((0, SUBLANES - b), (0, 0)))
    mod = _mod_call(c_pad, w_ada, b_ada[0][None, :])
    mod3 = mod[:b].reshape(b, 6, d)

    wi = w_in[0]
    o = 3 * d_attn
    wq = (wi[:, :d_attn] * (HEAD_DIM ** -0.5 * LOG2E)).astype(BF16)
    wk = wi[:, d_attn:2 * d_attn].astype(BF16)
    wv = wi[:, 2 * d_attn:o].astype(BF16)
    wf = jnp.pad(wi[:, o:o + N_HEADS], ((0, 0), (0, LANES - N_HEADS))).astype(BF16)
    wx = wi[:, o + N_HEADS:o + N_HEADS + d_lru].astype(BF16)
    wg = wi[:, o + N_HEADS + d_lru:].astype(BF16)
    bf_pad = jnp.pad(b_f[0], (0, LANES - N_HEADS))[None, :]
    q_aug, k_aug, vt_aug, xb, gb = _inproj_call(x, mod3, wq, wk, wv, wf, wx, wg, bf_pad)
    attn_t = _attn_call(q_aug, k_aug, vt_aug)
    lru = _lru_call(xb, gb, conv_w[0], conv_b[0][None, :], _gate_pairs(w_rg[0], w_ig[0]),
                    b_rg[0][None, :], b_ig[0][None, :], lru_lambda[0][None, :])

    n_route = N_GROUPS + n_exp
    wr = jnp.pad(jnp.concatenate([w_grp[0], w_exp[0]], axis=1).T,
                 ((0, BUCKET_ROWS - n_route), (0, 0)))
    br = jnp.pad(jnp.concatenate([b_grp[0], b_exp[0]]), (0, BUCKET_ROWS - n_route))
    br = jnp.broadcast_to(br[:, None], (BUCKET_ROWS, LANES))
    x1, u2ext, rinfo, counts = _mixout_call(
        attn_t, lru.reshape(t, d_lru), x.reshape(t, d), mod3,
        g_attn[0][None, :], g_lru[0][None, :], w_out,
        ln1_g[0][None, :], ln1_b[0][None, :], wr, br, s)

    dest = _rank_call(rinfo, counts).reshape(t)
    cnt = counts[:N_BUCKETS, 0].astype(jnp.int32)
    ends = jnp.cumsum((cnt + (TM_E - 1)) // TM_E)
    n_tiles = t // TM_E + N_BUCKETS
    tile_bucket = jnp.sum(ends[None, :] <= jnp.arange(n_tiles)[:, None], axis=1)
    tile_bucket = jnp.minimum(tile_bucket, N_BUCKETS - 1)
    n_valid = ends[N_BUCKETS - 1:]
    last_bucket = tile_bucket[jnp.maximum(n_valid[0] - 1, 0)]
    tile_bucket = jnp.where(jnp.arange(n_tiles) < n_valid[0], tile_bucket, last_bucket)
    tile_ea = jnp.asarray(_BUCKET_EA, jnp.int32)[tile_bucket]
    tile_eb = jnp.asarray(_BUCKET_EB, jnp.int32)[tile_bucket]

    xs, wg_bf, wu_bf, wd_bf = _dispatch_call(
        dest, (ends * TM_E).astype(jnp.int32), u2ext, n_tiles * TM_E, w_e_gate, w_e_up, w_e_down)
    ys = _experts_call(tile_ea, tile_eb, n_valid.astype(jnp.int32), xs, wg_bf, wu_bf, wd_bf)
    out = _final_call(dest, x1, mod3, ln2_g[0][None, :], ln2_b[0][None, :], ys, s)
    return out.reshape(b, s, d)
```

```python
import functools

import jax
import jax.numpy as jnp
from jax import lax
from jax.experimental import pallas as pl
from jax.experimental.pallas import tpu as pltpu

F32 = jnp.float32
BF16 = jnp.bfloat16

HEAD_DIM = 64
N_HEADS = 8
N_LRU_BLOCKS = 8
CONV_WIDTH = 4
LRU_C = 8.0
N_GROUPS = 4
EXPERTS_PER_GROUP = 4
N_PAIRS = 6
N_BUCKETS = N_GROUPS * N_PAIRS
LN_EPS = 1e-5
RMS_EPS = 1e-6
NEG_INF = -1e30
DEPTH = 1
DEEPNORM_ALPHA = (2.0 * DEPTH) ** 0.25
LOG2E = 1.4426950408889634

LANES = 128
SUBLANES = 8
HEAD_PAD = LANES
VT_ROWS = 80
BUCKET_ROWS = 32
TM = 512
TQ = 512
TM_E = 256
ISSUE_UNROLL = 64
GATHER_AHEAD = 2
VMEM_LIMIT = 56 * 1024 * 1024

_PAIRS = [(0, 1), (0, 2), (0, 3), (1, 2), (1, 3), (2, 3)]
_BUCKET_EA = [g * EXPERTS_PER_GROUP + a for g in range(N_GROUPS) for (a, b) in _PAIRS]
_BUCKET_EB = [g * EXPERTS_PER_GROUP + b for g in range(N_GROUPS) for (a, b) in _PAIRS]


def _dot(a, b):
    return jnp.dot(a, b, preferred_element_type=F32)


def _split2(a):
    hi = a.astype(BF16)
    lo = (a - hi.astype(F32)).astype(BF16)
    return hi, lo


def _split3(a):
    hi = a.astype(BF16)
    r = a - hi.astype(F32)
    mid = r.astype(BF16)
    lo = (r - mid.astype(F32)).astype(BF16)
    return hi, mid, lo


def _layer_norm(x):
    mu = jnp.mean(x, axis=-1, keepdims=True)
    xc = x - mu
    var = jnp.mean(xc * xc, axis=-1, keepdims=True)
    return xc * lax.rsqrt(var + LN_EPS)


def _sigmoid(x):
    return 0.5 * jnp.tanh(0.5 * x) + 0.5


def _rot_in_group(x, k):
    n, w = x.shape
    return pltpu.roll(x.reshape(n // SUBLANES, SUBLANES, w), k, 1).reshape(n, w)


def _log_sigmoid(z):
    return jnp.minimum(z, 0.0) - jnp.log1p(jnp.exp(-jnp.abs(z)))


def _cumsum_rows(x, carry):
    n, w = x.shape
    row_in_group = lax.broadcasted_iota(jnp.int32, (n, w), 0) % SUBLANES
    k = 1
    while k < SUBLANES:
        x = x + jnp.where(row_in_group >= k, _rot_in_group(x, k), 0.0)
        k *= 2
    groups = []
    for g in range(n // SUBLANES):
        blk = x[g * SUBLANES:(g + 1) * SUBLANES] + carry
        groups.append(blk)
        carry = blk[SUBLANES - 1:SUBLANES]
    return jnp.concatenate(groups, axis=0)


def _shift_rows(x, k, fill):
    n = x.shape[0]
    if k % SUBLANES == 0:
        return jnp.concatenate([jnp.full((k, x.shape[1]), fill, x.dtype), x[:n - k]], axis=0)
    row = lax.broadcasted_iota(jnp.int32, x.shape, 0)
    return jnp.where(row >= k, pltpu.roll(x, k, 0), fill)


def _mod_kernel(c_ref, w_ref, b_ref, o_ref):
    c = c_ref[...]
    s = c * _sigmoid(c)
    sh, sl = _split2(s)
    wh, wl = _split2(w_ref[0])
    o_ref[...] = _dot(sh, wh) + (_dot(sh, wl) + _dot(sl, wh)) + b_ref[...]


def _mod_call(c_pad, w_ada, b_ada):
    rows, d = c_pad.shape
    n = w_ada.shape[2]
    return pl.pallas_call(
        _mod_kernel,
        grid=(n // d,),
        in_specs=[pl.BlockSpec((rows, d), lambda j: (0, 0)),
                  pl.BlockSpec((1, d, d), lambda j: (0, 0, j)),
                  pl.BlockSpec((1, d), lambda j: (0, j))],
        out_specs=pl.BlockSpec((rows, d), lambda j: (0, j)),
        out_shape=jax.ShapeDtypeStruct((rows, n), F32),
        compiler_params=pltpu.CompilerParams(vmem_limit_bytes=VMEM_LIMIT),
        name="mod",
    )(c_pad, w_ada, b_ada)


def _inproj_kernel(x_ref, mod_ref, wq_ref, wk_ref, wv_ref, wf_ref, wx_ref, wg_ref, bf_ref,
                   q_ref, k_ref, vt_ref, xb_ref, gb_ref, carry_ref, v_scr):
    j = pl.program_id(1)

    @pl.when(j == 0)
    def _():
        carry_ref[...] = jnp.zeros_like(carry_ref)

    tm = x_ref.shape[1]
    u = _layer_norm(x_ref[0]) * (1.0 + mod_ref[0, 1:2, :]) + mod_ref[0, 0:1, :]
    ub = u.astype(BF16)

    logf = _log_sigmoid(_dot(ub, wf_ref[...]) + bf_ref[...])
    cum = _cumsum_rows(logf, carry_ref[...])
    carry_ref[...] = cum[tm - 1:tm, :]
    c_hi, c_mid, c_lo = [p.astype(F32) for p in _split3(cum * LOG2E)]

    lane = lax.broadcasted_iota(jnp.int32, (tm, HEAD_PAD), 1)
    d = HEAD_DIM
    q_all = _dot(ub, wq_ref[...])
    k_all = _dot(ub, wk_ref[...])
    for h in range(N_HEADS):
        ts = slice((h // 2) * LANES, (h // 2 + 1) * LANES)
        qh, kh = q_all[:, ts], k_all[:, ts]
        if h % 2:
            qh, kh = pltpu.roll(qh, d, 1), pltpu.roll(kh, d, 1)
        hi, mid, lo = c_hi[:, h:h + 1], c_mid[:, h:h + 1], c_lo[:, h:h + 1]
        q_ext = jnp.where(lane == d, hi, jnp.where(lane == d + 1, mid, jnp.where(
            lane == d + 2, lo, jnp.where(lane < d + 6, 1.0, 0.0))))
        k_ext = jnp.where(lane < d + 3, 1.0, jnp.where(lane == d + 3, -hi, jnp.where(
            lane == d + 4, -mid, jnp.where(lane == d + 5, -lo, 0.0))))
        q_ref[0, h] = jnp.where(lane < d, qh, q_ext).astype(BF16)
        k_ref[0, h] = jnp.where(lane < d, kh, k_ext).astype(BF16)

    v_scr[...] = _dot(ub, wv_ref[...])
    vt_all = v_scr[...].T
    pad_rows = lax.broadcasted_iota(jnp.int32, (VT_ROWS - d, tm), 0)
    ones_row = jnp.where(pad_rows == 0, 1.0, 0.0).astype(BF16)
    for h in range(N_HEADS):
        vt_ref[0, h, 0, :d, :] = vt_all[h * d:(h + 1) * d, :].astype(BF16)
        vt_ref[0, h, 0, d:, :] = ones_row

    xb_ref[0] = _dot(ub, wx_ref[...])
    gb_ref[0] = _dot(ub, wg_ref[...])


def _inproj_call(x, mod3, wq, wk, wv, wf, wx, wg, bf_pad):
    b, s, d = x.shape
    d_attn = wq.shape[1]
    d_lru = wx.shape[1]
    full = lambda shape: pl.BlockSpec(shape, lambda bi, j: (0,) * len(shape))
    head_spec = pl.BlockSpec((1, N_HEADS, TM, HEAD_PAD), lambda bi, j: (bi, 0, j, 0))
    vt_spec = pl.BlockSpec((1, N_HEADS, 1, VT_ROWS, TM), lambda bi, j: (bi, 0, j, 0, 0))
    row_spec = pl.BlockSpec((1, TM, d_lru), lambda bi, j: (bi, j, 0))
    head_shape = jax.ShapeDtypeStruct((b, N_HEADS, s, HEAD_PAD), BF16)
    return pl.pallas_call(
        _inproj_kernel,
        grid=(b, s // TM),
        in_specs=[pl.BlockSpec((1, TM, d), lambda bi, j: (bi, j, 0)),
                  pl.BlockSpec((1, 6, d), lambda bi, j: (bi, 0, 0)),
                  full((d, d_attn)), full((d, d_attn)), full((d, d_attn)), full((d, LANES)),
                  full((d, d_lru)), full((d, d_lru)), full((1, LANES))],
        out_specs=[head_spec, head_spec, vt_spec, row_spec, row_spec],
        out_shape=[head_shape, head_shape,
                   jax.ShapeDtypeStruct((b, N_HEADS, s // TM, VT_ROWS, TM), BF16),
                   jax.ShapeDtypeStruct((b, s, d_lru), F32),
                   jax.ShapeDtypeStruct((b, s, d_lru), F32)],
        scratch_shapes=[pltpu.VMEM((1, LANES), F32), pltpu.VMEM((TM, d_attn), F32)],
        compiler_params=pltpu.CompilerParams(
            dimension_semantics=("arbitrary", "arbitrary"), vmem_limit_bytes=VMEM_LIMIT),
        name="inproj",
    )(x, mod3, wq, wk, wv, wf, wx, wg, bf_pad)


def _attn_kernel(q_ref, k_ref, vt_ref, o_ref, s_scr, smax_scr, m_scr, acc_scr):
    tq = TQ
    nq = q_ref.shape[2] // tq
    heads = range(q_ref.shape[1])

    def scores_to(slot, i, j):
        for hh in heads:
            k = k_ref[0, hh, pl.ds(pl.multiple_of(j * tq, tq), tq), :]
            q = q_ref[0, hh, pl.ds(pl.multiple_of(i * tq, tq), tq), :]
            s = lax.dot_general(k, q, (((1,), (1,)), ((), ())), preferred_element_type=F32)
            s_scr[slot, hh] = s
            smax_scr[slot, hh] = jnp.max(s, axis=0, keepdims=True)

    def consume(slot, j, masked):
        for hh in heads:
            s = s_scr[slot, hh]
            if masked:
                key = lax.broadcasted_iota(jnp.int32, (tq, tq), 0)
                qry = lax.broadcasted_iota(jnp.int32, (tq, tq), 1)
                s = jnp.where(key <= qry, s, NEG_INF)
                s_max = jnp.max(s, axis=0, keepdims=True)
            else:
                s_max = smax_scr[slot, hh]
            m = m_scr[hh]
            m_new = jnp.maximum(m, s_max)
            p = jnp.exp2(s - m_new).astype(BF16)
            acc_scr[hh] = jnp.exp2(m - m_new) * acc_scr[hh] + _dot(vt_ref[0, hh, j], p)
            m_scr[hh] = m_new

    scores_to(2, 0, 0)

    def query_block(i, carry):
        m_scr[...] = jnp.full(m_scr.shape, NEG_INF, F32)
        acc_scr[...] = jnp.zeros(acc_scr.shape, F32)
        nxt = jnp.minimum(i + 1, nq - 1)
        n_mid = i - 1

        @pl.when(i == 0)
        def _():
            consume(2, 0, True)
            scores_to(2, nxt, 0)

        @pl.when(i >= 1)
        def _():
            scores_to(1, i, 1)
            consume(2, 0, False)

        def pair(kk, c):
            j = 1 + 2 * kk
            scores_to(0, i, j + 1)
            consume(1, j, False)
            scores_to(1, i, j + 2)
            consume(0, j + 1, False)
            return c

        lax.fori_loop(0, jnp.maximum(n_mid, 0) // 2, pair, 0)

        @pl.when(jnp.logical_and(i >= 1, n_mid % 2 == 1))
        def _():
            scores_to(0, i, i)
            consume(1, i - 1, False)
            scores_to(2, nxt, 0)
            consume(0, i, True)

        @pl.when(jnp.logical_and(i >= 1, n_mid % 2 == 0))
        def _():
            scores_to(2, nxt, 0)
            consume(1, i, True)

        for hh in heads:
            o_ref[0, 0, i, hh * HEAD_DIM:(hh + 1) * HEAD_DIM, :] = (
                acc_scr[hh, :HEAD_DIM, :] / acc_scr[hh, HEAD_DIM:HEAD_DIM + 1, :])
        return carry

    lax.fori_loop(0, nq, query_block, 0)


def _attn_call(q_aug, k_aug, vt_aug):
    b, h, s, hp = q_aug.shape
    assert TQ == TM
    nq = s // TQ
    return pl.pallas_call(
        _attn_kernel,
        grid=(b, h // 2),
        in_specs=[pl.BlockSpec((1, 2, s, hp), lambda bi, p: (bi, p, 0, 0)),
                  pl.BlockSpec((1, 2, s, hp), lambda bi, p: (bi, p, 0, 0)),
                  pl.BlockSpec((1, 2, nq, VT_ROWS, TQ), lambda bi, p: (bi, p, 0, 0, 0))],
        out_specs=pl.BlockSpec((1, 1, nq, 2 * HEAD_DIM, TQ), lambda bi, p: (bi, p, 0, 0, 0)),
        out_shape=jax.ShapeDtypeStruct((b, h // 2, nq, 2 * HEAD_DIM, TQ), F32),
        scratch_shapes=[pltpu.VMEM((3, 2, TQ, TQ), F32), pltpu.VMEM((3, 2, 1, TQ), F32),
                        pltpu.VMEM((2, 1, TQ), F32), pltpu.VMEM((2, VT_ROWS, TQ), F32)],
        compiler_params=pltpu.CompilerParams(
            dimension_semantics=("arbitrary", "arbitrary"), vmem_limit_bytes=VMEM_LIMIT),
        name="attention",
    )(q_aug, k_aug, vt_aug)


def _gelu_tanh(x):
    return 0.5 * x * (1.0 + jnp.tanh(0.7978845608028654 * (x + 0.044715 * (x * x * x))))


def _lru_kernel(xb_ref, gb_ref, cw_ref, cb_ref, wgate_ref, brg_ref, big_ref, lam_ref,
                o_ref, tail_ref, h_ref):
    j = pl.program_id(1)

    @pl.when(j == 0)
    def _():
        tail_ref[...] = jnp.zeros_like(tail_ref)
        h_ref[...] = jnp.zeros_like(h_ref)

    x = xb_ref[0]
    tm, dl = x.shape
    row_in_group = lax.broadcasted_iota(jnp.int32, (tm, dl), 0) % SUBLANES
    x_prev_group = jnp.concatenate([tail_ref[...], x[:tm - SUBLANES]], axis=0)
    xc = x * cw_ref[CONV_WIDTH - 1:CONV_WIDTH, :] + cb_ref[...]
    for k in range(1, CONV_WIDTH):
        xs = jnp.where(row_in_group < k, _rot_in_group(x_prev_group, k), _rot_in_group(x, k))
        xc = xc + xs * cw_ref[CONV_WIDTH - 1 - k:CONV_WIDTH - k, :]
    tail_ref[...] = x[tm - SUBLANES:, :]

    xcb = xc.astype(BF16)
    n_pairs = dl // LANES
    r_parts, i_parts = [], []
    for p in range(n_pairs):
        g = _dot(xcb[:, p * LANES:(p + 1) * LANES], wgate_ref[p])
        r_parts.append(g[:, :LANES])
        i_parts.append(g[:, LANES:])
    r = _sigmoid(jnp.concatenate(r_parts, axis=1) + brg_ref[...])
    ig = _sigmoid(jnp.concatenate(i_parts, axis=1) + big_ref[...])

    lam = lam_ref[...]
    softplus_neg_lam = jnp.maximum(-lam, 0.0) + jnp.log1p(jnp.exp(-jnp.abs(lam)))
    log_a = (-LRU_C) * r * softplus_neg_lam
    a = jnp.exp(log_a)
    v = 1.0 - a * a
    u = jnp.where(v > 0.0, v * lax.rsqrt(v), 0.0) * (ig * xc)

    k = 1
    while k < SUBLANES:
        keep = row_in_group >= k
        u = a * jnp.where(keep, _rot_in_group(u, k), 0.0) + u
        a = a * jnp.where(keep, _rot_in_group(a, k), 1.0)
        k *= 2
    h_prev = h_ref[...]
    groups = []
    for g in range(tm // SUBLANES):
        rows = slice(g * SUBLANES, (g + 1) * SUBLANES)
        hg = a[rows] * h_prev + u[rows]
        groups.append(hg)
        h_prev = hg[SUBLANES - 1:SUBLANES]
    h_ref[...] = h_prev
    o_ref[0] = jnp.concatenate(groups, axis=0) * _gelu_tanh(gb_ref[0])


def _lru_call(xb, gb, conv_w, conv_b, wgate, b_rg, b_ig, lam):
    b, s, dl = xb.shape
    row_spec = pl.BlockSpec((1, TM, dl), lambda bi, j: (bi, j, 0))
    full = lambda shape: pl.BlockSpec(shape, lambda bi, j: (0,) * len(shape))
    return pl.pallas_call(
        _lru_kernel,
        grid=(b, s // TM),
        in_specs=[row_spec, row_spec, full(conv_w.shape), full((1, dl)), full(wgate.shape),
                  full((1, dl)), full((1, dl)), full((1, dl))],
        out_specs=row_spec,
        out_shape=jax.ShapeDtypeStruct((b, s, dl), F32),
        scratch_shapes=[pltpu.VMEM((SUBLANES, dl), F32), pltpu.VMEM((1, dl), F32)],
        compiler_params=pltpu.CompilerParams(
            dimension_semantics=("arbitrary", "arbitrary"), vmem_limit_bytes=VMEM_LIMIT),
        name="lru",
    )(xb, gb, conv_w, conv_b, wgate, b_rg, b_ig, lam)


def _rms(x, gain):
    return x * lax.rsqrt(jnp.mean(x * x, axis=-1, keepdims=True) + RMS_EPS) * gain


def _mixout_kernel(attn_ref, lru_ref, x_ref, mod_ref, ga_ref, gl_ref, woa_ref, wol_ref,
                   g1_ref, b1_ref, wr_ref, br_ref, x1_ref, u2_ref, rinfo_ref, cnt_ref, wo_bf):
    tm, d = x_ref.shape

    @pl.when(pl.program_id(0) == 0)
    def _():
        wo_bf[0] = woa_ref[0].astype(BF16)
        wo_bf[1] = wol_ref[0].astype(BF16)

    attn = attn_ref[0, :, 0].reshape(ga_ref.shape[1], tm).T
    na = _rms(attn, ga_ref[...]).astype(BF16)
    nl = _rms(lru_ref[...], gl_ref[...]).astype(BF16)
    mix = _dot(na, wo_bf[0]) + _dot(nl, wo_bf[1])
    z = DEEPNORM_ALPHA * x_ref[...] + (1.0 + mod_ref[0, 2:3, :]) * mix
    x1 = _layer_norm(z) * g1_ref[...] + b1_ref[...]
    x1_ref[...] = x1
    u2 = _layer_norm(x1) * (1.0 + mod_ref[0, 4:5, :]) + mod_ref[0, 3:4, :]

    nt = (((1,), (1,)), ((), ()))
    uh, ul = _split2(u2)
    wh, wl = _split2(wr_ref[...])
    dg = lambda a, b_: lax.dot_general(a, b_, nt, preferred_element_type=F32)
    lg = dg(wh, uh) + (dg(wh, ul) + dg(wl, uh)) + br_ref[:, 0:1]

    def first_index(vals, target):
        idx = jnp.full_like(target, float(len(vals) - 1))
        for n in range(len(vals) - 2, -1, -1):
            idx = jnp.where(vals[n] == target, float(n), idx)
        return idx

    g = [lg[n:n + 1, :] for n in range(N_GROUPS)]
    gmax = functools.reduce(jnp.maximum, g)
    gsum = functools.reduce(lambda a, b_: a + b_, [jnp.exp(v - gmax) for v in g])
    grp_w = 1.0 / gsum
    gidx = first_index(g, gmax)

    sel = []
    for e in range(EXPERTS_PER_GROUP):
        v = lg[N_GROUPS + (N_GROUPS - 1) * EXPERTS_PER_GROUP + e:
               N_GROUPS + (N_GROUPS - 1) * EXPERTS_PER_GROUP + e + 1, :]
        for gi in range(N_GROUPS - 2, -1, -1):
            r0 = N_GROUPS + gi * EXPERTS_PER_GROUP + e
            v = jnp.where(gidx == float(gi), lg[r0:r0 + 1, :], v)
        sel.append(v)
    smax = functools.reduce(jnp.maximum, sel)
    i1 = first_index(sel, smax)
    rest = [jnp.where(i1 == float(e), -3e38, sel[e]) for e in range(EXPERTS_PER_GROUP)]
    rmax = functools.reduce(jnp.maximum, rest)
    i2 = first_index(rest, rmax)
    e2 = jnp.exp(rmax - smax)
    w1 = grp_w / (1.0 + e2)
    w2 = grp_w * e2 / (1.0 + e2)
    ia = jnp.minimum(i1, i2)
    ib = jnp.maximum(i1, i2)
    wa = jnp.where(i1 < i2, w1, w2)
    wb = jnp.where(i1 < i2, w2, w1)
    pair = jnp.where(ia == 0.0, ib - 1.0, jnp.where(ia == 1.0, ib + 1.0, 5.0))
    bucket = gidx * float(N_PAIRS) + pair

    @pl.when(pl.program_id(0) == 0)
    def _():
        cnt_ref[...] = jnp.zeros_like(cnt_ref)

    cid = lax.broadcasted_iota(jnp.int32, (BUCKET_ROWS, tm), 0).astype(F32)
    cnt_ref[...] += jnp.sum(jnp.where(cid == bucket, 1.0, 0.0), axis=1, keepdims=True)

    zrow = jnp.zeros_like(wa)
    rinfo_ref[...] = jnp.concatenate([bucket, wa, wb] + [zrow] * (SUBLANES - 3), axis=0)
    wt = jnp.concatenate([wa, wb, jnp.zeros((LANES - 2, tm), F32)], axis=0)
    u2_ref[:, :d] = u2
    u2_ref[:, d:] = wt.T


def _mixout_call(attn_t, lru, x2d, mod3, ga, gl, wo, g1, b1, wr, br, seq):
    t, d = x2d.shape
    _, n_pairs, _, pair_w, _ = attn_t.shape
    dh = n_pairs * pair_w
    assert wo.shape[1] == 2 * dh and lru.shape[1] == dh
    per_b = seq // TM
    full = lambda shape: pl.BlockSpec(shape, lambda i: (0,) * len(shape))
    return pl.pallas_call(
        _mixout_kernel,
        grid=(t // TM,),
        in_specs=[pl.BlockSpec((1, n_pairs, 1, pair_w, TM),
                               lambda i: (i // per_b, 0, i % per_b, 0, 0)),
                  pl.BlockSpec((TM, dh), lambda i: (i, 0)),
                  pl.BlockSpec((TM, d), lambda i: (i, 0)),
                  pl.BlockSpec((1, 6, d), lambda i: (i // per_b, 0, 0)),
                  full((1, dh)), full((1, dh)),
                  pl.BlockSpec((1, dh, d), lambda i: (0, 0, 0)),
                  pl.BlockSpec((1, dh, d), lambda i: (0, 1, 0)),
                  full((1, d)), full((1, d)), full((BUCKET_ROWS, d)), full((BUCKET_ROWS, LANES))],
        out_specs=[pl.BlockSpec((TM, d), lambda i: (i, 0)),
                   pl.BlockSpec((TM, d + LANES), lambda i: (i, 0)),
                   pl.BlockSpec((SUBLANES, TM), lambda i: (0, i)),
                   pl.BlockSpec((BUCKET_ROWS, LANES), lambda i: (0, 0))],
        out_shape=[jax.ShapeDtypeStruct((t, d), F32),
                   jax.ShapeDtypeStruct((t, d + LANES), F32),
                   jax.ShapeDtypeStruct((SUBLANES, t), F32),
                   jax.ShapeDtypeStruct((BUCKET_ROWS, LANES), F32)],
        scratch_shapes=[pltpu.VMEM((2, dh, d), BF16)],
        compiler_params=pltpu.CompilerParams(
            dimension_semantics=("arbitrary",), vmem_limit_bytes=VMEM_LIMIT),
        name="mixout",
    )(attn_t, lru, x2d, mod3, ga, gl, wo, wo, g1, b1, wr, br)


def _rank_kernel(rinfo_ref, counts_ref, dest_ref, carry_ref, offs_ref):
    tm = rinfo_ref.shape[1]

    @pl.when(pl.program_id(0) == 0)
    def _():
        carry_ref[...] = jnp.zeros_like(carry_ref)
        padded = jnp.floor((counts_ref[...] + float(TM_E - 1)) * (1.0 / TM_E)) * float(TM_E)
        inc = padded
        k = 1
        while k < BUCKET_ROWS:
            inc = inc + _shift_rows(inc, k, 0.0)
            k *= 2
        offs_ref[...] = inc - padded

    bucket = rinfo_ref[0:1, :]
    cid = lax.broadcasted_iota(jnp.int32, (BUCKET_ROWS, tm), 0).astype(F32)
    onehot = jnp.where(cid == bucket, 1.0, 0.0)
    srow = lax.broadcasted_iota(jnp.int32, (tm, tm), 0)
    scol = lax.broadcasted_iota(jnp.int32, (tm, tm), 1)
    upper = (srow <= scol).astype(BF16)
    prefix = _dot(onehot.astype(BF16), upper)
    carry = carry_ref[...]
    rank = jnp.sum(onehot * (prefix - 1.0 + carry[:, 0:1] + offs_ref[:, 0:1]),
                   axis=0, keepdims=True)
    dest_ref[...] = rank.astype(jnp.int32)
    carry_ref[...] = carry + prefix[:, tm - 1:tm]


def _rank_call(rinfo, counts):
    t = rinfo.shape[1]
    return pl.pallas_call(
        _rank_kernel,
        grid=(t // TM,),
        in_specs=[pl.BlockSpec((SUBLANES, TM), lambda i: (0, i)),
                  pl.BlockSpec((BUCKET_ROWS, LANES), lambda i: (0, 0))],
        out_specs=pl.BlockSpec((1, TM), lambda i: (0, i)),
        out_shape=jax.ShapeDtypeStruct((1, t), jnp.int32),
        scratch_shapes=[pltpu.VMEM((BUCKET_ROWS, LANES), F32),
                        pltpu.VMEM((BUCKET_ROWS, LANES), F32)],
        compiler_params=pltpu.CompilerParams(
            dimension_semantics=("arbitrary",), vmem_limit_bytes=VMEM_LIMIT),
        name="rank",
    )(rinfo, counts)


def _dispatch_kernel(dest_ref, ends_ref, u2_ref, wg_ref, wu_ref, wd_ref,
                     xs_ref, wg_bf_ref, wu_bf_ref, wd_bf_ref, zbuf, sem, zsem):
    tm = u2_ref.shape[0]
    t0 = pl.program_id(0) * tm

    @pl.when(pl.program_id(0) == 0)
    def _():
        zbuf[...] = jnp.zeros_like(zbuf)

        def tail_copy(bkt):
            end = ends_ref[bkt]
            start = ends_ref[bkt - 1] if bkt else 0
            tail = pl.multiple_of(jnp.maximum(end - TM_E, 0), TM_E)
            return end > start, pltpu.make_async_copy(zbuf, xs_ref.at[pl.ds(tail, TM_E)], zsem)

        for bkt in range(N_BUCKETS):
            nonempty, copy = tail_copy(bkt)
            pl.when(nonempty)(copy.start)
        for bkt in range(N_BUCKETS):
            nonempty, copy = tail_copy(bkt)
            pl.when(nonempty)(copy.wait)

        def unused_tile_copy(k):
            return pltpu.make_async_copy(
                zbuf, xs_ref.at[pl.ds(pl.multiple_of(k * TM_E, TM_E), TM_E)], zsem)

        first_unused = ends_ref[N_BUCKETS - 1] // TM_E
        n_tiles = xs_ref.shape[0] // TM_E
        lax.fori_loop(first_unused, n_tiles, lambda k, c: (unused_tile_copy(k).start(), c)[1], 0)
        lax.fori_loop(first_unused, n_tiles, lambda k, c: (unused_tile_copy(k).wait(), c)[1], 0)

    for r in range(tm):
        pltpu.make_async_copy(u2_ref.at[pl.ds(r, 1)],
                              xs_ref.at[pl.ds(dest_ref[t0 + r], 1)], sem).start()
    wg_bf_ref[0] = wg_ref[0, 0].astype(BF16)
    wu_bf_ref[0] = wu_ref[0, 0].astype(BF16)
    wd_bf_ref[0] = wd_ref[0, 0].astype(BF16)
    pltpu.make_async_copy(u2_ref, xs_ref.at[pl.ds(0, tm)], sem).wait()


def _dispatch_call(dest, bucket_ends, u2ext, n_rows, w_gate, w_up, w_down):
    t, w = u2ext.shape
    _, n_exp, d, de = w_gate.shape
    tm = t // n_exp
    assert t % n_exp == 0 and tm % ISSUE_UNROLL == 0
    w_in = lambda shape: pl.BlockSpec((1, 1) + shape, lambda i, dr, er: (0, i, 0, 0))
    w_out = lambda shape: pl.BlockSpec((1,) + shape, lambda i, dr, er: (i, 0, 0))
    grid_spec = pltpu.PrefetchScalarGridSpec(
        num_scalar_prefetch=2,
        grid=(n_exp,),
        in_specs=[pl.BlockSpec((tm, w), lambda i, dr, er: (i, 0)),
                  w_in((d, de)), w_in((d, de)), w_in((de, d))],
        out_specs=[pl.BlockSpec(memory_space=pl.ANY),
                   w_out((d, de)), w_out((d, de)), w_out((de, d))],
        scratch_shapes=[pltpu.VMEM((TM_E, w), F32), pltpu.SemaphoreType.DMA(()),
                        pltpu.SemaphoreType.DMA(())],
    )
    return pl.pallas_call(
        _dispatch_kernel,
        grid_spec=grid_spec,
        out_shape=[jax.ShapeDtypeStruct((n_rows, w), F32),
                   jax.ShapeDtypeStruct((n_exp, d, de), BF16),
                   jax.ShapeDtypeStruct((n_exp, d, de), BF16),
                   jax.ShapeDtypeStruct((n_exp, de, d), BF16)],
        compiler_params=pltpu.CompilerParams(
            dimension_semantics=("arbitrary",), vmem_limit_bytes=VMEM_LIMIT),
        name="dispatch",
    )(dest, bucket_ends, u2ext, w_gate, w_up, w_down)


def _experts_kernel(ea_ref, eb_ref, nv_ref, xs_ref, wga_ref, wua_ref, wda_ref,
                    wgb_ref, wub_ref, wdb_ref, ys_ref):
    del ea_ref, eb_ref
    i = pl.program_id(0)
    d = ys_ref.shape[1]

    @pl.when(i < nv_ref[0])
    def _():
        x = xs_ref[:, :d].astype(BF16)

        def expert(wg_ref, wu_ref, wd_ref):
            g = _dot(x, wg_ref[0])
            h = (g * _sigmoid(g)) * _dot(x, wu_ref[0])
            return _dot(h.astype(BF16), wd_ref[0])

        ya = xs_ref[:, d:d + 1] * expert(wga_ref, wua_ref, wda_ref)
        ys_ref[...] = ya + xs_ref[:, d + 1:d + 2] * expert(wgb_ref, wub_ref, wdb_ref)

    @pl.when(i >= nv_ref[0])
    def _():
        ys_ref[...] = jnp.zeros_like(ys_ref)


def _experts_call(tile_ea, tile_eb, n_valid, xs, wg, wu, wd):
    tp, w = xs.shape
    _, d, de = wg.shape
    row = lambda i, ea, eb, nv: (jnp.minimum(i, nv[0] - 1), 0)
    wa = lambda i, ea, eb, nv: (ea[i], 0, 0)
    wb = lambda i, ea, eb, nv: (eb[i], 0, 0)
    grid_spec = pltpu.PrefetchScalarGridSpec(
        num_scalar_prefetch=3,
        grid=(tp // TM_E,),
        in_specs=[pl.BlockSpec((TM_E, w), row),
                  pl.BlockSpec((1, d, de), wa), pl.BlockSpec((1, d, de), wa),
                  pl.BlockSpec((1, de, d), wa),
                  pl.BlockSpec((1, d, de), wb), pl.BlockSpec((1, d, de), wb),
                  pl.BlockSpec((1, de, d), wb)],
        out_specs=pl.BlockSpec((TM_E, d), lambda i, ea, eb, nv: (i, 0)),
    )
    return pl.pallas_call(
        _experts_kernel,
        grid_spec=grid_spec,
        out_shape=jax.ShapeDtypeStruct((tp, d), F32),
        compiler_params=pltpu.CompilerParams(
            dimension_semantics=("arbitrary",), vmem_limit_bytes=VMEM_LIMIT),
        name="experts",
    )(tile_ea, tile_eb, n_valid, xs, wg, wu, wd, wg, wu, wd)


def _final_kernel(dest_ref, x1_ref, mod_ref, g2_ref, b2_ref, ys_ref, o_ref, ybuf, sem):
    tm = x1_ref.shape[0]
    i = pl.program_id(0)
    slot = i % GATHER_AHEAD

    def issue_rows(tile, to_slot, r0, n):
        for r in range(n):
            pltpu.make_async_copy(ys_ref.at[pl.ds(dest_ref[tile * tm + r0 + r], 1)],
                                  ybuf.at[to_slot, pl.ds(r0 + r, 1)], sem.at[to_slot]).start()

    def normalise_rows(r0, n):
        rows = pl.ds(r0, n)
        z = DEEPNORM_ALPHA * x1_ref[rows, :] + (1.0 + mod_ref[0, 5:6, :]) * ybuf[slot, rows, :]
        o_ref[rows, :] = _layer_norm(z) * g2_ref[...] + b2_ref[...]

    n_chunks = tm // ISSUE_UNROLL

    @pl.when(i == 0)
    def _():
        for tile in range(GATHER_AHEAD):
            issue_rows(tile, tile, 0, tm)

    pltpu.make_async_copy(ys_ref.at[pl.ds(0, tm)], ybuf.at[slot], sem.at[slot]).wait()

    def chunk(c, issue_ahead):
        r0 = c * ISSUE_UNROLL
        normalise_rows(r0, ISSUE_UNROLL)
        if issue_ahead:
            issue_rows(i + GATHER_AHEAD, slot, r0, ISSUE_UNROLL)

    more = i + GATHER_AHEAD < pl.num_programs(0)

    @pl.when(more)
    def _():
        for c in range(n_chunks):
            chunk(c, True)

    @pl.when(jnp.logical_not(more))
    def _():
        for c in range(n_chunks):
            chunk(c, False)


def _final_call(dest, x1, mod3, g2, b2, ys, seq):
    t, d = x1.shape
    per_b = seq // TM
    grid_spec = pltpu.PrefetchScalarGridSpec(
        num_scalar_prefetch=1,
        grid=(t // TM,),
        in_specs=[pl.BlockSpec((TM, d), lambda i, dr: (i, 0)),
                  pl.BlockSpec((1, 6, d), lambda i, dr: (i // per_b, 0, 0)),
                  pl.BlockSpec((1, d), lambda i, dr: (0, 0)),
                  pl.BlockSpec((1, d), lambda i, dr: (0, 0)),
                  pl.BlockSpec(memory_space=pl.ANY)],
        out_specs=pl.BlockSpec((TM, d), lambda i, dr: (i, 0)),
        scratch_shapes=[pltpu.VMEM((GATHER_AHEAD, TM, d), F32),
                        pltpu.SemaphoreType.DMA((GATHER_AHEAD,))],
    )
    return pl.pallas_call(
        _final_kernel,
        grid_spec=grid_spec,
        out_shape=jax.ShapeDtypeStruct((t, d), F32),
        compiler_params=pltpu.CompilerParams(
            dimension_semantics=("arbitrary",), vmem_limit_bytes=VMEM_LIMIT),
        name="final",
    )(dest, x1, mod3, g2, b2, ys)


def _gate_pairs(w_rg, w_ig):
    def pairs(w):
        n, bs, _ = w.shape
        w = w.reshape(n // 2, 2, bs, bs)
        z = jnp.zeros((n // 2, bs, bs), w.dtype)
        top = jnp.concatenate([w[:, 0], z], axis=2)
        bot = jnp.concatenate([z, w[:, 1]], axis=2)
        return jnp.concatenate([top, bot], axis=1)
    return jnp.concatenate([pairs(w_rg), pairs(w_ig)], axis=2).astype(BF16)


def kernel(x, c, w_ada, b_ada, w_in, b_f, conv_w, conv_b, w_rg, b_rg, w_ig, b_ig, lru_lambda,
           g_attn, g_lru, w_out, ln1_g, ln1_b, w_grp, b_grp, w_exp, b_exp,
           w_e_gate, w_e_up, w_e_down, ln2_g, ln2_b):
    assert w_ada.shape[0] == DEPTH
    b, s, d = x.shape
    t = b * s
    d_attn = N_HEADS * HEAD_DIM
    d_lru = conv_w.shape[2]
    n_exp = w_e_gate.shape[1]
    assert s % TM == 0 and s % TQ == 0 and t % TM_E == 0

    c_pad = jnp.pad(c, ((0, SUBLANES - b), (0, 0)))
    mod = _mod_call(c_pad, w_ada, b_ada[0][None, :])
    mod3 = mod[:b].reshape(b, 6, d)

    wi = w_in[0]
    o = 3 * d_attn
    wq = (wi[:, :d_attn] * (HEAD_DIM ** -0.5 * LOG2E)).astype(BF16)
    wk = wi[:, d_attn:2 * d_attn].astype(BF16)
    wv = wi[:, 2 * d_attn:o].astype(BF16)
    wf = jnp.pad(wi[:, o:o + N_HEADS], ((0, 0), (0, LANES - N_HEADS))).astype(BF16)
    wx = wi[:, o + N_HEADS:o + N_HEADS + d_lru].astype(BF16)
    wg = wi[:, o + N_HEADS + d_lru:].astype(BF16)
    bf_pad = jnp.pad(b_f[0], (0, LANES - N_HEADS))[None, :]
    q_aug, k_aug, vt_aug, xb, gb = _inproj_call(x, mod3, wq, wk, wv, wf, wx, wg, bf_pad)
    attn_t = _attn_call(q_aug, k_aug, vt_aug)
    lru = _lru_call(xb, gb, conv_w[0], conv_b[0][None, :], _gate_pairs(w_rg[0], w_ig[0]),
                    b_rg[0][None, :], b_ig[0][None, :], lru_lambda[0][None, :])

    n_route = N_GROUPS + n_exp
    wr = jnp.pad(jnp.concatenate([w_grp[0], w_exp[0]], axis=1).T,
                 ((0, BUCKET_ROWS - n_route), (0, 0)))
    br = jnp.pad(jnp.concatenate([b_grp[0], b_exp[0]]), (0, BUCKET_ROWS - n_route))
    br = jnp.broadcast_to(br[:, None], (BUCKET_ROWS, LANES))
    x1, u2ext, rinfo, counts = _mixout_call(
        attn_t, lru.reshape(t, d_lru), x.reshape(t, d), mod3,
        g_attn[0][None, :], g_lru[0][None, :], w_out,
        ln1_g[0][None, :], ln1_b[0][None, :], wr, br, s)

    dest = _rank_call(rinfo, counts).reshape(t)
    cnt = counts[:N_BUCKETS, 0].astype(jnp.int32)
    ends = jnp.cumsum((cnt + (TM_E - 1)) // TM_E)
    n_tiles = t // TM_E + N_BUCKETS
    tile_bucket = jnp.sum(ends[None, :] <= jnp.arange(n_tiles)[:, None], axis=1)
    tile_bucket = jnp.minimum(tile_bucket, N_BUCKETS - 1)
    n_valid = ends[N_BUCKETS - 1:]
    last_bucket = tile_bucket[jnp.maximum(n_valid[0] - 1, 0)]
    tile_bucket = jnp.where(jnp.arange(n_tiles) < n_valid[0], tile_bucket, last_bucket)
    tile_ea = jnp.asarray(_BUCKET_EA, jnp.int32)[tile_bucket]
    tile_eb = jnp.asarray(_BUCKET_EB, jnp.int32)[tile_bucket]

    xs, wg_bf, wu_bf, wd_bf = _dispatch_call(
        dest, (ends * TM_E).astype(jnp.int32), u2ext, n_tiles * TM_E, w_e_gate, w_e_up, w_e_down)
    ys = _experts_call(tile_ea, tile_eb, n_valid.astype(jnp.int32), xs, wg_bf, wu_bf, wd_bf)
    out = _final_call(dest, x1, mod3, ln2_g[0][None, :], ln2_b[0][None, :], ys, s)
    return out.reshape(b, s, d)
```

```python
import functools

import jax
import jax.numpy as jnp
from jax import lax
from jax.experimental import pallas as pl
from jax.experimental.pallas import tpu as pltpu

F32 = jnp.float32
BF16 = jnp.bfloat16

HEAD_DIM = 64
N_HEADS = 8
N_LRU_BLOCKS = 8
CONV_WIDTH = 4
LRU_C = 8.0
N_GROUPS = 4
EXPERTS_PER_GROUP = 4
N_PAIRS = 6
N_BUCKETS = N_GROUPS * N_PAIRS
LN_EPS = 1e-5
RMS_EPS = 1e-6
NEG_INF = -1e30
DEPTH = 1
DEEPNORM_ALPHA = (2.0 * DEPTH) ** 0.25
LOG2E = 1.4426950408889634

LANES = 128
SUBLANES = 8
HEAD_PAD = LANES
VT_ROWS = 80
BUCKET_ROWS = 32
TM = 512
TQ = 512
TM_E = 256
ISSUE_UNROLL = 64
GATHER_AHEAD = 2
VMEM_LIMIT = 56 * 1024 * 1024

_PAIRS = [(0, 1), (0, 2), (0, 3), (1, 2), (1, 3), (2, 3)]
_BUCKET_EA = [g * EXPERTS_PER_GROUP + a for g in range(N_GROUPS) for (a, b) in _PAIRS]
_BUCKET_EB = [g * EXPERTS_PER_GROUP + b for g in range(N_GROUPS) for (a, b) in _PAIRS]


def _dot(a, b):
    return jnp.dot(a, b, preferred_element_type=F32)


def _split2(a):
    hi = a.astype(BF16)
    lo = (a - hi.astype(F32)).astype(BF16)
    return hi, lo


def _split3(a):
    hi = a.astype(BF16)
    r = a - hi.astype(F32)
    mid = r.astype(BF16)
    lo = (r - mid.astype(F32)).astype(BF16)
    return hi, mid, lo


def _layer_norm(x):
    mu = jnp.mean(x, axis=-1, keepdims=True)
    xc = x - mu
    var = jnp.mean(xc * xc, axis=-1, keepdims=True)
    return xc * lax.rsqrt(var + LN_EPS)


def _sigmoid(x):
    return 0.5 * jnp.tanh(0.5 * x) + 0.5


def _rot_in_group(x, k):
    n, w = x.shape
    return pltpu.roll(x.reshape(n // SUBLANES, SUBLANES, w), k, 1).reshape(n, w)


def _log_sigmoid(z):
    return jnp.minimum(z, 0.0) - jnp.log1p(jnp.exp(-jnp.abs(z)))


def _cumsum_rows(x, carry):
    n, w = x.shape
    row_in_group = lax.broadcasted_iota(jnp.int32, (n, w), 0) % SUBLANES
    k = 1
    while k < SUBLANES:
        x = x + jnp.where(row_in_group >= k, _rot_in_group(x, k), 0.0)
        k *= 2
    groups = []
    for g in range(n // SUBLANES):
        blk = x[g * SUBLANES:(g + 1) * SUBLANES] + carry
        groups.append(blk)
        carry = blk[SUBLANES - 1:SUBLANES]
    return jnp.concatenate(groups, axis=0)


def _shift_rows(x, k, fill):
    n = x.shape[0]
    if k % SUBLANES == 0:
        return jnp.concatenate([jnp.full((k, x.shape[1]), fill, x.dtype), x[:n - k]], axis=0)
    row = lax.broadcasted_iota(jnp.int32, x.shape, 0)
    return jnp.where(row >= k, pltpu.roll(x, k, 0), fill)


def _mod_kernel(c_ref, w_ref, b_ref, o_ref):
    c = c_ref[...]
    s = c * _sigmoid(c)
    sh, sl = _split2(s)
    wh, wl = _split2(w_ref[0])
    o_ref[...] = _dot(sh, wh) + (_dot(sh, wl) + _dot(sl, wh)) + b_ref[...]


def _mod_call(c_pad, w_ada, b_ada):
    rows, d = c_pad.shape
    n = w_ada.shape[2]
    return pl.pallas_call(
        _mod_kernel,
        grid=(n // d,),
        in_specs=[pl.BlockSpec((rows, d), lambda j: (0, 0)),
                  pl.BlockSpec((1, d, d), lambda j: (0, 0, j)),
                  pl.BlockSpec((1, d), lambda j: (0, j))],
        out_specs=pl.BlockSpec((rows, d), lambda j: (0, j)),
        out_shape=jax.ShapeDtypeStruct((rows, n), F32),
        compiler_params=pltpu.CompilerParams(vmem_limit_bytes=VMEM_LIMIT),
        name="mod",
    )(c_pad, w_ada, b_ada)


def _inproj_kernel(x_ref, mod_ref, wt_ref, bf_ref,
                   q_ref, k_ref, vt_ref, xb_ref, gb_ref, carry_ref, v_scr, w_scr, stage):
    j = pl.program_id(1)
    d_attn = N_HEADS * HEAD_DIM
    d_lru = xb_ref.shape[2]
    c_k, c_v, c_x = d_attn, 2 * d_attn, 3 * d_attn
    c_g, c_f = c_x + d_lru, c_x + 2 * d_lru
    r_f = 3 * d_attn
    r_x, r_g = r_f + N_HEADS, r_f + N_HEADS + d_lru

    @pl.when(jnp.logical_and(pl.program_id(0) == 0, j == 0))
    def _():
        def load_block(r0, n, c0, scale=None, keep_lanes=None):
            pltpu.sync_copy(wt_ref.at[0, pl.ds(r0, n), :], stage.at[pl.ds(0, n), :])
            blk = stage[0:n, :]
            if scale is not None:
                blk = blk * scale
            blk = blk.T
            if keep_lanes is not None:
                lane = lax.broadcasted_iota(jnp.int32, blk.shape, 1)
                blk = jnp.where(lane < keep_lanes, blk, 0.0)
            w_scr[:, c0:c0 + n] = blk.astype(BF16)

        load_block(0, d_attn, 0, scale=HEAD_DIM ** -0.5 * LOG2E)
        load_block(d_attn, d_attn, c_k)
        load_block(2 * d_attn, d_attn, c_v)
        load_block(r_x, d_lru, c_x)
        load_block(r_g, d_lru, c_g)
        load_block(r_f, LANES, c_f, keep_lanes=N_HEADS)

    @pl.when(j == 0)
    def _():
        carry_ref[...] = jnp.zeros_like(carry_ref)

    tm = x_ref.shape[1]
    u = _layer_norm(x_ref[0]) * (1.0 + mod_ref[0, 1:2, :]) + mod_ref[0, 0:1, :]
    ub = u.astype(BF16)

    logf = _log_sigmoid(_dot(ub, w_scr[:, c_f:c_f + LANES]) + bf_ref[...])
    cum = _cumsum_rows(logf, carry_ref[...])
    carry_ref[...] = cum[tm - 1:tm, :]
    c_hi, c_mid, c_lo = [p.astype(F32) for p in _split3(cum * LOG2E)]

    lane = lax.broadcasted_iota(jnp.int32, (tm, HEAD_PAD), 1)
    d = HEAD_DIM
    q_all = _dot(ub, w_scr[:, 0:c_k])
    k_all = _dot(ub, w_scr[:, c_k:c_v])
    for h in range(N_HEADS):
        ts = slice((h // 2) * LANES, (h // 2 + 1) * LANES)
        qh, kh = q_all[:, ts], k_all[:, ts]
        if h % 2:
            qh, kh = pltpu.roll(qh, d, 1), pltpu.roll(kh, d, 1)
        hi, mid, lo = c_hi[:, h:h + 1], c_mid[:, h:h + 1], c_lo[:, h:h + 1]
        q_ext = jnp.where(lane == d, hi, jnp.where(lane == d + 1, mid, jnp.where(
            lane == d + 2, lo, jnp.where(lane < d + 6, 1.0, 0.0))))
        k_ext = jnp.where(lane < d + 3, 1.0, jnp.where(lane == d + 3, -hi, jnp.where(
            lane == d + 4, -mid, jnp.where(lane == d + 5, -lo, 0.0))))
        q_ref[0, h] = jnp.where(lane < d, qh, q_ext).astype(BF16)
        k_ref[0, h] = jnp.where(lane < d, kh, k_ext).astype(BF16)

    v_scr[...] = _dot(ub, w_scr[:, c_v:c_x])
    vt_all = v_scr[...].T
    pad_rows = lax.broadcasted_iota(jnp.int32, (VT_ROWS - d, tm), 0)
    ones_row = jnp.where(pad_rows == 0, 1.0, 0.0).astype(BF16)
    for h in range(N_HEADS):
        vt_ref[0, h, 0, :d, :] = vt_all[h * d:(h + 1) * d, :].astype(BF16)
        vt_ref[0, h, 0, d:, :] = ones_row

    xb_ref[0] = _dot(ub, w_scr[:, c_x:c_g])
    gb_ref[0] = _dot(ub, w_scr[:, c_g:c_f])


def _inproj_call(x, mod3, w_in_t, bf_pad, d_lru):
    b, s, d = x.shape
    d_attn = N_HEADS * HEAD_DIM
    assert w_in_t.shape[1] == 3 * d_attn + N_HEADS + 2 * d_lru
    full = lambda shape: pl.BlockSpec(shape, lambda bi, j: (0,) * len(shape))
    head_spec = pl.BlockSpec((1, N_HEADS, TM, HEAD_PAD), lambda bi, j: (bi, 0, j, 0))
    vt_spec = pl.BlockSpec((1, N_HEADS, 1, VT_ROWS, TM), lambda bi, j: (bi, 0, j, 0, 0))
    row_spec = pl.BlockSpec((1, TM, d_lru), lambda bi, j: (bi, j, 0))
    head_shape = jax.ShapeDtypeStruct((b, N_HEADS, s, HEAD_PAD), BF16)
    return pl.pallas_call(
        _inproj_kernel,
        grid=(b, s // TM),
        in_specs=[pl.BlockSpec((1, TM, d), lambda bi, j: (bi, j, 0)),
                  pl.BlockSpec((1, 6, d), lambda bi, j: (bi, 0, 0)),
                  pl.BlockSpec(memory_space=pl.ANY), full((1, LANES))],
        out_specs=[head_spec, head_spec, vt_spec, row_spec, row_spec],
        out_shape=[head_shape, head_shape,
                   jax.ShapeDtypeStruct((b, N_HEADS, s // TM, VT_ROWS, TM), BF16),
                   jax.ShapeDtypeStruct((b, s, d_lru), F32),
                   jax.ShapeDtypeStruct((b, s, d_lru), F32)],
        scratch_shapes=[pltpu.VMEM((1, LANES), F32), pltpu.VMEM((TM, d_attn), F32),
                        pltpu.VMEM((d, 3 * d_attn + 2 * d_lru + LANES), BF16),
                        pltpu.VMEM((max(d_attn, d_lru), d), F32)],
        compiler_params=pltpu.CompilerParams(
            dimension_semantics=("arbitrary", "arbitrary"), vmem_limit_bytes=VMEM_LIMIT),
        name="inproj",
    )(x, mod3, w_in_t, bf_pad)


def _attn_kernel(q_ref, k_ref, vt_ref, o_ref, s_scr, smax_scr, m_scr, acc_scr):
    tq = TQ
    nq = q_ref.shape[2] // tq
    heads = range(q_ref.shape[1])

    def scores_to(slot, i, j):
        for hh in heads:
            k = k_ref[0, hh, pl.ds(pl.multiple_of(j * tq, tq), tq), :]
            q = q_ref[0, hh, pl.ds(pl.multiple_of(i * tq, tq), tq), :]
            s = lax.dot_general(k, q, (((1,), (1,)), ((), ())), preferred_element_type=F32)
            s_scr[slot, hh] = s
            smax_scr[slot, hh] = jnp.max(s, axis=0, keepdims=True)

    def consume(slot, j, masked):
        for hh in heads:
            s = s_scr[slot, hh]
            if masked:
                key = lax.broadcasted_iota(jnp.int32, (tq, tq), 0)
                qry = lax.broadcasted_iota(jnp.int32, (tq, tq), 1)
                s = jnp.where(key <= qry, s, NEG_INF)
                s_max = jnp.max(s, axis=0, keepdims=True)
            else:
                s_max = smax_scr[slot, hh]
            m = m_scr[hh]
            m_new = jnp.maximum(m, s_max)
            p = jnp.exp2(s - m_new).astype(BF16)
            acc_scr[hh] = jnp.exp2(m - m_new) * acc_scr[hh] + _dot(vt_ref[0, hh, j], p)
            m_scr[hh] = m_new

    scores_to(2, 0, 0)

    def query_block(i, carry):
        m_scr[...] = jnp.full(m_scr.shape, NEG_INF, F32)
        acc_scr[...] = jnp.zeros(acc_scr.shape, F32)
        nxt = jnp.minimum(i + 1, nq - 1)
        n_mid = i - 1

        @pl.when(i == 0)
        def _():
            consume(2, 0, True)
            scores_to(2, nxt, 0)

        @pl.when(i >= 1)
        def _():
            scores_to(1, i, 1)
            consume(2, 0, False)

        def pair(kk, c):
            j = 1 + 2 * kk
            scores_to(0, i, j + 1)
            consume(1, j, False)
            scores_to(1, i, j + 2)
            consume(0, j + 1, False)
            return c

        lax.fori_loop(0, jnp.maximum(n_mid, 0) // 2, pair, 0)

        @pl.when(jnp.logical_and(i >= 1, n_mid % 2 == 1))
        def _():
            scores_to(0, i, i)
            consume(1, i - 1, False)
            scores_to(2, nxt, 0)
            consume(0, i, True)

        @pl.when(jnp.logical_and(i >= 1, n_mid % 2 == 0))
        def _():
            scores_to(2, nxt, 0)
            consume(1, i, True)

        for hh in heads:
            o_ref[0, 0, i, hh * HEAD_DIM:(hh + 1) * HEAD_DIM, :] = (
                acc_scr[hh, :HEAD_DIM, :] / acc_scr[hh, HEAD_DIM:HEAD_DIM + 1, :])
        return carry

    lax.fori_loop(0, nq, query_block, 0)


def _attn_call(q_aug, k_aug, vt_aug):
    b, h, s, hp = q_aug.shape
    assert TQ == TM
    nq = s // TQ
    return pl.pallas_call(
        _attn_kernel,
        grid=(b, h // 2),
        in_specs=[pl.BlockSpec((1, 2, s, hp), lambda bi, p: (bi, p, 0, 0)),
                  pl.BlockSpec((1, 2, s, hp), lambda bi, p: (bi, p, 0, 0)),
                  pl.BlockSpec((1, 2, nq, VT_ROWS, TQ), lambda bi, p: (bi, p, 0, 0, 0))],
        out_specs=pl.BlockSpec((1, 1, nq, 2 * HEAD_DIM, TQ), lambda bi, p: (bi, p, 0, 0, 0)),
        out_shape=jax.ShapeDtypeStruct((b, h // 2, nq, 2 * HEAD_DIM, TQ), F32),
        scratch_shapes=[pltpu.VMEM((3, 2, TQ, TQ), F32), pltpu.VMEM((3, 2, 1, TQ), F32),
                        pltpu.VMEM((2, 1, TQ), F32), pltpu.VMEM((2, VT_ROWS, TQ), F32)],
        compiler_params=pltpu.CompilerParams(
            dimension_semantics=("arbitrary", "arbitrary"), vmem_limit_bytes=VMEM_LIMIT),
        name="attention",
    )(q_aug, k_aug, vt_aug)


def _gelu_tanh(x):
    return 0.5 * x * (1.0 + jnp.tanh(0.7978845608028654 * (x + 0.044715 * (x * x * x))))


def _lru_kernel(xb_ref, gb_ref, cw_ref, cb_ref, wgate_ref, brg_ref, big_ref, lam_ref,
                o_ref, tail_ref, h_ref):
    j = pl.program_id(1)

    @pl.when(j == 0)
    def _():
        tail_ref[...] = jnp.zeros_like(tail_ref)
        h_ref[...] = jnp.zeros_like(h_ref)

    x = xb_ref[0]
    tm, dl = x.shape
    row_in_group = lax.broadcasted_iota(jnp.int32, (tm, dl), 0) % SUBLANES
    x_prev_group = jnp.concatenate([tail_ref[...], x[:tm - SUBLANES]], axis=0)
    xc = x * cw_ref[CONV_WIDTH - 1:CONV_WIDTH, :] + cb_ref[...]
    for k in range(1, CONV_WIDTH):
        xs = jnp.where(row_in_group < k, _rot_in_group(x_prev_group, k), _rot_in_group(x, k))
        xc = xc + xs * cw_ref[CONV_WIDTH - 1 - k:CONV_WIDTH - k, :]
    tail_ref[...] = x[tm - SUBLANES:, :]

    xcb = xc.astype(BF16)
    n_pairs = dl // LANES
    r_parts, i_parts = [], []
    for p in range(n_pairs):
        g = _dot(xcb[:, p * LANES:(p + 1) * LANES], wgate_ref[p])
        r_parts.append(g[:, :LANES])
        i_parts.append(g[:, LANES:])
    r = _sigmoid(jnp.concatenate(r_parts, axis=1) + brg_ref[...])
    ig = _sigmoid(jnp.concatenate(i_parts, axis=1) + big_ref[...])

    lam = lam_ref[...]
    softplus_neg_lam = jnp.maximum(-lam, 0.0) + jnp.log1p(jnp.exp(-jnp.abs(lam)))
    log_a = (-LRU_C) * r * softplus_neg_lam
    a = jnp.exp(log_a)
    v = 1.0 - a * a
    u = jnp.where(v > 0.0, v * lax.rsqrt(v), 0.0) * (ig * xc)

    k = 1
    while k < SUBLANES:
        keep = row_in_group >= k
        u = a * jnp.where(keep, _rot_in_group(u, k), 0.0) + u
        a = a * jnp.where(keep, _rot_in_group(a, k), 1.0)
        k *= 2
    h_prev = h_ref[...]
    groups = []
    for g in range(tm // SUBLANES):
        rows = slice(g * SUBLANES, (g + 1) * SUBLANES)
        hg = a[rows] * h_prev + u[rows]
        groups.append(hg)
        h_prev = hg[SUBLANES - 1:SUBLANES]
    h_ref[...] = h_prev
    o_ref[0] = jnp.concatenate(groups, axis=0) * _gelu_tanh(gb_ref[0])


def _lru_call(xb, gb, conv_w, conv_b, wgate, b_rg, b_ig, lam):
    b, s, dl = xb.shape
    row_spec = pl.BlockSpec((1, TM, dl), lambda bi, j: (bi, j, 0))
    full = lambda shape: pl.BlockSpec(shape, lambda bi, j: (0,) * len(shape))
    return pl.pallas_call(
        _lru_kernel,
        grid=(b, s // TM),
        in_specs=[row_spec, row_spec, full(conv_w.shape), full((1, dl)), full(wgate.shape),
                  full((1, dl)), full((1, dl)), full((1, dl))],
        out_specs=row_spec,
        out_shape=jax.ShapeDtypeStruct((b, s, dl), F32),
        scratch_shapes=[pltpu.VMEM((SUBLANES, dl), F32), pltpu.VMEM((1, dl), F32)],
        compiler_params=pltpu.CompilerParams(
            dimension_semantics=("arbitrary", "arbitrary"), vmem_limit_bytes=VMEM_LIMIT),
        name="lru",
    )(xb, gb, conv_w, conv_b, wgate, b_rg, b_ig, lam)


def _rms(x, gain):
    return x * lax.rsqrt(jnp.mean(x * x, axis=-1, keepdims=True) + RMS_EPS) * gain


def _mixout_kernel(attn_ref, lru_ref, x_ref, mod_ref, ga_ref, gl_ref, woa_ref, wol_ref,
                   g1_ref, b1_ref, wr_ref, br_ref, x1_ref, u2_ref, rinfo_ref, cnt_ref, wo_bf):
    tm, d = x_ref.shape

    @pl.when(pl.program_id(0) == 0)
    def _():
        wo_bf[0] = woa_ref[0].astype(BF16)
        wo_bf[1] = wol_ref[0].astype(BF16)

    attn = attn_ref[0, :, 0].reshape(ga_ref.shape[1], tm).T
    na = _rms(attn, ga_ref[...]).astype(BF16)
    nl = _rms(lru_ref[...], gl_ref[...]).astype(BF16)
    mix = _dot(na, wo_bf[0]) + _dot(nl, wo_bf[1])
    z = DEEPNORM_ALPHA * x_ref[...] + (1.0 + mod_ref[0, 2:3, :]) * mix
    x1 = _layer_norm(z) * g1_ref[...] + b1_ref[...]
    x1_ref[...] = x1
    u2 = _layer_norm(x1) * (1.0 + mod_ref[0, 4:5, :]) + mod_ref[0, 3:4, :]

    nt = (((1,), (1,)), ((), ()))
    uh, ul = _split2(u2)
    wh, wl = _split2(wr_ref[...])
    dg = lambda a, b_: lax.dot_general(a, b_, nt, preferred_element_type=F32)
    lg = dg(wh, uh) + (dg(wh, ul) + dg(wl, uh)) + br_ref[:, 0:1]

    def first_index(vals, target):
        idx = jnp.full_like(target, float(len(vals) - 1))
        for n in range(len(vals) - 2, -1, -1):
            idx = jnp.where(vals[n] == target, float(n), idx)
        return idx

    g = [lg[n:n + 1, :] for n in range(N_GROUPS)]
    gmax = functools.reduce(jnp.maximum, g)
    gsum = functools.reduce(lambda a, b_: a + b_, [jnp.exp(v - gmax) for v in g])
    grp_w = 1.0 / gsum
    gidx = first_index(g, gmax)

    sel = []
    for e in range(EXPERTS_PER_GROUP):
        v = lg[N_GROUPS + (N_GROUPS - 1) * EXPERTS_PER_GROUP + e:
               N_GROUPS + (N_GROUPS - 1) * EXPERTS_PER_GROUP + e + 1, :]
        for gi in range(N_GROUPS - 2, -1, -1):
            r0 = N_GROUPS + gi * EXPERTS_PER_GROUP + e
            v = jnp.where(gidx == float(gi), lg[r0:r0 + 1, :], v)
        sel.append(v)
    smax = functools.reduce(jnp.maximum, sel)
    i1 = first_index(sel, smax)
    rest = [jnp.where(i1 == float(e), -3e38, sel[e]) for e in range(EXPERTS_PER_GROUP)]
    rmax = functools.reduce(jnp.maximum, rest)
    i2 = first_index(rest, rmax)
    e2 = jnp.exp(rmax - smax)
    w1 = grp_w / (1.0 + e2)
    w2 = grp_w * e2 / (1.0 + e2)
    ia = jnp.minimum(i1, i2)
    ib = jnp.maximum(i1, i2)
    wa = jnp.where(i1 < i2, w1, w2)
    wb = jnp.where(i1 < i2, w2, w1)
    pair = jnp.where(ia == 0.0, ib - 1.0, jnp.where(ia == 1.0, ib + 1.0, 5.0))
    bucket = gidx * float(N_PAIRS) + pair

    @pl.when(pl.program_id(0) == 0)
    def _():
        cnt_ref[...] = jnp.zeros_like(cnt_ref)

    cid = lax.broadcasted_iota(jnp.int32, (BUCKET_ROWS, tm), 0).astype(F32)
    cnt_ref[...] += jnp.sum(jnp.where(cid == bucket, 1.0, 0.0), axis=1, keepdims=True)

    zrow = jnp.zeros_like(wa)
    rinfo_ref[...] = jnp.concatenate([bucket, wa, wb] + [zrow] * (SUBLANES - 3), axis=0)
    wt = jnp.concatenate([wa, wb, jnp.zeros((LANES - 2, tm), F32)], axis=0)
    u2_ref[:, :d] = u2
    u2_ref[:, d:] = wt.T


def _mixout_call(attn_t, lru, x2d, mod3, ga, gl, wo, g1, b1, wr, br, seq):
    t, d = x2d.shape
    _, n_pairs, _, pair_w, _ = attn_t.shape
    dh = n_pairs * pair_w
    assert wo.shape[1] == 2 * dh and lru.shape[1] == dh
    per_b = seq // TM
    full = lambda shape: pl.BlockSpec(shape, lambda i: (0,) * len(shape))
    return pl.pallas_call(
        _mixout_kernel,
        grid=(t // TM,),
        in_specs=[pl.BlockSpec((1, n_pairs, 1, pair_w, TM),
                               lambda i: (i // per_b, 0, i % per_b, 0, 0)),
                  pl.BlockSpec((TM, dh), lambda i: (i, 0)),
                  pl.BlockSpec((TM, d), lambda i: (i, 0)),
                  pl.BlockSpec((1, 6, d), lambda i: (i // per_b, 0, 0)),
                  full((1, dh)), full((1, dh)),
                  pl.BlockSpec((1, dh, d), lambda i: (0, 0, 0)),
                  pl.BlockSpec((1, dh, d), lambda i: (0, 1, 0)),
                  full((1, d)), full((1, d)), full((BUCKET_ROWS, d)), full((BUCKET_ROWS, LANES))],
        out_specs=[pl.BlockSpec((TM, d), lambda i: (i, 0)),
                   pl.BlockSpec((TM, d + LANES), lambda i: (i, 0)),
                   pl.BlockSpec((SUBLANES, TM), lambda i: (0, i)),
                   pl.BlockSpec((BUCKET_ROWS, LANES), lambda i: (0, 0))],
        out_shape=[jax.ShapeDtypeStruct((t, d), F32),
                   jax.ShapeDtypeStruct((t, d + LANES), F32),
                   jax.ShapeDtypeStruct((SUBLANES, t), F32),
                   jax.ShapeDtypeStruct((BUCKET_ROWS, LANES), F32)],
        scratch_shapes=[pltpu.VMEM((2, dh, d), BF16)],
        compiler_params=pltpu.CompilerParams(
            dimension_semantics=("arbitrary",), vmem_limit_bytes=VMEM_LIMIT),
        name="mixout",
    )(attn_t, lru, x2d, mod3, ga, gl, wo, wo, g1, b1, wr, br)


def _rank_kernel(rinfo_ref, counts_ref, dest_ref, carry_ref, offs_ref):
    tm = rinfo_ref.shape[1]

    @pl.when(pl.program_id(0) == 0)
    def _():
        carry_ref[...] = jnp.zeros_like(carry_ref)
        padded = jnp.floor((counts_ref[...] + float(TM_E - 1)) * (1.0 / TM_E)) * float(TM_E)
        inc = padded
        k = 1
        while k < BUCKET_ROWS:
            inc = inc + _shift_rows(inc, k, 0.0)
            k *= 2
        offs_ref[...] = inc - padded

    bucket = rinfo_ref[0:1, :]
    cid = lax.broadcasted_iota(jnp.int32, (BUCKET_ROWS, tm), 0).astype(F32)
    onehot = jnp.where(cid == bucket, 1.0, 0.0)
    srow = lax.broadcasted_iota(jnp.int32, (tm, tm), 0)
    scol = lax.broadcasted_iota(jnp.int32, (tm, tm), 1)
    upper = (srow <= scol).astype(BF16)
    prefix = _dot(onehot.astype(BF16), upper)
    carry = carry_ref[...]
    rank = jnp.sum(onehot * (prefix - 1.0 + carry[:, 0:1] + offs_ref[:, 0:1]),
                   axis=0, keepdims=True)
    dest_ref[...] = rank.astype(jnp.int32)
    carry_ref[...] = carry + prefix[:, tm - 1:tm]


def _rank_call(rinfo, counts):
    t = rinfo.shape[1]
    return pl.pallas_call(
        _rank_kernel,
        grid=(t // TM,),
        in_specs=[pl.BlockSpec((SUBLANES, TM), lambda i: (0, i)),
                  pl.BlockSpec((BUCKET_ROWS, LANES), lambda i: (0, 0))],
        out_specs=pl.BlockSpec((1, TM), lambda i: (0, i)),
        out_shape=jax.ShapeDtypeStruct((1, t), jnp.int32),
        scratch_shapes=[pltpu.VMEM((BUCKET_ROWS, LANES), F32),
                        pltpu.VMEM((BUCKET_ROWS, LANES), F32)],
        compiler_params=pltpu.CompilerParams(
            dimension_semantics=("arbitrary",), vmem_limit_bytes=VMEM_LIMIT),
        name="rank",
    )(rinfo, counts)


def _dispatch_kernel(dest_ref, ends_ref, u2_ref, wg_ref, wu_ref, wd_ref,
                     xs_ref, wg_bf_ref, wu_bf_ref, wd_bf_ref, zbuf, sem, zsem):
    tm = u2_ref.shape[0]
    t0 = pl.program_id(0) * tm

    @pl.when(pl.program_id(0) == 0)
    def _():
        zbuf[...] = jnp.zeros_like(zbuf)

        def tail_copy(bkt):
            end = ends_ref[bkt]
            start = ends_ref[bkt - 1] if bkt else 0
            tail = pl.multiple_of(jnp.maximum(end - TM_E, 0), TM_E)
            return end > start, pltpu.make_async_copy(zbuf, xs_ref.at[pl.ds(tail, TM_E)], zsem)

        for bkt in range(N_BUCKETS):
            nonempty, copy = tail_copy(bkt)
            pl.when(nonempty)(copy.start)
        for bkt in range(N_BUCKETS):
            nonempty, copy = tail_copy(bkt)
            pl.when(nonempty)(copy.wait)

        def unused_tile_copy(k):
            return pltpu.make_async_copy(
                zbuf, xs_ref.at[pl.ds(pl.multiple_of(k * TM_E, TM_E), TM_E)], zsem)

        first_unused = ends_ref[N_BUCKETS - 1] // TM_E
        n_tiles = xs_ref.shape[0] // TM_E
        lax.fori_loop(first_unused, n_tiles, lambda k, c: (unused_tile_copy(k).start(), c)[1], 0)
        lax.fori_loop(first_unused, n_tiles, lambda k, c: (unused_tile_copy(k).wait(), c)[1], 0)

    for r in range(tm):
        pltpu.make_async_copy(u2_ref.at[pl.ds(r, 1)],
                              xs_ref.at[pl.ds(dest_ref[t0 + r], 1)], sem).start()
    wg_bf_ref[0] = wg_ref[0, 0].astype(BF16)
    wu_bf_ref[0] = wu_ref[0, 0].astype(BF16)
    wd_bf_ref[0] = wd_ref[0, 0].astype(BF16)
    pltpu.make_async_copy(u2_ref, xs_ref.at[pl.ds(0, tm)], sem).wait()


def _dispatch_call(dest, bucket_ends, u2ext, n_rows, w_gate, w_up, w_down):
    t, w = u2ext.shape
    _, n_exp, d, de = w_gate.shape
    tm = t // n_exp
    assert t % n_exp == 0 and tm % ISSUE_UNROLL == 0
    w_in = lambda shape: pl.BlockSpec((1, 1) + shape, lambda i, dr, er: (0, i, 0, 0))
    w_out = lambda shape: pl.BlockSpec((1,) + shape, lambda i, dr, er: (i, 0, 0))
    grid_spec = pltpu.PrefetchScalarGridSpec(
        num_scalar_prefetch=2,
        grid=(n_exp,),
        in_specs=[pl.BlockSpec((tm, w), lambda i, dr, er: (i, 0)),
                  w_in((d, de)), w_in((d, de)), w_in((de, d))],
        out_specs=[pl.BlockSpec(memory_space=pl.ANY),
                   w_out((d, de)), w_out((d, de)), w_out((de, d))],
        scratch_shapes=[pltpu.VMEM((TM_E, w), F32), pltpu.SemaphoreType.DMA(()),
                        pltpu.SemaphoreType.DMA(())],
    )
    return pl.pallas_call(
        _dispatch_kernel,
        grid_spec=grid_spec,
        out_shape=[jax.ShapeDtypeStruct((n_rows, w), F32),
                   jax.ShapeDtypeStruct((n_exp, d, de), BF16),
                   jax.ShapeDtypeStruct((n_exp, d, de), BF16),
                   jax.ShapeDtypeStruct((n_exp, de, d), BF16)],
        compiler_params=pltpu.CompilerParams(
            dimension_semantics=("arbitrary",), vmem_limit_bytes=VMEM_LIMIT),
        name="dispatch",
    )(dest, bucket_ends, u2ext, w_gate, w_up, w_down)


def _experts_kernel(ea_ref, eb_ref, nv_ref, xs_ref, wga_ref, wua_ref, wda_ref,
                    wgb_ref, wub_ref, wdb_ref, ys_ref):
    del ea_ref, eb_ref
    i = pl.program_id(0)
    d = ys_ref.shape[1]

    @pl.when(i < nv_ref[0])
    def _():
        x = xs_ref[:, :d].astype(BF16)

        def expert(wg_ref, wu_ref, wd_ref):
            g = _dot(x, wg_ref[0])
            h = (g * _sigmoid(g)) * _dot(x, wu_ref[0])
            return _dot(h.astype(BF16), wd_ref[0])

        ya = xs_ref[:, d:d + 1] * expert(wga_ref, wua_ref, wda_ref)
        ys_ref[...] = ya + xs_ref[:, d + 1:d + 2] * expert(wgb_ref, wub_ref, wdb_ref)

    @pl.when(i >= nv_ref[0])
    def _():
        ys_ref[...] = jnp.zeros_like(ys_ref)


def _experts_call(tile_ea, tile_eb, n_valid, xs, wg, wu, wd):
    tp, w = xs.shape
    _, d, de = wg.shape
    row = lambda i, ea, eb, nv: (jnp.minimum(i, nv[0] - 1), 0)
    wa = lambda i, ea, eb, nv: (ea[i], 0, 0)
    wb = lambda i, ea, eb, nv: (eb[i], 0, 0)
    grid_spec = pltpu.PrefetchScalarGridSpec(
        num_scalar_prefetch=3,
        grid=(tp // TM_E,),
        in_specs=[pl.BlockSpec((TM_E, w), row),
                  pl.BlockSpec((1, d, de), wa), pl.BlockSpec((1, d, de), wa),
                  pl.BlockSpec((1, de, d), wa),
                  pl.BlockSpec((1, d, de), wb), pl.BlockSpec((1, d, de), wb),
                  pl.BlockSpec((1, de, d), wb)],
        out_specs=pl.BlockSpec((TM_E, d), lambda i, ea, eb, nv: (i, 0)),
    )
    return pl.pallas_call(
        _experts_kernel,
        grid_spec=grid_spec,
        out_shape=jax.ShapeDtypeStruct((tp, d), F32),
        compiler_params=pltpu.CompilerParams(
            dimension_semantics=("arbitrary",), vmem_limit_bytes=VMEM_LIMIT),
        name="experts",
    )(tile_ea, tile_eb, n_valid, xs, wg, wu, wd, wg, wu, wd)


def _final_kernel(dest_ref, x1_ref, mod_ref, g2_ref, b2_ref, ys_ref, o_ref, ybuf, sem):
    tm = x1_ref.shape[0]
    i = pl.program_id(0)
    slot = i % GATHER_AHEAD

    def issue_rows(tile, to_slot, r0, n):
        for r in range(n):
            pltpu.make_async_copy(ys_ref.at[pl.ds(dest_ref[tile * tm + r0 + r], 1)],
                                  ybuf.at[to_slot, pl.ds(r0 + r, 1)], sem.at[to_slot]).start()

    def normalise_rows(r0, n):
        rows = pl.ds(r0, n)
        z = DEEPNORM_ALPHA * x1_ref[rows, :] + (1.0 + mod_ref[0, 5:6, :]) * ybuf[slot, rows, :]
        o_ref[rows, :] = _layer_norm(z) * g2_ref[...] + b2_ref[...]

    n_chunks = tm // ISSUE_UNROLL

    @pl.when(i == 0)
    def _():
        for tile in range(GATHER_AHEAD):
            issue_rows(tile, tile, 0, tm)

    pltpu.make_async_copy(ys_ref.at[pl.ds(0, tm)], ybuf.at[slot], sem.at[slot]).wait()

    def chunk(c, issue_ahead):
        r0 = c * ISSUE_UNROLL
        normalise_rows(r0, ISSUE_UNROLL)
        if issue_ahead:
            issue_rows(i + GATHER_AHEAD, slot, r0, ISSUE_UNROLL)

    more = i + GATHER_AHEAD < pl.num_programs(0)

    @pl.when(more)
    def _():
        for c in range(n_chunks):
            chunk(c, True)

    @pl.when(jnp.logical_not(more))
    def _():
        for c in range(n_chunks):
            chunk(c, False)


def _final_call(dest, x1, mod3, g2, b2, ys, seq):
    t, d = x1.shape
    per_b = seq // TM
    grid_spec = pltpu.PrefetchScalarGridSpec(
        num_scalar_prefetch=1,
        grid=(t // TM,),
        in_specs=[pl.BlockSpec((TM, d), lambda i, dr: (i, 0)),
                  pl.BlockSpec((1, 6, d), lambda i, dr: (i // per_b, 0, 0)),
                  pl.BlockSpec((1, d), lambda i, dr: (0, 0)),
                  pl.BlockSpec((1, d), lambda i, dr: (0, 0)),
                  pl.BlockSpec(memory_space=pl.ANY)],
        out_specs=pl.BlockSpec((TM, d), lambda i, dr: (i, 0)),
        scratch_shapes=[pltpu.VMEM((GATHER_AHEAD, TM, d), F32),
                        pltpu.SemaphoreType.DMA((GATHER_AHEAD,))],
    )
    return pl.pallas_call(
        _final_kernel,
        grid_spec=grid_spec,
        out_shape=jax.ShapeDtypeStruct((t, d), F32),
        compiler_params=pltpu.CompilerParams(
            dimension_semantics=("arbitrary",), vmem_limit_bytes=VMEM_LIMIT),
        name="final",
    )(dest, x1, mod3, g2, b2, ys)


def _gate_pairs(w_rg, w_ig):
    def pairs(w):
        n, bs, _ = w.shape
        w = w.reshape(n // 2, 2, bs, bs)
        z = jnp.zeros((n // 2, bs, bs), w.dtype)
        top = jnp.concatenate([w[:, 0], z], axis=2)
        bot = jnp.concatenate([z, w[:, 1]], axis=2)
        return jnp.concatenate([top, bot], axis=1)
    return jnp.concatenate([pairs(w_rg), pairs(w_ig)], axis=2).astype(BF16)


def kernel(x, c, w_ada, b_ada, w_in, b_f, conv_w, conv_b, w_rg, b_rg, w_ig, b_ig, lru_lambda,
           g_attn, g_lru, w_out, ln1_g, ln1_b, w_grp, b_grp, w_exp, b_exp,
           w_e_gate, w_e_up, w_e_down, ln2_g, ln2_b):
    assert w_ada.shape[0] == DEPTH
    b, s, d = x.shape
    t = b * s
    d_attn = N_HEADS * HEAD_DIM
    d_lru = conv_w.shape[2]
    n_exp = w_e_gate.shape[1]
    assert s % TM == 0 and s % TQ == 0 and t % TM_E == 0

    c_pad = jnp.pad(c, ((0, SUBLANES - b), (0, 0)))
    mod = _mod_call(c_pad, w_ada, b_ada[0][None, :])
    mod3 = mod[:b].reshape(b, 6, d)

    bf_pad = jnp.pad(b_f[0], (0, LANES - N_HEADS))[None, :]
    q_aug, k_aug, vt_aug, xb, gb = _inproj_call(
        x, mod3, jnp.swapaxes(w_in, 1, 2), bf_pad, d_lru)
    attn_t = _attn_call(q_aug, k_aug, vt_aug)
    lru = _lru_call(xb, gb, conv_w[0], conv_b[0][None, :], _gate_pairs(w_rg[0], w_ig[0]),
                    b_rg[0][None, :], b_ig[0][None, :], lru_lambda[0][None, :])

    n_route = N_GROUPS + n_exp
    wr = jnp.pad(jnp.concatenate([w_grp[0], w_exp[0]], axis=1).T,
                 ((0, BUCKET_ROWS - n_route), (0, 0)))
    br = jnp.pad(jnp.concatenate([b_grp[0], b_exp[0]]), (0, BUCKET_ROWS - n_route))
    br = jnp.broadcast_to(br[:, None], (BUCKET_ROWS, LANES))
    x1, u2ext, rinfo, counts = _mixout_call(
        attn_t, lru.reshape(t, d_lru), x.reshape(t, d), mod3,
        g_attn[0][None, :], g_lru[0][None, :], w_out,
        ln1_g[0][None, :], ln1_b[0][None, :], wr, br, s)

    dest = _rank_call(rinfo, counts).reshape(t)
    cnt = counts[:N_BUCKETS, 0].astype(jnp.int32)
    ends = jnp.cumsum((cnt + (TM_E - 1)) // TM_E)
    n_tiles = t // TM_E + N_BUCKETS
    tile_bucket = jnp.sum(ends[None, :] <= jnp.arange(n_tiles)[:, None], axis=1)
    tile_bucket = jnp.minimum(tile_bucket, N_BUCKETS - 1)
    n_valid = ends[N_BUCKETS - 1:]
    last_bucket = tile_bucket[jnp.maximum(n_valid[0] - 1, 0)]
    tile_bucket = jnp.where(jnp.arange(n_tiles) < n_valid[0], tile_bucket, last_bucket)
    tile_ea = jnp.asarray(_BUCKET_EA, jnp.int32)[tile_bucket]
    tile_eb = jnp.asarray(_BUCKET_EB, jnp.int32)[tile_bucket]

    xs, wg_bf, wu_bf, wd_bf = _dispatch_call(
        dest, (ends * TM_E).astype(jnp.int32), u2ext, n_tiles * TM_E, w_e_gate, w_e_up, w_e_down)
    ys = _experts_call(tile_ea, tile_eb, n_valid.astype(jnp.int32), xs, wg_bf, wu_bf, wd_bf)
    out = _final_call(dest, x1, mod3, ln2_g[0][None, :], ln2_b[0][None, :], ys, s)
    return out.reshape(b, s, d)
```

```python
import functools

import jax
import jax.numpy as jnp
from jax import lax
from jax.experimental import pallas as pl
from jax.experimental.pallas import tpu as pltpu

F32 = jnp.float32
BF16 = jnp.bfloat16

HEAD_DIM = 64
N_HEADS = 8
N_LRU_BLOCKS = 8
CONV_WIDTH = 4
LRU_C = 8.0
N_GROUPS = 4
EXPERTS_PER_GROUP = 4
N_PAIRS = 6
N_BUCKETS = N_GROUPS * N_PAIRS
LN_EPS = 1e-5
RMS_EPS = 1e-6
NEG_INF = -1e30
DEPTH = 1
DEEPNORM_ALPHA = (2.0 * DEPTH) ** 0.25
LOG2E = 1.4426950408889634

LANES = 128
SUBLANES = 8
HEAD_PAD = LANES
VT_ROWS = 80
BUCKET_ROWS = 32
TM = 512
TQ = 512
TM_E = 256
ISSUE_UNROLL = 64
GATHER_AHEAD = 2
RANK_CHUNKS = 4
VMEM_LIMIT = 56 * 1024 * 1024

_PAIRS = [(0, 1), (0, 2), (0, 3), (1, 2), (1, 3), (2, 3)]
_BUCKET_EA = [g * EXPERTS_PER_GROUP + a for g in range(N_GROUPS) for (a, b) in _PAIRS]
_BUCKET_EB = [g * EXPERTS_PER_GROUP + b for g in range(N_GROUPS) for (a, b) in _PAIRS]


def _dot(a, b):
    return jnp.dot(a, b, preferred_element_type=F32)


def _split2(a):
    hi = a.astype(BF16)
    lo = (a - hi.astype(F32)).astype(BF16)
    return hi, lo


def _split3(a):
    hi = a.astype(BF16)
    r = a - hi.astype(F32)
    mid = r.astype(BF16)
    lo = (r - mid.astype(F32)).astype(BF16)
    return hi, mid, lo


def _layer_norm(x):
    mu = jnp.mean(x, axis=-1, keepdims=True)
    xc = x - mu
    var = jnp.mean(xc * xc, axis=-1, keepdims=True)
    return xc * lax.rsqrt(var + LN_EPS)


def _sigmoid(x):
    return 0.5 * jnp.tanh(0.5 * x) + 0.5


def _rot_in_group(x, k):
    n, w = x.shape
    return pltpu.roll(x.reshape(n // SUBLANES, SUBLANES, w), k, 1).reshape(n, w)


def _log_sigmoid(z):
    return jnp.minimum(z, 0.0) - jnp.log1p(jnp.exp(-jnp.abs(z)))


def _cumsum_rows(x, carry):
    n, w = x.shape
    row_in_group = lax.broadcasted_iota(jnp.int32, (n, w), 0) % SUBLANES
    k = 1
    while k < SUBLANES:
        x = x + jnp.where(row_in_group >= k, _rot_in_group(x, k), 0.0)
        k *= 2
    groups = []
    for g in range(n // SUBLANES):
        blk = x[g * SUBLANES:(g + 1) * SUBLANES] + carry
        groups.append(blk)
        carry = blk[SUBLANES - 1:SUBLANES]
    return jnp.concatenate(groups, axis=0)


def _shift_rows(x, k, fill):
    n = x.shape[0]
    if k % SUBLANES == 0:
        return jnp.concatenate([jnp.full((k, x.shape[1]), fill, x.dtype), x[:n - k]], axis=0)
    row = lax.broadcasted_iota(jnp.int32, x.shape, 0)
    return jnp.where(row >= k, pltpu.roll(x, k, 0), fill)


def _mod_kernel(c_ref, w_ref, b_ref, o_ref):
    c = c_ref[...]
    s = c * _sigmoid(c)
    sh, sl = _split2(s)
    wh, wl = _split2(w_ref[0])
    o_ref[...] = _dot(sh, wh) + (_dot(sh, wl) + _dot(sl, wh)) + b_ref[...]


def _mod_call(c_pad, w_ada, b_ada):
    rows, d = c_pad.shape
    n = w_ada.shape[2]
    return pl.pallas_call(
        _mod_kernel,
        grid=(n // d,),
        in_specs=[pl.BlockSpec((rows, d), lambda j: (0, 0)),
                  pl.BlockSpec((1, d, d), lambda j: (0, 0, j)),
                  pl.BlockSpec((1, d), lambda j: (0, j))],
        out_specs=pl.BlockSpec((rows, d), lambda j: (0, j)),
        out_shape=jax.ShapeDtypeStruct((rows, n), F32),
        compiler_params=pltpu.CompilerParams(vmem_limit_bytes=VMEM_LIMIT),
        name="mod",
    )(c_pad, w_ada, b_ada)


def _inproj_kernel(x_ref, mod_ref, wt_ref, bf_ref,
                   q_ref, k_ref, vt_ref, xb_ref, gb_ref, carry_ref, v_scr, w_scr, stage, wsem):
    j = pl.program_id(1)
    d_attn = N_HEADS * HEAD_DIM
    d_lru = xb_ref.shape[2]
    c_k, c_v, c_x = d_attn, 2 * d_attn, 3 * d_attn
    c_g, c_f = c_x + d_lru, c_x + 2 * d_lru
    r_f = 3 * d_attn
    r_x, r_g = r_f + N_HEADS, r_f + N_HEADS + d_lru

    @pl.when(jnp.logical_and(pl.program_id(0) == 0, j == 0))
    def _():
        blocks = [(0, d_attn, 0, HEAD_DIM ** -0.5 * LOG2E, None), (d_attn, d_attn, c_k, None, None),
                  (2 * d_attn, d_attn, c_v, None, None), (r_x, d_lru, c_x, None, None),
                  (r_g, d_lru, c_g, None, None), (r_f, LANES, c_f, None, N_HEADS)]
        copies = [pltpu.make_async_copy(wt_ref.at[0, pl.ds(r0, n), :],
                                        stage.at[k, pl.ds(0, n), :], wsem.at[k])
                  for k, (r0, n, _, _, _) in enumerate(blocks)]
        for copy in copies:
            copy.start()
        for k, (_, n, c0, scale, keep_lanes) in enumerate(blocks):
            copies[k].wait()
            blk = stage[k, 0:n, :]
            if scale is not None:
                blk = blk * scale
            blk = blk.T
            if keep_lanes is not None:
                lane = lax.broadcasted_iota(jnp.int32, blk.shape, 1)
                blk = jnp.where(lane < keep_lanes, blk, 0.0)
            w_scr[:, c0:c0 + n] = blk.astype(BF16)

    @pl.when(j == 0)
    def _():
        carry_ref[...] = jnp.zeros_like(carry_ref)

    tm = x_ref.shape[1]
    u = _layer_norm(x_ref[0]) * (1.0 + mod_ref[0, 1:2, :]) + mod_ref[0, 0:1, :]
    ub = u.astype(BF16)

    logf = _log_sigmoid(_dot(ub, w_scr[:, c_f:c_f + LANES]) + bf_ref[...])
    cum = _cumsum_rows(logf, carry_ref[...])
    carry_ref[...] = cum[tm - 1:tm, :]
    c_hi, c_mid, c_lo = [p.astype(F32) for p in _split3(cum * LOG2E)]

    lane = lax.broadcasted_iota(jnp.int32, (tm, HEAD_PAD), 1)
    d = HEAD_DIM
    q_all = _dot(ub, w_scr[:, 0:c_k])
    k_all = _dot(ub, w_scr[:, c_k:c_v])
    for h in range(N_HEADS):
        ts = slice((h // 2) * LANES, (h // 2 + 1) * LANES)
        qh, kh = q_all[:, ts], k_all[:, ts]
        if h % 2:
            qh, kh = pltpu.roll(qh, d, 1), pltpu.roll(kh, d, 1)
        hi, mid, lo = c_hi[:, h:h + 1], c_mid[:, h:h + 1], c_lo[:, h:h + 1]
        q_ext = jnp.where(lane == d, hi, jnp.where(lane == d + 1, mid, jnp.where(
            lane == d + 2, lo, jnp.where(lane < d + 6, 1.0, 0.0))))
        k_ext = jnp.where(lane < d + 3, 1.0, jnp.where(lane == d + 3, -hi, jnp.where(
            lane == d + 4, -mid, jnp.where(lane == d + 5, -lo, 0.0))))
        q_ref[0, h] = jnp.where(lane < d, qh, q_ext).astype(BF16)
        k_ref[0, h] = jnp.where(lane < d, kh, k_ext).astype(BF16)

    v_scr[...] = _dot(ub, w_scr[:, c_v:c_x])
    vt_all = v_scr[...].T
    pad_rows = lax.broadcasted_iota(jnp.int32, (VT_ROWS - d, tm), 0)
    ones_row = jnp.where(pad_rows == 0, 1.0, 0.0).astype(BF16)
    for h in range(N_HEADS):
        vt_ref[0, h, 0, :d, :] = vt_all[h * d:(h + 1) * d, :].astype(BF16)
        vt_ref[0, h, 0, d:, :] = ones_row

    xb_ref[0] = _dot(ub, w_scr[:, c_x:c_g])
    gb_ref[0] = _dot(ub, w_scr[:, c_g:c_f])


def _inproj_call(x, mod3, w_in_t, bf_pad, d_lru):
    b, s, d = x.shape
    d_attn = N_HEADS * HEAD_DIM
    assert w_in_t.shape[1] == 3 * d_attn + N_HEADS + 2 * d_lru
    full = lambda shape: pl.BlockSpec(shape, lambda bi, j: (0,) * len(shape))
    head_spec = pl.BlockSpec((1, N_HEADS, TM, HEAD_PAD), lambda bi, j: (bi, 0, j, 0))
    vt_spec = pl.BlockSpec((1, N_HEADS, 1, VT_ROWS, TM), lambda bi, j: (bi, 0, j, 0, 0))
    row_spec = pl.BlockSpec((1, TM, d_lru), lambda bi, j: (bi, j, 0))
    head_shape = jax.ShapeDtypeStruct((b, N_HEADS, s, HEAD_PAD), BF16)
    return pl.pallas_call(
        _inproj_kernel,
        grid=(b, s // TM),
        in_specs=[pl.BlockSpec((1, TM, d), lambda bi, j: (bi, j, 0)),
                  pl.BlockSpec((1, 6, d), lambda bi, j: (bi, 0, 0)),
                  pl.BlockSpec(memory_space=pl.ANY), full((1, LANES))],
        out_specs=[head_spec, head_spec, vt_spec, row_spec, row_spec],
        out_shape=[head_shape, head_shape,
                   jax.ShapeDtypeStruct((b, N_HEADS, s // TM, VT_ROWS, TM), BF16),
                   jax.ShapeDtypeStruct((b, s, d_lru), F32),
                   jax.ShapeDtypeStruct((b, s, d_lru), F32)],
        scratch_shapes=[pltpu.VMEM((1, LANES), F32), pltpu.VMEM((TM, d_attn), F32),
                        pltpu.VMEM((d, 3 * d_attn + 2 * d_lru + LANES), BF16),
                        pltpu.VMEM((6, max(d_attn, d_lru), d), F32),
                        pltpu.SemaphoreType.DMA((6,))],
        compiler_params=pltpu.CompilerParams(
            dimension_semantics=("arbitrary", "arbitrary"), vmem_limit_bytes=VMEM_LIMIT),
        name="inproj",
    )(x, mod3, w_in_t, bf_pad)


def _attn_kernel(q_ref, k_ref, vt_ref, o_ref, s_scr, smax_scr, m_scr, acc_scr):
    tq = TQ
    nq = q_ref.shape[2] // tq
    heads = range(q_ref.shape[1])

    def scores_to(slot, i, j):
        for hh in heads:
            k = k_ref[0, hh, pl.ds(pl.multiple_of(j * tq, tq), tq), :]
            q = q_ref[0, hh, pl.ds(pl.multiple_of(i * tq, tq), tq), :]
            s = lax.dot_general(k, q, (((1,), (1,)), ((), ())), preferred_element_type=F32)
            s_scr[slot, hh] = s
            smax_scr[slot, hh] = jnp.max(s, axis=0, keepdims=True)

    def consume(slot, j, masked):
        for hh in heads:
            s = s_scr[slot, hh]
            if masked:
                key = lax.broadcasted_iota(jnp.int32, (tq, tq), 0)
                qry = lax.broadcasted_iota(jnp.int32, (tq, tq), 1)
                s = jnp.where(key <= qry, s, NEG_INF)
                s_max = jnp.max(s, axis=0, keepdims=True)
            else:
                s_max = smax_scr[slot, hh]
            m = m_scr[hh]
            m_new = jnp.maximum(m, s_max)
            p = jnp.exp2(s - m_new).astype(BF16)
            acc_scr[hh] = jnp.exp2(m - m_new) * acc_scr[hh] + _dot(vt_ref[0, hh, j], p)
            m_scr[hh] = m_new

    scores_to(2, 0, 0)

    def query_block(i, carry):
        m_scr[...] = jnp.full(m_scr.shape, NEG_INF, F32)
        acc_scr[...] = jnp.zeros(acc_scr.shape, F32)
        nxt = jnp.minimum(i + 1, nq - 1)
        n_mid = i - 1

        @pl.when(i == 0)
        def _():
            consume(2, 0, True)
            scores_to(2, nxt, 0)

        @pl.when(i >= 1)
        def _():
            scores_to(1, i, 1)
            consume(2, 0, False)

        def pair(kk, c):
            j = 1 + 2 * kk
            scores_to(0, i, j + 1)
            consume(1, j, False)
            scores_to(1, i, j + 2)
            consume(0, j + 1, False)
            return c

        lax.fori_loop(0, jnp.maximum(n_mid, 0) // 2, pair, 0)

        @pl.when(jnp.logical_and(i >= 1, n_mid % 2 == 1))
        def _():
            scores_to(0, i, i)
            consume(1, i - 1, False)
            scores_to(2, nxt, 0)
            consume(0, i, True)

        @pl.when(jnp.logical_and(i >= 1, n_mid % 2 == 0))
        def _():
            scores_to(2, nxt, 0)
            consume(1, i, True)

        for hh in heads:
            o_ref[0, 0, i, hh * HEAD_DIM:(hh + 1) * HEAD_DIM, :] = (
                acc_scr[hh, :HEAD_DIM, :] / acc_scr[hh, HEAD_DIM:HEAD_DIM + 1, :])
        return carry

    lax.fori_loop(0, nq, query_block, 0)


def _attn_call(q_aug, k_aug, vt_aug):
    b, h, s, hp = q_aug.shape
    assert TQ == TM
    nq = s // TQ
    return pl.pallas_call(
        _attn_kernel,
        grid=(b, h // 2),
        in_specs=[pl.BlockSpec((1, 2, s, hp), lambda bi, p: (bi, p, 0, 0)),
                  pl.BlockSpec((1, 2, s, hp), lambda bi, p: (bi, p, 0, 0)),
                  pl.BlockSpec((1, 2, nq, VT_ROWS, TQ), lambda bi, p: (bi, p, 0, 0, 0))],
        out_specs=pl.BlockSpec((1, 1, nq, 2 * HEAD_DIM, TQ), lambda bi, p: (bi, p, 0, 0, 0)),
        out_shape=jax.ShapeDtypeStruct((b, h // 2, nq, 2 * HEAD_DIM, TQ), F32),
        scratch_shapes=[pltpu.VMEM((3, 2, TQ, TQ), F32), pltpu.VMEM((3, 2, 1, TQ), F32),
                        pltpu.VMEM((2, 1, TQ), F32), pltpu.VMEM((2, VT_ROWS, TQ), F32)],
        compiler_params=pltpu.CompilerParams(
            dimension_semantics=("arbitrary", "arbitrary"), vmem_limit_bytes=VMEM_LIMIT),
        name="attention",
    )(q_aug, k_aug, vt_aug)


def _gelu_tanh(x):
    return 0.5 * x * (1.0 + jnp.tanh(0.7978845608028654 * (x + 0.044715 * (x * x * x))))


def _lru_kernel(xb_ref, gb_ref, cw_ref, cb_ref, wgate_ref, brg_ref, big_ref, lam_ref,
                o_ref, tail_ref, h_ref):
    j = pl.program_id(1)

    @pl.when(j == 0)
    def _():
        tail_ref[...] = jnp.zeros_like(tail_ref)
        h_ref[...] = jnp.zeros_like(h_ref)

    x = xb_ref[0]
    tm, dl = x.shape
    row_in_group = lax.broadcasted_iota(jnp.int32, (tm, dl), 0) % SUBLANES
    x_prev_group = jnp.concatenate([tail_ref[...], x[:tm - SUBLANES]], axis=0)
    xc = x * cw_ref[CONV_WIDTH - 1:CONV_WIDTH, :] + cb_ref[...]
    for k in range(1, CONV_WIDTH):
        xs = jnp.where(row_in_group < k, _rot_in_group(x_prev_group, k), _rot_in_group(x, k))
        xc = xc + xs * cw_ref[CONV_WIDTH - 1 - k:CONV_WIDTH - k, :]
    tail_ref[...] = x[tm - SUBLANES:, :]

    xcb = xc.astype(BF16)
    n_pairs = dl // LANES
    r_parts, i_parts = [], []
    for p in range(n_pairs):
        g = _dot(xcb[:, p * LANES:(p + 1) * LANES], wgate_ref[p])
        r_parts.append(g[:, :LANES])
        i_parts.append(g[:, LANES:])
    r = _sigmoid(jnp.concatenate(r_parts, axis=1) + brg_ref[...])
    ig = _sigmoid(jnp.concatenate(i_parts, axis=1) + big_ref[...])

    lam = lam_ref[...]
    softplus_neg_lam = jnp.maximum(-lam, 0.0) + jnp.log1p(jnp.exp(-jnp.abs(lam)))
    log_a = (-LRU_C) * r * softplus_neg_lam
    a = jnp.exp(log_a)
    v = 1.0 - a * a
    u = jnp.where(v > 0.0, v * lax.rsqrt(v), 0.0) * (ig * xc)

    k = 1
    while k < SUBLANES:
        keep = row_in_group >= k
        u = a * jnp.where(keep, _rot_in_group(u, k), 0.0) + u
        a = a * jnp.where(keep, _rot_in_group(a, k), 1.0)
        k *= 2
    h_prev = h_ref[...]
    groups = []
    for g in range(tm // SUBLANES):
        rows = slice(g * SUBLANES, (g + 1) * SUBLANES)
        hg = a[rows] * h_prev + u[rows]
        groups.append(hg)
        h_prev = hg[SUBLANES - 1:SUBLANES]
    h_ref[...] = h_prev
    o_ref[0] = jnp.concatenate(groups, axis=0) * _gelu_tanh(gb_ref[0])


def _lru_call(xb, gb, conv_w, conv_b, wgate, b_rg, b_ig, lam):
    b, s, dl = xb.shape
    row_spec = pl.BlockSpec((1, TM, dl), lambda bi, j: (bi, j, 0))
    full = lambda shape: pl.BlockSpec(shape, lambda bi, j: (0,) * len(shape))
    return pl.pallas_call(
        _lru_kernel,
        grid=(b, s // TM),
        in_specs=[row_spec, row_spec, full(conv_w.shape), full((1, dl)), full(wgate.shape),
                  full((1, dl)), full((1, dl)), full((1, dl))],
        out_specs=row_spec,
        out_shape=jax.ShapeDtypeStruct((b, s, dl), F32),
        scratch_shapes=[pltpu.VMEM((SUBLANES, dl), F32), pltpu.VMEM((1, dl), F32)],
        compiler_params=pltpu.CompilerParams(
            dimension_semantics=("arbitrary", "arbitrary"), vmem_limit_bytes=VMEM_LIMIT),
        name="lru",
    )(xb, gb, conv_w, conv_b, wgate, b_rg, b_ig, lam)


def _rms(x, gain):
    return x * lax.rsqrt(jnp.mean(x * x, axis=-1, keepdims=True) + RMS_EPS) * gain


def _mixout_kernel(attn_ref, lru_ref, x_ref, mod_ref, ga_ref, gl_ref, woa_ref, wol_ref,
                   g1_ref, b1_ref, wr_ref, br_ref, x1_ref, u2_ref, rinfo_ref, cnt_ref, wo_bf):
    tm, d = x_ref.shape

    @pl.when(pl.program_id(0) == 0)
    def _():
        wo_bf[0] = woa_ref[0].astype(BF16)
        wo_bf[1] = wol_ref[0].astype(BF16)

    attn = attn_ref[0, :, 0].reshape(ga_ref.shape[1], tm).T
    na = _rms(attn, ga_ref[...]).astype(BF16)
    nl = _rms(lru_ref[...], gl_ref[...]).astype(BF16)
    mix = _dot(na, wo_bf[0]) + _dot(nl, wo_bf[1])
    z = DEEPNORM_ALPHA * x_ref[...] + (1.0 + mod_ref[0, 2:3, :]) * mix
    x1 = _layer_norm(z) * g1_ref[...] + b1_ref[...]
    x1_ref[...] = x1
    u2 = _layer_norm(x1) * (1.0 + mod_ref[0, 4:5, :]) + mod_ref[0, 3:4, :]

    nt = (((1,), (1,)), ((), ()))
    uh, ul = _split2(u2)
    wh, wl = _split2(wr_ref[...])
    dg = lambda a, b_: lax.dot_general(a, b_, nt, preferred_element_type=F32)
    lg = dg(wh, uh) + (dg(wh, ul) + dg(wl, uh)) + br_ref[:, 0:1]

    def first_index(vals, target):
        idx = jnp.full_like(target, float(len(vals) - 1))
        for n in range(len(vals) - 2, -1, -1):
            idx = jnp.where(vals[n] == target, float(n), idx)
        return idx

    g = [lg[n:n + 1, :] for n in range(N_GROUPS)]
    gmax = functools.reduce(jnp.maximum, g)
    gsum = functools.reduce(lambda a, b_: a + b_, [jnp.exp(v - gmax) for v in g])
    grp_w = 1.0 / gsum
    gidx = first_index(g, gmax)

    sel = []
    for e in range(EXPERTS_PER_GROUP):
        v = lg[N_GROUPS + (N_GROUPS - 1) * EXPERTS_PER_GROUP + e:
               N_GROUPS + (N_GROUPS - 1) * EXPERTS_PER_GROUP + e + 1, :]
        for gi in range(N_GROUPS - 2, -1, -1):
            r0 = N_GROUPS + gi * EXPERTS_PER_GROUP + e
            v = jnp.where(gidx == float(gi), lg[r0:r0 + 1, :], v)
        sel.append(v)
    smax = functools.reduce(jnp.maximum, sel)
    i1 = first_index(sel, smax)
    rest = [jnp.where(i1 == float(e), -3e38, sel[e]) for e in range(EXPERTS_PER_GROUP)]
    rmax = functools.reduce(jnp.maximum, rest)
    i2 = first_index(rest, rmax)
    e2 = jnp.exp(rmax - smax)
    w1 = grp_w / (1.0 + e2)
    w2 = grp_w * e2 / (1.0 + e2)
    ia = jnp.minimum(i1, i2)
    ib = jnp.maximum(i1, i2)
    wa = jnp.where(i1 < i2, w1, w2)
    wb = jnp.where(i1 < i2, w2, w1)
    pair = jnp.where(ia == 0.0, ib - 1.0, jnp.where(ia == 1.0, ib + 1.0, 5.0))
    bucket = gidx * float(N_PAIRS) + pair

    @pl.when(pl.program_id(0) == 0)
    def _():
        cnt_ref[...] = jnp.zeros_like(cnt_ref)

    cid = lax.broadcasted_iota(jnp.int32, (BUCKET_ROWS, tm), 0).astype(F32)
    cnt_ref[...] += jnp.sum(jnp.where(cid == bucket, 1.0, 0.0), axis=1, keepdims=True)

    zrow = jnp.zeros_like(wa)
    rinfo_ref[...] = jnp.concatenate([bucket, wa, wb] + [zrow] * (SUBLANES - 3), axis=0)
    wt = jnp.concatenate([wa, wb, jnp.zeros((LANES - 2, tm), F32)], axis=0)
    u2_ref[:, :d] = u2
    u2_ref[:, d:] = wt.T


def _mixout_call(attn_t, lru, x2d, mod3, ga, gl, wo, g1, b1, wr, br, seq):
    t, d = x2d.shape
    _, n_pairs, _, pair_w, _ = attn_t.shape
    dh = n_pairs * pair_w
    assert wo.shape[1] == 2 * dh and lru.shape[1] == dh
    per_b = seq // TM
    full = lambda shape: pl.BlockSpec(shape, lambda i: (0,) * len(shape))
    return pl.pallas_call(
        _mixout_kernel,
        grid=(t // TM,),
        in_specs=[pl.BlockSpec((1, n_pairs, 1, pair_w, TM),
                               lambda i: (i // per_b, 0, i % per_b, 0, 0)),
                  pl.BlockSpec((TM, dh), lambda i: (i, 0)),
                  pl.BlockSpec((TM, d), lambda i: (i, 0)),
                  pl.BlockSpec((1, 6, d), lambda i: (i // per_b, 0, 0)),
                  full((1, dh)), full((1, dh)),
                  pl.BlockSpec((1, dh, d), lambda i: (0, 0, 0)),
                  pl.BlockSpec((1, dh, d), lambda i: (0, 1, 0)),
                  full((1, d)), full((1, d)), full((BUCKET_ROWS, d)), full((BUCKET_ROWS, LANES))],
        out_specs=[pl.BlockSpec((TM, d), lambda i: (i, 0)),
                   pl.BlockSpec((TM, d + LANES), lambda i: (i, 0)),
                   pl.BlockSpec((SUBLANES, TM), lambda i: (0, i)),
                   pl.BlockSpec((BUCKET_ROWS, LANES), lambda i: (0, 0))],
        out_shape=[jax.ShapeDtypeStruct((t, d), F32),
                   jax.ShapeDtypeStruct((t, d + LANES), F32),
                   jax.ShapeDtypeStruct((SUBLANES, t), F32),
                   jax.ShapeDtypeStruct((BUCKET_ROWS, LANES), F32)],
        scratch_shapes=[pltpu.VMEM((2, dh, d), BF16)],
        compiler_params=pltpu.CompilerParams(
            dimension_semantics=("arbitrary",), vmem_limit_bytes=VMEM_LIMIT),
        name="mixout",
    )(attn_t, lru, x2d, mod3, ga, gl, wo, wo, g1, b1, wr, br)


def _rank_kernel(rinfo_ref, counts_ref, dest_ref, carry_ref, offs_ref):
    tm = rinfo_ref.shape[1]

    @pl.when(pl.program_id(0) == 0)
    def _():
        carry_ref[...] = jnp.zeros_like(carry_ref)
        padded = jnp.floor((counts_ref[...] + float(TM_E - 1)) * (1.0 / TM_E)) * float(TM_E)
        inc = padded
        k = 1
        while k < BUCKET_ROWS:
            inc = inc + _shift_rows(inc, k, 0.0)
            k *= 2
        offs_ref[...] = inc - padded

    srow = lax.broadcasted_iota(jnp.int32, (TM, TM), 0)
    scol = lax.broadcasted_iota(jnp.int32, (TM, TM), 1)
    upper = (srow <= scol).astype(BF16)
    cid = lax.broadcasted_iota(jnp.int32, (BUCKET_ROWS, TM), 0).astype(F32)
    carry = carry_ref[...]
    for c in range(tm // TM):
        cols = slice(c * TM, (c + 1) * TM)
        onehot = jnp.where(cid == rinfo_ref[0:1, cols], 1.0, 0.0)
        prefix = _dot(onehot.astype(BF16), upper)
        rank = jnp.sum(onehot * (prefix - 1.0 + carry[:, 0:1] + offs_ref[:, 0:1]),
                       axis=0, keepdims=True)
        dest_ref[:, cols] = rank.astype(jnp.int32)
        carry = carry + prefix[:, TM - 1:TM]
    carry_ref[...] = carry


def _rank_call(rinfo, counts):
    t = rinfo.shape[1]
    tm = TM * RANK_CHUNKS
    assert t % tm == 0
    return pl.pallas_call(
        _rank_kernel,
        grid=(t // tm,),
        in_specs=[pl.BlockSpec((SUBLANES, tm), lambda i: (0, i)),
                  pl.BlockSpec((BUCKET_ROWS, LANES), lambda i: (0, 0))],
        out_specs=pl.BlockSpec((1, tm), lambda i: (0, i)),
        out_shape=jax.ShapeDtypeStruct((1, t), jnp.int32),
        scratch_shapes=[pltpu.VMEM((BUCKET_ROWS, LANES), F32),
                        pltpu.VMEM((BUCKET_ROWS, LANES), F32)],
        compiler_params=pltpu.CompilerParams(
            dimension_semantics=("arbitrary",), vmem_limit_bytes=VMEM_LIMIT),
        name="rank",
    )(rinfo, counts)


def _dispatch_kernel(dest_ref, ends_ref, u2_ref, wg_ref, wu_ref, wd_ref,
                     xs_ref, wg_bf_ref, wu_bf_ref, wd_bf_ref, zbuf, sem, zsem):
    tm = u2_ref.shape[0]
    t0 = pl.program_id(0) * tm

    @pl.when(pl.program_id(0) == 0)
    def _():
        zbuf[...] = jnp.zeros_like(zbuf)

        def tail_copy(bkt):
            end = ends_ref[bkt]
            start = ends_ref[bkt - 1] if bkt else 0
            tail = pl.multiple_of(jnp.maximum(end - TM_E, 0), TM_E)
            return end > start, pltpu.make_async_copy(zbuf, xs_ref.at[pl.ds(tail, TM_E)], zsem)

        for bkt in range(N_BUCKETS):
            nonempty, copy = tail_copy(bkt)
            pl.when(nonempty)(copy.start)
        for bkt in range(N_BUCKETS):
            nonempty, copy = tail_copy(bkt)
            pl.when(nonempty)(copy.wait)

        def unused_tile_copy(k):
            return pltpu.make_async_copy(
                zbuf, xs_ref.at[pl.ds(pl.multiple_of(k * TM_E, TM_E), TM_E)], zsem)

        first_unused = ends_ref[N_BUCKETS - 1] // TM_E
        n_tiles = xs_ref.shape[0] // TM_E
        lax.fori_loop(first_unused, n_tiles, lambda k, c: (unused_tile_copy(k).start(), c)[1], 0)
        lax.fori_loop(first_unused, n_tiles, lambda k, c: (unused_tile_copy(k).wait(), c)[1], 0)

    for r in range(tm):
        pltpu.make_async_copy(u2_ref.at[pl.ds(r, 1)],
                              xs_ref.at[pl.ds(dest_ref[t0 + r], 1)], sem).start()
    wg_bf_ref[0] = wg_ref[0, 0].astype(BF16)
    wu_bf_ref[0] = wu_ref[0, 0].astype(BF16)
    wd_bf_ref[0] = wd_ref[0, 0].astype(BF16)
    pltpu.make_async_copy(u2_ref, xs_ref.at[pl.ds(0, tm)], sem).wait()


def _dispatch_call(dest, bucket_ends, u2ext, n_rows, w_gate, w_up, w_down):
    t, w = u2ext.shape
    _, n_exp, d, de = w_gate.shape
    tm = t // n_exp
    assert t % n_exp == 0 and tm % ISSUE_UNROLL == 0
    w_in = lambda shape: pl.BlockSpec((1, 1) + shape, lambda i, dr, er: (0, i, 0, 0))
    w_out = lambda shape: pl.BlockSpec((1,) + shape, lambda i, dr, er: (i, 0, 0))
    grid_spec = pltpu.PrefetchScalarGridSpec(
        num_scalar_prefetch=2,
        grid=(n_exp,),
        in_specs=[pl.BlockSpec((tm, w), lambda i, dr, er: (i, 0)),
                  w_in((d, de)), w_in((d, de)), w_in((de, d))],
        out_specs=[pl.BlockSpec(memory_space=pl.ANY),
                   w_out((d, de)), w_out((d, de)), w_out((de, d))],
        scratch_shapes=[pltpu.VMEM((TM_E, w), F32), pltpu.SemaphoreType.DMA(()),
                        pltpu.SemaphoreType.DMA(())],
    )
    return pl.pallas_call(
        _dispatch_kernel,
        grid_spec=grid_spec,
        out_shape=[jax.ShapeDtypeStruct((n_rows, w), F32),
                   jax.ShapeDtypeStruct((n_exp, d, de), BF16),
                   jax.ShapeDtypeStruct((n_exp, d, de), BF16),
                   jax.ShapeDtypeStruct((n_exp, de, d), BF16)],
        compiler_params=pltpu.CompilerParams(
            dimension_semantics=("arbitrary",), vmem_limit_bytes=VMEM_LIMIT),
        name="dispatch",
    )(dest, bucket_ends, u2ext, w_gate, w_up, w_down)


def _experts_kernel(ea_ref, eb_ref, nv_ref, xs_ref, wga_ref, wua_ref, wda_ref,
                    wgb_ref, wub_ref, wdb_ref, ys_ref):
    del ea_ref, eb_ref
    i = pl.program_id(0)
    d = ys_ref.shape[1]

    @pl.when(i < nv_ref[0])
    def _():
        x = xs_ref[:, :d].astype(BF16)

        def expert(wg_ref, wu_ref, wd_ref):
            g = _dot(x, wg_ref[0])
            h = (g * _sigmoid(g)) * _dot(x, wu_ref[0])
            return _dot(h.astype(BF16), wd_ref[0])

        ya = xs_ref[:, d:d + 1] * expert(wga_ref, wua_ref, wda_ref)
        ys_ref[...] = ya + xs_ref[:, d + 1:d + 2] * expert(wgb_ref, wub_ref, wdb_ref)

    @pl.when(i >= nv_ref[0])
    def _():
        ys_ref[...] = jnp.zeros_like(ys_ref)


def _experts_call(tile_ea, tile_eb, n_valid, xs, wg, wu, wd):
    tp, w = xs.shape
    _, d, de = wg.shape
    row = lambda i, ea, eb, nv: (jnp.minimum(i, nv[0] - 1), 0)
    wa = lambda i, ea, eb, nv: (ea[i], 0, 0)
    wb = lambda i, ea, eb, nv: (eb[i], 0, 0)
    grid_spec = pltpu.PrefetchScalarGridSpec(
        num_scalar_prefetch=3,
        grid=(tp // TM_E,),
        in_specs=[pl.BlockSpec((TM_E, w), row),
                  pl.BlockSpec((1, d, de), wa), pl.BlockSpec((1, d, de), wa),
                  pl.BlockSpec((1, de, d), wa),
                  pl.BlockSpec((1, d, de), wb), pl.BlockSpec((1, d, de), wb),
                  pl.BlockSpec((1, de, d), wb)],
        out_specs=pl.BlockSpec((TM_E, d), lambda i, ea, eb, nv: (i, 0)),
    )
    return pl.pallas_call(
        _experts_kernel,
        grid_spec=grid_spec,
        out_shape=jax.ShapeDtypeStruct((tp, d), F32),
        compiler_params=pltpu.CompilerParams(
            dimension_semantics=("arbitrary",), vmem_limit_bytes=VMEM_LIMIT),
        name="experts",
    )(tile_ea, tile_eb, n_valid, xs, wg, wu, wd, wg, wu, wd)


def _final_kernel(dest_ref, x1_ref, mod_ref, g2_ref, b2_ref, ys_ref, o_ref, ybuf, sem):
    tm = x1_ref.shape[0]
    i = pl.program_id(0)
    slot = i % GATHER_AHEAD

    def issue_rows(tile, to_slot, r0, n):
        for r in range(n):
            pltpu.make_async_copy(ys_ref.at[pl.ds(dest_ref[tile * tm + r0 + r], 1)],
                                  ybuf.at[to_slot, pl.ds(r0 + r, 1)], sem.at[to_slot]).start()

    def normalise_rows(r0, n):
        rows = pl.ds(r0, n)
        z = DEEPNORM_ALPHA * x1_ref[rows, :] + (1.0 + mod_ref[0, 5:6, :]) * ybuf[slot, rows, :]
        o_ref[rows, :] = _layer_norm(z) * g2_ref[...] + b2_ref[...]

    n_chunks = tm // ISSUE_UNROLL

    @pl.when(i == 0)
    def _():
        for tile in range(GATHER_AHEAD):
            issue_rows(tile, tile, 0, tm)

    pltpu.make_async_copy(ys_ref.at[pl.ds(0, tm)], ybuf.at[slot], sem.at[slot]).wait()

    def chunk(c, issue_ahead):
        r0 = c * ISSUE_UNROLL
        normalise_rows(r0, ISSUE_UNROLL)
        if issue_ahead:
            issue_rows(i + GATHER_AHEAD, slot, r0, ISSUE_UNROLL)

    more = i + GATHER_AHEAD < pl.num_programs(0)

    @pl.when(more)
    def _():
        for c in range(n_chunks):
            chunk(c, True)

    @pl.when(jnp.logical_not(more))
    def _():
        for c in range(n_chunks):
            chunk(c, False)


def _final_call(dest, x1, mod3, g2, b2, ys, seq):
    t, d = x1.shape
    per_b = seq // TM
    grid_spec = pltpu.PrefetchScalarGridSpec(
        num_scalar_prefetch=1,
        grid=(t // TM,),
        in_specs=[pl.BlockSpec((TM, d), lambda i, dr: (i, 0)),
                  pl.BlockSpec((1, 6, d), lambda i, dr: (i // per_b, 0, 0)),
                  pl.BlockSpec((1, d), lambda i, dr: (0, 0)),
                  pl.BlockSpec((1, d), lambda i, dr: (0, 0)),
                  pl.BlockSpec(memory_space=pl.ANY)],
        out_specs=pl.BlockSpec((TM, d), lambda i, dr: (i, 0)),
        scratch_shapes=[pltpu.VMEM((GATHER_AHEAD, TM, d), F32),
                        pltpu.SemaphoreType.DMA((GATHER_AHEAD,))],
    )
    return pl.pallas_call(
        _final_kernel,
        grid_spec=grid_spec,
        out_shape=jax.ShapeDtypeStruct((t, d), F32),
        compiler_params=pltpu.CompilerParams(
            dimension_semantics=("arbitrary",), vmem_limit_bytes=VMEM_LIMIT),
        name="final",
    )(dest, x1, mod3, g2, b2, ys)


def _gate_pairs(w_rg, w_ig):
    def pairs(w):
        n, bs, _ = w.shape
        w = w.reshape(n // 2, 2, bs, bs)
        z = jnp.zeros((n // 2, bs, bs), w.dtype)
        top = jnp.concatenate([w[:, 0], z], axis=2)
        bot = jnp.concatenate([z, w[:, 1]], axis=2)
        return jnp.concatenate([top, bot], axis=1)
    return jnp.concatenate([pairs(w_rg), pairs(w_ig)], axis=2).astype(BF16)


def kernel(x, c, w_ada, b_ada, w_in, b_f, conv_w, conv_b, w_rg, b_rg, w_ig, b_ig, lru_lambda,
           g_attn, g_lru, w_out, ln1_g, ln1_b, w_grp, b_grp, w_exp, b_exp,
           w_e_gate, w_e_up, w_e_down, ln2_g, ln2_b):
    assert w_ada.shape[0] == DEPTH
    b, s, d = x.shape
    t = b * s
    d_attn = N_HEADS * HEAD_DIM
    d_lru = conv_w.shape[2]
    n_exp = w_e_gate.shape[1]
    assert s % TM == 0 and s % TQ == 0 and t % TM_E == 0

    c_pad = jnp.pad(c, ((0, SUBLANES - b), (0, 0)))
    mod = _mod_call(c_pad, w_ada, b_ada[0][None, :])
    mod3 = mod[:b].reshape(b, 6, d)

    bf_pad = jnp.pad(b_f[0], (0, LANES - N_HEADS))[None, :]
    q_aug, k_aug, vt_aug, xb, gb = _inproj_call(
        x, mod3, jnp.swapaxes(w_in, 1, 2), bf_pad, d_lru)
    attn_t = _attn_call(q_aug, k_aug, vt_aug)
    lru = _lru_call(xb, gb, conv_w[0], conv_b[0][None, :], _gate_pairs(w_rg[0], w_ig[0]),
                    b_rg[0][None, :], b_ig[0][None, :], lru_lambda[0][None, :])

    n_route = N_GROUPS + n_exp
    wr = jnp.pad(jnp.concatenate([w_grp[0], w_exp[0]], axis=1).T,
                 ((0, BUCKET_ROWS - n_route), (0, 0)))
    br = jnp.pad(jnp.concatenate([b_grp[0], b_exp[0]]), (0, BUCKET_ROWS - n_route))
    br = jnp.broadcast_to(br[:, None], (BUCKET_ROWS, LANES))
    x1, u2ext, rinfo, counts = _mixout_call(
        attn_t, lru.reshape(t, d_lru), x.reshape(t, d), mod3,
        g_attn[0][None, :], g_lru[0][None, :], w_out,
        ln1_g[0][None, :], ln1_b[0][None, :], wr, br, s)

    dest = _rank_call(rinfo, counts).reshape(t)
    cnt = counts[:N_BUCKETS, 0].astype(jnp.int32)
    ends = jnp.cumsum((cnt + (TM_E - 1)) // TM_E)
    n_tiles = t // TM_E + N_BUCKETS
    tile_bucket = jnp.sum(ends[None, :] <= jnp.arange(n_tiles)[:, None], axis=1)
    tile_bucket = jnp.minimum(tile_bucket, N_BUCKETS - 1)
    n_valid = ends[N_BUCKETS - 1:]
    last_bucket = tile_bucket[jnp.maximum(n_valid[0] - 1, 0)]
    tile_bucket = jnp.where(jnp.arange(n_tiles) < n_valid[0], tile_bucket, last_bucket)
    tile_ea = jnp.asarray(_BUCKET_EA, jnp.int32)[tile_bucket]
    tile_eb = jnp.asarray(_BUCKET_EB, jnp.int32)[tile_bucket]

    xs, wg_bf, wu_bf, wd_bf = _dispatch_call(
        dest, (ends * TM_E).astype(jnp.int32), u2ext, n_tiles * TM_E, w_e_gate, w_e_up, w_e_down)
    ys = _experts_call(tile_ea, tile_eb, n_valid.astype(jnp.int32), xs, wg_bf, wu_bf, wd_bf)
    out = _final_call(dest, x1, mod3, ln2_g[0][None, :], ln2_b[0][None, :], ys, s)
    return out.reshape(b, s, d)
```

```python
import functools

import jax
import jax.numpy as jnp
from jax import lax
from jax.experimental import pallas as pl
from jax.experimental.pallas import tpu as pltpu

F32 = jnp.float32
BF16 = jnp.bfloat16

HEAD_DIM = 64
N_HEADS = 8
CONV_WIDTH = 4
LRU_C = 8.0
N_GROUPS = 4
EXPERTS_PER_GROUP = 4
N_PAIRS = 6
N_BUCKETS = N_GROUPS * N_PAIRS
LN_EPS = 1e-5
RMS_EPS = 1e-6
NEG_INF = -1e30
DEPTH = 1
DEEPNORM_ALPHA = (2.0 * DEPTH) ** 0.25
LOG2E = 1.4426950408889634

LANES = 128
SUBLANES = 8
HEAD_PAD = LANES
VT_ROWS = 80
BUCKET_ROWS = 32
TM = 512
TQ = 512
TM_E = 256
ISSUE_UNROLL = 64
GATHER_AHEAD = 2
RANK_CHUNKS = 4
DMA_PRIORITIES = 2
VMEM_BYTES_V7X = 64 * 1024 * 1024
VMEM_LIMIT = VMEM_BYTES_V7X // 8 * 7

_PAIRS = [(0, 1), (0, 2), (0, 3), (1, 2), (1, 3), (2, 3)]
_BUCKET_EA = [g * EXPERTS_PER_GROUP + a for g in range(N_GROUPS) for (a, b) in _PAIRS]
_BUCKET_EB = [g * EXPERTS_PER_GROUP + b for g in range(N_GROUPS) for (a, b) in _PAIRS]


def _dot(a, b):
    return jnp.dot(a, b, preferred_element_type=F32)


def _split2(a):
    hi = a.astype(BF16)
    lo = (a - hi.astype(F32)).astype(BF16)
    return hi, lo


def _split3(a):
    hi = a.astype(BF16)
    r = a - hi.astype(F32)
    mid = r.astype(BF16)
    lo = (r - mid.astype(F32)).astype(BF16)
    return hi, mid, lo


def _layer_norm(x):
    mu = jnp.mean(x, axis=-1, keepdims=True)
    xc = x - mu
    var = jnp.mean(xc * xc, axis=-1, keepdims=True)
    return xc * lax.rsqrt(var + LN_EPS)


def _sigmoid(x):
    return 0.5 * jnp.tanh(0.5 * x) + 0.5


def _rot_in_group(x, k):
    n, w = x.shape
    return pltpu.roll(x.reshape(n // SUBLANES, SUBLANES, w), k, 1).reshape(n, w)


def _log_sigmoid(z):
    return jnp.minimum(z, 0.0) - jnp.log1p(jnp.exp(-jnp.abs(z)))


def _cumsum_rows(x, carry):
    n, w = x.shape
    row_in_group = lax.broadcasted_iota(jnp.int32, (n, w), 0) % SUBLANES
    k = 1
    while k < SUBLANES:
        x = x + jnp.where(row_in_group >= k, _rot_in_group(x, k), 0.0)
        k *= 2
    groups = []
    for g in range(n // SUBLANES):
        blk = x[g * SUBLANES:(g + 1) * SUBLANES] + carry
        groups.append(blk)
        carry = blk[SUBLANES - 1:SUBLANES]
    return jnp.concatenate(groups, axis=0)


def _shift_rows(x, k, fill):
    n = x.shape[0]
    if k % SUBLANES == 0:
        return jnp.concatenate([jnp.full((k, x.shape[1]), fill, x.dtype), x[:n - k]], axis=0)
    row = lax.broadcasted_iota(jnp.int32, x.shape, 0)
    return jnp.where(row >= k, pltpu.roll(x, k, 0), fill)


def _mod_kernel(c_ref, w_ref, b_ref, o_ref):
    c = c_ref[...]
    s = c * _sigmoid(c)
    sh, sl = _split2(s)
    wh, wl = _split2(w_ref[0])
    o_ref[...] = _dot(sh, wh) + (_dot(sh, wl) + _dot(sl, wh)) + b_ref[...]


def _mod_call(c_pad, w_ada, b_ada):
    rows, d = c_pad.shape
    n = w_ada.shape[2]
    return pl.pallas_call(
        _mod_kernel,
        grid=(n // d,),
        in_specs=[pl.BlockSpec((rows, d), lambda j: (0, 0)),
                  pl.BlockSpec((1, d, d), lambda j: (0, 0, j)),
                  pl.BlockSpec((1, d), lambda j: (0, j))],
        out_specs=pl.BlockSpec((rows, d), lambda j: (0, j)),
        out_shape=jax.ShapeDtypeStruct((rows, n), F32),
        compiler_params=pltpu.CompilerParams(vmem_limit_bytes=VMEM_LIMIT),
        name="mod",
    )(c_pad, w_ada, b_ada)


def _inproj_kernel(x_ref, mod_ref, wt_ref, bf_ref,
                   q_ref, k_ref, vt_ref, xb_ref, gb_ref, carry_ref, v_scr, w_scr, stage, wsem):
    j = pl.program_id(1)
    d_attn = N_HEADS * HEAD_DIM
    d_lru = xb_ref.shape[2]
    c_k, c_v, c_x = d_attn, 2 * d_attn, 3 * d_attn
    c_g, c_f = c_x + d_lru, c_x + 2 * d_lru
    r_f = 3 * d_attn
    r_x, r_g = r_f + N_HEADS, r_f + N_HEADS + d_lru

    @pl.when(jnp.logical_and(pl.program_id(0) == 0, j == 0))
    def _():
        blocks = [(0, d_attn, 0, HEAD_DIM ** -0.5 * LOG2E, None), (d_attn, d_attn, c_k, None, None),
                  (2 * d_attn, d_attn, c_v, None, None), (r_x, d_lru, c_x, None, None),
                  (r_g, d_lru, c_g, None, None), (r_f, LANES, c_f, None, N_HEADS)]
        copies = [pltpu.make_async_copy(wt_ref.at[0, pl.ds(r0, n), :],
                                        stage.at[k, pl.ds(0, n), :], wsem.at[k])
                  for k, (r0, n, _, _, _) in enumerate(blocks)]
        for copy in copies:
            copy.start()
        for k, (_, n, c0, scale, keep_lanes) in enumerate(blocks):
            copies[k].wait()
            blk = stage[k, 0:n, :]
            if scale is not None:
                blk = blk * scale
            blk = blk.T
            if keep_lanes is not None:
                lane = lax.broadcasted_iota(jnp.int32, blk.shape, 1)
                blk = jnp.where(lane < keep_lanes, blk, 0.0)
            w_scr[:, c0:c0 + n] = blk.astype(BF16)

    @pl.when(j == 0)
    def _():
        carry_ref[...] = jnp.zeros_like(carry_ref)

    tm = x_ref.shape[1]
    u = _layer_norm(x_ref[0]) * (1.0 + mod_ref[0, 1:2, :]) + mod_ref[0, 0:1, :]
    ub = u.astype(BF16)

    logf = _log_sigmoid(_dot(ub, w_scr[:, c_f:c_f + LANES]) + bf_ref[...])
    cum = _cumsum_rows(logf, carry_ref[...])
    carry_ref[...] = cum[tm - 1:tm, :]
    c_hi, c_mid, c_lo = [p.astype(F32) for p in _split3(cum * LOG2E)]

    lane = lax.broadcasted_iota(jnp.int32, (tm, HEAD_PAD), 1)
    d = HEAD_DIM
    q_all = _dot(ub, w_scr[:, 0:c_k])
    k_all = _dot(ub, w_scr[:, c_k:c_v])
    for h in range(N_HEADS):
        ts = slice((h // 2) * LANES, (h // 2 + 1) * LANES)
        qh, kh = q_all[:, ts], k_all[:, ts]
        if h % 2:
            qh, kh = pltpu.roll(qh, d, 1), pltpu.roll(kh, d, 1)
        hi, mid, lo = c_hi[:, h:h + 1], c_mid[:, h:h + 1], c_lo[:, h:h + 1]
        q_ext = jnp.where(lane == d, hi, jnp.where(lane == d + 1, mid, jnp.where(
            lane == d + 2, lo, jnp.where(lane < d + 6, 1.0, 0.0))))
        k_ext = jnp.where(lane < d + 3, 1.0, jnp.where(lane == d + 3, -hi, jnp.where(
            lane == d + 4, -mid, jnp.where(lane == d + 5, -lo, 0.0))))
        q_ref[0, h] = jnp.where(lane < d, qh, q_ext).astype(BF16)
        k_ref[0, h] = jnp.where(lane < d, kh, k_ext).astype(BF16)

    v_scr[...] = _dot(ub, w_scr[:, c_v:c_x])
    vt_all = v_scr[...].T
    pad_rows = lax.broadcasted_iota(jnp.int32, (VT_ROWS - d, tm), 0)
    ones_row = jnp.where(pad_rows == 0, 1.0, 0.0).astype(BF16)
    for h in range(N_HEADS):
        vt_ref[0, h, 0, :d, :] = vt_all[h * d:(h + 1) * d, :].astype(BF16)
        vt_ref[0, h, 0, d:, :] = ones_row

    xb_ref[0] = _dot(ub, w_scr[:, c_x:c_g])
    gb_ref[0] = _dot(ub, w_scr[:, c_g:c_f])


def _inproj_call(x, mod3, w_in_t, bf_pad, d_lru):
    b, s, d = x.shape
    d_attn = N_HEADS * HEAD_DIM
    assert w_in_t.shape[1] == 3 * d_attn + N_HEADS + 2 * d_lru
    full = lambda shape: pl.BlockSpec(shape, lambda bi, j: (0,) * len(shape))
    head_spec = pl.BlockSpec((1, N_HEADS, TM, HEAD_PAD), lambda bi, j: (bi, 0, j, 0))
    vt_spec = pl.BlockSpec((1, N_HEADS, 1, VT_ROWS, TM), lambda bi, j: (bi, 0, j, 0, 0))
    row_spec = pl.BlockSpec((1, TM, d_lru), lambda bi, j: (bi, j, 0))
    head_shape = jax.ShapeDtypeStruct((b, N_HEADS, s, HEAD_PAD), BF16)
    return pl.pallas_call(
        _inproj_kernel,
        grid=(b, s // TM),
        in_specs=[pl.BlockSpec((1, TM, d), lambda bi, j: (bi, j, 0)),
                  pl.BlockSpec((1, 6, d), lambda bi, j: (bi, 0, 0)),
                  pl.BlockSpec(memory_space=pl.ANY), full((1, LANES))],
        out_specs=[head_spec, head_spec, vt_spec, row_spec, row_spec],
        out_shape=[head_shape, head_shape,
                   jax.ShapeDtypeStruct((b, N_HEADS, s // TM, VT_ROWS, TM), BF16),
                   jax.ShapeDtypeStruct((b, s, d_lru), F32),
                   jax.ShapeDtypeStruct((b, s, d_lru), F32)],
        scratch_shapes=[pltpu.VMEM((1, LANES), F32), pltpu.VMEM((TM, d_attn), F32),
                        pltpu.VMEM((d, 3 * d_attn + 2 * d_lru + LANES), BF16),
                        pltpu.VMEM((6, max(d_attn, d_lru), d), F32),
                        pltpu.SemaphoreType.DMA((6,))],
        compiler_params=pltpu.CompilerParams(
            dimension_semantics=("arbitrary", "arbitrary"), vmem_limit_bytes=VMEM_LIMIT),
        name="inproj",
    )(x, mod3, w_in_t, bf_pad)


def _attn_kernel(q_ref, k_ref, vt_ref, o_ref, s_scr, smax_scr, m_scr, acc_scr):
    tq = TQ
    nq = q_ref.shape[2] // tq
    heads = range(q_ref.shape[1])

    def scores_to(slot, i, j):
        for hh in heads:
            k = k_ref[0, hh, pl.ds(pl.multiple_of(j * tq, tq), tq), :]
            q = q_ref[0, hh, pl.ds(pl.multiple_of(i * tq, tq), tq), :]
            s = lax.dot_general(k, q, (((1,), (1,)), ((), ())), preferred_element_type=F32)
            s_scr[slot, hh] = s
            smax_scr[slot, hh] = jnp.max(s, axis=0, keepdims=True)

    def consume(slot, j, masked):
        for hh in heads:
            s = s_scr[slot, hh]
            if masked:
                key = lax.broadcasted_iota(jnp.int32, (tq, tq), 0)
                qry = lax.broadcasted_iota(jnp.int32, (tq, tq), 1)
                s = jnp.where(key <= qry, s, NEG_INF)
                s_max = jnp.max(s, axis=0, keepdims=True)
            else:
                s_max = smax_scr[slot, hh]
            m = m_scr[hh]
            m_new = jnp.maximum(m, s_max)
            p = jnp.exp2(s - m_new).astype(BF16)
            acc_scr[hh] = jnp.exp2(m - m_new) * acc_scr[hh] + _dot(vt_ref[0, hh, j], p)
            m_scr[hh] = m_new

    scores_to(2, 0, 0)

    def query_block(i, carry):
        m_scr[...] = jnp.full(m_scr.shape, NEG_INF, F32)
        acc_scr[...] = jnp.zeros(acc_scr.shape, F32)
        nxt = jnp.minimum(i + 1, nq - 1)
        n_mid = i - 1

        @pl.when(i == 0)
        def _():
            consume(2, 0, True)
            scores_to(2, nxt, 0)

        @pl.when(i >= 1)
        def _():
            scores_to(1, i, 1)
            consume(2, 0, False)

        def pair(kk, c):
            j = 1 + 2 * kk
            scores_to(0, i, j + 1)
            consume(1, j, False)
            scores_to(1, i, j + 2)
            consume(0, j + 1, False)
            return c

        lax.fori_loop(0, jnp.maximum(n_mid, 0) // 2, pair, 0)

        @pl.when(jnp.logical_and(i >= 1, n_mid % 2 == 1))
        def _():
            scores_to(0, i, i)
            consume(1, i - 1, False)
            scores_to(2, nxt, 0)
            consume(0, i, True)

        @pl.when(jnp.logical_and(i >= 1, n_mid % 2 == 0))
        def _():
            scores_to(2, nxt, 0)
            consume(1, i, True)

        for hh in heads:
            o_ref[0, 0, i, hh * HEAD_DIM:(hh + 1) * HEAD_DIM, :] = (
                acc_scr[hh, :HEAD_DIM, :] / acc_scr[hh, HEAD_DIM:HEAD_DIM + 1, :])
        return carry

    lax.fori_loop(0, nq, query_block, 0)


def _attn_call(q_aug, k_aug, vt_aug):
    b, h, s, hp = q_aug.shape
    assert TQ == TM
    nq = s // TQ
    return pl.pallas_call(
        _attn_kernel,
        grid=(b, h // 2),
        in_specs=[pl.BlockSpec((1, 2, s, hp), lambda bi, p: (bi, p, 0, 0)),
                  pl.BlockSpec((1, 2, s, hp), lambda bi, p: (bi, p, 0, 0)),
                  pl.BlockSpec((1, 2, nq, VT_ROWS, TQ), lambda bi, p: (bi, p, 0, 0, 0))],
        out_specs=pl.BlockSpec((1, 1, nq, 2 * HEAD_DIM, TQ), lambda bi, p: (bi, p, 0, 0, 0)),
        out_shape=jax.ShapeDtypeStruct((b, h // 2, nq, 2 * HEAD_DIM, TQ), F32),
        scratch_shapes=[pltpu.VMEM((3, 2, TQ, TQ), F32), pltpu.VMEM((3, 2, 1, TQ), F32),
                        pltpu.VMEM((2, 1, TQ), F32), pltpu.VMEM((2, VT_ROWS, TQ), F32)],
        compiler_params=pltpu.CompilerParams(
            dimension_semantics=("arbitrary", "arbitrary"), vmem_limit_bytes=VMEM_LIMIT),
        name="attention",
    )(q_aug, k_aug, vt_aug)


def _gelu_tanh(x):
    return 0.5 * x * (1.0 + jnp.tanh(0.7978845608028654 * (x + 0.044715 * (x * x * x))))


def _lru_kernel(xb_ref, gb_ref, cw_ref, cb_ref, wgate_ref, brg_ref, big_ref, lam_ref,
                o_ref, tail_ref, h_ref):
    j = pl.program_id(1)

    @pl.when(j == 0)
    def _():
        tail_ref[...] = jnp.zeros_like(tail_ref)
        h_ref[...] = jnp.zeros_like(h_ref)

    x = xb_ref[0]
    tm, dl = x.shape
    row_in_group = lax.broadcasted_iota(jnp.int32, (tm, dl), 0) % SUBLANES
    x_prev_group = jnp.concatenate([tail_ref[...], x[:tm - SUBLANES]], axis=0)
    xc = x * cw_ref[CONV_WIDTH - 1:CONV_WIDTH, :] + cb_ref[...]
    for k in range(1, CONV_WIDTH):
        xs = jnp.where(row_in_group < k, _rot_in_group(x_prev_group, k), _rot_in_group(x, k))
        xc = xc + xs * cw_ref[CONV_WIDTH - 1 - k:CONV_WIDTH - k, :]
    tail_ref[...] = x[tm - SUBLANES:, :]

    xcb = xc.astype(BF16)
    n_pairs = dl // LANES
    r_parts, i_parts = [], []
    for p in range(n_pairs):
        g = _dot(xcb[:, p * LANES:(p + 1) * LANES], wgate_ref[p])
        r_parts.append(g[:, :LANES])
        i_parts.append(g[:, LANES:])
    r = _sigmoid(jnp.concatenate(r_parts, axis=1) + brg_ref[...])
    ig = _sigmoid(jnp.concatenate(i_parts, axis=1) + big_ref[...])

    lam = lam_ref[...]
    softplus_neg_lam = jnp.maximum(-lam, 0.0) + jnp.log1p(jnp.exp(-jnp.abs(lam)))
    log_a = (-LRU_C) * r * softplus_neg_lam
    a = jnp.exp(log_a)
    v = 1.0 - a * a
    u = jnp.where(v > 0.0, v * lax.rsqrt(v), 0.0) * (ig * xc)

    k = 1
    while k < SUBLANES:
        keep = row_in_group >= k
        u = a * jnp.where(keep, _rot_in_group(u, k), 0.0) + u
        a = a * jnp.where(keep, _rot_in_group(a, k), 1.0)
        k *= 2
    h_prev = h_ref[...]
    groups = []
    for g in range(tm // SUBLANES):
        rows = slice(g * SUBLANES, (g + 1) * SUBLANES)
        hg = a[rows] * h_prev + u[rows]
        groups.append(hg)
        h_prev = hg[SUBLANES - 1:SUBLANES]
    h_ref[...] = h_prev
    o_ref[0] = jnp.concatenate(groups, axis=0) * _gelu_tanh(gb_ref[0])


def _lru_call(xb, gb, conv_w, conv_b, wgate, b_rg, b_ig, lam):
    b, s, dl = xb.shape
    row_spec = pl.BlockSpec((1, TM, dl), lambda bi, j: (bi, j, 0))
    full = lambda shape: pl.BlockSpec(shape, lambda bi, j: (0,) * len(shape))
    return pl.pallas_call(
        _lru_kernel,
        grid=(b, s // TM),
        in_specs=[row_spec, row_spec, full(conv_w.shape), full((1, dl)), full(wgate.shape),
                  full((1, dl)), full((1, dl)), full((1, dl))],
        out_specs=row_spec,
        out_shape=jax.ShapeDtypeStruct((b, s, dl), F32),
        scratch_shapes=[pltpu.VMEM((SUBLANES, dl), F32), pltpu.VMEM((1, dl), F32)],
        compiler_params=pltpu.CompilerParams(
            dimension_semantics=("arbitrary", "arbitrary"), vmem_limit_bytes=VMEM_LIMIT),
        name="lru",
    )(xb, gb, conv_w, conv_b, wgate, b_rg, b_ig, lam)


def _rms(x, gain):
    return x * lax.rsqrt(jnp.mean(x * x, axis=-1, keepdims=True) + RMS_EPS) * gain


def _mixout_kernel(attn_ref, lru_ref, x_ref, mod_ref, ga_ref, gl_ref, woa_ref, wol_ref,
                   g1_ref, b1_ref, wr_ref, br_ref, x1_ref, u2_ref, rinfo_ref, cnt_ref, wo_bf):
    tm, d = x_ref.shape

    @pl.when(pl.program_id(0) == 0)
    def _():
        wo_bf[0] = woa_ref[0].astype(BF16)
        wo_bf[1] = wol_ref[0].astype(BF16)

    attn = attn_ref[0, :, 0].reshape(ga_ref.shape[1], tm).T
    na = _rms(attn, ga_ref[...]).astype(BF16)
    nl = _rms(lru_ref[...], gl_ref[...]).astype(BF16)
    mix = _dot(na, wo_bf[0]) + _dot(nl, wo_bf[1])
    z = DEEPNORM_ALPHA * x_ref[...] + (1.0 + mod_ref[0, 2:3, :]) * mix
    x1 = _layer_norm(z) * g1_ref[...] + b1_ref[...]
    x1_ref[...] = x1
    u2 = _layer_norm(x1) * (1.0 + mod_ref[0, 4:5, :]) + mod_ref[0, 3:4, :]

    nt = (((1,), (1,)), ((), ()))
    uh, ul = _split2(u2)
    wh, wl = _split2(wr_ref[...])
    dg = lambda a, b_: lax.dot_general(a, b_, nt, preferred_element_type=F32)
    lg = dg(wh, uh) + (dg(wh, ul) + dg(wl, uh)) + br_ref[:, 0:1]

    def first_index(vals, target):
        idx = jnp.full_like(target, float(len(vals) - 1))
        for n in range(len(vals) - 2, -1, -1):
            idx = jnp.where(vals[n] == target, float(n), idx)
        return idx

    g = [lg[n:n + 1, :] for n in range(N_GROUPS)]
    gmax = functools.reduce(jnp.maximum, g)
    gsum = functools.reduce(lambda a, b_: a + b_, [jnp.exp(v - gmax) for v in g])
    grp_w = 1.0 / gsum
    gidx = first_index(g, gmax)

    sel = []
    for e in range(EXPERTS_PER_GROUP):
        v = lg[N_GROUPS + (N_GROUPS - 1) * EXPERTS_PER_GROUP + e:
               N_GROUPS + (N_GROUPS - 1) * EXPERTS_PER_GROUP + e + 1, :]
        for gi in range(N_GROUPS - 2, -1, -1):
            r0 = N_GROUPS + gi * EXPERTS_PER_GROUP + e
            v = jnp.where(gidx == float(gi), lg[r0:r0 + 1, :], v)
        sel.append(v)
    smax = functools.reduce(jnp.maximum, sel)
    i1 = first_index(sel, smax)
    rest = [jnp.where(i1 == float(e), -3e38, sel[e]) for e in range(EXPERTS_PER_GROUP)]
    rmax = functools.reduce(jnp.maximum, rest)
    i2 = first_index(rest, rmax)
    e2 = jnp.exp(rmax - smax)
    w1 = grp_w / (1.0 + e2)
    w2 = grp_w * e2 / (1.0 + e2)
    ia = jnp.minimum(i1, i2)
    ib = jnp.maximum(i1, i2)
    wa = jnp.where(i1 < i2, w1, w2)
    wb = jnp.where(i1 < i2, w2, w1)
    pair = jnp.where(ia == 0.0, ib - 1.0, jnp.where(ia == 1.0, ib + 1.0, 5.0))
    bucket = gidx * float(N_PAIRS) + pair

    @pl.when(pl.program_id(0) == 0)
    def _():
        cnt_ref[...] = jnp.zeros_like(cnt_ref)

    cid = lax.broadcasted_iota(jnp.int32, (BUCKET_ROWS, tm), 0).astype(F32)
    cnt_ref[...] += jnp.sum(jnp.where(cid == bucket, 1.0, 0.0), axis=1, keepdims=True)

    zrow = jnp.zeros_like(wa)
    rinfo_ref[...] = jnp.concatenate([bucket, wa, wb] + [zrow] * (SUBLANES - 3), axis=0)
    wt = jnp.concatenate([wa, wb, jnp.zeros((LANES - 2, tm), F32)], axis=0)
    u2_ref[:, :d] = u2
    u2_ref[:, d:] = wt.T


def _mixout_call(attn_t, lru, x2d, mod3, ga, gl, wo, g1, b1, wr, br, seq):
    t, d = x2d.shape
    _, n_pairs, _, pair_w, _ = attn_t.shape
    dh = n_pairs * pair_w
    assert wo.shape[1] == 2 * dh and lru.shape[1] == dh
    per_b = seq // TM
    full = lambda shape: pl.BlockSpec(shape, lambda i: (0,) * len(shape))
    return pl.pallas_call(
        _mixout_kernel,
        grid=(t // TM,),
        in_specs=[pl.BlockSpec((1, n_pairs, 1, pair_w, TM),
                               lambda i: (i // per_b, 0, i % per_b, 0, 0)),
                  pl.BlockSpec((TM, dh), lambda i: (i, 0)),
                  pl.BlockSpec((TM, d), lambda i: (i, 0)),
                  pl.BlockSpec((1, 6, d), lambda i: (i // per_b, 0, 0)),
                  full((1, dh)), full((1, dh)),
                  pl.BlockSpec((1, dh, d), lambda i: (0, 0, 0)),
                  pl.BlockSpec((1, dh, d), lambda i: (0, 1, 0)),
                  full((1, d)), full((1, d)), full((BUCKET_ROWS, d)), full((BUCKET_ROWS, LANES))],
        out_specs=[pl.BlockSpec((TM, d), lambda i: (i, 0)),
                   pl.BlockSpec((TM, d + LANES), lambda i: (i, 0)),
                   pl.BlockSpec((SUBLANES, TM), lambda i: (0, i)),
                   pl.BlockSpec((BUCKET_ROWS, LANES), lambda i: (0, 0))],
        out_shape=[jax.ShapeDtypeStruct((t, d), F32),
                   jax.ShapeDtypeStruct((t, d + LANES), F32),
                   jax.ShapeDtypeStruct((SUBLANES, t), F32),
                   jax.ShapeDtypeStruct((BUCKET_ROWS, LANES), F32)],
        scratch_shapes=[pltpu.VMEM((2, dh, d), BF16)],
        compiler_params=pltpu.CompilerParams(
            dimension_semantics=("arbitrary",), vmem_limit_bytes=VMEM_LIMIT),
        name="mixout",
    )(attn_t, lru, x2d, mod3, ga, gl, wo, wo, g1, b1, wr, br)


def _rank_kernel(rinfo_ref, counts_ref, dest_ref, carry_ref, offs_ref):
    tm = rinfo_ref.shape[1]

    @pl.when(pl.program_id(0) == 0)
    def _():
        carry_ref[...] = jnp.zeros_like(carry_ref)
        padded = jnp.floor((counts_ref[...] + float(TM_E - 1)) * (1.0 / TM_E)) * float(TM_E)
        inc = padded
        k = 1
        while k < BUCKET_ROWS:
            inc = inc + _shift_rows(inc, k, 0.0)
            k *= 2
        offs_ref[...] = inc - padded

    srow = lax.broadcasted_iota(jnp.int32, (TM, TM), 0)
    scol = lax.broadcasted_iota(jnp.int32, (TM, TM), 1)
    upper = (srow <= scol).astype(BF16)
    cid = lax.broadcasted_iota(jnp.int32, (BUCKET_ROWS, TM), 0).astype(F32)
    carry = carry_ref[...]
    for c in range(tm // TM):
        cols = slice(c * TM, (c + 1) * TM)
        onehot = jnp.where(cid == rinfo_ref[0:1, cols], 1.0, 0.0)
        prefix = _dot(onehot.astype(BF16), upper)
        rank = jnp.sum(onehot * (prefix - 1.0 + carry[:, 0:1] + offs_ref[:, 0:1]),
                       axis=0, keepdims=True)
        dest_ref[:, cols] = rank.astype(jnp.int32)
        carry = carry + prefix[:, TM - 1:TM]
    carry_ref[...] = carry


def _rank_call(rinfo, counts):
    t = rinfo.shape[1]
    tm = TM * RANK_CHUNKS
    assert t % tm == 0
    return pl.pallas_call(
        _rank_kernel,
        grid=(t // tm,),
        in_specs=[pl.BlockSpec((SUBLANES, tm), lambda i: (0, i)),
                  pl.BlockSpec((BUCKET_ROWS, LANES), lambda i: (0, 0))],
        out_specs=pl.BlockSpec((1, tm), lambda i: (0, i)),
        out_shape=jax.ShapeDtypeStruct((1, t), jnp.int32),
        scratch_shapes=[pltpu.VMEM((BUCKET_ROWS, LANES), F32),
                        pltpu.VMEM((BUCKET_ROWS, LANES), F32)],
        compiler_params=pltpu.CompilerParams(
            dimension_semantics=("arbitrary",), vmem_limit_bytes=VMEM_LIMIT),
        name="rank",
    )(rinfo, counts)


def _dispatch_kernel(dest_ref, ends_ref, u2_ref, wg_ref, wu_ref, wd_ref,
                     xs_ref, wg_bf_ref, wu_bf_ref, wd_bf_ref, zbuf, sem, zsem):
    tm = u2_ref.shape[0]
    t0 = pl.program_id(0) * tm

    @pl.when(pl.program_id(0) == 0)
    def _():
        zbuf[...] = jnp.zeros_like(zbuf)

        def tail_copy(bkt):
            end = ends_ref[bkt]
            start = ends_ref[bkt - 1] if bkt else 0
            tail = pl.multiple_of(jnp.maximum(end - TM_E, 0), TM_E)
            return end > start, pltpu.make_async_copy(zbuf, xs_ref.at[pl.ds(tail, TM_E)], zsem)

        for bkt in range(N_BUCKETS):
            nonempty, copy = tail_copy(bkt)
            pl.when(nonempty)(copy.start)
        for bkt in range(N_BUCKETS):
            nonempty, copy = tail_copy(bkt)
            pl.when(nonempty)(copy.wait)

        def unused_tile_copy(k):
            return pltpu.make_async_copy(
                zbuf, xs_ref.at[pl.ds(pl.multiple_of(k * TM_E, TM_E), TM_E)], zsem)

        first_unused = ends_ref[N_BUCKETS - 1] // TM_E
        n_tiles = xs_ref.shape[0] // TM_E
        lax.fori_loop(first_unused, n_tiles, lambda k, c: (unused_tile_copy(k).start(), c)[1], 0)
        lax.fori_loop(first_unused, n_tiles, lambda k, c: (unused_tile_copy(k).wait(), c)[1], 0)

    for r in range(tm):
        pltpu.make_async_copy(u2_ref.at[pl.ds(r, 1)], xs_ref.at[pl.ds(dest_ref[t0 + r], 1)],
                              sem).start(priority=r % DMA_PRIORITIES)
    wg_bf_ref[0] = wg_ref[0, 0].astype(BF16)
    wu_bf_ref[0] = wu_ref[0, 0].astype(BF16)
    wd_bf_ref[0] = wd_ref[0, 0].astype(BF16)
    pltpu.make_async_copy(u2_ref, xs_ref.at[pl.ds(0, tm)], sem).wait()


def _dispatch_call(dest, bucket_ends, u2ext, n_rows, w_gate, w_up, w_down):
    t, w = u2ext.shape
    _, n_exp, d, de = w_gate.shape
    tm = t // n_exp
    assert t % n_exp == 0 and tm % ISSUE_UNROLL == 0
    w_in = lambda shape: pl.BlockSpec((1, 1) + shape, lambda i, dr, er: (0, i, 0, 0))
    w_out = lambda shape: pl.BlockSpec((1,) + shape, lambda i, dr, er: (i, 0, 0))
    grid_spec = pltpu.PrefetchScalarGridSpec(
        num_scalar_prefetch=2,
        grid=(n_exp,),
        in_specs=[pl.BlockSpec((tm, w), lambda i, dr, er: (i, 0)),
                  w_in((d, de)), w_in((d, de)), w_in((de, d))],
        out_specs=[pl.BlockSpec(memory_space=pl.ANY),
                   w_out((d, de)), w_out((d, de)), w_out((de, d))],
        scratch_shapes=[pltpu.VMEM((TM_E, w), F32), pltpu.SemaphoreType.DMA(()),
                        pltpu.SemaphoreType.DMA(())],
    )
    return pl.pallas_call(
        _dispatch_kernel,
        grid_spec=grid_spec,
        out_shape=[jax.ShapeDtypeStruct((n_rows, w), F32),
                   jax.ShapeDtypeStruct((n_exp, d, de), BF16),
                   jax.ShapeDtypeStruct((n_exp, d, de), BF16),
                   jax.ShapeDtypeStruct((n_exp, de, d), BF16)],
        compiler_params=pltpu.CompilerParams(
            dimension_semantics=("arbitrary",), vmem_limit_bytes=VMEM_LIMIT),
        name="dispatch",
    )(dest, bucket_ends, u2ext, w_gate, w_up, w_down)


def _experts_kernel(ea_ref, eb_ref, nv_ref, xs_ref, wga_ref, wua_ref, wda_ref,
                    wgb_ref, wub_ref, wdb_ref, ys_ref):
    del ea_ref, eb_ref
    i = pl.program_id(0)
    d = ys_ref.shape[1]

    @pl.when(i < nv_ref[0])
    def _():
        x = xs_ref[:, :d].astype(BF16)

        def expert(wg_ref, wu_ref, wd_ref):
            g = _dot(x, wg_ref[0])
            h = (g * _sigmoid(g)) * _dot(x, wu_ref[0])
            return _dot(h.astype(BF16), wd_ref[0])

        ya = xs_ref[:, d:d + 1] * expert(wga_ref, wua_ref, wda_ref)
        ys_ref[...] = ya + xs_ref[:, d + 1:d + 2] * expert(wgb_ref, wub_ref, wdb_ref)

    @pl.when(i >= nv_ref[0])
    def _():
        ys_ref[...] = jnp.zeros_like(ys_ref)


def _experts_call(tile_ea, tile_eb, n_valid, xs, wg, wu, wd):
    tp, w = xs.shape
    _, d, de = wg.shape
    row = lambda i, ea, eb, nv: (jnp.minimum(i, nv[0] - 1), 0)
    wa = lambda i, ea, eb, nv: (ea[i], 0, 0)
    wb = lambda i, ea, eb, nv: (eb[i], 0, 0)
    grid_spec = pltpu.PrefetchScalarGridSpec(
        num_scalar_prefetch=3,
        grid=(tp // TM_E,),
        in_specs=[pl.BlockSpec((TM_E, w), row),
                  pl.BlockSpec((1, d, de), wa), pl.BlockSpec((1, d, de), wa),
                  pl.BlockSpec((1, de, d), wa),
                  pl.BlockSpec((1, d, de), wb), pl.BlockSpec((1, d, de), wb),
                  pl.BlockSpec((1, de, d), wb)],
        out_specs=pl.BlockSpec((TM_E, d), lambda i, ea, eb, nv: (i, 0)),
    )
    return pl.pallas_call(
        _experts_kernel,
        grid_spec=grid_spec,
        out_shape=jax.ShapeDtypeStruct((tp, d), F32),
        compiler_params=pltpu.CompilerParams(
            dimension_semantics=("arbitrary",), vmem_limit_bytes=VMEM_LIMIT),
        name="experts",
    )(tile_ea, tile_eb, n_valid, xs, wg, wu, wd, wg, wu, wd)


def _final_kernel(dest_ref, x1_ref, mod_ref, g2_ref, b2_ref, ys_ref, o_ref, ybuf, sem):
    tm = x1_ref.shape[0]
    i = pl.program_id(0)
    slot = i % GATHER_AHEAD

    def issue_rows(tile, to_slot, r0, n):
        for r in range(n):
            pltpu.make_async_copy(
                ys_ref.at[pl.ds(dest_ref[tile * tm + r0 + r], 1)],
                ybuf.at[to_slot, pl.ds(r0 + r, 1)], sem.at[to_slot],
            ).start(priority=(r0 + r) % DMA_PRIORITIES)

    def normalise_rows(r0, n):
        rows = pl.ds(r0, n)
        z = DEEPNORM_ALPHA * x1_ref[rows, :] + (1.0 + mod_ref[0, 5:6, :]) * ybuf[slot, rows, :]
        o_ref[rows, :] = _layer_norm(z) * g2_ref[...] + b2_ref[...]

    n_chunks = tm // ISSUE_UNROLL

    @pl.when(i == 0)
    def _():
        for tile in range(GATHER_AHEAD):
            issue_rows(tile, tile, 0, tm)

    pltpu.make_async_copy(ys_ref.at[pl.ds(0, tm)], ybuf.at[slot], sem.at[slot]).wait()

    def chunk(c, issue_ahead):
        r0 = c * ISSUE_UNROLL
        normalise_rows(r0, ISSUE_UNROLL)
        if issue_ahead:
            issue_rows(i + GATHER_AHEAD, slot, r0, ISSUE_UNROLL)

    more = i + GATHER_AHEAD < pl.num_programs(0)

    @pl.when(more)
    def _():
        for c in range(n_chunks):
            chunk(c, True)

    @pl.when(jnp.logical_not(more))
    def _():
        for c in range(n_chunks):
            chunk(c, False)


def _final_call(dest, x1, mod3, g2, b2, ys, seq):
    t, d = x1.shape
    per_b = seq // TM
    grid_spec = pltpu.PrefetchScalarGridSpec(
        num_scalar_prefetch=1,
        grid=(t // TM,),
        in_specs=[pl.BlockSpec((TM, d), lambda i, dr: (i, 0)),
                  pl.BlockSpec((1, 6, d), lambda i, dr: (i // per_b, 0, 0)),
                  pl.BlockSpec((1, d), lambda i, dr: (0, 0)),
                  pl.BlockSpec((1, d), lambda i, dr: (0, 0)),
                  pl.BlockSpec(memory_space=pl.ANY)],
        out_specs=pl.BlockSpec((TM, d), lambda i, dr: (i, 0)),
        scratch_shapes=[pltpu.VMEM((GATHER_AHEAD, TM, d), F32),
                        pltpu.SemaphoreType.DMA((GATHER_AHEAD,))],
    )
    return pl.pallas_call(
        _final_kernel,
        grid_spec=grid_spec,
        out_shape=jax.ShapeDtypeStruct((t, d), F32),
        compiler_params=pltpu.CompilerParams(
            dimension_semantics=("arbitrary",), vmem_limit_bytes=VMEM_LIMIT),
        name="final",
    )(dest, x1, mod3, g2, b2, ys)


def _gate_pairs(w_rg, w_ig):
    def pairs(w):
        n, bs, _ = w.shape
        w = w.reshape(n // 2, 2, bs, bs)
        z = jnp.zeros((n // 2, bs, bs), w.dtype)
        top = jnp.concatenate([w[:, 0], z], axis=2)
        bot = jnp.concatenate([z, w[:, 1]], axis=2)
        return jnp.concatenate([top, bot], axis=1)
    return jnp.concatenate([pairs(w_rg), pairs(w_ig)], axis=2).astype(BF16)


def kernel(x, c, w_ada, b_ada, w_in, b_f, conv_w, conv_b, w_rg, b_rg, w_ig, b_ig, lru_lambda,
           g_attn, g_lru, w_out, ln1_g, ln1_b, w_grp, b_grp, w_exp, b_exp,
           w_e_gate, w_e_up, w_e_down, ln2_g, ln2_b):
    assert w_ada.shape[0] == DEPTH
    b, s, d = x.shape
    t = b * s
    d_attn = N_HEADS * HEAD_DIM
    d_lru = conv_w.shape[2]
    n_exp = w_e_gate.shape[1]
    assert s % TM == 0 and s % TQ == 0 and t % TM_E == 0

    c_pad = jnp.pad(c, ((0, SUBLANES - b), (0, 0)))
    mod = _mod_call(c_pad, w_ada, b_ada[0][None, :])
    mod3 = mod[:b].reshape(b, 6, d)

    bf_pad = jnp.pad(b_f[0], (0, LANES - N_HEADS))[None, :]
    q_aug, k_aug, vt_aug, xb, gb = _inproj_call(
        x, mod3, jnp.swapaxes(w_in, 1, 2), bf_pad, d_lru)
    attn_t = _attn_call(q_aug, k_aug, vt_aug)
    lru = _lru_call(xb, gb, conv_w[0], conv_b[0][None, :], _gate_pairs(w_rg[0], w_ig[0]),
                    b_rg[0][None, :], b_ig[0][None, :], lru_lambda[0][None, :])

    n_route = N_GROUPS + n_exp
    wr = jnp.pad(jnp.concatenate([w_grp[0], w_exp[0]], axis=1).T,
                 ((0, BUCKET_ROWS - n_route), (0, 0)))
    br = jnp.pad(jnp.concatenate([b_grp[0], b_exp[0]]), (0, BUCKET_ROWS - n_route))
    br = jnp.broadcast_to(br[:, None], (BUCKET_ROWS, LANES))
    x1, u2ext, rinfo, counts = _mixout_call(
        attn_t, lru.reshape(t, d_lru), x.reshape(t, d), mod3,
        g_attn[0][None, :], g_lru[0][None, :], w_out,
        ln1_g[0][None, :], ln1_b[0][None, :], wr, br, s)

    dest = _rank_call(rinfo, counts).reshape(t)
    cnt = counts[:N_BUCKETS, 0].astype(jnp.int32)
    ends = jnp.cumsum((cnt + (TM_E - 1)) // TM_E)
    n_tiles = t // TM_E + N_BUCKETS
    tile_bucket = jnp.sum(ends[None, :] <= jnp.arange(n_tiles)[:, None], axis=1)
    tile_bucket = jnp.minimum(tile_bucket, N_BUCKETS - 1)
    n_valid = ends[N_BUCKETS - 1:]
    last_bucket = tile_bucket[jnp.maximum(n_valid[0] - 1, 0)]
    tile_bucket = jnp.where(jnp.arange(n_tiles) < n_valid[0], tile_bucket, last_bucket)
    tile_ea = jnp.asarray(_BUCKET_EA, jnp.int32)[tile_bucket]
    tile_eb = jnp.asarray(_BUCKET_EB, jnp.int32)[tile_bucket]

    xs, wg_bf, wu_bf, wd_bf = _dispatch_call(
        dest, (ends * TM_E).astype(jnp.int32), u2ext, n_tiles * TM_E, w_e_gate, w_e_up, w_e_down)
    ys = _experts_call(tile_ea, tile_eb, n_valid.astype(jnp.int32), xs, wg_bf, wu_bf, wd_bf)
    out = _final_call(dest, x1, mod3, ln2_g[0][None, :], ln2_b[0][None, :], ys, s)
    return out.reshape(b, s, d)
```

```python
import functools

import jax
import jax.numpy as jnp
from jax import lax
from jax.experimental import pallas as pl
from jax.experimental.pallas import tpu as pltpu

F32 = jnp.float32
BF16 = jnp.bfloat16

HEAD_DIM = 64
N_HEADS = 8
CONV_WIDTH = 4
LRU_C = 8.0
N_GROUPS = 4
EXPERTS_PER_GROUP = 4
N_PAIRS = 6
N_BUCKETS = N_GROUPS * N_PAIRS
LN_EPS = 1e-5
RMS_EPS = 1e-6
NEG_INF = -1e30
DEPTH = 1
DEEPNORM_ALPHA = (2.0 * DEPTH) ** 0.25
LOG2E = 1.4426950408889634

LANES = 128
SUBLANES = 8
HEAD_PAD = LANES
VT_ROWS = 80
BUCKET_ROWS = 32
TM = 512
TQ = 512
TM_E = 256
ISSUE_UNROLL = 64
GATHER_AHEAD = 2
RANK_CHUNKS = 4
DMA_PRIORITIES = 2
VMEM_BYTES_V7X = 64 * 1024 * 1024
VMEM_LIMIT = VMEM_BYTES_V7X // 8 * 7

_PAIRS = [(0, 1), (0, 2), (0, 3), (1, 2), (1, 3), (2, 3)]
_BUCKET_EA = [g * EXPERTS_PER_GROUP + a for g in range(N_GROUPS) for (a, b) in _PAIRS]
_BUCKET_EB = [g * EXPERTS_PER_GROUP + b for g in range(N_GROUPS) for (a, b) in _PAIRS]


def _dot(a, b):
    return jnp.dot(a, b, preferred_element_type=F32)


def _split2(a):
    hi = a.astype(BF16)
    lo = (a - hi.astype(F32)).astype(BF16)
    return hi, lo


def _split3(a):
    hi = a.astype(BF16)
    r = a - hi.astype(F32)
    mid = r.astype(BF16)
    lo = (r - mid.astype(F32)).astype(BF16)
    return hi, mid, lo


def _layer_norm(x):
    mu = jnp.mean(x, axis=-1, keepdims=True)
    xc = x - mu
    var = jnp.mean(xc * xc, axis=-1, keepdims=True)
    return xc * lax.rsqrt(var + LN_EPS)


def _sigmoid(x):
    return 0.5 * jnp.tanh(0.5 * x) + 0.5


def _rot_in_group(x, k):
    n, w = x.shape
    return pltpu.roll(x.reshape(n // SUBLANES, SUBLANES, w), k, 1).reshape(n, w)


def _log_sigmoid(z):
    return jnp.minimum(z, 0.0) - jnp.log1p(jnp.exp(-jnp.abs(z)))


def _cumsum_rows(x, carry):
    n, w = x.shape
    row_in_group = lax.broadcasted_iota(jnp.int32, (n, w), 0) % SUBLANES
    k = 1
    while k < SUBLANES:
        x = x + jnp.where(row_in_group >= k, _rot_in_group(x, k), 0.0)
        k *= 2
    groups = []
    for g in range(n // SUBLANES):
        blk = x[g * SUBLANES:(g + 1) * SUBLANES] + carry
        groups.append(blk)
        carry = blk[SUBLANES - 1:SUBLANES]
    return jnp.concatenate(groups, axis=0)


def _shift_rows(x, k, fill):
    n = x.shape[0]
    if k % SUBLANES == 0:
        return jnp.concatenate([jnp.full((k, x.shape[1]), fill, x.dtype), x[:n - k]], axis=0)
    row = lax.broadcasted_iota(jnp.int32, x.shape, 0)
    return jnp.where(row >= k, pltpu.roll(x, k, 0), fill)


def _mod_kernel(c_ref, w_ref, b_ref, o_ref):
    c = c_ref[...]
    s = c * _sigmoid(c)
    sh, sl = _split2(s)
    wh, wl = _split2(w_ref[0])
    o_ref[...] = _dot(sh, wh) + (_dot(sh, wl) + _dot(sl, wh)) + b_ref[...]


def _mod_call(c_pad, w_ada, b_ada):
    rows, d = c_pad.shape
    n = w_ada.shape[2]
    return pl.pallas_call(
        _mod_kernel,
        grid=(n // d,),
        in_specs=[pl.BlockSpec((rows, d), lambda j: (0, 0)),
                  pl.BlockSpec((1, d, d), lambda j: (0, 0, j)),
                  pl.BlockSpec((1, d), lambda j: (0, j))],
        out_specs=pl.BlockSpec((rows, d), lambda j: (0, j)),
        out_shape=jax.ShapeDtypeStruct((rows, n), F32),
        compiler_params=pltpu.CompilerParams(vmem_limit_bytes=VMEM_LIMIT),
        name="mod",
    )(c_pad, w_ada, b_ada)


def _inproj_kernel(x_ref, mod_ref, wt_ref, bf_ref,
                   q_ref, k_ref, vt_ref, xb_ref, gb_ref, carry_ref, v_scr, w_scr, stage, wsem):
    j = pl.program_id(1)
    d_attn = N_HEADS * HEAD_DIM
    d_lru = xb_ref.shape[2]
    c_k, c_v, c_x = d_attn, 2 * d_attn, 3 * d_attn
    c_g, c_f = c_x + d_lru, c_x + 2 * d_lru
    r_f = 3 * d_attn
    r_x, r_g = r_f + N_HEADS, r_f + N_HEADS + d_lru

    @pl.when(jnp.logical_and(pl.program_id(0) == 0, j == 0))
    def _():
        blocks = [(0, d_attn, 0, HEAD_DIM ** -0.5 * LOG2E, None), (d_attn, d_attn, c_k, None, None),
                  (2 * d_attn, d_attn, c_v, None, None), (r_x, d_lru, c_x, None, None),
                  (r_g, d_lru, c_g, None, None), (r_f, LANES, c_f, None, N_HEADS)]
        copies = [pltpu.make_async_copy(wt_ref.at[0, pl.ds(r0, n), :],
                                        stage.at[k, pl.ds(0, n), :], wsem.at[k])
                  for k, (r0, n, _, _, _) in enumerate(blocks)]
        for copy in copies:
            copy.start()
        for k, (_, n, c0, scale, keep_lanes) in enumerate(blocks):
            copies[k].wait()
            blk = stage[k, 0:n, :]
            if scale is not None:
                blk = blk * scale
            blk = blk.T
            if keep_lanes is not None:
                lane = lax.broadcasted_iota(jnp.int32, blk.shape, 1)
                blk = jnp.where(lane < keep_lanes, blk, 0.0)
            w_scr[:, c0:c0 + n] = blk.astype(BF16)

    @pl.when(j == 0)
    def _():
        carry_ref[...] = jnp.zeros_like(carry_ref)

    tm = x_ref.shape[1]
    u = _layer_norm(x_ref[0]) * (1.0 + mod_ref[0, 1:2, :]) + mod_ref[0, 0:1, :]
    ub = u.astype(BF16)

    logf = _log_sigmoid(_dot(ub, w_scr[:, c_f:c_f + LANES]) + bf_ref[...])
    cum = _cumsum_rows(logf, carry_ref[...])
    carry_ref[...] = cum[tm - 1:tm, :]
    c_hi, c_mid, c_lo = [p.astype(F32) for p in _split3(cum * LOG2E)]

    lane = lax.broadcasted_iota(jnp.int32, (tm, HEAD_PAD), 1)
    d = HEAD_DIM
    q_all = _dot(ub, w_scr[:, 0:c_k])
    k_all = _dot(ub, w_scr[:, c_k:c_v])
    for h in range(N_HEADS):
        ts = slice((h // 2) * LANES, (h // 2 + 1) * LANES)
        qh, kh = q_all[:, ts], k_all[:, ts]
        if h % 2:
            qh, kh = pltpu.roll(qh, d, 1), pltpu.roll(kh, d, 1)
        hi, mid, lo = c_hi[:, h:h + 1], c_mid[:, h:h + 1], c_lo[:, h:h + 1]
        q_ext = jnp.where(lane == d, hi, jnp.where(lane == d + 1, mid, jnp.where(
            lane == d + 2, lo, jnp.where(lane < d + 6, 1.0, 0.0))))
        k_ext = jnp.where(lane < d + 3, 1.0, jnp.where(lane == d + 3, -hi, jnp.where(
            lane == d + 4, -mid, jnp.where(lane == d + 5, -lo, 0.0))))
        q_ref[0, h] = jnp.where(lane < d, qh, q_ext).astype(BF16)
        k_ref[0, h] = jnp.where(lane < d, kh, k_ext).astype(BF16)

    v_scr[...] = _dot(ub, w_scr[:, c_v:c_x])
    vt_all = v_scr[...].T
    pad_rows = lax.broadcasted_iota(jnp.int32, (VT_ROWS - d, tm), 0)
    ones_row = jnp.where(pad_rows == 0, 1.0, 0.0).astype(BF16)
    for h in range(N_HEADS):
        vt_ref[0, h, 0, :d, :] = vt_all[h * d:(h + 1) * d, :].astype(BF16)
        vt_ref[0, h, 0, d:, :] = ones_row

    xb_ref[0] = _dot(ub, w_scr[:, c_x:c_g])
    gb_ref[0] = _dot(ub, w_scr[:, c_g:c_f])


def _inproj_call(x, mod3, w_in_t, bf_pad, d_lru):
    b, s, d = x.shape
    d_attn = N_HEADS * HEAD_DIM
    assert w_in_t.shape[1] == 3 * d_attn + N_HEADS + 2 * d_lru
    full = lambda shape: pl.BlockSpec(shape, lambda bi, j: (0,) * len(shape))
    head_spec = pl.BlockSpec((1, N_HEADS, TM, HEAD_PAD), lambda bi, j: (bi, 0, j, 0))
    vt_spec = pl.BlockSpec((1, N_HEADS, 1, VT_ROWS, TM), lambda bi, j: (bi, 0, j, 0, 0))
    row_spec = pl.BlockSpec((1, TM, d_lru), lambda bi, j: (bi, j, 0))
    head_shape = jax.ShapeDtypeStruct((b, N_HEADS, s, HEAD_PAD), BF16)
    return pl.pallas_call(
        _inproj_kernel,
        grid=(b, s // TM),
        in_specs=[pl.BlockSpec((1, TM, d), lambda bi, j: (bi, j, 0)),
                  pl.BlockSpec((1, 6, d), lambda bi, j: (bi, 0, 0)),
                  pl.BlockSpec(memory_space=pl.ANY), full((1, LANES))],
        out_specs=[head_spec, head_spec, vt_spec, row_spec, row_spec],
        out_shape=[head_shape, head_shape,
                   jax.ShapeDtypeStruct((b, N_HEADS, s // TM, VT_ROWS, TM), BF16),
                   jax.ShapeDtypeStruct((b, s, d_lru), F32),
                   jax.ShapeDtypeStruct((b, s, d_lru), F32)],
        scratch_shapes=[pltpu.VMEM((1, LANES), F32), pltpu.VMEM((TM, d_attn), F32),
                        pltpu.VMEM((d, 3 * d_attn + 2 * d_lru + LANES), BF16),
                        pltpu.VMEM((6, max(d_attn, d_lru), d), F32),
                        pltpu.SemaphoreType.DMA((6,))],
        compiler_params=pltpu.CompilerParams(
            dimension_semantics=("arbitrary", "arbitrary"), vmem_limit_bytes=VMEM_LIMIT),
        name="inproj",
    )(x, mod3, w_in_t, bf_pad)


def _attn_kernel(q_ref, k_ref, vt_ref, o_ref, s_scr, smax_scr, m_scr, acc_scr):
    tq = TQ
    nq = q_ref.shape[2] // tq
    heads = range(q_ref.shape[1])

    def scores_to(slot, i, j):
        for hh in heads:
            k = k_ref[0, hh, pl.ds(pl.multiple_of(j * tq, tq), tq), :]
            q = q_ref[0, hh, pl.ds(pl.multiple_of(i * tq, tq), tq), :]
            s = lax.dot_general(k, q, (((1,), (1,)), ((), ())), preferred_element_type=F32)
            s_scr[slot, hh] = s
            smax_scr[slot, hh] = jnp.max(s, axis=0, keepdims=True)

    def consume(slot, j, masked):
        for hh in heads:
            s = s_scr[slot, hh]
            if masked:
                key = lax.broadcasted_iota(jnp.int32, (tq, tq), 0)
                qry = lax.broadcasted_iota(jnp.int32, (tq, tq), 1)
                s = jnp.where(key <= qry, s, NEG_INF)
                s_max = jnp.max(s, axis=0, keepdims=True)
            else:
                s_max = smax_scr[slot, hh]
            m = m_scr[hh]
            m_new = jnp.maximum(m, s_max)
            p = jnp.exp2(s - m_new).astype(BF16)
            acc_scr[hh] = jnp.exp2(m - m_new) * acc_scr[hh] + _dot(vt_ref[0, hh, j], p)
            m_scr[hh] = m_new

    scores_to(2, 0, 0)

    def query_block(i, carry):
        m_scr[...] = jnp.full(m_scr.shape, NEG_INF, F32)
        acc_scr[...] = jnp.zeros(acc_scr.shape, F32)
        nxt = jnp.minimum(i + 1, nq - 1)
        n_mid = i - 1

        @pl.when(i == 0)
        def _():
            consume(2, 0, True)
            scores_to(2, nxt, 0)

        @pl.when(i >= 1)
        def _():
            scores_to(1, i, 1)
            consume(2, 0, False)

        def pair(kk, c):
            j = 1 + 2 * kk
            scores_to(0, i, j + 1)
            consume(1, j, False)
            scores_to(1, i, j + 2)
            consume(0, j + 1, False)
            return c

        lax.fori_loop(0, jnp.maximum(n_mid, 0) // 2, pair, 0)

        @pl.when(jnp.logical_and(i >= 1, n_mid % 2 == 1))
        def _():
            scores_to(0, i, i)
            consume(1, i - 1, False)
            scores_to(2, nxt, 0)
            consume(0, i, True)

        @pl.when(jnp.logical_and(i >= 1, n_mid % 2 == 0))
        def _():
            scores_to(2, nxt, 0)
            consume(1, i, True)

        for hh in heads:
            o_ref[0, 0, i, hh * HEAD_DIM:(hh + 1) * HEAD_DIM, :] = (
                acc_scr[hh, :HEAD_DIM, :] / acc_scr[hh, HEAD_DIM:HEAD_DIM + 1, :])
        return carry

    lax.fori_loop(0, nq, query_block, 0)


def _attn_call(q_aug, k_aug, vt_aug):
    b, h, s, hp = q_aug.shape
    assert TQ == TM
    nq = s // TQ
    return pl.pallas_call(
        _attn_kernel,
        grid=(b, h // 2),
        in_specs=[pl.BlockSpec((1, 2, s, hp), lambda bi, p: (bi, p, 0, 0)),
                  pl.BlockSpec((1, 2, s, hp), lambda bi, p: (bi, p, 0, 0)),
                  pl.BlockSpec((1, 2, nq, VT_ROWS, TQ), lambda bi, p: (bi, p, 0, 0, 0))],
        out_specs=pl.BlockSpec((1, 1, nq, 2 * HEAD_DIM, TQ), lambda bi, p: (bi, p, 0, 0, 0)),
        out_shape=jax.ShapeDtypeStruct((b, h // 2, nq, 2 * HEAD_DIM, TQ), F32),
        scratch_shapes=[pltpu.VMEM((3, 2, TQ, TQ), F32), pltpu.VMEM((3, 2, 1, TQ), F32),
                        pltpu.VMEM((2, 1, TQ), F32), pltpu.VMEM((2, VT_ROWS, TQ), F32)],
        compiler_params=pltpu.CompilerParams(
            dimension_semantics=("arbitrary", "arbitrary"), vmem_limit_bytes=VMEM_LIMIT),
        name="attention",
    )(q_aug, k_aug, vt_aug)


def _gelu_tanh(x):
    return 0.5 * x * (1.0 + jnp.tanh(0.7978845608028654 * (x + 0.044715 * (x * x * x))))


def _lru_kernel(xb_ref, gb_ref, cw_ref, cb_ref, wgate_ref, brg_ref, big_ref, lam_ref,
                o_ref, tail_ref, h_ref):
    j = pl.program_id(1)

    @pl.when(j == 0)
    def _():
        tail_ref[...] = jnp.zeros_like(tail_ref)
        h_ref[...] = jnp.zeros_like(h_ref)

    x = xb_ref[0]
    tm, dl = x.shape
    row_in_group = lax.broadcasted_iota(jnp.int32, (tm, dl), 0) % SUBLANES
    x_prev_group = jnp.concatenate([tail_ref[...], x[:tm - SUBLANES]], axis=0)
    xc = x * cw_ref[CONV_WIDTH - 1:CONV_WIDTH, :] + cb_ref[...]
    for k in range(1, CONV_WIDTH):
        xs = jnp.where(row_in_group < k, _rot_in_group(x_prev_group, k), _rot_in_group(x, k))
        xc = xc + xs * cw_ref[CONV_WIDTH - 1 - k:CONV_WIDTH - k, :]
    tail_ref[...] = x[tm - SUBLANES:, :]

    xcb = xc.astype(BF16)
    n_pairs = dl // LANES
    r_parts, i_parts = [], []
    for p in range(n_pairs):
        g = _dot(xcb[:, p * LANES:(p + 1) * LANES], wgate_ref[p])
        r_parts.append(g[:, :LANES])
        i_parts.append(g[:, LANES:])
    r = _sigmoid(jnp.concatenate(r_parts, axis=1) + brg_ref[...])
    ig = _sigmoid(jnp.concatenate(i_parts, axis=1) + big_ref[...])

    lam = lam_ref[...]
    softplus_neg_lam = jnp.maximum(-lam, 0.0) + jnp.log1p(jnp.exp(-jnp.abs(lam)))
    log_a = (-LRU_C) * r * softplus_neg_lam
    a = jnp.exp(log_a)
    v = 1.0 - a * a
    u = jnp.where(v > 0.0, v * lax.rsqrt(v), 0.0) * (ig * xc)

    k = 1
    while k < SUBLANES:
        keep = row_in_group >= k
        u = a * jnp.where(keep, _rot_in_group(u, k), 0.0) + u
        a = a * jnp.where(keep, _rot_in_group(a, k), 1.0)
        k *= 2
    h_prev = h_ref[...]
    groups = []
    for g in range(tm // SUBLANES):
        rows = slice(g * SUBLANES, (g + 1) * SUBLANES)
        hg = a[rows] * h_prev + u[rows]
        groups.append(hg)
        h_prev = hg[SUBLANES - 1:SUBLANES]
    h_ref[...] = h_prev
    o_ref[0] = jnp.concatenate(groups, axis=0) * _gelu_tanh(gb_ref[0])


def _lru_call(xb, gb, conv_w, conv_b, wgate, b_rg, b_ig, lam):
    b, s, dl = xb.shape
    row_spec = pl.BlockSpec((1, TM, dl), lambda bi, j: (bi, j, 0))
    full = lambda shape: pl.BlockSpec(shape, lambda bi, j: (0,) * len(shape))
    return pl.pallas_call(
        _lru_kernel,
        grid=(b, s // TM),
        in_specs=[row_spec, row_spec, full(conv_w.shape), full((1, dl)), full(wgate.shape),
                  full((1, dl)), full((1, dl)), full((1, dl))],
        out_specs=row_spec,
        out_shape=jax.ShapeDtypeStruct((b, s, dl), F32),
        scratch_shapes=[pltpu.VMEM((SUBLANES, dl), F32), pltpu.VMEM((1, dl), F32)],
        compiler_params=pltpu.CompilerParams(
            dimension_semantics=("arbitrary", "arbitrary"), vmem_limit_bytes=VMEM_LIMIT),
        name="lru",
    )(xb, gb, conv_w, conv_b, wgate, b_rg, b_ig, lam)


def _rms(x, gain):
    return x * lax.rsqrt(jnp.mean(x * x, axis=-1, keepdims=True) + RMS_EPS) * gain


def _mixout_kernel(attn_ref, lru_ref, x_ref, mod_ref, ga_ref, gl_ref, woa_ref, wol_ref,
                   g1_ref, b1_ref, wr_ref, br_ref, x1_ref, u2_ref, rinfo_ref, cnt_ref, wo_bf):
    tm, d = x_ref.shape

    @pl.when(pl.program_id(0) == 0)
    def _():
        wo_bf[0] = woa_ref[0].astype(BF16)
        wo_bf[1] = wol_ref[0].astype(BF16)

    attn = attn_ref[0, :, 0].reshape(ga_ref.shape[1], tm).T
    na = _rms(attn, ga_ref[...]).astype(BF16)
    nl = _rms(lru_ref[...], gl_ref[...]).astype(BF16)
    mix = _dot(na, wo_bf[0]) + _dot(nl, wo_bf[1])
    z = DEEPNORM_ALPHA * x_ref[...] + (1.0 + mod_ref[0, 2:3, :]) * mix
    x1 = _layer_norm(z) * g1_ref[...] + b1_ref[...]
    x1_ref[...] = x1
    u2 = _layer_norm(x1) * (1.0 + mod_ref[0, 4:5, :]) + mod_ref[0, 3:4, :]

    nt = (((1,), (1,)), ((), ()))
    uh, ul = _split2(u2)
    wh, wl = _split2(wr_ref[...])
    dg = lambda a, b_: lax.dot_general(a, b_, nt, preferred_element_type=F32)
    lg = dg(wh, uh) + (dg(wh, ul) + dg(wl, uh)) + br_ref[:, 0:1]

    def first_index(vals, target):
        idx = jnp.full_like(target, float(len(vals) - 1))
        for n in range(len(vals) - 2, -1, -1):
            idx = jnp.where(vals[n] == target, float(n), idx)
        return idx

    g = [lg[n:n + 1, :] for n in range(N_GROUPS)]
    gmax = functools.reduce(jnp.maximum, g)
    gsum = functools.reduce(lambda a, b_: a + b_, [jnp.exp(v - gmax) for v in g])
    grp_w = 1.0 / gsum
    gidx = first_index(g, gmax)

    sel = []
    for e in range(EXPERTS_PER_GROUP):
        v = lg[N_GROUPS + (N_GROUPS - 1) * EXPERTS_PER_GROUP + e:
               N_GROUPS + (N_GROUPS - 1) * EXPERTS_PER_GROUP + e + 1, :]
        for gi in range(N_GROUPS - 2, -1, -1):
            r0 = N_GROUPS + gi * EXPERTS_PER_GROUP + e
            v = jnp.where(gidx == float(gi), lg[r0:r0 + 1, :], v)
        sel.append(v)
    smax = functools.reduce(jnp.maximum, sel)
    i1 = first_index(sel, smax)
    rest = [jnp.where(i1 == float(e), -3e38, sel[e]) for e in range(EXPERTS_PER_GROUP)]
    rmax = functools.reduce(jnp.maximum, rest)
    i2 = first_index(rest, rmax)
    e2 = jnp.exp(rmax - smax)
    w1 = grp_w / (1.0 + e2)
    w2 = grp_w * e2 / (1.0 + e2)
    ia = jnp.minimum(i1, i2)
    ib = jnp.maximum(i1, i2)
    wa = jnp.where(i1 < i2, w1, w2)
    wb = jnp.where(i1 < i2, w2, w1)
    pair = jnp.where(ia == 0.0, ib - 1.0, jnp.where(ia == 1.0, ib + 1.0, 5.0))
    bucket = gidx * float(N_PAIRS) + pair

    @pl.when(pl.program_id(0) == 0)
    def _():
        cnt_ref[...] = jnp.zeros_like(cnt_ref)

    cid = lax.broadcasted_iota(jnp.int32, (BUCKET_ROWS, tm), 0).astype(F32)
    cnt_ref[...] += jnp.sum(jnp.where(cid == bucket, 1.0, 0.0), axis=1, keepdims=True)

    zrow = jnp.zeros_like(wa)
    rinfo_ref[...] = jnp.concatenate([bucket, wa, wb] + [zrow] * (SUBLANES - 3), axis=0)
    wt = jnp.concatenate([wa, wb, jnp.zeros((LANES - 2, tm), F32)], axis=0)
    u2_ref[:, :d] = u2
    u2_ref[:, d:] = wt.T


def _mixout_call(attn_t, lru, x2d, mod3, ga, gl, wo, g1, b1, wr, br, seq):
    t, d = x2d.shape
    _, n_pairs, _, pair_w, _ = attn_t.shape
    dh = n_pairs * pair_w
    assert wo.shape[1] == 2 * dh and lru.shape[1] == dh
    per_b = seq // TM
    full = lambda shape: pl.BlockSpec(shape, lambda i: (0,) * len(shape))
    return pl.pallas_call(
        _mixout_kernel,
        grid=(t // TM,),
        in_specs=[pl.BlockSpec((1, n_pairs, 1, pair_w, TM),
                               lambda i: (i // per_b, 0, i % per_b, 0, 0)),
                  pl.BlockSpec((TM, dh), lambda i: (i, 0)),
                  pl.BlockSpec((TM, d), lambda i: (i, 0)),
                  pl.BlockSpec((1, 6, d), lambda i: (i // per_b, 0, 0)),
                  full((1, dh)), full((1, dh)),
                  pl.BlockSpec((1, dh, d), lambda i: (0, 0, 0)),
                  pl.BlockSpec((1, dh, d), lambda i: (0, 1, 0)),
                  full((1, d)), full((1, d)), full((BUCKET_ROWS, d)), full((BUCKET_ROWS, LANES))],
        out_specs=[pl.BlockSpec((TM, d), lambda i: (i, 0)),
                   pl.BlockSpec((TM, d + LANES), lambda i: (i, 0)),
                   pl.BlockSpec((SUBLANES, TM), lambda i: (0, i)),
                   pl.BlockSpec((BUCKET_ROWS, LANES), lambda i: (0, 0))],
        out_shape=[jax.ShapeDtypeStruct((t, d), F32),
                   jax.ShapeDtypeStruct((t, d + LANES), F32),
                   jax.ShapeDtypeStruct((SUBLANES, t), F32),
                   jax.ShapeDtypeStruct((BUCKET_ROWS, LANES), F32)],
        scratch_shapes=[pltpu.VMEM((2, dh, d), BF16)],
        compiler_params=pltpu.CompilerParams(
            dimension_semantics=("arbitrary",), vmem_limit_bytes=VMEM_LIMIT),
        name="mixout",
    )(attn_t, lru, x2d, mod3, ga, gl, wo, wo, g1, b1, wr, br)


def _rank_kernel(rinfo_ref, counts_ref, dest_ref, carry_ref, offs_ref):
    tm = rinfo_ref.shape[1]

    @pl.when(pl.program_id(0) == 0)
    def _():
        carry_ref[...] = jnp.zeros_like(carry_ref)
        padded = jnp.floor((counts_ref[...] + float(TM_E - 1)) * (1.0 / TM_E)) * float(TM_E)
        inc = padded
        k = 1
        while k < BUCKET_ROWS:
            inc = inc + _shift_rows(inc, k, 0.0)
            k *= 2
        offs_ref[...] = inc - padded

    srow = lax.broadcasted_iota(jnp.int32, (TM, TM), 0)
    scol = lax.broadcasted_iota(jnp.int32, (TM, TM), 1)
    upper = (srow <= scol).astype(BF16)
    cid = lax.broadcasted_iota(jnp.int32, (BUCKET_ROWS, TM), 0).astype(F32)
    carry = carry_ref[...]
    for c in range(tm // TM):
        cols = slice(c * TM, (c + 1) * TM)
        onehot = jnp.where(cid == rinfo_ref[0:1, cols], 1.0, 0.0)
        prefix = _dot(onehot.astype(BF16), upper)
        rank = jnp.sum(onehot * (prefix - 1.0 + carry[:, 0:1] + offs_ref[:, 0:1]),
                       axis=0, keepdims=True)
        dest_ref[:, cols] = rank.astype(jnp.int32)
        carry = carry + prefix[:, TM - 1:TM]
    carry_ref[...] = carry


def _rank_call(rinfo, counts):
    t = rinfo.shape[1]
    tm = TM * RANK_CHUNKS
    assert t % tm == 0
    return pl.pallas_call(
        _rank_kernel,
        grid=(t // tm,),
        in_specs=[pl.BlockSpec((SUBLANES, tm), lambda i: (0, i)),
                  pl.BlockSpec((BUCKET_ROWS, LANES), lambda i: (0, 0))],
        out_specs=pl.BlockSpec((1, tm), lambda i: (0, i)),
        out_shape=jax.ShapeDtypeStruct((1, t), jnp.int32),
        scratch_shapes=[pltpu.VMEM((BUCKET_ROWS, LANES), F32),
                        pltpu.VMEM((BUCKET_ROWS, LANES), F32)],
        compiler_params=pltpu.CompilerParams(
            dimension_semantics=("arbitrary",), vmem_limit_bytes=VMEM_LIMIT),
        name="rank",
    )(rinfo, counts)


def _dispatch_kernel(dest_ref, ends_ref, u2_ref, wg_ref, wu_ref, wd_ref,
                     xs_ref, wg_bf_ref, wu_bf_ref, wd_bf_ref, zbuf, sem, zsem):
    tm = u2_ref.shape[0]
    t0 = pl.program_id(0) * tm

    @pl.when(pl.program_id(0) == 0)
    def _():
        zbuf[...] = jnp.zeros_like(zbuf)

        def tail_copy(bkt):
            end = ends_ref[bkt]
            start = ends_ref[bkt - 1] if bkt else 0
            tail = pl.multiple_of(jnp.maximum(end - TM_E, 0), TM_E)
            return end > start, pltpu.make_async_copy(zbuf, xs_ref.at[pl.ds(tail, TM_E)], zsem)

        for bkt in range(N_BUCKETS):
            nonempty, copy = tail_copy(bkt)
            pl.when(nonempty)(copy.start)
        for bkt in range(N_BUCKETS):
            nonempty, copy = tail_copy(bkt)
            pl.when(nonempty)(copy.wait)

        def unused_tile_copy(k):
            return pltpu.make_async_copy(
                zbuf, xs_ref.at[pl.ds(pl.multiple_of(k * TM_E, TM_E), TM_E)], zsem)

        first_unused = ends_ref[N_BUCKETS - 1] // TM_E
        n_tiles = xs_ref.shape[0] // TM_E
        lax.fori_loop(first_unused, n_tiles, lambda k, c: (unused_tile_copy(k).start(), c)[1], 0)
        lax.fori_loop(first_unused, n_tiles, lambda k, c: (unused_tile_copy(k).wait(), c)[1], 0)

    for r in range(tm):
        pltpu.make_async_copy(u2_ref.at[pl.ds(r, 1)], xs_ref.at[pl.ds(dest_ref[t0 + r], 1)],
                              sem).start(priority=r % DMA_PRIORITIES)
    wg_bf_ref[0] = wg_ref[0, 0].astype(BF16)
    wu_bf_ref[0] = wu_ref[0, 0].astype(BF16)
    wd_bf_ref[0] = wd_ref[0, 0].astype(BF16)
    pltpu.make_async_copy(u2_ref, xs_ref.at[pl.ds(0, tm)], sem).wait()


def _dispatch_call(dest, bucket_ends, u2ext, n_rows, w_gate, w_up, w_down):
    t, w = u2ext.shape
    _, n_exp, d, de = w_gate.shape
    tm = t // n_exp
    assert t % n_exp == 0 and tm % ISSUE_UNROLL == 0
    w_in = lambda shape: pl.BlockSpec((1, 1) + shape, lambda i, dr, er: (0, i, 0, 0))
    w_out = lambda shape: pl.BlockSpec((1,) + shape, lambda i, dr, er: (i, 0, 0))
    grid_spec = pltpu.PrefetchScalarGridSpec(
        num_scalar_prefetch=2,
        grid=(n_exp,),
        in_specs=[pl.BlockSpec((tm, w), lambda i, dr, er: (i, 0)),
                  w_in((d, de)), w_in((d, de)), w_in((de, d))],
        out_specs=[pl.BlockSpec(memory_space=pl.ANY),
                   w_out((d, de)), w_out((d, de)), w_out((de, d))],
        scratch_shapes=[pltpu.VMEM((TM_E, w), F32), pltpu.SemaphoreType.DMA(()),
                        pltpu.SemaphoreType.DMA(())],
    )
    return pl.pallas_call(
        _dispatch_kernel,
        grid_spec=grid_spec,
        out_shape=[jax.ShapeDtypeStruct((n_rows, w), F32),
                   jax.ShapeDtypeStruct((n_exp, d, de), BF16),
                   jax.ShapeDtypeStruct((n_exp, d, de), BF16),
                   jax.ShapeDtypeStruct((n_exp, de, d), BF16)],
        compiler_params=pltpu.CompilerParams(
            dimension_semantics=("arbitrary",), vmem_limit_bytes=VMEM_LIMIT),
        name="dispatch",
    )(dest, bucket_ends, u2ext, w_gate, w_up, w_down)


def _experts_kernel(ea_ref, eb_ref, nv_ref, xs_ref, wga_ref, wua_ref, wda_ref,
                    wgb_ref, wub_ref, wdb_ref, ys_ref):
    del ea_ref, eb_ref
    i = pl.program_id(0)
    d = wda_ref.shape[2]

    @pl.when(i < nv_ref[0])
    def _():
        x = xs_ref[:, :d].astype(BF16)

        def expert(wg_ref, wu_ref, wd_ref):
            g = _dot(x, wg_ref[0])
            h = (g * _sigmoid(g)) * _dot(x, wu_ref[0])
            return _dot(h.astype(BF16), wd_ref[0])

        ya = xs_ref[:, d:d + 1] * expert(wga_ref, wua_ref, wda_ref)
        y = ya + xs_ref[:, d + 1:d + 2] * expert(wgb_ref, wub_ref, wdb_ref)
        ys_ref[...] = y.reshape(ys_ref.shape)

    @pl.when(i >= nv_ref[0])
    def _():
        ys_ref[...] = jnp.zeros_like(ys_ref)


def _experts_call(tile_ea, tile_eb, n_valid, xs, wg, wu, wd):
    tp, w = xs.shape
    _, d, de = wg.shape
    row = lambda i, ea, eb, nv: (jnp.minimum(i, nv[0] - 1), 0)
    wa = lambda i, ea, eb, nv: (ea[i], 0, 0)
    wb = lambda i, ea, eb, nv: (eb[i], 0, 0)
    grid_spec = pltpu.PrefetchScalarGridSpec(
        num_scalar_prefetch=3,
        grid=(tp // TM_E,),
        in_specs=[pl.BlockSpec((TM_E, w), row),
                  pl.BlockSpec((1, d, de), wa), pl.BlockSpec((1, d, de), wa),
                  pl.BlockSpec((1, de, d), wa),
                  pl.BlockSpec((1, d, de), wb), pl.BlockSpec((1, d, de), wb),
                  pl.BlockSpec((1, de, d), wb)],
        out_specs=pl.BlockSpec((TM_E, d // LANES, LANES), lambda i, ea, eb, nv: (i, 0, 0)),
    )
    return pl.pallas_call(
        _experts_kernel,
        grid_spec=grid_spec,
        out_shape=jax.ShapeDtypeStruct((tp, d // LANES, LANES), F32),
        compiler_params=pltpu.CompilerParams(
            dimension_semantics=("arbitrary",), vmem_limit_bytes=VMEM_LIMIT),
        name="experts",
    )(tile_ea, tile_eb, n_valid, xs, wg, wu, wd, wg, wu, wd)


def _final_kernel(dest_ref, x1_ref, mod_ref, g2_ref, b2_ref, ys_ref, o_ref, ybuf, sem):
    tm = x1_ref.shape[0]
    i = pl.program_id(0)
    slot = i % GATHER_AHEAD

    def issue_rows(tile, to_slot, r0, n):
        for r in range(n):
            pltpu.make_async_copy(
                ys_ref.at[pl.ds(dest_ref[tile * tm + r0 + r], 1)],
                ybuf.at[to_slot, pl.ds(r0 + r, 1)], sem.at[to_slot],
            ).start(priority=(r0 + r) % DMA_PRIORITIES)

    def normalise_rows(r0, n):
        rows = pl.ds(r0, n)
        y = ybuf[slot, rows].reshape(n, x1_ref.shape[1])
        z = DEEPNORM_ALPHA * x1_ref[rows, :] + (1.0 + mod_ref[0, 5:6, :]) * y
        o_ref[rows, :] = _layer_norm(z) * g2_ref[...] + b2_ref[...]

    n_chunks = tm // ISSUE_UNROLL

    @pl.when(i == 0)
    def _():
        for tile in range(GATHER_AHEAD):
            issue_rows(tile, tile, 0, tm)

    pltpu.make_async_copy(ys_ref.at[pl.ds(0, tm)], ybuf.at[slot], sem.at[slot]).wait()

    def chunk(c, issue_ahead):
        r0 = c * ISSUE_UNROLL
        normalise_rows(r0, ISSUE_UNROLL)
        if issue_ahead:
            issue_rows(i + GATHER_AHEAD, slot, r0, ISSUE_UNROLL)

    more = i + GATHER_AHEAD < pl.num_programs(0)

    @pl.when(more)
    def _():
        for c in range(n_chunks):
            chunk(c, True)

    @pl.when(jnp.logical_not(more))
    def _():
        for c in range(n_chunks):
            chunk(c, False)


def _final_call(dest, x1, mod3, g2, b2, ys, seq):
    t, d = x1.shape
    per_b = seq // TM
    grid_spec = pltpu.PrefetchScalarGridSpec(
        num_scalar_prefetch=1,
        grid=(t // TM,),
        in_specs=[pl.BlockSpec((TM, d), lambda i, dr: (i, 0)),
                  pl.BlockSpec((1, 6, d), lambda i, dr: (i // per_b, 0, 0)),
                  pl.BlockSpec((1, d), lambda i, dr: (0, 0)),
                  pl.BlockSpec((1, d), lambda i, dr: (0, 0)),
                  pl.BlockSpec(memory_space=pl.ANY)],
        out_specs=pl.BlockSpec((TM, d), lambda i, dr: (i, 0)),
        scratch_shapes=[pltpu.VMEM((GATHER_AHEAD, TM, d // LANES, LANES), F32),
                        pltpu.SemaphoreType.DMA((GATHER_AHEAD,))],
    )
    return pl.pallas_call(
        _final_kernel,
        grid_spec=grid_spec,
        out_shape=jax.ShapeDtypeStruct((t, d), F32),
        compiler_params=pltpu.CompilerParams(
            dimension_semantics=("arbitrary",), vmem_limit_bytes=VMEM_LIMIT),
        name="final",
    )(dest, x1, mod3, g2, b2, ys)


def _gate_pairs(w_rg, w_ig):
    def pairs(w):
        n, bs, _ = w.shape
        w = w.reshape(n // 2, 2, bs, bs)
        z = jnp.zeros((n // 2, bs, bs), w.dtype)
        top = jnp.concatenate([w[:, 0], z], axis=2)
        bot = jnp.concatenate([z, w[:, 1]], axis=2)
        return jnp.concatenate([top, bot], axis=1)
    return jnp.concatenate([pairs(w_rg), pairs(w_ig)], axis=2).astype(BF16)


def kernel(x, c, w_ada, b_ada, w_in, b_f, conv_w, conv_b, w_rg, b_rg, w_ig, b_ig, lru_lambda,
           g_attn, g_lru, w_out, ln1_g, ln1_b, w_grp, b_grp, w_exp, b_exp,
           w_e_gate, w_e_up, w_e_down, ln2_g, ln2_b):
    assert w_ada.shape[0] == DEPTH
    b, s, d = x.shape
    t = b * s
    d_attn = N_HEADS * HEAD_DIM
    d_lru = conv_w.shape[2]
    n_exp = w_e_gate.shape[1]
    assert s % TM == 0 and s % TQ == 0 and t % TM_E == 0

    c_pad = jnp.pad(c, ((0, SUBLANES - b), (0, 0)))
    mod = _mod_call(c_pad, w_ada, b_ada[0][None, :])
    mod3 = mod[:b].reshape(b, 6, d)

    bf_pad = jnp.pad(b_f[0], (0, LANES - N_HEADS))[None, :]
    q_aug, k_aug, vt_aug, xb, gb = _inproj_call(
        x, mod3, jnp.swapaxes(w_in, 1, 2), bf_pad, d_lru)
    attn_t = _attn_call(q_aug, k_aug, vt_aug)
    lru = _lru_call(xb, gb, conv_w[0], conv_b[0][None, :], _gate_pairs(w_rg[0], w_ig[0]),
                    b_rg[0][None, :], b_ig[0][None, :], lru_lambda[0][None, :])

    n_route = N_GROUPS + n_exp
    wr = jnp.pad(jnp.concatenate([w_grp[0], w_exp[0]], axis=1).T,
                 ((0, BUCKET_ROWS - n_route), (0, 0)))
    br = jnp.pad(jnp.concatenate([b_grp[0], b_exp[0]]), (0, BUCKET_ROWS - n_route))
    br = jnp.broadcast_to(br[:, None], (BUCKET_ROWS, LANES))
    x1, u2ext, rinfo, counts = _mixout_call(
        attn_t, lru.reshape(t, d_lru), x.reshape(t, d), mod3,
        g_attn[0][None, :], g_lru[0][None, :], w_out,
        ln1_g[0][None, :], ln1_b[0][None, :], wr, br, s)

    dest = _rank_call(rinfo, counts).reshape(t)
    cnt = counts[:N_BUCKETS, 0].astype(jnp.int32)
    ends = jnp.cumsum((cnt + (TM_E - 1)) // TM_E)
    n_tiles = t // TM_E + N_BUCKETS
    tile_bucket = jnp.sum(ends[None, :] <= jnp.arange(n_tiles)[:, None], axis=1)
    tile_bucket = jnp.minimum(tile_bucket, N_BUCKETS - 1)
    n_valid = ends[N_BUCKETS - 1:]
    last_bucket = tile_bucket[jnp.maximum(n_valid[0] - 1, 0)]
    tile_bucket = jnp.where(jnp.arange(n_tiles) < n_valid[0], tile_bucket, last_bucket)
    tile_ea = jnp.asarray(_BUCKET_EA, jnp.int32)[tile_bucket]
    tile_eb = jnp.asarray(_BUCKET_EB, jnp.int32)[tile_bucket]

    xs, wg_bf, wu_bf, wd_bf = _dispatch_call(
        dest, (ends * TM_E).astype(jnp.int32), u2ext, n_tiles * TM_E, w_e_gate, w_e_up, w_e_down)
    ys = _experts_call(tile_ea, tile_eb, n_valid.astype(jnp.int32), xs, wg_bf, wu_bf, wd_bf)
    out = _final_call(dest, x1, mod3, ln2_g[0][None, :], ln2_b[0][None, :], ys, s)
    return out.reshape(b, s, d)
```

```python
import functools

import jax
import jax.numpy as jnp
from jax import lax
from jax.experimental import pallas as pl
from jax.experimental.pallas import tpu as pltpu

F32 = jnp.float32
BF16 = jnp.bfloat16

HEAD_DIM = 64
N_HEADS = 8
CONV_WIDTH = 4
LRU_C = 8.0
N_GROUPS = 4
EXPERTS_PER_GROUP = 4
N_PAIRS = 6
N_BUCKETS = N_GROUPS * N_PAIRS
LN_EPS = 1e-5
RMS_EPS = 1e-6
NEG_INF = -1e30
DEPTH = 1
DEEPNORM_ALPHA = (2.0 * DEPTH) ** 0.25
LOG2E = 1.4426950408889634

LANES = 128
SUBLANES = 8
HEAD_PAD = LANES
VT_ROWS = 80
BUCKET_ROWS = 32
TM = 512
TQ = 512
TM_L = 1024
TM_F = 1024
TM_E = 256
ISSUE_UNROLL = 64
GATHER_AHEAD = 2
RANK_CHUNKS = 4
DMA_PRIORITIES = 2
VMEM_BYTES_V7X = 64 * 1024 * 1024
VMEM_LIMIT = VMEM_BYTES_V7X // 8 * 7

_PAIRS = [(0, 1), (0, 2), (0, 3), (1, 2), (1, 3), (2, 3)]
_BUCKET_EA = [g * EXPERTS_PER_GROUP + a for g in range(N_GROUPS) for (a, b) in _PAIRS]
_BUCKET_EB = [g * EXPERTS_PER_GROUP + b for g in range(N_GROUPS) for (a, b) in _PAIRS]


def _dot(a, b):
    return jnp.dot(a, b, preferred_element_type=F32)


def _split2(a):
    hi = a.astype(BF16)
    lo = (a - hi.astype(F32)).astype(BF16)
    return hi, lo


def _split3(a):
    hi = a.astype(BF16)
    r = a - hi.astype(F32)
    mid = r.astype(BF16)
    lo = (r - mid.astype(F32)).astype(BF16)
    return hi, mid, lo


def _layer_norm(x):
    mu = jnp.mean(x, axis=-1, keepdims=True)
    xc = x - mu
    var = jnp.mean(xc * xc, axis=-1, keepdims=True)
    return xc * lax.rsqrt(var + LN_EPS)


def _sigmoid(x):
    return 0.5 * jnp.tanh(0.5 * x) + 0.5


def _rot_in_group(x, k):
    n, w = x.shape
    return pltpu.roll(x.reshape(n // SUBLANES, SUBLANES, w), k, 1).reshape(n, w)


def _log_sigmoid(z):
    return jnp.minimum(z, 0.0) - jnp.log1p(jnp.exp(-jnp.abs(z)))


def _cumsum_rows(x, carry):
    n, w = x.shape
    row_in_group = lax.broadcasted_iota(jnp.int32, (n, w), 0) % SUBLANES
    k = 1
    while k < SUBLANES:
        x = x + jnp.where(row_in_group >= k, _rot_in_group(x, k), 0.0)
        k *= 2
    groups = []
    for g in range(n // SUBLANES):
        blk = x[g * SUBLANES:(g + 1) * SUBLANES] + carry
        groups.append(blk)
        carry = blk[SUBLANES - 1:SUBLANES]
    return jnp.concatenate(groups, axis=0)


def _shift_rows(x, k, fill):
    n = x.shape[0]
    if k % SUBLANES == 0:
        return jnp.concatenate([jnp.full((k, x.shape[1]), fill, x.dtype), x[:n - k]], axis=0)
    row = lax.broadcasted_iota(jnp.int32, x.shape, 0)
    return jnp.where(row >= k, pltpu.roll(x, k, 0), fill)


def _mod_kernel(c_ref, w_ref, b_ref, o_ref):
    c = c_ref[...]
    s = c * _sigmoid(c)
    sh, sl = _split2(s)
    wh, wl = _split2(w_ref[0])
    o_ref[...] = _dot(sh, wh) + (_dot(sh, wl) + _dot(sl, wh)) + b_ref[...]


def _mod_call(c_pad, w_ada, b_ada):
    rows, d = c_pad.shape
    n = w_ada.shape[2]
    return pl.pallas_call(
        _mod_kernel,
        grid=(n // d,),
        in_specs=[pl.BlockSpec((rows, d), lambda j: (0, 0)),
                  pl.BlockSpec((1, d, d), lambda j: (0, 0, j)),
                  pl.BlockSpec((1, d), lambda j: (0, j))],
        out_specs=pl.BlockSpec((rows, d), lambda j: (0, j)),
        out_shape=jax.ShapeDtypeStruct((rows, n), F32),
        compiler_params=pltpu.CompilerParams(vmem_limit_bytes=VMEM_LIMIT),
        name="mod",
    )(c_pad, w_ada, b_ada)


def _inproj_kernel(x_ref, mod_ref, wt_ref, bf_ref,
                   q_ref, k_ref, vt_ref, xb_ref, gb_ref, carry_ref, v_scr, w_scr, stage, wsem):
    j = pl.program_id(1)
    d_attn = N_HEADS * HEAD_DIM
    d_lru = xb_ref.shape[2]
    c_k, c_v, c_x = d_attn, 2 * d_attn, 3 * d_attn
    c_g, c_f = c_x + d_lru, c_x + 2 * d_lru
    r_f = 3 * d_attn
    r_x, r_g = r_f + N_HEADS, r_f + N_HEADS + d_lru

    @pl.when(jnp.logical_and(pl.program_id(0) == 0, j == 0))
    def _():
        blocks = [(0, d_attn, 0, HEAD_DIM ** -0.5 * LOG2E, None), (d_attn, d_attn, c_k, None, None),
                  (2 * d_attn, d_attn, c_v, None, None), (r_x, d_lru, c_x, None, None),
                  (r_g, d_lru, c_g, None, None), (r_f, LANES, c_f, None, N_HEADS)]
        copies = [pltpu.make_async_copy(wt_ref.at[0, pl.ds(r0, n), :],
                                        stage.at[k, pl.ds(0, n), :], wsem.at[k])
                  for k, (r0, n, _, _, _) in enumerate(blocks)]
        for copy in copies:
            copy.start()
        for k, (_, n, c0, scale, keep_lanes) in enumerate(blocks):
            copies[k].wait()
            blk = stage[k, 0:n, :]
            if scale is not None:
                blk = blk * scale
            blk = blk.T
            if keep_lanes is not None:
                lane = lax.broadcasted_iota(jnp.int32, blk.shape, 1)
                blk = jnp.where(lane < keep_lanes, blk, 0.0)
            w_scr[:, c0:c0 + n] = blk.astype(BF16)

    @pl.when(j == 0)
    def _():
        carry_ref[...] = jnp.zeros_like(carry_ref)

    tm = x_ref.shape[1]
    u = _layer_norm(x_ref[0]) * (1.0 + mod_ref[0, 1:2, :]) + mod_ref[0, 0:1, :]
    ub = u.astype(BF16)

    logf = _log_sigmoid(_dot(ub, w_scr[:, c_f:c_f + LANES]) + bf_ref[...])
    cum = _cumsum_rows(logf, carry_ref[...])
    carry_ref[...] = cum[tm - 1:tm, :]
    c_hi, c_mid, c_lo = [p.astype(F32) for p in _split3(cum * LOG2E)]

    lane = lax.broadcasted_iota(jnp.int32, (tm, HEAD_PAD), 1)
    d = HEAD_DIM
    q_all = _dot(ub, w_scr[:, 0:c_k])
    k_all = _dot(ub, w_scr[:, c_k:c_v])
    for h in range(N_HEADS):
        ts = slice((h // 2) * LANES, (h // 2 + 1) * LANES)
        qh, kh = q_all[:, ts], k_all[:, ts]
        if h % 2:
            qh, kh = pltpu.roll(qh, d, 1), pltpu.roll(kh, d, 1)
        hi, mid, lo = c_hi[:, h:h + 1], c_mid[:, h:h + 1], c_lo[:, h:h + 1]
        q_ext = jnp.where(lane == d, hi, jnp.where(lane == d + 1, mid, jnp.where(
            lane == d + 2, lo, jnp.where(lane < d + 6, 1.0, 0.0))))
        k_ext = jnp.where(lane < d + 3, 1.0, jnp.where(lane == d + 3, -hi, jnp.where(
            lane == d + 4, -mid, jnp.where(lane == d + 5, -lo, 0.0))))
        q_ref[0, h] = jnp.where(lane < d, qh, q_ext).astype(BF16)
        k_ref[0, h] = jnp.where(lane < d, kh, k_ext).astype(BF16)

    v_scr[...] = _dot(ub, w_scr[:, c_v:c_x])
    vt_all = v_scr[...].T
    pad_rows = lax.broadcasted_iota(jnp.int32, (VT_ROWS - d, tm), 0)
    ones_row = jnp.where(pad_rows == 0, 1.0, 0.0).astype(BF16)
    for h in range(N_HEADS):
        vt_ref[0, h, 0, :d, :] = vt_all[h * d:(h + 1) * d, :].astype(BF16)
        vt_ref[0, h, 0, d:, :] = ones_row

    xb_ref[0] = _dot(ub, w_scr[:, c_x:c_g])
    gb_ref[0] = _dot(ub, w_scr[:, c_g:c_f])


def _inproj_call(x, mod3, w_in_t, bf_pad, d_lru):
    b, s, d = x.shape
    d_attn = N_HEADS * HEAD_DIM
    assert w_in_t.shape[1] == 3 * d_attn + N_HEADS + 2 * d_lru
    full = lambda shape: pl.BlockSpec(shape, lambda bi, j: (0,) * len(shape))
    head_spec = pl.BlockSpec((1, N_HEADS, TM, HEAD_PAD), lambda bi, j: (bi, 0, j, 0))
    vt_spec = pl.BlockSpec((1, N_HEADS, 1, VT_ROWS, TM), lambda bi, j: (bi, 0, j, 0, 0))
    row_spec = pl.BlockSpec((1, TM, d_lru), lambda bi, j: (bi, j, 0))
    head_shape = jax.ShapeDtypeStruct((b, N_HEADS, s, HEAD_PAD), BF16)
    return pl.pallas_call(
        _inproj_kernel,
        grid=(b, s // TM),
        in_specs=[pl.BlockSpec((1, TM, d), lambda bi, j: (bi, j, 0)),
                  pl.BlockSpec((1, 6, d), lambda bi, j: (bi, 0, 0)),
                  pl.BlockSpec(memory_space=pl.ANY), full((1, LANES))],
        out_specs=[head_spec, head_spec, vt_spec, row_spec, row_spec],
        out_shape=[head_shape, head_shape,
                   jax.ShapeDtypeStruct((b, N_HEADS, s // TM, VT_ROWS, TM), BF16),
                   jax.ShapeDtypeStruct((b, s, d_lru), F32),
                   jax.ShapeDtypeStruct((b, s, d_lru), F32)],
        scratch_shapes=[pltpu.VMEM((1, LANES), F32), pltpu.VMEM((TM, d_attn), F32),
                        pltpu.VMEM((d, 3 * d_attn + 2 * d_lru + LANES), BF16),
                        pltpu.VMEM((6, max(d_attn, d_lru), d), F32),
                        pltpu.SemaphoreType.DMA((6,))],
        compiler_params=pltpu.CompilerParams(
            dimension_semantics=("arbitrary", "arbitrary"), vmem_limit_bytes=VMEM_LIMIT),
        name="inproj",
    )(x, mod3, w_in_t, bf_pad)


def _attn_kernel(q_ref, k_ref, vt_ref, o_ref, s_scr, smax_scr, m_scr, acc_scr):
    tq = TQ
    nq = q_ref.shape[2] // tq
    heads = range(q_ref.shape[1])

    def scores_to(slot, i, j):
        for hh in heads:
            k = k_ref[0, hh, pl.ds(pl.multiple_of(j * tq, tq), tq), :]
            q = q_ref[0, hh, pl.ds(pl.multiple_of(i * tq, tq), tq), :]
            s = lax.dot_general(k, q, (((1,), (1,)), ((), ())), preferred_element_type=F32)
            s_scr[slot, hh] = s
            smax_scr[slot, hh] = jnp.max(s, axis=0, keepdims=True)

    def consume(slot, j, masked):
        for hh in heads:
            s = s_scr[slot, hh]
            if masked:
                key = lax.broadcasted_iota(jnp.int32, (tq, tq), 0)
                qry = lax.broadcasted_iota(jnp.int32, (tq, tq), 1)
                s = jnp.where(key <= qry, s, NEG_INF)
                s_max = jnp.max(s, axis=0, keepdims=True)
            else:
                s_max = smax_scr[slot, hh]
            m = m_scr[hh]
            m_new = jnp.maximum(m, s_max)
            p = jnp.exp2(s - m_new).astype(BF16)
            acc_scr[hh] = jnp.exp2(m - m_new) * acc_scr[hh] + _dot(vt_ref[0, hh, j], p)
            m_scr[hh] = m_new

    scores_to(2, 0, 0)

    def query_block(i, carry):
        m_scr[...] = jnp.full(m_scr.shape, NEG_INF, F32)
        acc_scr[...] = jnp.zeros(acc_scr.shape, F32)
        nxt = jnp.minimum(i + 1, nq - 1)
        n_mid = i - 1

        @pl.when(i == 0)
        def _():
            consume(2, 0, True)
            scores_to(2, nxt, 0)

        @pl.when(i >= 1)
        def _():
            scores_to(1, i, 1)
            consume(2, 0, False)

        def pair(kk, c):
            j = 1 + 2 * kk
            scores_to(0, i, j + 1)
            consume(1, j, False)
            scores_to(1, i, j + 2)
            consume(0, j + 1, False)
            return c

        lax.fori_loop(0, jnp.maximum(n_mid, 0) // 2, pair, 0)

        @pl.when(jnp.logical_and(i >= 1, n_mid % 2 == 1))
        def _():
            scores_to(0, i, i)
            consume(1, i - 1, False)
            scores_to(2, nxt, 0)
            consume(0, i, True)

        @pl.when(jnp.logical_and(i >= 1, n_mid % 2 == 0))
        def _():
            scores_to(2, nxt, 0)
            consume(1, i, True)

        for hh in heads:
            o_ref[0, 0, i, hh * HEAD_DIM:(hh + 1) * HEAD_DIM, :] = (
                acc_scr[hh, :HEAD_DIM, :] / acc_scr[hh, HEAD_DIM:HEAD_DIM + 1, :])
        return carry

    lax.fori_loop(0, nq, query_block, 0)


def _attn_call(q_aug, k_aug, vt_aug):
    b, h, s, hp = q_aug.shape
    assert TQ == TM
    nq = s // TQ
    return pl.pallas_call(
        _attn_kernel,
        grid=(b, h // 2),
        in_specs=[pl.BlockSpec((1, 2, s, hp), lambda bi, p: (bi, p, 0, 0)),
                  pl.BlockSpec((1, 2, s, hp), lambda bi, p: (bi, p, 0, 0)),
                  pl.BlockSpec((1, 2, nq, VT_ROWS, TQ), lambda bi, p: (bi, p, 0, 0, 0))],
        out_specs=pl.BlockSpec((1, 1, nq, 2 * HEAD_DIM, TQ), lambda bi, p: (bi, p, 0, 0, 0)),
        out_shape=jax.ShapeDtypeStruct((b, h // 2, nq, 2 * HEAD_DIM, TQ), F32),
        scratch_shapes=[pltpu.VMEM((3, 2, TQ, TQ), F32), pltpu.VMEM((3, 2, 1, TQ), F32),
                        pltpu.VMEM((2, 1, TQ), F32), pltpu.VMEM((2, VT_ROWS, TQ), F32)],
        compiler_params=pltpu.CompilerParams(
            dimension_semantics=("arbitrary", "arbitrary"), vmem_limit_bytes=VMEM_LIMIT),
        name="attention",
    )(q_aug, k_aug, vt_aug)


def _gelu_tanh(x):
    return 0.5 * x * (1.0 + jnp.tanh(0.7978845608028654 * (x + 0.044715 * (x * x * x))))


def _lru_kernel(xb_ref, gb_ref, cw_ref, cb_ref, wgate_ref, brg_ref, big_ref, lam_ref,
                o_ref, tail_ref, h_ref):
    j = pl.program_id(1)

    @pl.when(j == 0)
    def _():
        tail_ref[...] = jnp.zeros_like(tail_ref)
        h_ref[...] = jnp.zeros_like(h_ref)

    x = xb_ref[0]
    tm, dl = x.shape
    row_in_group = lax.broadcasted_iota(jnp.int32, (tm, dl), 0) % SUBLANES
    x_prev_group = jnp.concatenate([tail_ref[...], x[:tm - SUBLANES]], axis=0)
    xc = x * cw_ref[CONV_WIDTH - 1:CONV_WIDTH, :] + cb_ref[...]
    for k in range(1, CONV_WIDTH):
        xs = _rot_in_group(jnp.where(row_in_group >= SUBLANES - k, x_prev_group, x), k)
        xc = xc + xs * cw_ref[CONV_WIDTH - 1 - k:CONV_WIDTH - k, :]
    tail_ref[...] = x[tm - SUBLANES:, :]

    xcb = xc.astype(BF16)
    n_pairs = dl // LANES
    r_parts, i_parts = [], []
    for p in range(n_pairs):
        g = _dot(xcb[:, p * LANES:(p + 1) * LANES], wgate_ref[p])
        r_parts.append(g[:, :LANES])
        i_parts.append(g[:, LANES:])
    r = _sigmoid(jnp.concatenate(r_parts, axis=1) + brg_ref[...])
    ig = _sigmoid(jnp.concatenate(i_parts, axis=1) + big_ref[...])

    lam = lam_ref[...]
    softplus_neg_lam = jnp.maximum(-lam, 0.0) + jnp.log1p(jnp.exp(-jnp.abs(lam)))
    log_a = (-LRU_C) * r * softplus_neg_lam
    a = jnp.exp(log_a)
    v = 1.0 - a * a
    u = jnp.where(v > 0.0, v * lax.rsqrt(v), 0.0) * (ig * xc)

    k = 1
    while k < SUBLANES:
        keep = row_in_group >= k
        u = a * jnp.where(keep, _rot_in_group(u, k), 0.0) + u
        a = a * jnp.where(keep, _rot_in_group(a, k), 1.0)
        k *= 2
    h_prev = h_ref[...]
    groups = []
    for g in range(tm // SUBLANES):
        rows = slice(g * SUBLANES, (g + 1) * SUBLANES)
        hg = a[rows] * h_prev + u[rows]
        groups.append(hg)
        h_prev = hg[SUBLANES - 1:SUBLANES]
    h_ref[...] = h_prev
    o_ref[0] = jnp.concatenate(groups, axis=0) * _gelu_tanh(gb_ref[0])


def _lru_call(xb, gb, conv_w, conv_b, wgate, b_rg, b_ig, lam):
    b, s, dl = xb.shape
    assert s % TM_L == 0
    row_spec = pl.BlockSpec((1, TM_L, dl), lambda bi, j: (bi, j, 0))
    full = lambda shape: pl.BlockSpec(shape, lambda bi, j: (0,) * len(shape))
    return pl.pallas_call(
        _lru_kernel,
        grid=(b, s // TM_L),
        in_specs=[row_spec, row_spec, full(conv_w.shape), full((1, dl)), full(wgate.shape),
                  full((1, dl)), full((1, dl)), full((1, dl))],
        out_specs=row_spec,
        out_shape=jax.ShapeDtypeStruct((b, s, dl), F32),
        scratch_shapes=[pltpu.VMEM((SUBLANES, dl), F32), pltpu.VMEM((1, dl), F32)],
        compiler_params=pltpu.CompilerParams(
            dimension_semantics=("arbitrary", "arbitrary"), vmem_limit_bytes=VMEM_LIMIT),
        name="lru",
    )(xb, gb, conv_w, conv_b, wgate, b_rg, b_ig, lam)


def _rms(x, gain):
    return x * lax.rsqrt(jnp.mean(x * x, axis=-1, keepdims=True) + RMS_EPS) * gain


def _mixout_kernel(attn_ref, lru_ref, x_ref, mod_ref, ga_ref, gl_ref, woa_ref, wol_ref,
                   g1_ref, b1_ref, wr_ref, br_ref, x1_ref, u2_ref, rinfo_ref, cnt_ref, wo_bf):
    tm, d = x_ref.shape

    @pl.when(pl.program_id(0) == 0)
    def _():
        wo_bf[0] = woa_ref[0].astype(BF16)
        wo_bf[1] = wol_ref[0].astype(BF16)

    attn = attn_ref[0, :, 0].reshape(ga_ref.shape[1], tm).T
    na = _rms(attn, ga_ref[...]).astype(BF16)
    nl = _rms(lru_ref[...], gl_ref[...]).astype(BF16)
    mix = _dot(na, wo_bf[0]) + _dot(nl, wo_bf[1])
    z = DEEPNORM_ALPHA * x_ref[...] + (1.0 + mod_ref[0, 2:3, :]) * mix
    x1 = _layer_norm(z) * g1_ref[...] + b1_ref[...]
    x1_ref[...] = x1
    u2 = _layer_norm(x1) * (1.0 + mod_ref[0, 4:5, :]) + mod_ref[0, 3:4, :]

    nt = (((1,), (1,)), ((), ()))
    uh, ul = _split2(u2)
    wh, wl = _split2(wr_ref[...])
    dg = lambda a, b_: lax.dot_general(a, b_, nt, preferred_element_type=F32)
    lg = dg(wh, uh) + (dg(wh, ul) + dg(wl, uh)) + br_ref[:, 0:1]

    def first_index(vals, target):
        idx = jnp.full_like(target, float(len(vals) - 1))
        for n in range(len(vals) - 2, -1, -1):
            idx = jnp.where(vals[n] == target, float(n), idx)
        return idx

    g = [lg[n:n + 1, :] for n in range(N_GROUPS)]
    gmax = functools.reduce(jnp.maximum, g)
    gsum = functools.reduce(lambda a, b_: a + b_, [jnp.exp(v - gmax) for v in g])
    grp_w = 1.0 / gsum
    gidx = first_index(g, gmax)

    sel = []
    for e in range(EXPERTS_PER_GROUP):
        v = lg[N_GROUPS + (N_GROUPS - 1) * EXPERTS_PER_GROUP + e:
               N_GROUPS + (N_GROUPS - 1) * EXPERTS_PER_GROUP + e + 1, :]
        for gi in range(N_GROUPS - 2, -1, -1):
            r0 = N_GROUPS + gi * EXPERTS_PER_GROUP + e
            v = jnp.where(gidx == float(gi), lg[r0:r0 + 1, :], v)
        sel.append(v)
    smax = functools.reduce(jnp.maximum, sel)
    i1 = first_index(sel, smax)
    rest = [jnp.where(i1 == float(e), -3e38, sel[e]) for e in range(EXPERTS_PER_GROUP)]
    rmax = functools.reduce(jnp.maximum, rest)
    i2 = first_index(rest, rmax)
    e2 = jnp.exp(rmax - smax)
    w1 = grp_w / (1.0 + e2)
    w2 = grp_w * e2 / (1.0 + e2)
    ia = jnp.minimum(i1, i2)
    ib = jnp.maximum(i1, i2)
    wa = jnp.where(i1 < i2, w1, w2)
    wb = jnp.where(i1 < i2, w2, w1)
    pair = jnp.where(ia == 0.0, ib - 1.0, jnp.where(ia == 1.0, ib + 1.0, 5.0))
    bucket = gidx * float(N_PAIRS) + pair

    @pl.when(pl.program_id(0) == 0)
    def _():
        cnt_ref[...] = jnp.zeros_like(cnt_ref)

    cid = lax.broadcasted_iota(jnp.int32, (BUCKET_ROWS, tm), 0).astype(F32)
    cnt_ref[...] += jnp.sum(jnp.where(cid == bucket, 1.0, 0.0), axis=1, keepdims=True)

    zrow = jnp.zeros_like(wa)
    rinfo_ref[...] = jnp.concatenate([bucket, wa, wb] + [zrow] * (SUBLANES - 3), axis=0)
    wt = jnp.concatenate([wa, wb, jnp.zeros((LANES - 2, tm), F32)], axis=0)
    u2_ref[:, :d] = u2
    u2_ref[:, d:] = wt.T


def _mixout_call(attn_t, lru, x2d, mod3, ga, gl, wo, g1, b1, wr, br, seq):
    t, d = x2d.shape
    _, n_pairs, _, pair_w, _ = attn_t.shape
    dh = n_pairs * pair_w
    assert wo.shape[1] == 2 * dh and lru.shape[1] == dh
    per_b = seq // TM
    full = lambda shape: pl.BlockSpec(shape, lambda i: (0,) * len(shape))
    return pl.pallas_call(
        _mixout_kernel,
        grid=(t // TM,),
        in_specs=[pl.BlockSpec((1, n_pairs, 1, pair_w, TM),
                               lambda i: (i // per_b, 0, i % per_b, 0, 0)),
                  pl.BlockSpec((TM, dh), lambda i: (i, 0)),
                  pl.BlockSpec((TM, d), lambda i: (i, 0)),
                  pl.BlockSpec((1, 6, d), lambda i: (i // per_b, 0, 0)),
                  full((1, dh)), full((1, dh)),
                  pl.BlockSpec((1, dh, d), lambda i: (0, 0, 0)),
                  pl.BlockSpec((1, dh, d), lambda i: (0, 1, 0)),
                  full((1, d)), full((1, d)), full((BUCKET_ROWS, d)), full((BUCKET_ROWS, LANES))],
        out_specs=[pl.BlockSpec((TM, d), lambda i: (i, 0)),
                   pl.BlockSpec((TM, d + LANES), lambda i: (i, 0)),
                   pl.BlockSpec((SUBLANES, TM), lambda i: (0, i)),
                   pl.BlockSpec((BUCKET_ROWS, LANES), lambda i: (0, 0))],
        out_shape=[jax.ShapeDtypeStruct((t, d), F32),
                   jax.ShapeDtypeStruct((t, d + LANES), F32),
                   jax.ShapeDtypeStruct((SUBLANES, t), F32),
                   jax.ShapeDtypeStruct((BUCKET_ROWS, LANES), F32)],
        scratch_shapes=[pltpu.VMEM((2, dh, d), BF16)],
        compiler_params=pltpu.CompilerParams(
            dimension_semantics=("arbitrary",), vmem_limit_bytes=VMEM_LIMIT),
        name="mixout",
    )(attn_t, lru, x2d, mod3, ga, gl, wo, wo, g1, b1, wr, br)


def _rank_kernel(rinfo_ref, counts_ref, dest_ref, carry_ref, offs_ref):
    tm = rinfo_ref.shape[1]

    @pl.when(pl.program_id(0) == 0)
    def _():
        carry_ref[...] = jnp.zeros_like(carry_ref)
        padded = jnp.floor((counts_ref[...] + float(TM_E - 1)) * (1.0 / TM_E)) * float(TM_E)
        inc = padded
        k = 1
        while k < BUCKET_ROWS:
            inc = inc + _shift_rows(inc, k, 0.0)
            k *= 2
        offs_ref[...] = inc - padded

    srow = lax.broadcasted_iota(jnp.int32, (TM, TM), 0)
    scol = lax.broadcasted_iota(jnp.int32, (TM, TM), 1)
    upper = (srow <= scol).astype(BF16)
    cid = lax.broadcasted_iota(jnp.int32, (BUCKET_ROWS, TM), 0).astype(F32)
    carry = carry_ref[...]
    for c in range(tm // TM):
        cols = slice(c * TM, (c + 1) * TM)
        onehot = jnp.where(cid == rinfo_ref[0:1, cols], 1.0, 0.0)
        prefix = _dot(onehot.astype(BF16), upper)
        rank = jnp.sum(onehot * (prefix - 1.0 + carry[:, 0:1] + offs_ref[:, 0:1]),
                       axis=0, keepdims=True)
        dest_ref[:, cols] = rank.astype(jnp.int32)
        carry = carry + prefix[:, TM - 1:TM]
    carry_ref[...] = carry


def _rank_call(rinfo, counts):
    t = rinfo.shape[1]
    tm = TM * RANK_CHUNKS
    assert t % tm == 0
    return pl.pallas_call(
        _rank_kernel,
        grid=(t // tm,),
        in_specs=[pl.BlockSpec((SUBLANES, tm), lambda i: (0, i)),
                  pl.BlockSpec((BUCKET_ROWS, LANES), lambda i: (0, 0))],
        out_specs=pl.BlockSpec((1, tm), lambda i: (0, i)),
        out_shape=jax.ShapeDtypeStruct((1, t), jnp.int32),
        scratch_shapes=[pltpu.VMEM((BUCKET_ROWS, LANES), F32),
                        pltpu.VMEM((BUCKET_ROWS, LANES), F32)],
        compiler_params=pltpu.CompilerParams(
            dimension_semantics=("arbitrary",), vmem_limit_bytes=VMEM_LIMIT),
        name="rank",
    )(rinfo, counts)


def _dispatch_kernel(dest_ref, ends_ref, u2_ref, wg_ref, wu_ref, wd_ref,
                     xs_ref, wg_bf_ref, wu_bf_ref, wd_bf_ref, zbuf, sem, zsem):
    tm = u2_ref.shape[0]
    t0 = pl.program_id(0) * tm

    @pl.when(pl.program_id(0) == 0)
    def _():
        zbuf[...] = jnp.zeros_like(zbuf)

        def tail_copy(bkt):
            end = ends_ref[bkt]
            start = ends_ref[bkt - 1] if bkt else 0
            tail = pl.multiple_of(jnp.maximum(end - TM_E, 0), TM_E)
            return end > start, pltpu.make_async_copy(zbuf, xs_ref.at[pl.ds(tail, TM_E)], zsem)

        for bkt in range(N_BUCKETS):
            nonempty, copy = tail_copy(bkt)
            pl.when(nonempty)(copy.start)
        for bkt in range(N_BUCKETS):
            nonempty, copy = tail_copy(bkt)
            pl.when(nonempty)(copy.wait)

        def unused_tile_copy(k):
            return pltpu.make_async_copy(
                zbuf, xs_ref.at[pl.ds(pl.multiple_of(k * TM_E, TM_E), TM_E)], zsem)

        first_unused = ends_ref[N_BUCKETS - 1] // TM_E
        n_tiles = xs_ref.shape[0] // TM_E
        lax.fori_loop(first_unused, n_tiles, lambda k, c: (unused_tile_copy(k).start(), c)[1], 0)
        lax.fori_loop(first_unused, n_tiles, lambda k, c: (unused_tile_copy(k).wait(), c)[1], 0)

    for r in range(tm):
        pltpu.make_async_copy(u2_ref.at[pl.ds(r, 1)], xs_ref.at[pl.ds(dest_ref[t0 + r], 1)],
                              sem).start(priority=r % DMA_PRIORITIES)
    wg_bf_ref[0] = wg_ref[0, 0].astype(BF16)
    wu_bf_ref[0] = wu_ref[0, 0].astype(BF16)
    wd_bf_ref[0] = wd_ref[0, 0].astype(BF16)
    pltpu.make_async_copy(u2_ref, xs_ref.at[pl.ds(0, tm)], sem).wait()


def _dispatch_call(dest, bucket_ends, u2ext, n_rows, w_gate, w_up, w_down):
    t, w = u2ext.shape
    _, n_exp, d, de = w_gate.shape
    tm = t // n_exp
    assert t % n_exp == 0 and tm % ISSUE_UNROLL == 0
    w_in = lambda shape: pl.BlockSpec((1, 1) + shape, lambda i, dr, er: (0, i, 0, 0))
    w_out = lambda shape: pl.BlockSpec((1,) + shape, lambda i, dr, er: (i, 0, 0))
    grid_spec = pltpu.PrefetchScalarGridSpec(
        num_scalar_prefetch=2,
        grid=(n_exp,),
        in_specs=[pl.BlockSpec((tm, w), lambda i, dr, er: (i, 0)),
                  w_in((d, de)), w_in((d, de)), w_in((de, d))],
        out_specs=[pl.BlockSpec(memory_space=pl.ANY),
                   w_out((d, de)), w_out((d, de)), w_out((de, d))],
        scratch_shapes=[pltpu.VMEM((TM_E, w), F32), pltpu.SemaphoreType.DMA(()),
                        pltpu.SemaphoreType.DMA(())],
    )
    return pl.pallas_call(
        _dispatch_kernel,
        grid_spec=grid_spec,
        out_shape=[jax.ShapeDtypeStruct((n_rows, w), F32),
                   jax.ShapeDtypeStruct((n_exp, d, de), BF16),
                   jax.ShapeDtypeStruct((n_exp, d, de), BF16),
                   jax.ShapeDtypeStruct((n_exp, de, d), BF16)],
        compiler_params=pltpu.CompilerParams(
            dimension_semantics=("arbitrary",), vmem_limit_bytes=VMEM_LIMIT),
        name="dispatch",
    )(dest, bucket_ends, u2ext, w_gate, w_up, w_down)


def _experts_kernel(ea_ref, eb_ref, nv_ref, xs_ref, wga_ref, wua_ref, wda_ref,
                    wgb_ref, wub_ref, wdb_ref, ys_ref):
    del ea_ref, eb_ref
    i = pl.program_id(0)
    d = wda_ref.shape[2]

    @pl.when(i < nv_ref[0])
    def _():
        x = xs_ref[:, :d].astype(BF16)

        def expert(wg_ref, wu_ref, wd_ref):
            g = _dot(x, wg_ref[0])
            h = (g * _sigmoid(g)) * _dot(x, wu_ref[0])
            return _dot(h.astype(BF16), wd_ref[0])

        ya = xs_ref[:, d:d + 1] * expert(wga_ref, wua_ref, wda_ref)
        y = ya + xs_ref[:, d + 1:d + 2] * expert(wgb_ref, wub_ref, wdb_ref)
        ys_ref[...] = y.reshape(ys_ref.shape)

    @pl.when(i >= nv_ref[0])
    def _():
        ys_ref[...] = jnp.zeros_like(ys_ref)


def _experts_call(tile_ea, tile_eb, n_valid, xs, wg, wu, wd):
    tp, w = xs.shape
    _, d, de = wg.shape
    row = lambda i, ea, eb, nv: (jnp.minimum(i, nv[0] - 1), 0)
    wa = lambda i, ea, eb, nv: (ea[i], 0, 0)
    wb = lambda i, ea, eb, nv: (eb[i], 0, 0)
    grid_spec = pltpu.PrefetchScalarGridSpec(
        num_scalar_prefetch=3,
        grid=(tp // TM_E,),
        in_specs=[pl.BlockSpec((TM_E, w), row),
                  pl.BlockSpec((1, d, de), wa), pl.BlockSpec((1, d, de), wa),
                  pl.BlockSpec((1, de, d), wa),
                  pl.BlockSpec((1, d, de), wb), pl.BlockSpec((1, d, de), wb),
                  pl.BlockSpec((1, de, d), wb)],
        out_specs=pl.BlockSpec((TM_E, d // LANES, LANES), lambda i, ea, eb, nv: (i, 0, 0)),
    )
    return pl.pallas_call(
        _experts_kernel,
        grid_spec=grid_spec,
        out_shape=jax.ShapeDtypeStruct((tp, d // LANES, LANES), F32),
        compiler_params=pltpu.CompilerParams(
            dimension_semantics=("arbitrary",), vmem_limit_bytes=VMEM_LIMIT),
        name="experts",
    )(tile_ea, tile_eb, n_valid, xs, wg, wu, wd, wg, wu, wd)


def _final_kernel(dest_ref, x1_ref, mod_ref, g2_ref, b2_ref, ys_ref, o_ref, ybuf, sem):
    tm = x1_ref.shape[0]
    i = pl.program_id(0)
    slot = i % GATHER_AHEAD

    def issue_rows(tile, to_slot, r0, n):
        for r in range(n):
            pltpu.make_async_copy(
                ys_ref.at[pl.ds(dest_ref[tile * tm + r0 + r], 1)],
                ybuf.at[to_slot, pl.ds(r0 + r, 1)], sem.at[to_slot],
            ).start(priority=(r0 + r) % DMA_PRIORITIES)

    def normalise_rows(r0, n):
        rows = pl.ds(r0, n)
        y = ybuf[slot, rows].reshape(n, x1_ref.shape[1])
        z = DEEPNORM_ALPHA * x1_ref[rows, :] + (1.0 + mod_ref[0, 5:6, :]) * y
        o_ref[rows, :] = _layer_norm(z) * g2_ref[...] + b2_ref[...]

    n_chunks = tm // ISSUE_UNROLL

    @pl.when(i == 0)
    def _():
        for tile in range(GATHER_AHEAD):
            issue_rows(tile, tile, 0, tm)

    pltpu.make_async_copy(ys_ref.at[pl.ds(0, tm)], ybuf.at[slot], sem.at[slot]).wait()

    def chunk(c, issue_ahead):
        r0 = c * ISSUE_UNROLL
        normalise_rows(r0, ISSUE_UNROLL)
        if issue_ahead:
            issue_rows(i + GATHER_AHEAD, slot, r0, ISSUE_UNROLL)

    more = i + GATHER_AHEAD < pl.num_programs(0)

    @pl.when(more)
    def _():
        for c in range(n_chunks):
            chunk(c, True)

    @pl.when(jnp.logical_not(more))
    def _():
        for c in range(n_chunks):
            chunk(c, False)


def _final_call(dest, x1, mod3, g2, b2, ys, seq):
    t, d = x1.shape
    assert seq % TM_F == 0
    per_b = seq // TM_F
    grid_spec = pltpu.PrefetchScalarGridSpec(
        num_scalar_prefetch=1,
        grid=(t // TM_F,),
        in_specs=[pl.BlockSpec((TM_F, d), lambda i, dr: (i, 0)),
                  pl.BlockSpec((1, 6, d), lambda i, dr: (i // per_b, 0, 0)),
                  pl.BlockSpec((1, d), lambda i, dr: (0, 0)),
                  pl.BlockSpec((1, d), lambda i, dr: (0, 0)),
                  pl.BlockSpec(memory_space=pl.ANY)],
        out_specs=pl.BlockSpec((TM_F, d), lambda i, dr: (i, 0)),
        scratch_shapes=[pltpu.VMEM((GATHER_AHEAD, TM_F, d // LANES, LANES), F32),
                        pltpu.SemaphoreType.DMA((GATHER_AHEAD,))],
    )
    return pl.pallas_call(
        _final_kernel,
        grid_spec=grid_spec,
        out_shape=jax.ShapeDtypeStruct((t, d), F32),
        compiler_params=pltpu.CompilerParams(
            dimension_semantics=("arbitrary",), vmem_limit_bytes=VMEM_LIMIT),
        name="final",
    )(dest, x1, mod3, g2, b2, ys)


def _gate_pairs(w_rg, w_ig):
    def pairs(w):
        n, bs, _ = w.shape
        w = w.reshape(n // 2, 2, bs, bs)
        z = jnp.zeros((n // 2, bs, bs), w.dtype)
        top = jnp.concatenate([w[:, 0], z], axis=2)
        bot = jnp.concatenate([z, w[:, 1]], axis=2)
        return jnp.concatenate([top, bot], axis=1)
    return jnp.concatenate([pairs(w_rg), pairs(w_ig)], axis=2).astype(BF16)


def kernel(x, c, w_ada, b_ada, w_in, b_f, conv_w, conv_b, w_rg, b_rg, w_ig, b_ig, lru_lambda,
           g_attn, g_lru, w_out, ln1_g, ln1_b, w_grp, b_grp, w_exp, b_exp,
           w_e_gate, w_e_up, w_e_down, ln2_g, ln2_b):
    assert w_ada.shape[0] == DEPTH
    b, s, d = x.shape
    t = b * s
    d_attn = N_HEADS * HEAD_DIM
    d_lru = conv_w.shape[2]
    n_exp = w_e_gate.shape[1]
    assert s % TM == 0 and s % TQ == 0 and t % TM_E == 0

    c_pad = jnp.pad(c, ((0, SUBLANES - b), (0, 0)))
    mod = _mod_call(c_pad, w_ada, b_ada[0][None, :])
    mod3 = mod[:b].reshape(b, 6, d)

    bf_pad = jnp.pad(b_f[0], (0, LANES - N_HEADS))[None, :]
    q_aug, k_aug, vt_aug, xb, gb = _inproj_call(
        x, mod3, jnp.swapaxes(w_in, 1, 2), bf_pad, d_lru)
    attn_t = _attn_call(q_aug, k_aug, vt_aug)
    lru = _lru_call(xb, gb, conv_w[0], conv_b[0][None, :], _gate_pairs(w_rg[0], w_ig[0]),
                    b_rg[0][None, :], b_ig[0][None, :], lru_lambda[0][None, :])

    n_route = N_GROUPS + n_exp
    wr = jnp.pad(jnp.concatenate([w_grp[0], w_exp[0]], axis=1).T,
                 ((0, BUCKET_ROWS - n_route), (0, 0)))
    br = jnp.pad(jnp.concatenate([b_grp[0], b_exp[0]]), (0, BUCKET_ROWS - n_route))
    br = jnp.broadcast_to(br[:, None], (BUCKET_ROWS, LANES))
    x1, u2ext, rinfo, counts = _mixout_call(
        attn_t, lru.reshape(t, d_lru), x.reshape(t, d), mod3,
        g_attn[0][None, :], g_lru[0][None, :], w_out,
        ln1_g[0][None, :], ln1_b[0][None, :], wr, br, s)

    dest = _rank_call(rinfo, counts).reshape(t)
    cnt = counts[:N_BUCKETS, 0].astype(jnp.int32)
    ends = jnp.cumsum((cnt + (TM_E - 1)) // TM_E)
    n_tiles = t // TM_E + N_BUCKETS
    tile_bucket = jnp.sum(ends[None, :] <= jnp.arange(n_tiles)[:, None], axis=1)
    tile_bucket = jnp.minimum(tile_bucket, N_BUCKETS - 1)
    n_valid = ends[N_BUCKETS - 1:]
    last_bucket = tile_bucket[jnp.maximum(n_valid[0] - 1, 0)]
    tile_bucket = jnp.where(jnp.arange(n_tiles) < n_valid[0], tile_bucket, last_bucket)
    tile_ea = jnp.asarray(_BUCKET_EA, jnp.int32)[tile_bucket]
    tile_eb = jnp.asarray(_BUCKET_EB, jnp.int32)[tile_bucket]

    xs, wg_bf, wu_bf, wd_bf = _dispatch_call(
        dest, (ends * TM_E).astype(jnp.int32), u2ext, n_tiles * TM_E, w_e_gate, w_e_up, w_e_down)
    ys = _experts_call(tile_ea, tile_eb, n_valid.astype(jnp.int32), xs, wg_bf, wu_bf, wd_bf)
    out = _final_call(dest, x1, mod3, ln2_g[0][None, :], ln2_b[0][None, :], ys, s)
    return out.reshape(b, s, d)
```

```python
import functools

import jax
import jax.numpy as jnp
from jax import lax
from jax.experimental import pallas as pl
from jax.experimental.pallas import tpu as pltpu

F32 = jnp.float32
BF16 = jnp.bfloat16

HEAD_DIM = 64
N_HEADS = 8
CONV_WIDTH = 4
LRU_C = 8.0
N_GROUPS = 4
EXPERTS_PER_GROUP = 4
N_PAIRS = 6
N_BUCKETS = N_GROUPS * N_PAIRS
LN_EPS = 1e-5
RMS_EPS = 1e-6
NEG_INF = -1e30
DEPTH = 1
DEEPNORM_ALPHA = (2.0 * DEPTH) ** 0.25
LOG2E = 1.4426950408889634

LANES = 128
SUBLANES = 8
HEAD_PAD = LANES
VT_ROWS = 80
BUCKET_ROWS = 32
TM = 512
TQ = 512
TM_E = 256
TM_D = 1024
ISSUE_UNROLL = 64
GATHER_AHEAD = 2
RANK_CHUNKS = 4
DMA_PRIORITIES = 2
VMEM_BYTES_V7X = 64 * 1024 * 1024
VMEM_LIMIT = VMEM_BYTES_V7X // 8 * 7

_PAIRS = [(0, 1), (0, 2), (0, 3), (1, 2), (1, 3), (2, 3)]
_BUCKET_EA = [g * EXPERTS_PER_GROUP + a for g in range(N_GROUPS) for (a, b) in _PAIRS]
_BUCKET_EB = [g * EXPERTS_PER_GROUP + b for g in range(N_GROUPS) for (a, b) in _PAIRS]


def _dot(a, b):
    return jnp.dot(a, b, preferred_element_type=F32)


def _split2(a):
    hi = a.astype(BF16)
    lo = (a - hi.astype(F32)).astype(BF16)
    return hi, lo


def _split3(a):
    hi = a.astype(BF16)
    r = a - hi.astype(F32)
    mid = r.astype(BF16)
    lo = (r - mid.astype(F32)).astype(BF16)
    return hi, mid, lo


def _layer_norm(x):
    mu = jnp.mean(x, axis=-1, keepdims=True)
    xc = x - mu
    var = jnp.mean(xc * xc, axis=-1, keepdims=True)
    return xc * lax.rsqrt(var + LN_EPS)


def _sigmoid(x):
    return 0.5 * jnp.tanh(0.5 * x) + 0.5


def _rot_in_group(x, k):
    n, w = x.shape
    return pltpu.roll(x.reshape(n // SUBLANES, SUBLANES, w), k, 1).reshape(n, w)


def _log_sigmoid(z):
    return jnp.minimum(z, 0.0) - jnp.log1p(jnp.exp(-jnp.abs(z)))


def _cumsum_rows(x, carry):
    n, w = x.shape
    row_in_group = lax.broadcasted_iota(jnp.int32, (n, w), 0) % SUBLANES
    k = 1
    while k < SUBLANES:
        x = x + jnp.where(row_in_group >= k, _rot_in_group(x, k), 0.0)
        k *= 2
    groups = []
    for g in range(n // SUBLANES):
        blk = x[g * SUBLANES:(g + 1) * SUBLANES] + carry
        groups.append(blk)
        carry = blk[SUBLANES - 1:SUBLANES]
    return jnp.concatenate(groups, axis=0)


def _shift_rows(x, k, fill):
    n = x.shape[0]
    if k % SUBLANES == 0:
        return jnp.concatenate([jnp.full((k, x.shape[1]), fill, x.dtype), x[:n - k]], axis=0)
    row = lax.broadcasted_iota(jnp.int32, x.shape, 0)
    return jnp.where(row >= k, pltpu.roll(x, k, 0), fill)


def _mod_kernel(c_ref, w_ref, b_ref, o_ref):
    c = c_ref[...]
    s = c * _sigmoid(c)
    sh, sl = _split2(s)
    wh, wl = _split2(w_ref[0])
    o_ref[...] = _dot(sh, wh) + (_dot(sh, wl) + _dot(sl, wh)) + b_ref[...]


def _mod_call(c_pad, w_ada, b_ada):
    rows, d = c_pad.shape
    n = w_ada.shape[2]
    return pl.pallas_call(
        _mod_kernel,
        grid=(n // d,),
        in_specs=[pl.BlockSpec((rows, d), lambda j: (0, 0)),
                  pl.BlockSpec((1, d, d), lambda j: (0, 0, j)),
                  pl.BlockSpec((1, d), lambda j: (0, j))],
        out_specs=pl.BlockSpec((rows, d), lambda j: (0, j)),
        out_shape=jax.ShapeDtypeStruct((rows, n), F32),
        compiler_params=pltpu.CompilerParams(vmem_limit_bytes=VMEM_LIMIT),
        name="mod",
    )(c_pad, w_ada, b_ada)


def _inproj_kernel(x_ref, mod_ref, wt_ref, bf_ref, wg_ref, wu_ref, wd_ref,
                   q_ref, k_ref, vt_ref, xb_ref, gb_ref, wg_bf_ref, wu_bf_ref, wd_bf_ref,
                   carry_ref, v_scr, w_scr, stage, wsem):
    j = pl.program_id(1)
    d_attn = N_HEADS * HEAD_DIM
    d_lru = xb_ref.shape[2]
    c_k, c_v, c_x = d_attn, 2 * d_attn, 3 * d_attn
    c_g, c_f = c_x + d_lru, c_x + 2 * d_lru
    r_f = 3 * d_attn
    r_x, r_g = r_f + N_HEADS, r_f + N_HEADS + d_lru

    @pl.when(jnp.logical_and(pl.program_id(0) == 0, j == 0))
    def _():
        blocks = [(0, d_attn, 0, HEAD_DIM ** -0.5 * LOG2E, None), (d_attn, d_attn, c_k, None, None),
                  (2 * d_attn, d_attn, c_v, None, None), (r_x, d_lru, c_x, None, None),
                  (r_g, d_lru, c_g, None, None), (r_f, LANES, c_f, None, N_HEADS)]
        copies = [pltpu.make_async_copy(wt_ref.at[0, pl.ds(r0, n), :],
                                        stage.at[k, pl.ds(0, n), :], wsem.at[k])
                  for k, (r0, n, _, _, _) in enumerate(blocks)]
        for copy in copies:
            copy.start()
        for k, (_, n, c0, scale, keep_lanes) in enumerate(blocks):
            copies[k].wait()
            blk = stage[k, 0:n, :]
            if scale is not None:
                blk = blk * scale
            blk = blk.T
            if keep_lanes is not None:
                lane = lax.broadcasted_iota(jnp.int32, blk.shape, 1)
                blk = jnp.where(lane < keep_lanes, blk, 0.0)
            w_scr[:, c0:c0 + n] = blk.astype(BF16)

    @pl.when(j == 0)
    def _():
        carry_ref[...] = jnp.zeros_like(carry_ref)

    tm = x_ref.shape[1]
    u = _layer_norm(x_ref[0]) * (1.0 + mod_ref[0, 1:2, :]) + mod_ref[0, 0:1, :]
    ub = u.astype(BF16)

    logf = _log_sigmoid(_dot(ub, w_scr[:, c_f:c_f + LANES]) + bf_ref[...])
    cum = _cumsum_rows(logf, carry_ref[...])
    carry_ref[...] = cum[tm - 1:tm, :]
    c_hi, c_mid, c_lo = [p.astype(F32) for p in _split3(cum * LOG2E)]

    lane = lax.broadcasted_iota(jnp.int32, (tm, HEAD_PAD), 1)
    d = HEAD_DIM
    q_all = _dot(ub, w_scr[:, 0:c_k])
    k_all = _dot(ub, w_scr[:, c_k:c_v])
    for h in range(N_HEADS):
        ts = slice((h // 2) * LANES, (h // 2 + 1) * LANES)
        qh, kh = q_all[:, ts], k_all[:, ts]
        if h % 2:
            qh, kh = pltpu.roll(qh, d, 1), pltpu.roll(kh, d, 1)
        hi, mid, lo = c_hi[:, h:h + 1], c_mid[:, h:h + 1], c_lo[:, h:h + 1]
        q_ext = jnp.where(lane == d, hi, jnp.where(lane == d + 1, mid, jnp.where(
            lane == d + 2, lo, jnp.where(lane < d + 6, 1.0, 0.0))))
        k_ext = jnp.where(lane < d + 3, 1.0, jnp.where(lane == d + 3, -hi, jnp.where(
            lane == d + 4, -mid, jnp.where(lane == d + 5, -lo, 0.0))))
        q_ref[0, h] = jnp.where(lane < d, qh, q_ext).astype(BF16)
        k_ref[0, h] = jnp.where(lane < d, kh, k_ext).astype(BF16)

    v_scr[...] = _dot(ub, w_scr[:, c_v:c_x])
    vt_all = v_scr[...].T
    pad_rows = lax.broadcasted_iota(jnp.int32, (VT_ROWS - d, tm), 0)
    ones_row = jnp.where(pad_rows == 0, 1.0, 0.0).astype(BF16)
    for h in range(N_HEADS):
        vt_ref[0, h, 0, :d, :] = vt_all[h * d:(h + 1) * d, :].astype(BF16)
        vt_ref[0, h, 0, d:, :] = ones_row

    xb_ref[0] = _dot(ub, w_scr[:, c_x:c_g])
    gb_ref[0] = _dot(ub, w_scr[:, c_g:c_f])

    wg_bf_ref[0] = wg_ref[0, 0].astype(BF16)
    wu_bf_ref[0] = wu_ref[0, 0].astype(BF16)
    wd_bf_ref[0] = wd_ref[0, 0].astype(BF16)


def _inproj_call(x, mod3, w_in_t, bf_pad, d_lru, w_gate, w_up, w_down):
    b, s, d = x.shape
    d_attn = N_HEADS * HEAD_DIM
    assert w_in_t.shape[1] == 3 * d_attn + N_HEADS + 2 * d_lru
    _, n_exp, _, de = w_gate.shape
    nj = s // TM
    assert (b * nj) % n_exp == 0
    parts = b * nj // n_exp
    step = lambda bi, j: bi * nj + j
    w_src = lambda rows, cols: pl.BlockSpec(
        (1, 1, rows // parts, cols), lambda bi, j: (0, step(bi, j) // parts, step(bi, j) % parts, 0))
    w_dst = lambda rows, cols: pl.BlockSpec(
        (1, rows // parts, cols), lambda bi, j: (step(bi, j) // parts, step(bi, j) % parts, 0))
    full = lambda shape: pl.BlockSpec(shape, lambda bi, j: (0,) * len(shape))
    head_spec = pl.BlockSpec((1, N_HEADS, TM, HEAD_PAD), lambda bi, j: (bi, 0, j, 0))
    vt_spec = pl.BlockSpec((1, N_HEADS, 1, VT_ROWS, TM), lambda bi, j: (bi, 0, j, 0, 0))
    row_spec = pl.BlockSpec((1, TM, d_lru), lambda bi, j: (bi, j, 0))
    head_shape = jax.ShapeDtypeStruct((b, N_HEADS, s, HEAD_PAD), BF16)
    return pl.pallas_call(
        _inproj_kernel,
        grid=(b, s // TM),
        in_specs=[pl.BlockSpec((1, TM, d), lambda bi, j: (bi, j, 0)),
                  pl.BlockSpec((1, 6, d), lambda bi, j: (bi, 0, 0)),
                  pl.BlockSpec(memory_space=pl.ANY), full((1, LANES)),
                  w_src(d, de), w_src(d, de), w_src(de, d)],
        out_specs=[head_spec, head_spec, vt_spec, row_spec, row_spec,
                   w_dst(d, de), w_dst(d, de), w_dst(de, d)],
        out_shape=[head_shape, head_shape,
                   jax.ShapeDtypeStruct((b, N_HEADS, s // TM, VT_ROWS, TM), BF16),
                   jax.ShapeDtypeStruct((b, s, d_lru), F32),
                   jax.ShapeDtypeStruct((b, s, d_lru), F32),
                   jax.ShapeDtypeStruct((n_exp, d, de), BF16),
                   jax.ShapeDtypeStruct((n_exp, d, de), BF16),
                   jax.ShapeDtypeStruct((n_exp, de, d), BF16)],
        scratch_shapes=[pltpu.VMEM((1, LANES), F32), pltpu.VMEM((TM, d_attn), F32),
                        pltpu.VMEM((d, 3 * d_attn + 2 * d_lru + LANES), BF16),
                        pltpu.VMEM((6, max(d_attn, d_lru), d), F32),
                        pltpu.SemaphoreType.DMA((6,))],
        compiler_params=pltpu.CompilerParams(
            dimension_semantics=("arbitrary", "arbitrary"), vmem_limit_bytes=VMEM_LIMIT),
        name="inproj",
    )(x, mod3, w_in_t, bf_pad, w_gate, w_up, w_down)


def _attn_kernel(q_ref, k_ref, vt_ref, o_ref, s_scr, smax_scr, m_scr, acc_scr):
    tq = TQ
    nq = q_ref.shape[2] // tq
    heads = range(q_ref.shape[1])

    def scores_to(slot, i, j):
        for hh in heads:
            k = k_ref[0, hh, pl.ds(pl.multiple_of(j * tq, tq), tq), :]
            q = q_ref[0, hh, pl.ds(pl.multiple_of(i * tq, tq), tq), :]
            s = lax.dot_general(k, q, (((1,), (1,)), ((), ())), preferred_element_type=F32)
            s_scr[slot, hh] = s
            smax_scr[slot, hh] = jnp.max(s, axis=0, keepdims=True)

    def consume(slot, j, masked):
        for hh in heads:
            s = s_scr[slot, hh]
            if masked:
                key = lax.broadcasted_iota(jnp.int32, (tq, tq), 0)
                qry = lax.broadcasted_iota(jnp.int32, (tq, tq), 1)
                s = jnp.where(key <= qry, s, NEG_INF)
                s_max = jnp.max(s, axis=0, keepdims=True)
            else:
                s_max = smax_scr[slot, hh]
            m = m_scr[hh]
            m_new = jnp.maximum(m, s_max)
            p = jnp.exp2(s - m_new).astype(BF16)
            acc_scr[hh] = jnp.exp2(m - m_new) * acc_scr[hh] + _dot(vt_ref[0, hh, j], p)
            m_scr[hh] = m_new

    scores_to(2, 0, 0)

    def query_block(i, carry):
        m_scr[...] = jnp.full(m_scr.shape, NEG_INF, F32)
        acc_scr[...] = jnp.zeros(acc_scr.shape, F32)
        nxt = jnp.minimum(i + 1, nq - 1)
        n_mid = i - 1

        @pl.when(i == 0)
        def _():
            consume(2, 0, True)
            scores_to(2, nxt, 0)

        @pl.when(i >= 1)
        def _():
            scores_to(1, i, 1)
            consume(2, 0, False)

        def pair(kk, c):
            j = 1 + 2 * kk
            scores_to(0, i, j + 1)
            consume(1, j, False)
            scores_to(1, i, j + 2)
            consume(0, j + 1, False)
            return c

        lax.fori_loop(0, jnp.maximum(n_mid, 0) // 2, pair, 0)

        @pl.when(jnp.logical_and(i >= 1, n_mid % 2 == 1))
        def _():
            scores_to(0, i, i)
            consume(1, i - 1, False)
            scores_to(2, nxt, 0)
            consume(0, i, True)

        @pl.when(jnp.logical_and(i >= 1, n_mid % 2 == 0))
        def _():
            scores_to(2, nxt, 0)
            consume(1, i, True)

        for hh in heads:
            o_ref[0, 0, i, hh * HEAD_DIM:(hh + 1) * HEAD_DIM, :] = (
                acc_scr[hh, :HEAD_DIM, :] / acc_scr[hh, HEAD_DIM:HEAD_DIM + 1, :])
        return carry

    lax.fori_loop(0, nq, query_block, 0)


def _attn_call(q_aug, k_aug, vt_aug):
    b, h, s, hp = q_aug.shape
    assert TQ == TM
    nq = s // TQ
    return pl.pallas_call(
        _attn_kernel,
        grid=(b, h // 2),
        in_specs=[pl.BlockSpec((1, 2, s, hp), lambda bi, p: (bi, p, 0, 0)),
                  pl.BlockSpec((1, 2, s, hp), lambda bi, p: (bi, p, 0, 0)),
                  pl.BlockSpec((1, 2, nq, VT_ROWS, TQ), lambda bi, p: (bi, p, 0, 0, 0))],
        out_specs=pl.BlockSpec((1, 1, nq, 2 * HEAD_DIM, TQ), lambda bi, p: (bi, p, 0, 0, 0)),
        out_shape=jax.ShapeDtypeStruct((b, h // 2, nq, 2 * HEAD_DIM, TQ), F32),
        scratch_shapes=[pltpu.VMEM((3, 2, TQ, TQ), F32), pltpu.VMEM((3, 2, 1, TQ), F32),
                        pltpu.VMEM((2, 1, TQ), F32), pltpu.VMEM((2, VT_ROWS, TQ), F32)],
        compiler_params=pltpu.CompilerParams(
            dimension_semantics=("arbitrary", "arbitrary"), vmem_limit_bytes=VMEM_LIMIT),
        name="attention",
    )(q_aug, k_aug, vt_aug)


def _gelu_tanh(x):
    return 0.5 * x * (1.0 + jnp.tanh(0.7978845608028654 * (x + 0.044715 * (x * x * x))))


def _lru_kernel(xb_ref, gb_ref, cw_ref, cb_ref, wgate_ref, brg_ref, big_ref, lam_ref,
                o_ref, tail_ref, h_ref):
    j = pl.program_id(1)

    @pl.when(j == 0)
    def _():
        tail_ref[...] = jnp.zeros_like(tail_ref)
        h_ref[...] = jnp.zeros_like(h_ref)

    x = xb_ref[0]
    tm, dl = x.shape
    row_in_group = lax.broadcasted_iota(jnp.int32, (tm, dl), 0) % SUBLANES
    x_prev_group = jnp.concatenate([tail_ref[...], x[:tm - SUBLANES]], axis=0)
    xc = x * cw_ref[CONV_WIDTH - 1:CONV_WIDTH, :] + cb_ref[...]
    for k in range(1, CONV_WIDTH):
        xs = jnp.where(row_in_group < k, _rot_in_group(x_prev_group, k), _rot_in_group(x, k))
        xc = xc + xs * cw_ref[CONV_WIDTH - 1 - k:CONV_WIDTH - k, :]
    tail_ref[...] = x[tm - SUBLANES:, :]

    xcb = xc.astype(BF16)
    n_pairs = dl // LANES
    r_parts, i_parts = [], []
    for p in range(n_pairs):
        g = _dot(xcb[:, p * LANES:(p + 1) * LANES], wgate_ref[p])
        r_parts.append(g[:, :LANES])
        i_parts.append(g[:, LANES:])
    r = _sigmoid(jnp.concatenate(r_parts, axis=1) + brg_ref[...])
    ig = _sigmoid(jnp.concatenate(i_parts, axis=1) + big_ref[...])

    lam = lam_ref[...]
    softplus_neg_lam = jnp.maximum(-lam, 0.0) + jnp.log1p(jnp.exp(-jnp.abs(lam)))
    log_a = (-LRU_C) * r * softplus_neg_lam
    a = jnp.exp(log_a)
    v = 1.0 - a * a
    u = jnp.where(v > 0.0, v * lax.rsqrt(v), 0.0) * (ig * xc)

    k = 1
    while k < SUBLANES:
        keep = row_in_group >= k
        u = a * jnp.where(keep, _rot_in_group(u, k), 0.0) + u
        a = a * jnp.where(keep, _rot_in_group(a, k), 1.0)
        k *= 2
    h_prev = h_ref[...]
    groups = []
    for g in range(tm // SUBLANES):
        rows = slice(g * SUBLANES, (g + 1) * SUBLANES)
        hg = a[rows] * h_prev + u[rows]
        groups.append(hg)
        h_prev = hg[SUBLANES - 1:SUBLANES]
    h_ref[...] = h_prev
    o_ref[0] = jnp.concatenate(groups, axis=0) * _gelu_tanh(gb_ref[0])


def _lru_call(xb, gb, conv_w, conv_b, wgate, b_rg, b_ig, lam):
    b, s, dl = xb.shape
    row_spec = pl.BlockSpec((1, TM, dl), lambda bi, j: (bi, j, 0))
    full = lambda shape: pl.BlockSpec(shape, lambda bi, j: (0,) * len(shape))
    return pl.pallas_call(
        _lru_kernel,
        grid=(b, s // TM),
        in_specs=[row_spec, row_spec, full(conv_w.shape), full((1, dl)), full(wgate.shape),
                  full((1, dl)), full((1, dl)), full((1, dl))],
        out_specs=row_spec,
        out_shape=jax.ShapeDtypeStruct((b, s, dl), F32),
        scratch_shapes=[pltpu.VMEM((SUBLANES, dl), F32), pltpu.VMEM((1, dl), F32)],
        compiler_params=pltpu.CompilerParams(
            dimension_semantics=("arbitrary", "arbitrary"), vmem_limit_bytes=VMEM_LIMIT),
        name="lru",
    )(xb, gb, conv_w, conv_b, wgate, b_rg, b_ig, lam)


def _rms(x, gain):
    return x * lax.rsqrt(jnp.mean(x * x, axis=-1, keepdims=True) + RMS_EPS) * gain


def _mixout_kernel(attn_ref, lru_ref, x_ref, mod_ref, ga_ref, gl_ref, woa_ref, wol_ref,
                   g1_ref, b1_ref, wr_ref, br_ref, x1_ref, u2_ref, rinfo_ref, cnt_ref, wo_bf):
    tm, d = x_ref.shape

    @pl.when(pl.program_id(0) == 0)
    def _():
        wo_bf[0] = woa_ref[0].astype(BF16)
        wo_bf[1] = wol_ref[0].astype(BF16)

    attn = attn_ref[0, :, 0].reshape(ga_ref.shape[1], tm).T
    na = _rms(attn, ga_ref[...]).astype(BF16)
    nl = _rms(lru_ref[...], gl_ref[...]).astype(BF16)
    mix = _dot(na, wo_bf[0]) + _dot(nl, wo_bf[1])
    z = DEEPNORM_ALPHA * x_ref[...] + (1.0 + mod_ref[0, 2:3, :]) * mix
    x1 = _layer_norm(z) * g1_ref[...] + b1_ref[...]
    x1_ref[...] = x1
    u2 = _layer_norm(x1) * (1.0 + mod_ref[0, 4:5, :]) + mod_ref[0, 3:4, :]

    nt = (((1,), (1,)), ((), ()))
    uh, ul = _split2(u2)
    wh, wl = _split2(wr_ref[...])
    dg = lambda a, b_: lax.dot_general(a, b_, nt, preferred_element_type=F32)
    lg = dg(wh, uh) + (dg(wh, ul) + dg(wl, uh)) + br_ref[:, 0:1]

    def first_index(vals, target):
        idx = jnp.full_like(target, float(len(vals) - 1))
        for n in range(len(vals) - 2, -1, -1):
            idx = jnp.where(vals[n] == target, float(n), idx)
        return idx

    g = [lg[n:n + 1, :] for n in range(N_GROUPS)]
    gmax = functools.reduce(jnp.maximum, g)
    gsum = functools.reduce(lambda a, b_: a + b_, [jnp.exp(v - gmax) for v in g])
    grp_w = 1.0 / gsum
    gidx = first_index(g, gmax)

    sel = []
    for e in range(EXPERTS_PER_GROUP):
        v = lg[N_GROUPS + (N_GROUPS - 1) * EXPERTS_PER_GROUP + e:
               N_GROUPS + (N_GROUPS - 1) * EXPERTS_PER_GROUP + e + 1, :]
        for gi in range(N_GROUPS - 2, -1, -1):
            r0 = N_GROUPS + gi * EXPERTS_PER_GROUP + e
            v = jnp.where(gidx == float(gi), lg[r0:r0 + 1, :], v)
        sel.append(v)
    smax = functools.reduce(jnp.maximum, sel)
    i1 = first_index(sel, smax)
    rest = [jnp.where(i1 == float(e), -3e38, sel[e]) for e in range(EXPERTS_PER_GROUP)]
    rmax = functools.reduce(jnp.maximum, rest)
    i2 = first_index(rest, rmax)
    e2 = jnp.exp(rmax - smax)
    w1 = grp_w / (1.0 + e2)
    w2 = grp_w * e2 / (1.0 + e2)
    ia = jnp.minimum(i1, i2)
    ib = jnp.maximum(i1, i2)
    wa = jnp.where(i1 < i2, w1, w2)
    wb = jnp.where(i1 < i2, w2, w1)
    pair = jnp.where(ia == 0.0, ib - 1.0, jnp.where(ia == 1.0, ib + 1.0, 5.0))
    bucket = gidx * float(N_PAIRS) + pair

    @pl.when(pl.program_id(0) == 0)
    def _():
        cnt_ref[...] = jnp.zeros_like(cnt_ref)

    cid = lax.broadcasted_iota(jnp.int32, (BUCKET_ROWS, tm), 0).astype(F32)
    cnt_ref[...] += jnp.sum(jnp.where(cid == bucket, 1.0, 0.0), axis=1, keepdims=True)

    zrow = jnp.zeros_like(wa)
    rinfo_ref[...] = jnp.concatenate([bucket, wa, wb] + [zrow] * (SUBLANES - 3), axis=0)
    wt = jnp.concatenate([wa, wb, jnp.zeros((LANES - 2, tm), F32)], axis=0)
    u2_ref[:, :d] = u2
    u2_ref[:, d:] = wt.T


def _mixout_call(attn_t, lru, x2d, mod3, ga, gl, wo, g1, b1, wr, br, seq):
    t, d = x2d.shape
    _, n_pairs, _, pair_w, _ = attn_t.shape
    dh = n_pairs * pair_w
    assert wo.shape[1] == 2 * dh and lru.shape[1] == dh
    per_b = seq // TM
    full = lambda shape: pl.BlockSpec(shape, lambda i: (0,) * len(shape))
    return pl.pallas_call(
        _mixout_kernel,
        grid=(t // TM,),
        in_specs=[pl.BlockSpec((1, n_pairs, 1, pair_w, TM),
                               lambda i: (i // per_b, 0, i % per_b, 0, 0)),
                  pl.BlockSpec((TM, dh), lambda i: (i, 0)),
                  pl.BlockSpec((TM, d), lambda i: (i, 0)),
                  pl.BlockSpec((1, 6, d), lambda i: (i // per_b, 0, 0)),
                  full((1, dh)), full((1, dh)),
                  pl.BlockSpec((1, dh, d), lambda i: (0, 0, 0)),
                  pl.BlockSpec((1, dh, d), lambda i: (0, 1, 0)),
                  full((1, d)), full((1, d)), full((BUCKET_ROWS, d)), full((BUCKET_ROWS, LANES))],
        out_specs=[pl.BlockSpec((TM, d), lambda i: (i, 0)),
                   pl.BlockSpec((TM, d + LANES), lambda i: (i, 0)),
                   pl.BlockSpec((SUBLANES, TM), lambda i: (0, i)),
                   pl.BlockSpec((BUCKET_ROWS, LANES), lambda i: (0, 0))],
        out_shape=[jax.ShapeDtypeStruct((t, d), F32),
                   jax.ShapeDtypeStruct((t, d + LANES), F32),
                   jax.ShapeDtypeStruct((SUBLANES, t), F32),
                   jax.ShapeDtypeStruct((BUCKET_ROWS, LANES), F32)],
        scratch_shapes=[pltpu.VMEM((2, dh, d), BF16)],
        compiler_params=pltpu.CompilerParams(
            dimension_semantics=("arbitrary",), vmem_limit_bytes=VMEM_LIMIT),
        name="mixout",
    )(attn_t, lru, x2d, mod3, ga, gl, wo, wo, g1, b1, wr, br)


def _rank_kernel(rinfo_ref, counts_ref, dest_ref, carry_ref, offs_ref):
    tm = rinfo_ref.shape[1]

    @pl.when(pl.program_id(0) == 0)
    def _():
        carry_ref[...] = jnp.zeros_like(carry_ref)
        padded = jnp.floor((counts_ref[...] + float(TM_E - 1)) * (1.0 / TM_E)) * float(TM_E)
        inc = padded
        k = 1
        while k < BUCKET_ROWS:
            inc = inc + _shift_rows(inc, k, 0.0)
            k *= 2
        offs_ref[...] = inc - padded

    srow = lax.broadcasted_iota(jnp.int32, (TM, TM), 0)
    scol = lax.broadcasted_iota(jnp.int32, (TM, TM), 1)
    upper = (srow <= scol).astype(BF16)
    cid = lax.broadcasted_iota(jnp.int32, (BUCKET_ROWS, TM), 0).astype(F32)
    carry = carry_ref[...]
    for c in range(tm // TM):
        cols = slice(c * TM, (c + 1) * TM)
        onehot = jnp.where(cid == rinfo_ref[0:1, cols], 1.0, 0.0)
        prefix = _dot(onehot.astype(BF16), upper)
        rank = jnp.sum(onehot * (prefix - 1.0 + carry[:, 0:1] + offs_ref[:, 0:1]),
                       axis=0, keepdims=True)
        dest_ref[:, cols] = rank.astype(jnp.int32)
        carry = carry + prefix[:, TM - 1:TM]
    carry_ref[...] = carry


def _rank_call(rinfo, counts):
    t = rinfo.shape[1]
    tm = TM * RANK_CHUNKS
    assert t % tm == 0
    return pl.pallas_call(
        _rank_kernel,
        grid=(t // tm,),
        in_specs=[pl.BlockSpec((SUBLANES, tm), lambda i: (0, i)),
                  pl.BlockSpec((BUCKET_ROWS, LANES), lambda i: (0, 0))],
        out_specs=pl.BlockSpec((1, tm), lambda i: (0, i)),
        out_shape=jax.ShapeDtypeStruct((1, t), jnp.int32),
        scratch_shapes=[pltpu.VMEM((BUCKET_ROWS, LANES), F32),
                        pltpu.VMEM((BUCKET_ROWS, LANES), F32)],
        compiler_params=pltpu.CompilerParams(
            dimension_semantics=("arbitrary",), vmem_limit_bytes=VMEM_LIMIT),
        name="rank",
    )(rinfo, counts)


def _dispatch_kernel(dest_ref, ends_ref, u2_ref, xs_ref, zbuf, sem, zsem):
    tm = u2_ref.shape[0]
    t0 = pl.program_id(0) * tm

    @pl.when(pl.program_id(0) == 0)
    def _():
        zbuf[...] = jnp.zeros_like(zbuf)

        def tail_copy(bkt):
            end = ends_ref[bkt]
            start = ends_ref[bkt - 1] if bkt else 0
            tail = pl.multiple_of(jnp.maximum(end - TM_E, 0), TM_E)
            return end > start, pltpu.make_async_copy(zbuf, xs_ref.at[pl.ds(tail, TM_E)], zsem)

        for bkt in range(N_BUCKETS):
            nonempty, copy = tail_copy(bkt)
            pl.when(nonempty)(copy.start)
        for bkt in range(N_BUCKETS):
            nonempty, copy = tail_copy(bkt)
            pl.when(nonempty)(copy.wait)

        def unused_tile_copy(k):
            return pltpu.make_async_copy(
                zbuf, xs_ref.at[pl.ds(pl.multiple_of(k * TM_E, TM_E), TM_E)], zsem)

        first_unused = ends_ref[N_BUCKETS - 1] // TM_E
        n_tiles = xs_ref.shape[0] // TM_E
        lax.fori_loop(first_unused, n_tiles, lambda k, c: (unused_tile_copy(k).start(), c)[1], 0)
        lax.fori_loop(first_unused, n_tiles, lambda k, c: (unused_tile_copy(k).wait(), c)[1], 0)

    for r in range(tm):
        pltpu.make_async_copy(u2_ref.at[pl.ds(r, 1)], xs_ref.at[pl.ds(dest_ref[t0 + r], 1)],
                              sem).start(priority=r % DMA_PRIORITIES)
    pltpu.make_async_copy(u2_ref, xs_ref.at[pl.ds(0, tm)], sem).wait()


def _dispatch_call(dest, bucket_ends, u2ext, n_rows):
    t, w = u2ext.shape
    tm = TM_D
    assert t % tm == 0
    grid_spec = pltpu.PrefetchScalarGridSpec(
        num_scalar_prefetch=2,
        grid=(t // tm,),
        in_specs=[pl.BlockSpec((tm, w), lambda i, dr, er: (i, 0))],
        out_specs=pl.BlockSpec(memory_space=pl.ANY),
        scratch_shapes=[pltpu.VMEM((TM_E, w), F32), pltpu.SemaphoreType.DMA(()),
                        pltpu.SemaphoreType.DMA(())],
    )
    return pl.pallas_call(
        _dispatch_kernel,
        grid_spec=grid_spec,
        out_shape=jax.ShapeDtypeStruct((n_rows, w), F32),
        compiler_params=pltpu.CompilerParams(
            dimension_semantics=("arbitrary",), vmem_limit_bytes=VMEM_LIMIT),
        name="dispatch",
    )(dest, bucket_ends, u2ext)


def _experts_kernel(ea_ref, eb_ref, nv_ref, xs_ref, wga_ref, wua_ref, wda_ref,
                    wgb_ref, wub_ref, wdb_ref, ys_ref):
    del ea_ref, eb_ref
    i = pl.program_id(0)
    d = wda_ref.shape[2]

    @pl.when(i < nv_ref[0])
    def _():
        x = xs_ref[:, :d].astype(BF16)

        def expert(wg_ref, wu_ref, wd_ref):
            g = _dot(x, wg_ref[0])
            h = (g * _sigmoid(g)) * _dot(x, wu_ref[0])
            return _dot(h.astype(BF16), wd_ref[0])

        ya = xs_ref[:, d:d + 1] * expert(wga_ref, wua_ref, wda_ref)
        y = ya + xs_ref[:, d + 1:d + 2] * expert(wgb_ref, wub_ref, wdb_ref)
        ys_ref[...] = y.reshape(ys_ref.shape)

    @pl.when(i >= nv_ref[0])
    def _():
        ys_ref[...] = jnp.zeros_like(ys_ref)


def _experts_call(tile_ea, tile_eb, n_valid, xs, wg, wu, wd):
    tp, w = xs.shape
    _, d, de = wg.shape
    row = lambda i, ea, eb, nv: (jnp.minimum(i, nv[0] - 1), 0)
    wa = lambda i, ea, eb, nv: (ea[i], 0, 0)
    wb = lambda i, ea, eb, nv: (eb[i], 0, 0)
    grid_spec = pltpu.PrefetchScalarGridSpec(
        num_scalar_prefetch=3,
        grid=(tp // TM_E,),
        in_specs=[pl.BlockSpec((TM_E, w), row),
                  pl.BlockSpec((1, d, de), wa), pl.BlockSpec((1, d, de), wa),
                  pl.BlockSpec((1, de, d), wa),
                  pl.BlockSpec((1, d, de), wb), pl.BlockSpec((1, d, de), wb),
                  pl.BlockSpec((1, de, d), wb)],
        out_specs=pl.BlockSpec((TM_E, d // LANES, LANES), lambda i, ea, eb, nv: (i, 0, 0)),
    )
    return pl.pallas_call(
        _experts_kernel,
        grid_spec=grid_spec,
        out_shape=jax.ShapeDtypeStruct((tp, d // LANES, LANES), F32),
        compiler_params=pltpu.CompilerParams(
            dimension_semantics=("arbitrary",), vmem_limit_bytes=VMEM_LIMIT),
        name="experts",
    )(tile_ea, tile_eb, n_valid, xs, wg, wu, wd, wg, wu, wd)


def _final_kernel(dest_ref, x1_ref, mod_ref, g2_ref, b2_ref, ys_ref, o_ref, ybuf, sem):
    tm = x1_ref.shape[0]
    i = pl.program_id(0)
    slot = i % GATHER_AHEAD

    def issue_rows(tile, to_slot, r0, n):
        for r in range(n):
            pltpu.make_async_copy(
                ys_ref.at[pl.ds(dest_ref[tile * tm + r0 + r], 1)],
                ybuf.at[to_slot, pl.ds(r0 + r, 1)], sem.at[to_slot],
            ).start(priority=(r0 + r) % DMA_PRIORITIES)

    def normalise_rows(r0, n):
        rows = pl.ds(r0, n)
        y = ybuf[slot, rows].reshape(n, x1_ref.shape[1])
        z = DEEPNORM_ALPHA * x1_ref[rows, :] + (1.0 + mod_ref[0, 5:6, :]) * y
        o_ref[rows, :] = _layer_norm(z) * g2_ref[...] + b2_ref[...]

    n_chunks = tm // ISSUE_UNROLL

    @pl.when(i == 0)
    def _():
        for tile in range(GATHER_AHEAD):
            issue_rows(tile, tile, 0, tm)

    pltpu.make_async_copy(ys_ref.at[pl.ds(0, tm)], ybuf.at[slot], sem.at[slot]).wait()

    def chunk(c, issue_ahead):
        r0 = c * ISSUE_UNROLL
        normalise_rows(r0, ISSUE_UNROLL)
        if issue_ahead:
            issue_rows(i + GATHER_AHEAD, slot, r0, ISSUE_UNROLL)

    more = i + GATHER_AHEAD < pl.num_programs(0)

    @pl.when(more)
    def _():
        for c in range(n_chunks):
            chunk(c, True)

    @pl.when(jnp.logical_not(more))
    def _():
        for c in range(n_chunks):
            chunk(c, False)


def _final_call(dest, x1, mod3, g2, b2, ys, seq):
    t, d = x1.shape
    per_b = seq // TM
    grid_spec = pltpu.PrefetchScalarGridSpec(
        num_scalar_prefetch=1,
        grid=(t // TM,),
        in_specs=[pl.BlockSpec((TM, d), lambda i, dr: (i, 0)),
                  pl.BlockSpec((1, 6, d), lambda i, dr: (i // per_b, 0, 0)),
                  pl.BlockSpec((1, d), lambda i, dr: (0, 0)),
                  pl.BlockSpec((1, d), lambda i, dr: (0, 0)),
                  pl.BlockSpec(memory_space=pl.ANY)],
        out_specs=pl.BlockSpec((TM, d), lambda i, dr: (i, 0)),
        scratch_shapes=[pltpu.VMEM((GATHER_AHEAD, TM, d // LANES, LANES), F32),
                        pltpu.SemaphoreType.DMA((GATHER_AHEAD,))],
    )
    return pl.pallas_call(
        _final_kernel,
        grid_spec=grid_spec,
        out_shape=jax.ShapeDtypeStruct((t, d), F32),
        compiler_params=pltpu.CompilerParams(
            dimension_semantics=("arbitrary",), vmem_limit_bytes=VMEM_LIMIT),
        name="final",
    )(dest, x1, mod3, g2, b2, ys)


def _gate_pairs(w_rg, w_ig):
    def pairs(w):
        n, bs, _ = w.shape
        w = w.reshape(n // 2, 2, bs, bs)
        z = jnp.zeros((n // 2, bs, bs), w.dtype)
        top = jnp.concatenate([w[:, 0], z], axis=2)
        bot = jnp.concatenate([z, w[:, 1]], axis=2)
        return jnp.concatenate([top, bot], axis=1)
    return jnp.concatenate([pairs(w_rg), pairs(w_ig)], axis=2).astype(BF16)


def kernel(x, c, w_ada, b_ada, w_in, b_f, conv_w, conv_b, w_rg, b_rg, w_ig, b_ig, lru_lambda,
           g_attn, g_lru, w_out, ln1_g, ln1_b, w_grp, b_grp, w_exp, b_exp,
           w_e_gate, w_e_up, w_e_down, ln2_g, ln2_b):
    assert w_ada.shape[0] == DEPTH
    b, s, d = x.shape
    t = b * s
    d_attn = N_HEADS * HEAD_DIM
    d_lru = conv_w.shape[2]
    n_exp = w_e_gate.shape[1]
    assert s % TM == 0 and s % TQ == 0 and t % TM_E == 0

    c_pad = jnp.pad(c, ((0, SUBLANES - b), (0, 0)))
    mod = _mod_call(c_pad, w_ada, b_ada[0][None, :])
    mod3 = mod[:b].reshape(b, 6, d)

    bf_pad = jnp.pad(b_f[0], (0, LANES - N_HEADS))[None, :]
    q_aug, k_aug, vt_aug, xb, gb, wg_bf, wu_bf, wd_bf = _inproj_call(
        x, mod3, jnp.swapaxes(w_in, 1, 2), bf_pad, d_lru, w_e_gate, w_e_up, w_e_down)
    attn_t = _attn_call(q_aug, k_aug, vt_aug)
    lru = _lru_call(xb, gb, conv_w[0], conv_b[0][None, :], _gate_pairs(w_rg[0], w_ig[0]),
                    b_rg[0][None, :], b_ig[0][None, :], lru_lambda[0][None, :])

    n_route = N_GROUPS + n_exp
    wr = jnp.pad(jnp.concatenate([w_grp[0], w_exp[0]], axis=1).T,
                 ((0, BUCKET_ROWS - n_route), (0, 0)))
    br = jnp.pad(jnp.concatenate([b_grp[0], b_exp[0]]), (0, BUCKET_ROWS - n_route))
    br = jnp.broadcast_to(br[:, None], (BUCKET_ROWS, LANES))
    x1, u2ext, rinfo, counts = _mixout_call(
        attn_t, lru.reshape(t, d_lru), x.reshape(t, d), mod3,
        g_attn[0][None, :], g_lru[0][None, :], w_out,
        ln1_g[0][None, :], ln1_b[0][None, :], wr, br, s)

    dest = _rank_call(rinfo, counts).reshape(t)
    cnt = counts[:N_BUCKETS, 0].astype(jnp.int32)
    ends = jnp.cumsum((cnt + (TM_E - 1)) // TM_E)
    n_tiles = t // TM_E + N_BUCKETS
    tile_bucket = jnp.sum(ends[None, :] <= jnp.arange(n_tiles)[:, None], axis=1)
    tile_bucket = jnp.minimum(tile_bucket, N_BUCKETS - 1)
    n_valid = ends[N_BUCKETS - 1:]
    last_bucket = tile_bucket[jnp.maximum(n_valid[0] - 1, 0)]
    tile_bucket = jnp.where(jnp.arange(n_tiles) < n_valid[0], tile_bucket, last_bucket)
    tile_ea = jnp.asarray(_BUCKET_EA, jnp.int32)[tile_bucket]
    tile_eb = jnp.asarray(_BUCKET_EB, jnp.int32)[tile_bucket]

    xs = _dispatch_call(dest, (ends * TM_E).astype(jnp.int32), u2ext, n_tiles * TM_E)
    ys = _experts_call(tile_ea, tile_eb, n_valid.astype(jnp.int32), xs, wg_bf, wu_bf, wd_bf)
    out = _final_call(dest, x1, mod3, ln2_g[0][None, :], ln2_b[0][None, :], ys, s)
    return out.reshape(b, s, d)
```

```python
import functools

import jax
import jax.numpy as jnp
from jax import lax
from jax.experimental import pallas as pl
from jax.experimental.pallas import tpu as pltpu

F32 = jnp.float32
BF16 = jnp.bfloat16

HEAD_DIM = 64
N_HEADS = 8
CONV_WIDTH = 4
LRU_C = 8.0
N_GROUPS = 4
EXPERTS_PER_GROUP = 4
N_PAIRS = 6
N_BUCKETS = N_GROUPS * N_PAIRS
LN_EPS = 1e-5
RMS_EPS = 1e-6
NEG_INF = -1e30
DEPTH = 1
DEEPNORM_ALPHA = (2.0 * DEPTH) ** 0.25
LOG2E = 1.4426950408889634

LANES = 128
SUBLANES = 8
HEAD_PAD = LANES
VT_ROWS = 80
BUCKET_ROWS = 32
TM = 512
TQ = 512
TM_E = 256
TM_D = 2048
TM_L = 1024
ISSUE_UNROLL = 64
GATHER_AHEAD = 2
RANK_CHUNKS = 4
DMA_PRIORITIES = 2
VMEM_BYTES_V7X = 64 * 1024 * 1024
VMEM_LIMIT = VMEM_BYTES_V7X // 8 * 7

_PAIRS = [(0, 1), (0, 2), (0, 3), (1, 2), (1, 3), (2, 3)]
_BUCKET_EA = [g * EXPERTS_PER_GROUP + a for g in range(N_GROUPS) for (a, b) in _PAIRS]
_BUCKET_EB = [g * EXPERTS_PER_GROUP + b for g in range(N_GROUPS) for (a, b) in _PAIRS]


def _dot(a, b):
    return jnp.dot(a, b, preferred_element_type=F32)


def _split2(a):
    hi = a.astype(BF16)
    lo = (a - hi.astype(F32)).astype(BF16)
    return hi, lo


def _split3(a):
    hi = a.astype(BF16)
    r = a - hi.astype(F32)
    mid = r.astype(BF16)
    lo = (r - mid.astype(F32)).astype(BF16)
    return hi, mid, lo


def _layer_norm(x):
    mu = jnp.mean(x, axis=-1, keepdims=True)
    xc = x - mu
    var = jnp.mean(xc * xc, axis=-1, keepdims=True)
    return xc * lax.rsqrt(var + LN_EPS)


def _sigmoid(x):
    return 0.5 * jnp.tanh(0.5 * x) + 0.5


def _rot_in_group(x, k):
    n, w = x.shape
    return pltpu.roll(x.reshape(n // SUBLANES, SUBLANES, w), k, 1).reshape(n, w)


def _log_sigmoid(z):
    return jnp.minimum(z, 0.0) - jnp.log1p(jnp.exp(-jnp.abs(z)))


def _cumsum_rows(x, carry):
    n, w = x.shape
    row_in_group = lax.broadcasted_iota(jnp.int32, (n, w), 0) % SUBLANES
    k = 1
    while k < SUBLANES:
        x = x + jnp.where(row_in_group >= k, _rot_in_group(x, k), 0.0)
        k *= 2
    groups = []
    for g in range(n // SUBLANES):
        blk = x[g * SUBLANES:(g + 1) * SUBLANES] + carry
        groups.append(blk)
        carry = blk[SUBLANES - 1:SUBLANES]
    return jnp.concatenate(groups, axis=0)


def _shift_rows(x, k, fill):
    n = x.shape[0]
    if k % SUBLANES == 0:
        return jnp.concatenate([jnp.full((k, x.shape[1]), fill, x.dtype), x[:n - k]], axis=0)
    row = lax.broadcasted_iota(jnp.int32, x.shape, 0)
    return jnp.where(row >= k, pltpu.roll(x, k, 0), fill)


def _mod_kernel(c_ref, w_ref, b_ref, o_ref):
    c = c_ref[...]
    s = c * _sigmoid(c)
    sh, sl = _split2(s)
    wh, wl = _split2(w_ref[0])
    o_ref[...] = _dot(sh, wh) + (_dot(sh, wl) + _dot(sl, wh)) + b_ref[...]


def _mod_call(c_pad, w_ada, b_ada):
    rows, d = c_pad.shape
    n = w_ada.shape[2]
    return pl.pallas_call(
        _mod_kernel,
        grid=(n // d,),
        in_specs=[pl.BlockSpec((rows, d), lambda j: (0, 0)),
                  pl.BlockSpec((1, d, d), lambda j: (0, 0, j)),
                  pl.BlockSpec((1, d), lambda j: (0, j))],
        out_specs=pl.BlockSpec((rows, d), lambda j: (0, j)),
        out_shape=jax.ShapeDtypeStruct((rows, n), F32),
        compiler_params=pltpu.CompilerParams(vmem_limit_bytes=VMEM_LIMIT),
        name="mod",
    )(c_pad, w_ada, b_ada)


def _inproj_kernel(x_ref, mod_ref, wt_ref, bf_ref, wg_ref, wu_ref, wd_ref,
                   q_ref, k_ref, vt_ref, xb_ref, gb_ref, wg_bf_ref, wu_bf_ref, wd_bf_ref,
                   carry_ref, v_scr, w_scr, stage, wsem):
    j = pl.program_id(1)
    d_attn = N_HEADS * HEAD_DIM
    d_lru = xb_ref.shape[2]
    c_k, c_v, c_x = d_attn, 2 * d_attn, 3 * d_attn
    c_g, c_f = c_x + d_lru, c_x + 2 * d_lru
    r_f = 3 * d_attn
    r_x, r_g = r_f + N_HEADS, r_f + N_HEADS + d_lru

    @pl.when(jnp.logical_and(pl.program_id(0) == 0, j == 0))
    def _():
        blocks = [(0, d_attn, 0, HEAD_DIM ** -0.5 * LOG2E, None), (d_attn, d_attn, c_k, None, None),
                  (2 * d_attn, d_attn, c_v, None, None), (r_x, d_lru, c_x, None, None),
                  (r_g, d_lru, c_g, None, None), (r_f, LANES, c_f, None, N_HEADS)]
        copies = [pltpu.make_async_copy(wt_ref.at[0, pl.ds(r0, n), :],
                                        stage.at[k, pl.ds(0, n), :], wsem.at[k])
                  for k, (r0, n, _, _, _) in enumerate(blocks)]
        for copy in copies:
            copy.start()
        for k, (_, n, c0, scale, keep_lanes) in enumerate(blocks):
            copies[k].wait()
            blk = stage[k, 0:n, :]
            if scale is not None:
                blk = blk * scale
            blk = blk.T
            if keep_lanes is not None:
                lane = lax.broadcasted_iota(jnp.int32, blk.shape, 1)
                blk = jnp.where(lane < keep_lanes, blk, 0.0)
            w_scr[:, c0:c0 + n] = blk.astype(BF16)

    @pl.when(j == 0)
    def _():
        carry_ref[...] = jnp.zeros_like(carry_ref)

    tm = x_ref.shape[1]
    u = _layer_norm(x_ref[0]) * (1.0 + mod_ref[0, 1:2, :]) + mod_ref[0, 0:1, :]
    ub = u.astype(BF16)

    logf = _log_sigmoid(_dot(ub, w_scr[:, c_f:c_f + LANES]) + bf_ref[...])
    cum = _cumsum_rows(logf, carry_ref[...])
    carry_ref[...] = cum[tm - 1:tm, :]
    c_hi, c_mid, c_lo = [p.astype(F32) for p in _split3(cum * LOG2E)]

    lane = lax.broadcasted_iota(jnp.int32, (tm, HEAD_PAD), 1)
    d = HEAD_DIM
    q_all = _dot(ub, w_scr[:, 0:c_k])
    k_all = _dot(ub, w_scr[:, c_k:c_v])
    for h in range(N_HEADS):
        ts = slice((h // 2) * LANES, (h // 2 + 1) * LANES)
        qh, kh = q_all[:, ts], k_all[:, ts]
        if h % 2:
            qh, kh = pltpu.roll(qh, d, 1), pltpu.roll(kh, d, 1)
        hi, mid, lo = c_hi[:, h:h + 1], c_mid[:, h:h + 1], c_lo[:, h:h + 1]
        q_ext = jnp.where(lane == d, hi, jnp.where(lane == d + 1, mid, jnp.where(
            lane == d + 2, lo, jnp.where(lane < d + 6, 1.0, 0.0))))
        k_ext = jnp.where(lane < d + 3, 1.0, jnp.where(lane == d + 3, -hi, jnp.where(
            lane == d + 4, -mid, jnp.where(lane == d + 5, -lo, 0.0))))
        q_ref[0, h] = jnp.where(lane < d, qh, q_ext).astype(BF16)
        k_ref[0, h] = jnp.where(lane < d, kh, k_ext).astype(BF16)

    v_scr[...] = _dot(ub, w_scr[:, c_v:c_x])
    vt_all = v_scr[...].T
    pad_rows = lax.broadcasted_iota(jnp.int32, (VT_ROWS - d, tm), 0)
    ones_row = jnp.where(pad_rows == 0, 1.0, 0.0).astype(BF16)
    for h in range(N_HEADS):
        vt_ref[0, h, 0, :d, :] = vt_all[h * d:(h + 1) * d, :].astype(BF16)
        vt_ref[0, h, 0, d:, :] = ones_row

    xb_ref[0] = _dot(ub, w_scr[:, c_x:c_g])
    gb_ref[0] = _dot(ub, w_scr[:, c_g:c_f])

    wg_bf_ref[0] = wg_ref[0, 0].astype(BF16)
    wu_bf_ref[0] = wu_ref[0, 0].astype(BF16)
    wd_bf_ref[0] = wd_ref[0, 0].astype(BF16)


def _inproj_call(x, mod3, w_in_t, bf_pad, d_lru, w_gate, w_up, w_down):
    b, s, d = x.shape
    d_attn = N_HEADS * HEAD_DIM
    assert w_in_t.shape[1] == 3 * d_attn + N_HEADS + 2 * d_lru
    _, n_exp, _, de = w_gate.shape
    nj = s // TM
    assert (b * nj) % n_exp == 0
    parts = b * nj // n_exp
    step = lambda bi, j: bi * nj + j
    w_src = lambda rows, cols: pl.BlockSpec(
        (1, 1, rows // parts, cols), lambda bi, j: (0, step(bi, j) // parts, step(bi, j) % parts, 0))
    w_dst = lambda rows, cols: pl.BlockSpec(
        (1, rows // parts, cols), lambda bi, j: (step(bi, j) // parts, step(bi, j) % parts, 0))
    full = lambda shape: pl.BlockSpec(shape, lambda bi, j: (0,) * len(shape))
    head_spec = pl.BlockSpec((1, N_HEADS, TM, HEAD_PAD), lambda bi, j: (bi, 0, j, 0))
    vt_spec = pl.BlockSpec((1, N_HEADS, 1, VT_ROWS, TM), lambda bi, j: (bi, 0, j, 0, 0))
    row_spec = pl.BlockSpec((1, TM, d_lru), lambda bi, j: (bi, j, 0))
    head_shape = jax.ShapeDtypeStruct((b, N_HEADS, s, HEAD_PAD), BF16)
    return pl.pallas_call(
        _inproj_kernel,
        grid=(b, s // TM),
        in_specs=[pl.BlockSpec((1, TM, d), lambda bi, j: (bi, j, 0)),
                  pl.BlockSpec((1, 6, d), lambda bi, j: (bi, 0, 0)),
                  pl.BlockSpec(memory_space=pl.ANY), full((1, LANES)),
                  w_src(d, de), w_src(d, de), w_src(de, d)],
        out_specs=[head_spec, head_spec, vt_spec, row_spec, row_spec,
                   w_dst(d, de), w_dst(d, de), w_dst(de, d)],
        out_shape=[head_shape, head_shape,
                   jax.ShapeDtypeStruct((b, N_HEADS, s // TM, VT_ROWS, TM), BF16),
                   jax.ShapeDtypeStruct((b, s, d_lru), F32),
                   jax.ShapeDtypeStruct((b, s, d_lru), F32),
                   jax.ShapeDtypeStruct((n_exp, d, de), BF16),
                   jax.ShapeDtypeStruct((n_exp, d, de), BF16),
                   jax.ShapeDtypeStruct((n_exp, de, d), BF16)],
        scratch_shapes=[pltpu.VMEM((1, LANES), F32), pltpu.VMEM((TM, d_attn), F32),
                        pltpu.VMEM((d, 3 * d_attn + 2 * d_lru + LANES), BF16),
                        pltpu.VMEM((6, max(d_attn, d_lru), d), F32),
                        pltpu.SemaphoreType.DMA((6,))],
        compiler_params=pltpu.CompilerParams(
            dimension_semantics=("arbitrary", "arbitrary"), vmem_limit_bytes=VMEM_LIMIT),
        name="inproj",
    )(x, mod3, w_in_t, bf_pad, w_gate, w_up, w_down)


def _attn_kernel(q_ref, k_ref, vt_ref, o_ref, s_scr, smax_scr, m_scr, acc_scr):
    tq = TQ
    nq = q_ref.shape[2] // tq
    heads = range(q_ref.shape[1])

    def scores_to(slot, i, j):
        for hh in heads:
            k = k_ref[0, hh, pl.ds(pl.multiple_of(j * tq, tq), tq), :]
            q = q_ref[0, hh, pl.ds(pl.multiple_of(i * tq, tq), tq), :]
            s = lax.dot_general(k, q, (((1,), (1,)), ((), ())), preferred_element_type=F32)
            s_scr[slot, hh] = s
            smax_scr[slot, hh] = jnp.max(s, axis=0, keepdims=True)

    def consume(slot, j, masked):
        for hh in heads:
            s = s_scr[slot, hh]
            if masked:
                key = lax.broadcasted_iota(jnp.int32, (tq, tq), 0)
                qry = lax.broadcasted_iota(jnp.int32, (tq, tq), 1)
                s = jnp.where(key <= qry, s, NEG_INF)
                s_max = jnp.max(s, axis=0, keepdims=True)
            else:
                s_max = smax_scr[slot, hh]
            m = m_scr[hh]
            m_new = jnp.maximum(m, s_max)
            p = jnp.exp2(s - m_new).astype(BF16)
            acc_scr[hh] = jnp.exp2(m - m_new) * acc_scr[hh] + _dot(vt_ref[0, hh, j], p)
            m_scr[hh] = m_new

    scores_to(2, 0, 0)

    def query_block(i, carry):
        m_scr[...] = jnp.full(m_scr.shape, NEG_INF, F32)
        acc_scr[...] = jnp.zeros(acc_scr.shape, F32)
        nxt = jnp.minimum(i + 1, nq - 1)
        n_mid = i - 1

        @pl.when(i == 0)
        def _():
            consume(2, 0, True)
            scores_to(2, nxt, 0)

        @pl.when(i >= 1)
        def _():
            scores_to(1, i, 1)
            consume(2, 0, False)

        def pair(kk, c):
            j = 1 + 2 * kk
            scores_to(0, i, j + 1)
            consume(1, j, False)
            scores_to(1, i, j + 2)
            consume(0, j + 1, False)
            return c

        lax.fori_loop(0, jnp.maximum(n_mid, 0) // 2, pair, 0)

        @pl.when(jnp.logical_and(i >= 1, n_mid % 2 == 1))
        def _():
            scores_to(0, i, i)
            consume(1, i - 1, False)
            scores_to(2, nxt, 0)
            consume(0, i, True)

        @pl.when(jnp.logical_and(i >= 1, n_mid % 2 == 0))
        def _():
            scores_to(2, nxt, 0)
            consume(1, i, True)

        for hh in heads:
            o_ref[0, 0, i, hh * HEAD_DIM:(hh + 1) * HEAD_DIM, :] = (
                acc_scr[hh, :HEAD_DIM, :] / acc_scr[hh, HEAD_DIM:HEAD_DIM + 1, :])
        return carry

    lax.fori_loop(0, nq, query_block, 0)


def _attn_call(q_aug, k_aug, vt_aug):
    b, h, s, hp = q_aug.shape
    assert TQ == TM
    nq = s // TQ
    return pl.pallas_call(
        _attn_kernel,
        grid=(b, h // 2),
        in_specs=[pl.BlockSpec((1, 2, s, hp), lambda bi, p: (bi, p, 0, 0)),
                  pl.BlockSpec((1, 2, s, hp), lambda bi, p: (bi, p, 0, 0)),
                  pl.BlockSpec((1, 2, nq, VT_ROWS, TQ), lambda bi, p: (bi, p, 0, 0, 0))],
        out_specs=pl.BlockSpec((1, 1, nq, 2 * HEAD_DIM, TQ), lambda bi, p: (bi, p, 0, 0, 0)),
        out_shape=jax.ShapeDtypeStruct((b, h // 2, nq, 2 * HEAD_DIM, TQ), F32),
        scratch_shapes=[pltpu.VMEM((3, 2, TQ, TQ), F32), pltpu.VMEM((3, 2, 1, TQ), F32),
                        pltpu.VMEM((2, 1, TQ), F32), pltpu.VMEM((2, VT_ROWS, TQ), F32)],
        compiler_params=pltpu.CompilerParams(
            dimension_semantics=("arbitrary", "arbitrary"), vmem_limit_bytes=VMEM_LIMIT),
        name="attention",
    )(q_aug, k_aug, vt_aug)


def _gelu_tanh(x):
    return 0.5 * x * (1.0 + jnp.tanh(0.7978845608028654 * (x + 0.044715 * (x * x * x))))


def _lru_kernel(xb_ref, gb_ref, cw_ref, cb_ref, wgate_ref, brg_ref, big_ref, lam_ref,
                o_ref, tail_ref, h_ref):
    j = pl.program_id(1)

    @pl.when(j == 0)
    def _():
        tail_ref[...] = jnp.zeros_like(tail_ref)
        h_ref[...] = jnp.zeros_like(h_ref)

    x = xb_ref[0]
    tm, dl = x.shape
    row_in_group = lax.broadcasted_iota(jnp.int32, (tm, dl), 0) % SUBLANES
    x_prev_group = jnp.concatenate([tail_ref[...], x[:tm - SUBLANES]], axis=0)
    xc = x * cw_ref[CONV_WIDTH - 1:CONV_WIDTH, :] + cb_ref[...]
    for k in range(1, CONV_WIDTH):
        xs = _rot_in_group(jnp.where(row_in_group >= SUBLANES - k, x_prev_group, x), k)
        xc = xc + xs * cw_ref[CONV_WIDTH - 1 - k:CONV_WIDTH - k, :]
    tail_ref[...] = x[tm - SUBLANES:, :]

    xcb = xc.astype(BF16)
    n_pairs = dl // LANES
    r_parts, i_parts = [], []
    for p in range(n_pairs):
        g = _dot(xcb[:, p * LANES:(p + 1) * LANES], wgate_ref[p])
        r_parts.append(g[:, :LANES])
        i_parts.append(g[:, LANES:])
    r = _sigmoid(jnp.concatenate(r_parts, axis=1) + brg_ref[...])
    ig = _sigmoid(jnp.concatenate(i_parts, axis=1) + big_ref[...])

    lam = lam_ref[...]
    softplus_neg_lam = jnp.maximum(-lam, 0.0) + jnp.log1p(jnp.exp(-jnp.abs(lam)))
    log_a = (-LRU_C) * r * softplus_neg_lam
    a = jnp.exp(log_a)
    v = 1.0 - a * a
    u = jnp.where(v > 0.0, v * lax.rsqrt(v), 0.0) * (ig * xc)

    k = 1
    while k < SUBLANES:
        keep = row_in_group >= k
        u = a * jnp.where(keep, _rot_in_group(u, k), 0.0) + u
        a = a * jnp.where(keep, _rot_in_group(a, k), 1.0)
        k *= 2
    h_prev = h_ref[...]
    groups = []
    for g in range(tm // SUBLANES):
        rows = slice(g * SUBLANES, (g + 1) * SUBLANES)
        hg = a[rows] * h_prev + u[rows]
        groups.append(hg)
        h_prev = hg[SUBLANES - 1:SUBLANES]
    h_ref[...] = h_prev
    o_ref[0] = jnp.concatenate(groups, axis=0) * _gelu_tanh(gb_ref[0])


def _lru_call(xb, gb, conv_w, conv_b, wgate, b_rg, b_ig, lam):
    b, s, dl = xb.shape
    assert s % TM_L == 0
    row_spec = pl.BlockSpec((1, TM_L, dl), lambda bi, j: (bi, j, 0))
    full = lambda shape: pl.BlockSpec(shape, lambda bi, j: (0,) * len(shape))
    return pl.pallas_call(
        _lru_kernel,
        grid=(b, s // TM_L),
        in_specs=[row_spec, row_spec, full(conv_w.shape), full((1, dl)), full(wgate.shape),
                  full((1, dl)), full((1, dl)), full((1, dl))],
        out_specs=row_spec,
        out_shape=jax.ShapeDtypeStruct((b, s, dl), F32),
        scratch_shapes=[pltpu.VMEM((SUBLANES, dl), F32), pltpu.VMEM((1, dl), F32)],
        compiler_params=pltpu.CompilerParams(
            dimension_semantics=("arbitrary", "arbitrary"), vmem_limit_bytes=VMEM_LIMIT),
        name="lru",
    )(xb, gb, conv_w, conv_b, wgate, b_rg, b_ig, lam)


def _rms(x, gain):
    return x * lax.rsqrt(jnp.mean(x * x, axis=-1, keepdims=True) + RMS_EPS) * gain


def _mixout_kernel(attn_ref, lru_ref, x_ref, mod_ref, ga_ref, gl_ref, woa_ref, wol_ref,
                   g1_ref, b1_ref, wr_ref, br_ref, x1_ref, u2_ref, rinfo_ref, cnt_ref, wo_bf):
    tm, d = x_ref.shape

    @pl.when(pl.program_id(0) == 0)
    def _():
        wo_bf[0] = woa_ref[0].astype(BF16)
        wo_bf[1] = wol_ref[0].astype(BF16)

    attn = attn_ref[0, :, 0].reshape(ga_ref.shape[1], tm).T
    na = _rms(attn, ga_ref[...]).astype(BF16)
    nl = _rms(lru_ref[...], gl_ref[...]).astype(BF16)
    mix = _dot(na, wo_bf[0]) + _dot(nl, wo_bf[1])
    z = DEEPNORM_ALPHA * x_ref[...] + (1.0 + mod_ref[0, 2:3, :]) * mix
    x1 = _layer_norm(z) * g1_ref[...] + b1_ref[...]
    x1_ref[...] = x1
    u2 = _layer_norm(x1) * (1.0 + mod_ref[0, 4:5, :]) + mod_ref[0, 3:4, :]

    nt = (((1,), (1,)), ((), ()))
    uh, ul = _split2(u2)
    wh, wl = _split2(wr_ref[...])
    dg = lambda a, b_: lax.dot_general(a, b_, nt, preferred_element_type=F32)
    lg = dg(wh, uh) + (dg(wh, ul) + dg(wl, uh)) + br_ref[:, 0:1]

    def first_index(vals, target):
        idx = jnp.full_like(target, float(len(vals) - 1))
        for n in range(len(vals) - 2, -1, -1):
            idx = jnp.where(vals[n] == target, float(n), idx)
        return idx

    g = [lg[n:n + 1, :] for n in range(N_GROUPS)]
    gmax = functools.reduce(jnp.maximum, g)
    gsum = functools.reduce(lambda a, b_: a + b_, [jnp.exp(v - gmax) for v in g])
    grp_w = 1.0 / gsum
    gidx = first_index(g, gmax)

    sel = []
    for e in range(EXPERTS_PER_GROUP):
        v = lg[N_GROUPS + (N_GROUPS - 1) * EXPERTS_PER_GROUP + e:
               N_GROUPS + (N_GROUPS - 1) * EXPERTS_PER_GROUP + e + 1, :]
        for gi in range(N_GROUPS - 2, -1, -1):
            r0 = N_GROUPS + gi * EXPERTS_PER_GROUP + e
            v = jnp.where(gidx == float(gi), lg[r0:r0 + 1, :], v)
        sel.append(v)
    smax = functools.reduce(jnp.maximum, sel)
    i1 = first_index(sel, smax)
    rest = [jnp.where(i1 == float(e), -3e38, sel[e]) for e in range(EXPERTS_PER_GROUP)]
    rmax = functools.reduce(jnp.maximum, rest)
    i2 = first_index(rest, rmax)
    e2 = jnp.exp(rmax - smax)
    w1 = grp_w / (1.0 + e2)
    w2 = grp_w * e2 / (1.0 + e2)
    ia = jnp.minimum(i1, i2)
    ib = jnp.maximum(i1, i2)
    wa = jnp.where(i1 < i2, w1, w2)
    wb = jnp.where(i1 < i2, w2, w1)
    pair = jnp.where(ia == 0.0, ib - 1.0, jnp.where(ia == 1.0, ib + 1.0, 5.0))
    bucket = gidx * float(N_PAIRS) + pair

    @pl.when(pl.program_id(0) == 0)
    def _():
        cnt_ref[...] = jnp.zeros_like(cnt_ref)

    cid = lax.broadcasted_iota(jnp.int32, (BUCKET_ROWS, tm), 0).astype(F32)
    cnt_ref[...] += jnp.sum(jnp.where(cid == bucket, 1.0, 0.0), axis=1, keepdims=True)

    zrow = jnp.zeros_like(wa)
    rinfo_ref[...] = jnp.concatenate([bucket, wa, wb] + [zrow] * (SUBLANES - 3), axis=0)
    wt = jnp.concatenate([wa, wb, jnp.zeros((LANES - 2, tm), F32)], axis=0)
    u2_ref[:, :d] = u2
    u2_ref[:, d:] = wt.T


def _mixout_call(attn_t, lru, x2d, mod3, ga, gl, wo, g1, b1, wr, br, seq):
    t, d = x2d.shape
    _, n_pairs, _, pair_w, _ = attn_t.shape
    dh = n_pairs * pair_w
    assert wo.shape[1] == 2 * dh and lru.shape[1] == dh
    per_b = seq // TM
    full = lambda shape: pl.BlockSpec(shape, lambda i: (0,) * len(shape))
    return pl.pallas_call(
        _mixout_kernel,
        grid=(t // TM,),
        in_specs=[pl.BlockSpec((1, n_pairs, 1, pair_w, TM),
                               lambda i: (i // per_b, 0, i % per_b, 0, 0)),
                  pl.BlockSpec((TM, dh), lambda i: (i, 0)),
                  pl.BlockSpec((TM, d), lambda i: (i, 0)),
                  pl.BlockSpec((1, 6, d), lambda i: (i // per_b, 0, 0)),
                  full((1, dh)), full((1, dh)),
                  pl.BlockSpec((1, dh, d), lambda i: (0, 0, 0)),
                  pl.BlockSpec((1, dh, d), lambda i: (0, 1, 0)),
                  full((1, d)), full((1, d)), full((BUCKET_ROWS, d)), full((BUCKET_ROWS, LANES))],
        out_specs=[pl.BlockSpec((TM, d), lambda i: (i, 0)),
                   pl.BlockSpec((TM, d + LANES), lambda i: (i, 0)),
                   pl.BlockSpec((SUBLANES, TM), lambda i: (0, i)),
                   pl.BlockSpec((BUCKET_ROWS, LANES), lambda i: (0, 0))],
        out_shape=[jax.ShapeDtypeStruct((t, d), F32),
                   jax.ShapeDtypeStruct((t, d + LANES), F32),
                   jax.ShapeDtypeStruct((SUBLANES, t), F32),
                   jax.ShapeDtypeStruct((BUCKET_ROWS, LANES), F32)],
        scratch_shapes=[pltpu.VMEM((2, dh, d), BF16)],
        compiler_params=pltpu.CompilerParams(
            dimension_semantics=("arbitrary",), vmem_limit_bytes=VMEM_LIMIT),
        name="mixout",
    )(attn_t, lru, x2d, mod3, ga, gl, wo, wo, g1, b1, wr, br)


def _rank_kernel(rinfo_ref, counts_ref, dest_ref, carry_ref, offs_ref):
    tm = rinfo_ref.shape[1]

    @pl.when(pl.program_id(0) == 0)
    def _():
        carry_ref[...] = jnp.zeros_like(carry_ref)
        padded = jnp.floor((counts_ref[...] + float(TM_E - 1)) * (1.0 / TM_E)) * float(TM_E)
        inc = padded
        k = 1
        while k < BUCKET_ROWS:
            inc = inc + _shift_rows(inc, k, 0.0)
            k *= 2
        offs_ref[...] = inc - padded

    srow = lax.broadcasted_iota(jnp.int32, (TM, TM), 0)
    scol = lax.broadcasted_iota(jnp.int32, (TM, TM), 1)
    upper = (srow <= scol).astype(BF16)
    cid = lax.broadcasted_iota(jnp.int32, (BUCKET_ROWS, TM), 0).astype(F32)
    carry = carry_ref[...]
    for c in range(tm // TM):
        cols = slice(c * TM, (c + 1) * TM)
        onehot = jnp.where(cid == rinfo_ref[0:1, cols], 1.0, 0.0)
        prefix = _dot(onehot.astype(BF16), upper)
        rank = jnp.sum(onehot * (prefix - 1.0 + carry[:, 0:1] + offs_ref[:, 0:1]),
                       axis=0, keepdims=True)
        dest_ref[:, cols] = rank.astype(jnp.int32)
        carry = carry + prefix[:, TM - 1:TM]
    carry_ref[...] = carry


def _rank_call(rinfo, counts):
    t = rinfo.shape[1]
    tm = TM * RANK_CHUNKS
    assert t % tm == 0
    return pl.pallas_call(
        _rank_kernel,
        grid=(t // tm,),
        in_specs=[pl.BlockSpec((SUBLANES, tm), lambda i: (0, i)),
                  pl.BlockSpec((BUCKET_ROWS, LANES), lambda i: (0, 0))],
        out_specs=pl.BlockSpec((1, tm), lambda i: (0, i)),
        out_shape=jax.ShapeDtypeStruct((1, t), jnp.int32),
        scratch_shapes=[pltpu.VMEM((BUCKET_ROWS, LANES), F32),
                        pltpu.VMEM((BUCKET_ROWS, LANES), F32)],
        compiler_params=pltpu.CompilerParams(
            dimension_semantics=("arbitrary",), vmem_limit_bytes=VMEM_LIMIT),
        name="rank",
    )(rinfo, counts)


def _dispatch_kernel(dest_ref, ends_ref, u2_ref, xs_ref, zbuf, sem, zsem):
    tm = u2_ref.shape[0]
    t0 = pl.program_id(0) * tm

    @pl.when(pl.program_id(0) == 0)
    def _():
        zbuf[...] = jnp.zeros_like(zbuf)

        def tail_copy(bkt):
            end = ends_ref[bkt]
            start = ends_ref[bkt - 1] if bkt else 0
            tail = pl.multiple_of(jnp.maximum(end - TM_E, 0), TM_E)
            return end > start, pltpu.make_async_copy(zbuf, xs_ref.at[pl.ds(tail, TM_E)], zsem)

        for bkt in range(N_BUCKETS):
            nonempty, copy = tail_copy(bkt)
            pl.when(nonempty)(copy.start)
        for bkt in range(N_BUCKETS):
            nonempty, copy = tail_copy(bkt)
            pl.when(nonempty)(copy.wait)

        def unused_tile_copy(k):
            return pltpu.make_async_copy(
                zbuf, xs_ref.at[pl.ds(pl.multiple_of(k * TM_E, TM_E), TM_E)], zsem)

        first_unused = ends_ref[N_BUCKETS - 1] // TM_E
        n_tiles = xs_ref.shape[0] // TM_E
        lax.fori_loop(first_unused, n_tiles, lambda k, c: (unused_tile_copy(k).start(), c)[1], 0)
        lax.fori_loop(first_unused, n_tiles, lambda k, c: (unused_tile_copy(k).wait(), c)[1], 0)

    for r in range(tm):
        pltpu.make_async_copy(u2_ref.at[pl.ds(r, 1)], xs_ref.at[pl.ds(dest_ref[t0 + r], 1)],
                              sem).start(priority=r % DMA_PRIORITIES)
    pltpu.make_async_copy(u2_ref, xs_ref.at[pl.ds(0, tm)], sem).wait()


def _dispatch_call(dest, bucket_ends, u2ext, n_rows):
    t, w = u2ext.shape
    tm = TM_D
    assert t % tm == 0
    grid_spec = pltpu.PrefetchScalarGridSpec(
        num_scalar_prefetch=2,
        grid=(t // tm,),
        in_specs=[pl.BlockSpec((tm, w), lambda i, dr, er: (i, 0))],
        out_specs=pl.BlockSpec(memory_space=pl.ANY),
        scratch_shapes=[pltpu.VMEM((TM_E, w), F32), pltpu.SemaphoreType.DMA(()),
                        pltpu.SemaphoreType.DMA(())],
    )
    return pl.pallas_call(
        _dispatch_kernel,
        grid_spec=grid_spec,
        out_shape=jax.ShapeDtypeStruct((n_rows, w), F32),
        compiler_params=pltpu.CompilerParams(
            dimension_semantics=("arbitrary",), vmem_limit_bytes=VMEM_LIMIT),
        name="dispatch",
    )(dest, bucket_ends, u2ext)


def _experts_kernel(ea_ref, eb_ref, nv_ref, xs_ref, wga_ref, wua_ref, wda_ref,
                    wgb_ref, wub_ref, wdb_ref, ys_ref):
    del ea_ref, eb_ref
    i = pl.program_id(0)
    d = wda_ref.shape[2]

    @pl.when(i < nv_ref[0])
    def _():
        x = xs_ref[:, :d].astype(BF16)

        def expert(wg_ref, wu_ref, wd_ref):
            g = _dot(x, wg_ref[0])
            h = (g * _sigmoid(g)) * _dot(x, wu_ref[0])
            return _dot(h.astype(BF16), wd_ref[0])

        ya = xs_ref[:, d:d + 1] * expert(wga_ref, wua_ref, wda_ref)
        y = ya + xs_ref[:, d + 1:d + 2] * expert(wgb_ref, wub_ref, wdb_ref)
        ys_ref[...] = y.reshape(ys_ref.shape)

    @pl.when(i >= nv_ref[0])
    def _():
        ys_ref[...] = jnp.zeros_like(ys_ref)


def _experts_call(tile_ea, tile_eb, n_valid, xs, wg, wu, wd):
    tp, w = xs.shape
    _, d, de = wg.shape
    row = lambda i, ea, eb, nv: (jnp.minimum(i, nv[0] - 1), 0)
    wa = lambda i, ea, eb, nv: (ea[i], 0, 0)
    wb = lambda i, ea, eb, nv: (eb[i], 0, 0)
    grid_spec = pltpu.PrefetchScalarGridSpec(
        num_scalar_prefetch=3,
        grid=(tp // TM_E,),
        in_specs=[pl.BlockSpec((TM_E, w), row),
                  pl.BlockSpec((1, d, de), wa), pl.BlockSpec((1, d, de), wa),
                  pl.BlockSpec((1, de, d), wa),
                  pl.BlockSpec((1, d, de), wb), pl.BlockSpec((1, d, de), wb),
                  pl.BlockSpec((1, de, d), wb)],
        out_specs=pl.BlockSpec((TM_E, d // LANES, LANES), lambda i, ea, eb, nv: (i, 0, 0)),
    )
    return pl.pallas_call(
        _experts_kernel,
        grid_spec=grid_spec,
        out_shape=jax.ShapeDtypeStruct((tp, d // LANES, LANES), F32),
        compiler_params=pltpu.CompilerParams(
            dimension_semantics=("arbitrary",), vmem_limit_bytes=VMEM_LIMIT),
        name="experts",
    )(tile_ea, tile_eb, n_valid, xs, wg, wu, wd, wg, wu, wd)


def _final_kernel(dest_ref, x1_ref, mod_ref, g2_ref, b2_ref, ys_ref, o_ref, ybuf, sem):
    tm = x1_ref.shape[0]
    i = pl.program_id(0)
    slot = i % GATHER_AHEAD

    def issue_rows(tile, to_slot, r0, n):
        for r in range(n):
            pltpu.make_async_copy(
                ys_ref.at[pl.ds(dest_ref[tile * tm + r0 + r], 1)],
                ybuf.at[to_slot, pl.ds(r0 + r, 1)], sem.at[to_slot],
            ).start(priority=(r0 + r) % DMA_PRIORITIES)

    def normalise_rows(r0, n):
        rows = pl.ds(r0, n)
        y = ybuf[slot, rows].reshape(n, x1_ref.shape[1])
        z = DEEPNORM_ALPHA * x1_ref[rows, :] + (1.0 + mod_ref[0, 5:6, :]) * y
        o_ref[rows, :] = _layer_norm(z) * g2_ref[...] + b2_ref[...]

    n_chunks = tm // ISSUE_UNROLL

    @pl.when(i == 0)
    def _():
        for tile in range(GATHER_AHEAD):
            issue_rows(tile, tile, 0, tm)

    pltpu.make_async_copy(ys_ref.at[pl.ds(0, tm)], ybuf.at[slot], sem.at[slot]).wait()

    def chunk(c, issue_ahead):
        r0 = c * ISSUE_UNROLL
        normalise_rows(r0, ISSUE_UNROLL)
        if issue_ahead:
            issue_rows(i + GATHER_AHEAD, slot, r0, ISSUE_UNROLL)

    more = i + GATHER_AHEAD < pl.num_programs(0)

    @pl.when(more)
    def _():
        for c in range(n_chunks):
            chunk(c, True)

    @pl.when(jnp.logical_not(more))
    def _():
        for c in range(n_chunks):
            chunk(c, False)


def _final_call(dest, x1, mod3, g2, b2, ys, seq):
    t, d = x1.shape
    per_b = seq // TM
    grid_spec = pltpu.PrefetchScalarGridSpec(
        num_scalar_prefetch=1,
        grid=(t // TM,),
        in_specs=[pl.BlockSpec((TM, d), lambda i, dr: (i, 0)),
                  pl.BlockSpec((1, 6, d), lambda i, dr: (i // per_b, 0, 0)),
                  pl.BlockSpec((1, d), lambda i, dr: (0, 0)),
                  pl.BlockSpec((1, d), lambda i, dr: (0, 0)),
                  pl.BlockSpec(memory_space=pl.ANY)],
        out_specs=pl.BlockSpec((TM, d), lambda i, dr: (i, 0)),
        scratch_shapes=[pltpu.VMEM((GATHER_AHEAD, TM, d // LANES, LANES), F32),
                        pltpu.SemaphoreType.DMA((GATHER_AHEAD,))],
    )
    return pl.pallas_call(
        _final_kernel,
        grid_spec=grid_spec,
        out_shape=jax.ShapeDtypeStruct((t, d), F32),
        compiler_params=pltpu.CompilerParams(
            dimension_semantics=("arbitrary",), vmem_limit_bytes=VMEM_LIMIT),
        name="final",
    )(dest, x1, mod3, g2, b2, ys)


def _gate_pairs(w_rg, w_ig):
    def pairs(w):
        n, bs, _ = w.shape
        w = w.reshape(n // 2, 2, bs, bs)
        z = jnp.zeros((n // 2, bs, bs), w.dtype)
        top = jnp.concatenate([w[:, 0], z], axis=2)
        bot = jnp.concatenate([z, w[:, 1]], axis=2)
        return jnp.concatenate([top, bot], axis=1)
    return jnp.concatenate([pairs(w_rg), pairs(w_ig)], axis=2).astype(BF16)


def kernel(x, c, w_ada, b_ada, w_in, b_f, conv_w, conv_b, w_rg, b_rg, w_ig, b_ig, lru_lambda,
           g_attn, g_lru, w_out, ln1_g, ln1_b, w_grp, b_grp, w_exp, b_exp,
           w_e_gate, w_e_up, w_e_down, ln2_g, ln2_b):
    assert w_ada.shape[0] == DEPTH
    b, s, d = x.shape
    t = b * s
    d_attn = N_HEADS * HEAD_DIM
    d_lru = conv_w.shape[2]
    n_exp = w_e_gate.shape[1]
    assert s % TM == 0 and s % TQ == 0 and t % TM_E == 0

    c_pad = jnp.pad(c, ((0, SUBLANES - b), (0, 0)))
    mod = _mod_call(c_pad, w_ada, b_ada[0][None, :])
    mod3 = mod[:b].reshape(b, 6, d)

    bf_pad = jnp.pad(b_f[0], (0, LANES - N_HEADS))[None, :]
    q_aug, k_aug, vt_aug, xb, gb, wg_bf, wu_bf, wd_bf = _inproj_call(
        x, mod3, jnp.swapaxes(w_in, 1, 2), bf_pad, d_lru, w_e_gate, w_e_up, w_e_down)
    attn_t = _attn_call(q_aug, k_aug, vt_aug)
    lru = _lru_call(xb, gb, conv_w[0], conv_b[0][None, :], _gate_pairs(w_rg[0], w_ig[0]),
                    b_rg[0][None, :], b_ig[0][None, :], lru_lambda[0][None, :])

    n_route = N_GROUPS + n_exp
    wr = jnp.pad(jnp.concatenate([w_grp[0], w_exp[0]], axis=1).T,
                 ((0, BUCKET_ROWS - n_route), (0, 0)))
    br = jnp.pad(jnp.concatenate([b_grp[0], b_exp[0]]), (0, BUCKET_ROWS - n_route))
    br = jnp.broadcast_to(br[:, None], (BUCKET_ROWS, LANES))
    x1, u2ext, rinfo, counts = _mixout_call(
        attn_t, lru.reshape(t, d_lru), x.reshape(t, d), mod3,
        g_attn[0][None, :], g_lru[0][None, :], w_out,
        ln1_g[0][None, :], ln1_b[0][None, :], wr, br, s)

    dest = _rank_call(rinfo, counts).reshape(t)
    cnt = counts[:N_BUCKETS, 0].astype(jnp.int32)
    ends = jnp.cumsum((cnt + (TM_E - 1)) // TM_E)
    n_tiles = t // TM_E + N_BUCKETS
    tile_bucket = jnp.sum(ends[None, :] <= jnp.arange(n_tiles)[:, None], axis=1)
    tile_bucket = jnp.minimum(tile_bucket, N_BUCKETS - 1)
    n_valid = ends[N_BUCKETS - 1:]
    last_bucket = tile_bucket[jnp.maximum(n_valid[0] - 1, 0)]
    tile_bucket = jnp.where(jnp.arange(n_tiles) < n_valid[0], tile_bucket, last_bucket)
    tile_ea = jnp.asarray(_BUCKET_EA, jnp.int32)[tile_bucket]
    tile_eb = jnp.asarray(_BUCKET_EB, jnp.int32)[tile_bucket]

    xs = _dispatch_call(dest, (ends * TM_E).astype(jnp.int32), u2ext, n_tiles * TM_E)
    ys = _experts_call(tile_ea, tile_eb, n_valid.astype(jnp.int32), xs, wg_bf, wu_bf, wd_bf)
    out = _final_call(dest, x1, mod3, ln2_g[0][None, :], ln2_b[0][None, :], ys, s)
    return out.reshape(b, s, d)
```
